```python
import math
import jax, jax.numpy as jnp
from jax import lax
import numpy as np

D_MODEL = 4096
BATCH = 1
SEQ = 16384
DEPTH = 2

N_META = 16
D_CONV = D_MODEL // 2
D_SSM = D_MODEL // 4
SSM_GROUP = 16
N_GROUPS = D_SSM // SSM_GROUP
SSM_STATE = 64
D_FF = 3 * D_MODEL
CONV_W = 3
N_IN = 3 * D_CONV + D_SSM + 2 * D_MODEL
EPS = 1e-6

kernel_name = "hybrid_conv_s5_gated_encoder"


def rmsnorm(x, g):
    x32 = x.astype(jnp.float32)
    y = x32 * lax.rsqrt(jnp.mean(x32 * x32, axis=-1, keepdims=True) + EPS)
    return (y * g.astype(jnp.float32)).astype(x.dtype)


def dwconv3(x, w):
    xp = jnp.pad(x, ((0, 0), (1, 1), (0, 0)))
    return xp[:, :-2] * w[0] + xp[:, 1:-1] * w[1] + xp[:, 2:] * w[2]


def _scan_combine(e1, e2):
    a1r, a1i, b1r, b1i = e1
    a2r, a2i, b2r, b2i = e2
    return (a2r * a1r - a2i * a1i,
            a2r * a1i + a2i * a1r,
            a2r * b1r - a2i * b1i + b2r,
            a2r * b1i + a2i * b1r + b2i)


def s5_direction(u, lam_re, lam_im, log_step, b_re, b_im, c_re, c_im, reverse):
    dt = jnp.exp(log_step)[:, None]
    mag = jnp.exp(lam_re * dt)
    abar_re = mag * jnp.cos(lam_im * dt)
    abar_im = mag * jnp.sin(lam_im * dt)
    nr, ni = abar_re - 1.0, abar_im
    den = lam_re * lam_re + lam_im * lam_im
    f_re = (nr * lam_re + ni * lam_im) / den
    f_im = (ni * lam_re - nr * lam_im) / den
    bb_re = f_re[..., None] * b_re - f_im[..., None] * b_im
    bb_im = f_re[..., None] * b_im + f_im[..., None] * b_re
    bu_re = jnp.einsum('blgh,gph->blgp', u, bb_re)
    bu_im = jnp.einsum('blgh,gph->blgp', u, bb_im)
    a_re = jnp.broadcast_to(abar_re, bu_re.shape)
    a_im = jnp.broadcast_to(abar_im, bu_re.shape)
    _, _, s_re, s_im = lax.associative_scan(
        _scan_combine, (a_re, a_im, bu_re, bu_im), axis=1, reverse=reverse)
    return (jnp.einsum('blgp,ghp->blgh', s_re, c_re)
            - jnp.einsum('blgp,ghp->blgh', s_im, c_im))


def setup_inputs(seed: int = 0) -> dict:
    key = jax.random.key(seed)
    ks = jax.random.split(key, 24)
    f32 = jnp.float32

    def nrm(k, shape, scale):
        return jax.random.normal(k, shape, f32) * scale

    n_idx = jnp.arange(SSM_STATE, dtype=f32)
    lam_re = -0.5 + nrm(ks[8], (DEPTH, 2, N_GROUPS, SSM_STATE), 0.01)
    lam_im = math.pi * n_idx + nrm(ks[9], (DEPTH, 2, N_GROUPS, SSM_STATE), 0.01)
    log_step = jax.random.uniform(ks[10], (DEPTH, 2, N_GROUPS), f32,
                                  math.log(1e-3), math.log(1e-1))
    return {
        "x": nrm(ks[0], (BATCH, SEQ, D_MODEL), 1.0),
        "meta_tokens": nrm(ks[1], (N_META, D_MODEL), 1.0),
        "norm_mix": 1.0 + nrm(ks[2], (DEPTH, D_MODEL), 0.02),
        "w_in": nrm(ks[3], (DEPTH, D_MODEL, N_IN), D_MODEL ** -0.5),
        "conv_a_w": nrm(ks[4], (DEPTH, CONV_W, D_CONV), CONV_W ** -0.5),
        "w_a": nrm(ks[5], (DEPTH, D_CONV, D_MODEL), D_CONV ** -0.5),
        "ssm_lambda_re": lam_re,
        "ssm_lambda_im": lam_im,
        "ssm_log_step": log_step,
        "ssm_b_re": nrm(ks[11], (DEPTH, 2, N_GROUPS, SSM_STATE, SSM_GROUP), (2 * SSM_GROUP) ** -0.5),
        "ssm_b_im": nrm(ks[12], (DEPTH, 2, N_GROUPS, SSM_STATE, SSM_GROUP), (2 * SSM_GROUP) ** -0.5),
        "ssm_c_re": nrm(ks[13], (DEPTH, 2, N_GROUPS, SSM_GROUP, SSM_STATE), 0.5),
        "ssm_c_im": nrm(ks[14], (DEPTH, 2, N_GROUPS, SSM_GROUP, SSM_STATE), 0.5),
        "ssm_d": nrm(ks[15], (DEPTH, D_SSM), 1.0),
        "w_glu": nrm(ks[16], (DEPTH, D_SSM, 2 * D_MODEL), D_SSM ** -0.5),
        "w_out": nrm(ks[17], (DEPTH, D_MODEL, D_MODEL), D_MODEL ** -0.5),
        "norm_ffn": 1.0 + nrm(ks[18], (DEPTH, D_MODEL), 0.02),
        "w_up": nrm(ks[19], (DEPTH, D_MODEL, 2 * D_FF), D_MODEL ** -0.5),
        "conv_ffn_w": nrm(ks[20], (DEPTH, CONV_W, 2 * D_FF), CONV_W ** -0.5),
        "w_down": nrm(ks[21], (DEPTH, D_FF, D_MODEL), D_FF ** -0.5),
        "norm_final": 1.0 + nrm(ks[22], (D_MODEL,), 0.02),
    }


def reference(x, meta_tokens, norm_mix, w_in, conv_a_w, w_a,
              ssm_lambda_re, ssm_lambda_im, ssm_log_step,
              ssm_b_re, ssm_b_im, ssm_c_re, ssm_c_im, ssm_d,
              w_glu, w_out, norm_ffn, w_up, conv_ffn_w, w_down, norm_final):
    bsz = x.shape[0]
    meta = jnp.broadcast_to(meta_tokens[None].astype(x.dtype), (bsz, N_META, D_MODEL))
    h_res = jnp.concatenate([meta, x], axis=1)
    L = h_res.shape[1]

    for l in range(DEPTH):
        h = rmsnorm(h_res, norm_mix[l])
        proj = h @ w_in[l]
        o = 0
        bg = proj[..., o:o + D_CONV]; o += D_CONV
        cg = proj[..., o:o + D_CONV]; o += D_CONV
        v = proj[..., o:o + D_CONV]; o += D_CONV
        u = proj[..., o:o + D_SSM]; o += D_SSM
        g_a = jax.nn.sigmoid(proj[..., o:o + D_MODEL]); o += D_MODEL
        g_b = jax.nn.sigmoid(proj[..., o:o + D_MODEL])

        y_a = (bg * dwconv3(cg * v, conv_a_w[l])) @ w_a[l]

        u32 = u.astype(jnp.float32).reshape(bsz, L, N_GROUPS, SSM_GROUP)
        lre = ssm_lambda_re[l].astype(jnp.float32)
        lim = ssm_lambda_im[l].astype(jnp.float32)
        lst = ssm_log_step[l].astype(jnp.float32)
        bre = ssm_b_re[l].astype(jnp.float32); bim = ssm_b_im[l].astype(jnp.float32)
        cre = ssm_c_re[l].astype(jnp.float32); cim = ssm_c_im[l].astype(jnp.float32)
        y_f = s5_direction(u32, lre[0], lim[0], lst[0], bre[0], bim[0], cre[0], cim[0], False)
        y_bw = s5_direction(u32, lre[1], lim[1], lst[1], bre[1], bim[1], cre[1], cim[1], True)
        s = (y_f + y_bw).reshape(bsz, L, D_SSM) + ssm_d[l].astype(jnp.float32) * u32.reshape(bsz, L, D_SSM)
        s = jax.nn.gelu(s).astype(x.dtype)
        glu = s @ w_glu[l]
        y_b = glu[..., :D_MODEL] * jax.nn.sigmoid(glu[..., D_MODEL:])

        h_res = h_res + (g_a * y_a + g_b * y_b) @ w_out[l]

        h = rmsnorm(h_res, norm_ffn[l])
        up = dwconv3(h @ w_up[l], conv_ffn_w[l])
        act = jax.nn.gelu(up[..., :D_FF]) * up[..., D_FF:]
        h_res = h_res + act @ w_down[l]

    out = rmsnorm(h_res, norm_final)
    return out[:, N_META:]
```

```python
import functools

import numpy as np
import jax
import jax.numpy as jnp
from jax import lax
from jax.experimental import pallas as pl
from jax.experimental.pallas import tpu as pltpu

F32 = jnp.float32
BF16 = jnp.bfloat16
EPS = 1e-6

T_BLK = 8
S5_CB_WIDTH = 256
BF16_ROWS = 16
VMEM_CAP_BYTES = 60000 * 1024
VMEM_TEMP_BYTES = 12 * 1024 * 1024

TM_TARGET = 640
TM_EW_TARGET = 320
R_S5_TARGET = 416
SCAN_LANES_TARGET = 256


def _round_up(n, m):
    return (n + m - 1) // m * m


def _pick_tile(n, target, align):
    best = 0
    for t in range(align, min(n, target) + 1, align):
        if n % t == 0:
            best = t
    assert best > 0, (n, target, align)
    return best


def _nbytes(shape, dtype):
    return int(np.prod(shape)) * jnp.dtype(dtype).itemsize


def _compiler_params(semantics, block_bytes, scratch_bytes=0):
    est = 2 * block_bytes + scratch_bytes + VMEM_TEMP_BYTES
    return pltpu.CompilerParams(dimension_semantics=semantics,
                                vmem_limit_bytes=int(min(VMEM_CAP_BYTES, est)))


def _dot(a, b):
    return jnp.dot(a, b, preferred_element_type=F32)


def _rmsnorm_kernel(x_ref, g_ref, o_ref):
    x = x_ref[...]
    ms = jnp.mean(x * x, axis=-1, keepdims=True)
    o_ref[...] = (x * lax.rsqrt(ms + EPS) * g_ref[...]).astype(o_ref.dtype)


def _rmsnorm(x, g, out_dtype, tm):
    n, d = x.shape
    blocks = _nbytes((tm, d), F32) + _nbytes((tm, d), out_dtype)
    return pl.pallas_call(
        _rmsnorm_kernel,
        out_shape=jax.ShapeDtypeStruct((n, d), out_dtype),
        grid=(n // tm,),
        in_specs=[pl.BlockSpec((tm, d), lambda i: (i, 0)),
                  pl.BlockSpec((1, d), lambda i: (0, 0))],
        out_specs=pl.BlockSpec((tm, d), lambda i: (i, 0)),
        compiler_params=_compiler_params(("parallel",), blocks),
        name="rmsnorm",
    )(x, g.reshape(1, d).astype(F32))


def _mm_kernel(*refs, n_extra, epilogue, tm):
    a_ref, w_ref = refs[0], refs[1]
    extras = refs[2:2 + n_extra]
    o_ref = refs[2 + n_extra]
    acc = _dot(a_ref[...], w_ref[...])
    row0 = pl.program_id(0) * tm
    o_ref[...] = epilogue(acc, row0, *[e[...] for e in extras]).astype(o_ref.dtype)


def _mm(a, w, *, col0, n_cols, tm, tn, epilogue, extras=(), extra_specs=(), out_shape, out_spec,
        aliases=None, name):
    n, k = a.shape
    assert n % tm == 0 and n_cols % tn == 0 and col0 % tn == 0
    jb0 = col0 // tn
    blocks = (_nbytes((tm, k), a.dtype) + _nbytes((k, tn), w.dtype) + _nbytes((tm, tn), F32)
              + sum(_nbytes((tm, tn), e.dtype) for e in extras))
    return pl.pallas_call(
        functools.partial(_mm_kernel, n_extra=len(extras), epilogue=epilogue, tm=tm),
        out_shape=out_shape,
        grid=(n // tm, n_cols // tn),
        in_specs=[pl.BlockSpec((tm, k), lambda i, j: (i, 0)),
                  pl.BlockSpec((k, tn), lambda i, j: (0, jb0 + j)),
                  *extra_specs],
        out_specs=out_spec,
        input_output_aliases=aliases or {},
        compiler_params=_compiler_params(("parallel", "arbitrary"), blocks),
        name=name,
    )(a, w, *extras)


def _epi_identity(acc, row0):
    return acc


def _epi_sigmoid(acc, row0):
    return jax.nn.sigmoid(acc)


def _epi_gate(acc, row0, gate):
    return acc * gate.astype(F32)


def _epi_residual(acc, row0, res, *, n_valid):
    rows = row0 + lax.broadcasted_iota(jnp.int32, acc.shape, 0)
    return jnp.where(rows < n_valid, res + acc, 0.0)


def _inproj_conv_kernel(a_ref, wb_ref, wc_ref, wv_ref, bg_ref, cv_ref):
    a = a_ref[...]
    bg_ref[...] = _dot(a, wb_ref[...]).astype(bg_ref.dtype)
    cv_ref[...] = (_dot(a, wc_ref[...]) * _dot(a, wv_ref[...])).astype(cv_ref.dtype)


def _inproj_conv(hn, w_in, d_conv, tm, tn):
    n, k = hn.shape
    nb = d_conv // tn
    blocks = _nbytes((tm, k), BF16) + 3 * _nbytes((k, tn), BF16) + 2 * _nbytes((tm, tn), BF16)
    out = jax.ShapeDtypeStruct((n, d_conv), BF16)
    return pl.pallas_call(
        _inproj_conv_kernel,
        out_shape=(out, out),
        grid=(n // tm, nb),
        in_specs=[pl.BlockSpec((tm, k), lambda i, j: (i, 0)),
                  pl.BlockSpec((k, tn), lambda i, j: (0, j)),
                  pl.BlockSpec((k, tn), lambda i, j: (0, nb + j)),
                  pl.BlockSpec((k, tn), lambda i, j: (0, 2 * nb + j))],
        out_specs=(pl.BlockSpec((tm, tn), lambda i, j: (i, j)),
                   pl.BlockSpec((tm, tn), lambda i, j: (i, j))),
        compiler_params=_compiler_params(("parallel", "arbitrary"), blocks),
        name="inproj_conv",
    )(hn, w_in, w_in, w_in)


def _shifted_rows(x, prev_row, next_row):
    t = x.shape[0]
    row = lax.broadcasted_iota(jnp.int32, x.shape, 0)
    up = jnp.where(row == 0, prev_row, pltpu.roll(x, 1, 0))
    dn = jnp.where(row == t - 1, next_row, pltpu.roll(x, t - 1, 0))
    return up, dn


def _conv_gate_kernel(bg_ref, cv_ref, cvp_ref, cvn_ref, w_ref, z_ref):
    i = pl.program_id(0)
    last = pl.num_programs(0) - 1
    x = cv_ref[...].astype(F32)
    prev = cvp_ref[...].astype(F32)[BF16_ROWS - 1:BF16_ROWS, :]
    nxt = cvn_ref[...].astype(F32)[0:1, :]
    prev = jnp.where(i > 0, prev, 0.0)
    nxt = jnp.where(i < last, nxt, 0.0)
    up, dn = _shifted_rows(x, prev, nxt)
    w = w_ref[...]
    conv = up * w[0:1, :] + x * w[1:2, :] + dn * w[2:3, :]
    z_ref[...] = (bg_ref[...].astype(F32) * conv).astype(z_ref.dtype)


def _conv_gate(bg, cv, conv_w, tm, tc):
    n, c = bg.shape
    hb = tm // BF16_ROWS
    n_hb = n // BF16_ROWS
    blocks = 3 * _nbytes((tm, tc), BF16) + 2 * _nbytes((BF16_ROWS, tc), BF16) + _nbytes((3, tc), F32)
    return pl.pallas_call(
        _conv_gate_kernel,
        out_shape=jax.ShapeDtypeStruct((n, c), BF16),
        grid=(n // tm, c // tc),
        in_specs=[pl.BlockSpec((tm, tc), lambda i, j: (i, j)),
                  pl.BlockSpec((tm, tc), lambda i, j: (i, j)),
                  pl.BlockSpec((BF16_ROWS, tc), lambda i, j: (jnp.maximum(i * hb - 1, 0), j)),
                  pl.BlockSpec((BF16_ROWS, tc), lambda i, j: (jnp.minimum((i + 1) * hb, n_hb - 1), j)),
                  pl.BlockSpec((3, tc), lambda i, j: (0, j))],
        out_specs=pl.BlockSpec((tm, tc), lambda i, j: (i, j)),
        compiler_params=_compiler_params(("parallel", "parallel"), blocks),
        name="conv_gate",
    )(bg, cv, cv, cv, conv_w)


def _cmul(ar, ai, br, bi):
    return ar * br - ai * bi, ar * bi + ai * br


def _s5_prep_kernel(lre_ref, lim_ref, lst_ref, bre_ref, bim_ref, cre_ref, cim_ref,
                    ere_ref, eim_ref, care_ref, caim_ref, klag_ref, apre_ref, apim_ref):
    lam_re, lam_im = lre_ref[...], lim_ref[...]
    dt = jnp.exp(lst_ref[...])
    mag = jnp.exp(lam_re * dt)
    a_re = mag * jnp.cos(lam_im * dt)
    a_im = mag * jnp.sin(lam_im * dt)
    nr, ni = a_re - 1.0, a_im
    den = lam_re * lam_re + lam_im * lam_im
    f_re = (nr * lam_re + ni * lam_im) / den
    f_im = (ni * lam_re - nr * lam_im) / den
    b_re, b_im = bre_ref[...], bim_ref[...]
    bb_re, bb_im = _cmul(f_re, f_im, b_re, b_im)
    c_re, c_im = cre_ref[...], cim_ref[...]

    pw = [(jnp.ones_like(a_re), jnp.zeros_like(a_im))]
    for _ in range(T_BLK):
        pw.append(_cmul(pw[-1][0], pw[-1][1], a_re, a_im))

    e_re, e_im, ca_re, ca_im = [], [], [], []
    for k in range(T_BLK):
        er, ei = _cmul(pw[k][0], pw[k][1], bb_re, bb_im)
        e_re.append(er)
        e_im.append(ei)
        cr, ci = _cmul(pw[k + 1][0], pw[k + 1][1], c_re, c_im)
        ca_re.append(cr)
        ca_im.append(-ci)
    e_re = jnp.concatenate(e_re, axis=1)
    e_im = jnp.concatenate(e_im, axis=1)
    ere_ref[...] = e_re
    eim_ref[...] = e_im
    care_ref[...] = jnp.concatenate(ca_re, axis=1)
    caim_ref[...] = jnp.concatenate(ca_im, axis=1)

    dn = (((2,), (2,)), ((0,), (0,)))
    klag_ref[...] = (lax.dot_general(c_re, e_re, dn, precision=lax.Precision.HIGHEST,
                                     preferred_element_type=F32)
                     - lax.dot_general(c_im, e_im, dn, precision=lax.Precision.HIGHEST,
                                       preferred_element_type=F32))

    q_re, q_im = pw[T_BLK]
    r_re, r_im = q_re, q_im
    for m in range(8):
        apre_ref[m] = r_re
        apim_ref[m] = r_im
        r_re, r_im = _cmul(r_re, r_im, q_re, q_im)


def _s5_prep(lam_re, lam_im, log_step, b_re, b_im, c_re, c_im):
    depth, _, g, p = lam_re.shape
    h = b_re.shape[-1]
    lead = (depth, 2)
    b_re_t = jnp.swapaxes(b_re, -1, -2).astype(F32)
    b_im_t = jnp.swapaxes(b_im, -1, -2).astype(F32)

    def spec(*tail):
        zeros = (0,) * len(tail)
        return pl.BlockSpec((None, None) + tail, lambda l, d: (l, d) + zeros)

    th = T_BLK * h
    outs = (jax.ShapeDtypeStruct(lead + (g, th, p), F32),) * 4 + (
        jax.ShapeDtypeStruct(lead + (g, h, th), F32),
        jax.ShapeDtypeStruct(lead + (8, g, 1, p), F32),
        jax.ShapeDtypeStruct(lead + (8, g, 1, p), F32))
    blocks = 4 * _nbytes((g, th, 128), F32) + 8 * _nbytes((g, h, 128), F32)
    return pl.pallas_call(
        _s5_prep_kernel,
        out_shape=outs,
        grid=lead,
        in_specs=[spec(g, 1, p), spec(g, 1, p), spec(g, 1, 1), spec(g, h, p), spec(g, h, p),
                  spec(g, h, p), spec(g, h, p)],
        out_specs=(spec(g, th, p),) * 4 + (spec(g, h, th), spec(8, g, 1, p), spec(8, g, 1, p)),
        compiler_params=_compiler_params(("parallel", "parallel"), blocks, 16 * _nbytes((g, th, 128), F32)),
        name="s5_prep",
    )(lam_re.astype(F32)[:, :, :, None, :], lam_im.astype(F32)[:, :, :, None, :],
      log_step.astype(F32)[..., None, None], b_re_t, b_im_t, c_re.astype(F32), c_im.astype(F32))


def _s5_weights(prep, d_skip, n_cb, gl, h, p):
    e_re, e_im, ca_re, ca_im, klag, ap_re, ap_im = prep
    depth = e_re.shape[0]
    t = T_BLK
    eye = jnp.eye(gl, dtype=F32)

    def split(x):
        return x.reshape(depth, 2, n_cb * gl, t, h, p)

    e = jnp.stack([split(e_re), split(e_im)], axis=2)
    e_sel = jnp.stack([e[:, 0, :, :, ::-1], e[:, 1]], axis=1)
    e_sel = e_sel.reshape(depth, 2, 2, n_cb, gl, t, h, p)
    w_s = jnp.einsum('ldrcgshp,gq->lcsqhdrgp', e_sel, eye)
    w_s = w_s.reshape(depth, n_cb, t * gl * h, 4 * gl * p).astype(BF16)

    def carry(ca):
        ca = split(ca)
        sel = jnp.stack([ca[:, 0], ca[:, 1, :, ::-1]], axis=1)
        sel = sel.reshape(depth, 2, n_cb, gl, t, h, p)
        w = jnp.einsum('ldcgthp,gq->lcdqptgh', sel, eye)
        return w.reshape(depth, n_cb, 2 * gl * p, t * gl * h).astype(BF16)

    w_cre, w_cim = carry(ca_re), carry(ca_im)

    kl = klag.reshape(depth, 2, n_cb * gl, h, t, h)
    sig = np.arange(t)[:, None]
    tau = np.arange(t)[None, :]
    kf = kl[:, 0][:, :, :, np.clip(tau - sig, 0, t - 1), :] * jnp.asarray(tau >= sig, F32)[None, None, None, :, :, None]
    kb = kl[:, 1][:, :, :, np.clip(sig - tau, 0, t - 1), :] * jnp.asarray(sig >= tau, F32)[None, None, None, :, :, None]
    kst = (kf + kb).reshape(depth, n_cb, gl, h, t, t, h)
    w_i = jnp.einsum('lcghsti,gq->lcsqitgh', kst, eye)
    w_i = w_i.reshape(depth, n_cb, t * gl * h, t * gl * h).astype(BF16)

    def lanes(ap):
        ap = jnp.stack([ap[:, 0], ap[:, 1, ::-1]], axis=1)
        ap = ap.reshape(depth, 2, 8, n_cb, gl * p)
        return jnp.transpose(ap, (0, 3, 1, 2, 4))
    apw = jnp.stack([lanes(ap_re), lanes(ap_im)], axis=3)

    d_t = jnp.tile(d_skip.astype(F32).reshape(depth, n_cb, 1, gl * h), (1, 1, 1, t))
    return w_s, w_cre, w_cim, w_i, apw, d_t


def _s5_states_kernel(u_ref, w_ref, s_ref):
    s_ref[...] = _dot(u_ref[...].astype(BF16), w_ref[...])


def _s5_states(u3, w_s, r, tn):
    n_cb, nb, kw = u3.shape
    ns = w_s.shape[-1]
    blocks = _nbytes((r, kw), F32) + _nbytes((kw, tn), BF16) + _nbytes((r, tn), F32)
    return pl.pallas_call(
        _s5_states_kernel,
        out_shape=jax.ShapeDtypeStruct((n_cb, nb, ns), F32),
        grid=(n_cb, ns // tn, nb // r),
        in_specs=[pl.BlockSpec((None, r, kw), lambda c, j, i: (c, i, 0)),
                  pl.BlockSpec((None, kw, tn), lambda c, j, i: (c, 0, j))],
        out_specs=pl.BlockSpec((None, r, tn), lambda c, j, i: (c, i, j)),
        compiler_params=_compiler_params(("parallel", "parallel", "arbitrary"), blocks),
        name="s5_states",
    )(u3, w_s)


def _s5_scan_kernel(sre_ref, sim_ref, apre_ref, apim_ref, xre_ref, xim_ref, *, n_tiles):
    d = pl.program_id(1)
    lw = sre_ref.shape[-1]
    ap_re, ap_im = apre_ref[...], apim_ref[...]
    row = lax.broadcasted_iota(jnp.int32, (8, lw), 0)

    def run(reverse):
        steps = []
        for shift in (1, 2, 4):
            src = (8 - shift) if reverse else (shift - 1)
            keep = (row < 8 - shift) if reverse else (row >= shift)
            steps.append((8 - shift if reverse else shift,
                          jnp.where(keep, ap_re[src:src + 1, :], 0.0),
                          jnp.where(keep, ap_im[src:src + 1, :], 0.0)))
        edge = 7 if reverse else 0
        out_row = 0 if reverse else 7

        def body(k, carry):
            c_re, c_im = carry
            t = (n_tiles - 1 - k) if reverse else k
            r0 = pl.multiple_of(t * 8, 8)
            x_re = sre_ref[pl.ds(r0, 8), :]
            x_im = sim_ref[pl.ds(r0, 8), :]
            for amount, m_re, m_im in steps:
                s_re = pltpu.roll(x_re, amount, 0)
                s_im = pltpu.roll(x_im, amount, 0)
                x_re, x_im = x_re + m_re * s_re - m_im * s_im, x_im + m_re * s_im + m_im * s_re
            inc_re = x_re + ap_re * c_re - ap_im * c_im
            inc_im = x_im + ap_re * c_im + ap_im * c_re
            nb_shift = 7 if reverse else 1
            xre_ref[pl.ds(r0, 8), :] = jnp.where(row == edge, c_re, pltpu.roll(inc_re, nb_shift, 0))
            xim_ref[pl.ds(r0, 8), :] = jnp.where(row == edge, c_im, pltpu.roll(inc_im, nb_shift, 0))
            return inc_re[out_row:out_row + 1, :], inc_im[out_row:out_row + 1, :]

        zero = jnp.zeros((1, lw), F32)
        lax.fori_loop(0, n_tiles, body, (zero, zero))

    @pl.when(d == 0)
    def _():
        run(False)

    @pl.when(d == 1)
    def _():
        run(True)


def _s5_scan(s, apw, lw):
    n_cb, nb, ns = s.shape
    glp = ns // 4
    nq = glp // lw
    blocks = 4 * _nbytes((nb, lw), F32) + 2 * _nbytes((8, lw), F32)
    out = jax.ShapeDtypeStruct((n_cb, nb, 2 * glp), F32)

    def s_spec(ri):
        return pl.BlockSpec((None, nb, lw), lambda c, d, q: (c, 0, (2 * d + ri) * nq + q))

    def ap_spec(ri):
        return pl.BlockSpec((None, None, None, 8, lw), lambda c, d, q: (c, d, ri, 0, q))

    x_spec = pl.BlockSpec((None, nb, lw), lambda c, d, q: (c, 0, d * nq + q))
    return pl.pallas_call(
        functools.partial(_s5_scan_kernel, n_tiles=nb // 8),
        out_shape=(out, out),
        grid=(n_cb, 2, nq),
        in_specs=[s_spec(0), s_spec(1), ap_spec(0), ap_spec(1)],
        out_specs=(x_spec, x_spec),
        compiler_params=_compiler_params(("parallel", "parallel", "parallel"), blocks),
        name="s5_scan",
    )(s, s, apw, apw)


def _s5_out_kernel(u_ref, xre_ref, xim_ref, wi_ref, wcre_ref, wcim_ref, d_ref, uj_ref, s_ref):
    y = _dot(u_ref[...].astype(BF16), wi_ref[...])
    y = y + _dot(xre_ref[...].astype(BF16), wcre_ref[...])
    y = y + _dot(xim_ref[...].astype(BF16), wcim_ref[...])
    y = y + d_ref[...] * uj_ref[...]
    s_ref[...] = jax.nn.gelu(y).astype(s_ref.dtype)


def _s5_out(u3, x_re, x_im, w_i, w_cre, w_cim, d_t, r, tn):
    n_cb, nb, kw = u3.shape
    kx = x_re.shape[-1]
    blocks = (_nbytes((r, kw), F32) + 2 * _nbytes((r, kx), F32) + _nbytes((kw, tn), BF16)
              + 2 * _nbytes((kx, tn), BF16) + 3 * _nbytes((r, tn), F32))
    return pl.pallas_call(
        _s5_out_kernel,
        out_shape=jax.ShapeDtypeStruct((n_cb, nb, kw), BF16),
        grid=(n_cb, kw // tn, nb // r),
        in_specs=[pl.BlockSpec((None, r, kw), lambda c, j, i: (c, i, 0)),
                  pl.BlockSpec((None, r, kx), lambda c, j, i: (c, i, 0)),
                  pl.BlockSpec((None, r, kx), lambda c, j, i: (c, i, 0)),
                  pl.BlockSpec((None, kw, tn), lambda c, j, i: (c, 0, j)),
                  pl.BlockSpec((None, kx, tn), lambda c, j, i: (c, 0, j)),
                  pl.BlockSpec((None, kx, tn), lambda c, j, i: (c, 0, j)),
                  pl.BlockSpec((None, 1, tn), lambda c, j, i: (c, 0, j)),
                  pl.BlockSpec((None, r, tn), lambda c, j, i: (c, i, j))],
        out_specs=pl.BlockSpec((None, r, tn), lambda c, j, i: (c, i, j)),
        compiler_params=_compiler_params(("parallel", "parallel", "arbitrary"), blocks),
        name="s5_out",
    )(u3, x_re, x_im, w_i, w_cre, w_cim, d_t, u3)


def _glu_merge_kernel(s_ref, w1_ref, w2_ref, pa_ref, gb_ref, m_ref, *, n_cb):
    a = jnp.concatenate([s_ref[c] for c in range(n_cb)], axis=1)
    y_b = _dot(a, w1_ref[...]) * jax.nn.sigmoid(_dot(a, w2_ref[...]))
    m_ref[...] = (pa_ref[...].astype(F32) + gb_ref[...].astype(F32) * y_b).astype(m_ref.dtype)


def _glu_merge(s, w_glu, pa, gates, d_model, tm, tn):
    n_cb, n, cbw = s.shape
    k = n_cb * cbw
    nb = d_model // tn
    blocks = _nbytes((n_cb, tm, cbw), BF16) + 2 * _nbytes((k, tn), BF16) + 3 * _nbytes((tm, tn), BF16)
    return pl.pallas_call(
        functools.partial(_glu_merge_kernel, n_cb=n_cb),
        out_shape=jax.ShapeDtypeStruct((n, d_model), BF16),
        grid=(n // tm, nb),
        in_specs=[pl.BlockSpec((n_cb, tm, cbw), lambda i, j: (0, i, 0)),
                  pl.BlockSpec((k, tn), lambda i, j: (0, j)),
                  pl.BlockSpec((k, tn), lambda i, j: (0, nb + j)),
                  pl.BlockSpec((tm, tn), lambda i, j: (i, j)),
                  pl.BlockSpec((tm, tn), lambda i, j: (i, nb + j))],
        out_specs=pl.BlockSpec((tm, tn), lambda i, j: (i, j)),
        compiler_params=_compiler_params(("parallel", "arbitrary"), blocks),
        name="glu_merge",
    )(s, w_glu, w_glu, pa, gates)


def _ffn_up_kernel(h_ref, hp_ref, hn_ref, wg_ref, wv_ref, cg_ref, cv_ref, act_ref, lhs_ref, *, tm):
    i = pl.program_id(0)
    j = pl.program_id(1)
    last = pl.num_programs(0) - 1
    halo = BF16_ROWS

    @pl.when(j == 0)
    def _():
        lhs_ref[0:halo, :] = jnp.where(i > 0, hp_ref[...], jnp.zeros_like(hp_ref))
        lhs_ref[halo:halo + tm, :] = h_ref[...]
        lhs_ref[halo + tm:, :] = jnp.where(i < last, hn_ref[...], jnp.zeros_like(hn_ref))

    a = lhs_ref[...]
    rows = tm + 2 * halo

    def conv(w_ref, cw_ref):
        u = _dot(a, w_ref[...])
        cw = cw_ref[...]
        up = pltpu.roll(u, 1, 0)[halo:halo + tm]
        dn = pltpu.roll(u, rows - 1, 0)[halo:halo + tm]
        return up * cw[0:1, :] + u[halo:halo + tm] * cw[1:2, :] + dn * cw[2:3, :]

    act_ref[...] = (jax.nn.gelu(conv(wg_ref, cg_ref)) * conv(wv_ref, cv_ref)).astype(act_ref.dtype)


def _ffn_up(hn, w_up, conv_w, d_ff, tm, tn):
    n, k = hn.shape
    nb = d_ff // tn
    hb = tm // BF16_ROWS
    n_hb = n // BF16_ROWS
    blocks = (_nbytes((tm + 2 * BF16_ROWS, k), BF16) + 2 * _nbytes((k, tn), BF16) + 2 * _nbytes((3, tn), F32)
              + _nbytes((tm, tn), BF16))
    scratch = _nbytes((tm + 2 * BF16_ROWS, k), BF16)
    return pl.pallas_call(
        functools.partial(_ffn_up_kernel, tm=tm),
        out_shape=jax.ShapeDtypeStruct((n, d_ff), BF16),
        grid=(n // tm, nb),
        in_specs=[pl.BlockSpec((tm, k), lambda i, j: (i, 0)),
                  pl.BlockSpec((BF16_ROWS, k), lambda i, j: (jnp.maximum(i * hb - 1, 0), 0)),
                  pl.BlockSpec((BF16_ROWS, k), lambda i, j: (jnp.minimum((i + 1) * hb, n_hb - 1), 0)),
                  pl.BlockSpec((k, tn), lambda i, j: (0, j)),
                  pl.BlockSpec((k, tn), lambda i, j: (0, nb + j)),
                  pl.BlockSpec((3, tn), lambda i, j: (0, j)),
                  pl.BlockSpec((3, tn), lambda i, j: (0, nb + j))],
        out_specs=pl.BlockSpec((tm, tn), lambda i, j: (i, j)),
        scratch_shapes=[pltpu.VMEM((tm + 2 * BF16_ROWS, k), BF16)],
        compiler_params=_compiler_params(("parallel", "arbitrary"), blocks, scratch),
        name="ffn_up",
    )(hn, hn, hn, w_up, w_up, conv_w, conv_w)


def kernel(x, meta_tokens, norm_mix, w_in, conv_a_w, w_a, ssm_lambda_re, ssm_lambda_im, ssm_log_step,
           ssm_b_re, ssm_b_im, ssm_c_re, ssm_c_im, ssm_d, w_glu, w_out, norm_ffn, w_up, conv_ffn_w,
           w_down, norm_final):
    bsz, seq, d_model = x.shape
    n_meta = meta_tokens.shape[0]
    depth = w_in.shape[0]
    d_conv = w_a.shape[1]
    d_ssm = ssm_d.shape[1]
    d_ff = w_down.shape[1]
    n_groups, p_state, h_grp = ssm_b_re.shape[2:]
    assert w_in.shape[2] == 3 * d_conv + d_ssm + 2 * d_model and n_groups * h_grp == d_ssm

    cbw = min(S5_CB_WIDTH, d_ssm)
    n_cb = d_ssm // cbw
    gl = cbw // h_grp
    assert n_cb * cbw == d_ssm and gl * h_grp == cbw

    n_true = n_meta + seq
    row_align = T_BLK * BF16_ROWS
    n_pad = _round_up(n_true, 10 * row_align) if n_true >= 40 * row_align else _round_up(n_true, row_align)
    nb = n_pad // T_BLK

    tm = _pick_tile(n_pad, TM_TARGET, BF16_ROWS)
    tm_ew = _pick_tile(n_pad, TM_EW_TARGET, BF16_ROWS)
    r_s5 = _pick_tile(nb, R_S5_TARGET, BF16_ROWS)
    tn = lambda n, target=512: _pick_tile(n, target, 128)

    wb = lambda w: w.astype(BF16)
    w_in_b, w_a_b, w_glu_b, w_out_b, w_up_b, w_down_b = map(wb, (w_in, w_a, w_glu, w_out, w_up, w_down))

    prep = _s5_prep(ssm_lambda_re, ssm_lambda_im, ssm_log_step, ssm_b_re, ssm_b_im, ssm_c_re, ssm_c_im)
    w_s, w_cre, w_cim, w_i, apw, d_t = _s5_weights(prep, ssm_d, n_cb, gl, h_grp, p_state)

    residual = functools.partial(_epi_residual, n_valid=n_true)
    tile_spec = lambda t: pl.BlockSpec((tm, t), lambda i, j: (i, j))
    u_off = 3 * d_conv
    gate_off = u_off + d_ssm

    outs = []
    for b in range(bsz):
        h_res = jnp.concatenate([meta_tokens.astype(F32), x[b].astype(F32),
                                 jnp.zeros((n_pad - n_true, d_model), F32)], axis=0)
        for l in range(depth):
            hn = _rmsnorm(h_res, norm_mix[l], BF16, tm_ew)
            t_c = tn(d_conv)
            bg, cv = _inproj_conv(hn, w_in_b[l], d_conv, tm, t_c)
            u = _mm(hn, w_in_b[l], col0=u_off, n_cols=d_ssm, tm=tm, tn=cbw, epilogue=_epi_identity,
                    out_shape=jax.ShapeDtypeStruct((n_cb, n_pad, cbw), F32),
                    out_spec=pl.BlockSpec((None, tm, cbw), lambda i, j: (j, i, 0)), name="inproj_u")
            t_g = tn(2 * d_model)
            gates = _mm(hn, w_in_b[l], col0=gate_off, n_cols=2 * d_model, tm=tm, tn=t_g,
                        epilogue=_epi_sigmoid, out_shape=jax.ShapeDtypeStruct((n_pad, 2 * d_model), BF16),
                        out_spec=tile_spec(t_g), name="inproj_gates")

            z = _conv_gate(bg, cv, conv_a_w[l].astype(F32), tm, t_c)
            t_d = tn(d_model)
            pa = _mm(z, w_a_b[l], col0=0, n_cols=d_model, tm=tm, tn=t_d, epilogue=_epi_gate,
                     extras=(gates,), extra_specs=(tile_spec(t_d),),
                     out_shape=jax.ShapeDtypeStruct((n_pad, d_model), BF16), out_spec=tile_spec(t_d),
                     name="branch_a")

            u3 = u.reshape(n_cb, nb, T_BLK * cbw)
            t_s = tn(T_BLK * cbw)
            states = _s5_states(u3, w_s[l], r_s5, tn(w_s.shape[-1]))
            x_re, x_im = _s5_scan(states, apw[l], tn(gl * p_state, SCAN_LANES_TARGET))
            s = _s5_out(u3, x_re, x_im, w_i[l], w_cre[l], w_cim[l], d_t[l], r_s5, t_s)
            s = s.reshape(n_cb, n_pad, cbw)

            merged = _glu_merge(s, w_glu_b[l], pa, gates, d_model, tm, t_d)
            h_res = _mm(merged, w_out_b[l], col0=0, n_cols=d_model, tm=tm, tn=t_d, epilogue=residual,
                        extras=(h_res,), extra_specs=(tile_spec(t_d),),
                        out_shape=jax.ShapeDtypeStruct((n_pad, d_model), F32), out_spec=tile_spec(t_d),
                        aliases={2: 0}, name="out_proj")

            hn = _rmsnorm(h_res, norm_ffn[l], BF16, tm_ew)
            act = _ffn_up(hn, w_up_b[l], conv_ffn_w[l].astype(F32), d_ff, tm, tn(d_ff))
            t_o = tn(d_model, 256)
            h_res = _mm(act, w_down_b[l], col0=0, n_cols=d_model, tm=tm, tn=t_o, epilogue=residual,
                        extras=(h_res,), extra_specs=(pl.BlockSpec((tm, t_o), lambda i, j: (i, j)),),
                        out_shape=jax.ShapeDtypeStruct((n_pad, d_model), F32),
                        out_spec=pl.BlockSpec((tm, t_o), lambda i, j: (i, j)),
                        aliases={2: 0}, name="ffn_down")

        out = _rmsnorm(h_res, norm_final, x.dtype, tm_ew)
        outs.append(out[n_meta:n_true])
    return jnp.stack(outs, axis=0)
```

```python
import functools

import numpy as np
import jax
import jax.numpy as jnp
from jax import lax
from jax.experimental import pallas as pl
from jax.experimental.pallas import tpu as pltpu

F32 = jnp.float32
BF16 = jnp.bfloat16
EPS = 1e-6

T_BLK = 8
S5_CB_WIDTH = 256
BF16_ROWS = 16
LANES = 128
VMEM_CAP_BYTES = 60000 * 1024
VMEM_TEMP_BYTES = 12 * 1024 * 1024

TM_TARGET = 640
TM_EW_TARGET = 320
R_S5_TARGET = 416
SCAN_LANES_TARGET = 256


def _round_up(n, m):
    return (n + m - 1) // m * m


def _pick_tile(n, target, align):
    best = 0
    for t in range(align, min(n, target) + 1, align):
        if n % t == 0:
            best = t
    assert best > 0, (n, target, align)
    return best


def _nbytes(shape, dtype):
    return int(np.prod(shape)) * jnp.dtype(dtype).itemsize


def _compiler_params(semantics, block_bytes, scratch_bytes=0):
    est = 2 * block_bytes + scratch_bytes + VMEM_TEMP_BYTES
    return pltpu.CompilerParams(dimension_semantics=semantics,
                                vmem_limit_bytes=int(min(VMEM_CAP_BYTES, est)))


def _dot(a, b):
    return jnp.dot(a, b, preferred_element_type=F32)


def _rmsnorm_kernel(x_ref, g_ref, o_ref):
    x = x_ref[...]
    ms = jnp.mean(x * x, axis=-1, keepdims=True)
    o_ref[...] = (x * lax.rsqrt(ms + EPS) * g_ref[...]).astype(o_ref.dtype)


def _rmsnorm(x, g, out_dtype, tm):
    n, d = x.shape
    blocks = _nbytes((tm, d), F32) + _nbytes((tm, d), out_dtype)
    return pl.pallas_call(
        _rmsnorm_kernel,
        out_shape=jax.ShapeDtypeStruct((n, d), out_dtype),
        grid=(n // tm,),
        in_specs=[pl.BlockSpec((tm, d), lambda i: (i, 0)),
                  pl.BlockSpec((1, d), lambda i: (0, 0))],
        out_specs=pl.BlockSpec((tm, d), lambda i: (i, 0)),
        compiler_params=_compiler_params(("parallel",), blocks),
        name="rmsnorm",
    )(x, g.reshape(1, d).astype(F32))


def _mm_kernel(*refs, n_extra, epilogue, tm):
    a_ref, w_ref = refs[0], refs[1]
    extras = refs[2:2 + n_extra]
    o_ref = refs[2 + n_extra]
    acc = _dot(a_ref[...], w_ref[...])
    row0 = pl.program_id(0) * tm
    o_ref[...] = epilogue(acc, row0, *[e[...] for e in extras]).astype(o_ref.dtype)


def _mm(a, w, *, col0, n_cols, tm, tn, epilogue, extras=(), extra_specs=(), out_shape, out_spec,
        aliases=None, name):
    n, k = a.shape
    assert n % tm == 0 and n_cols % tn == 0 and col0 % tn == 0
    jb0 = col0 // tn
    blocks = (_nbytes((tm, k), a.dtype) + _nbytes((k, tn), w.dtype) + _nbytes((tm, tn), F32)
              + sum(_nbytes((tm, tn), e.dtype) for e in extras))
    return pl.pallas_call(
        functools.partial(_mm_kernel, n_extra=len(extras), epilogue=epilogue, tm=tm),
        out_shape=out_shape,
        grid=(n // tm, n_cols // tn),
        in_specs=[pl.BlockSpec((tm, k), lambda i, j: (i, 0)),
                  pl.BlockSpec((k, tn), lambda i, j: (0, jb0 + j)),
                  *extra_specs],
        out_specs=out_spec,
        input_output_aliases=aliases or {},
        compiler_params=_compiler_params(("parallel", "arbitrary"), blocks),
        name=name,
    )(a, w, *extras)


def _epi_pieces(acc, row0):
    return jnp.stack([acc[:, h * LANES:(h + 1) * LANES] for h in range(acc.shape[1] // LANES)], axis=0)


def _epi_sigmoid(acc, row0):
    return jax.nn.sigmoid(acc)


def _epi_gate(acc, row0, gate):
    return acc * gate.astype(F32)


def _epi_residual(acc, row0, res, *, n_valid):
    rows = row0 + lax.broadcasted_iota(jnp.int32, acc.shape, 0)
    return jnp.where(rows < n_valid, res + acc, 0.0)


def _inproj_conv_kernel(a_ref, wb_ref, wc_ref, wv_ref, bg_ref, cv_ref):
    a = a_ref[...]
    bg_ref[...] = _dot(a, wb_ref[...]).astype(bg_ref.dtype)
    cv_ref[...] = (_dot(a, wc_ref[...]) * _dot(a, wv_ref[...])).astype(cv_ref.dtype)


def _inproj_conv(hn, w_in, d_conv, tm, tn):
    n, k = hn.shape
    nb = d_conv // tn
    blocks = _nbytes((tm, k), BF16) + 3 * _nbytes((k, tn), BF16) + 2 * _nbytes((tm, tn), BF16)
    out = jax.ShapeDtypeStruct((n, d_conv), BF16)
    return pl.pallas_call(
        _inproj_conv_kernel,
        out_shape=(out, out),
        grid=(n // tm, nb),
        in_specs=[pl.BlockSpec((tm, k), lambda i, j: (i, 0)),
                  pl.BlockSpec((k, tn), lambda i, j: (0, j)),
                  pl.BlockSpec((k, tn), lambda i, j: (0, nb + j)),
                  pl.BlockSpec((k, tn), lambda i, j: (0, 2 * nb + j))],
        out_specs=(pl.BlockSpec((tm, tn), lambda i, j: (i, j)),
                   pl.BlockSpec((tm, tn), lambda i, j: (i, j))),
        compiler_params=_compiler_params(("parallel", "arbitrary"), blocks),
        name="inproj_conv",
    )(hn, w_in, w_in, w_in)


def _shifted_rows(x, prev_row, next_row):
    t = x.shape[0]
    row = lax.broadcasted_iota(jnp.int32, x.shape, 0)
    up = jnp.where(row == 0, prev_row, pltpu.roll(x, 1, 0))
    dn = jnp.where(row == t - 1, next_row, pltpu.roll(x, t - 1, 0))
    return up, dn


def _conv_gate_kernel(bg_ref, cv_ref, cvp_ref, cvn_ref, w_ref, z_ref):
    i = pl.program_id(0)
    last = pl.num_programs(0) - 1
    x = cv_ref[...].astype(F32)
    prev = cvp_ref[...].astype(F32)[BF16_ROWS - 1:BF16_ROWS, :]
    nxt = cvn_ref[...].astype(F32)[0:1, :]
    prev = jnp.where(i > 0, prev, 0.0)
    nxt = jnp.where(i < last, nxt, 0.0)
    up, dn = _shifted_rows(x, prev, nxt)
    w = w_ref[...]
    conv = up * w[0:1, :] + x * w[1:2, :] + dn * w[2:3, :]
    z_ref[...] = (bg_ref[...].astype(F32) * conv).astype(z_ref.dtype)


def _conv_gate(bg, cv, conv_w, tm, tc):
    n, c = bg.shape
    hb = tm // BF16_ROWS
    n_hb = n // BF16_ROWS
    blocks = 3 * _nbytes((tm, tc), BF16) + 2 * _nbytes((BF16_ROWS, tc), BF16) + _nbytes((3, tc), F32)
    return pl.pallas_call(
        _conv_gate_kernel,
        out_shape=jax.ShapeDtypeStruct((n, c), BF16),
        grid=(n // tm, c // tc),
        in_specs=[pl.BlockSpec((tm, tc), lambda i, j: (i, j)),
                  pl.BlockSpec((tm, tc), lambda i, j: (i, j)),
                  pl.BlockSpec((BF16_ROWS, tc), lambda i, j: (jnp.maximum(i * hb - 1, 0), j)),
                  pl.BlockSpec((BF16_ROWS, tc), lambda i, j: (jnp.minimum((i + 1) * hb, n_hb - 1), j)),
                  pl.BlockSpec((3, tc), lambda i, j: (0, j))],
        out_specs=pl.BlockSpec((tm, tc), lambda i, j: (i, j)),
        compiler_params=_compiler_params(("parallel", "parallel"), blocks),
        name="conv_gate",
    )(bg, cv, cv, cv, conv_w)


def _cmul(ar, ai, br, bi):
    return ar * br - ai * bi, ar * bi + ai * br


def _s5_prep_kernel(lre_ref, lim_ref, lst_ref, bre_ref, bim_ref, cre_ref, cim_ref,
                    ere_ref, eim_ref, care_ref, caim_ref, klag_ref, apre_ref, apim_ref):
    lam_re, lam_im = lre_ref[...], lim_ref[...]
    dt = jnp.exp(lst_ref[...])
    mag = jnp.exp(lam_re * dt)
    a_re = mag * jnp.cos(lam_im * dt)
    a_im = mag * jnp.sin(lam_im * dt)
    nr, ni = a_re - 1.0, a_im
    den = lam_re * lam_re + lam_im * lam_im
    f_re = (nr * lam_re + ni * lam_im) / den
    f_im = (ni * lam_re - nr * lam_im) / den
    b_re, b_im = bre_ref[...], bim_ref[...]
    bb_re, bb_im = _cmul(f_re, f_im, b_re, b_im)
    c_re, c_im = cre_ref[...], cim_ref[...]

    pw = [(jnp.ones_like(a_re), jnp.zeros_like(a_im))]
    for _ in range(T_BLK):
        pw.append(_cmul(pw[-1][0], pw[-1][1], a_re, a_im))

    e_re, e_im, ca_re, ca_im = [], [], [], []
    for k in range(T_BLK):
        er, ei = _cmul(pw[k][0], pw[k][1], bb_re, bb_im)
        e_re.append(er)
        e_im.append(ei)
        cr, ci = _cmul(pw[k + 1][0], pw[k + 1][1], c_re, c_im)
        ca_re.append(cr)
        ca_im.append(-ci)
    e_re = jnp.concatenate(e_re, axis=1)
    e_im = jnp.concatenate(e_im, axis=1)
    ere_ref[...] = e_re
    eim_ref[...] = e_im
    care_ref[...] = jnp.concatenate(ca_re, axis=1)
    caim_ref[...] = jnp.concatenate(ca_im, axis=1)

    dn = (((2,), (2,)), ((0,), (0,)))
    klag_ref[...] = (lax.dot_general(c_re, e_re, dn, precision=lax.Precision.HIGHEST,
                                     preferred_element_type=F32)
                     - lax.dot_general(c_im, e_im, dn, precision=lax.Precision.HIGHEST,
                                       preferred_element_type=F32))

    q_re, q_im = pw[T_BLK]
    r_re, r_im = q_re, q_im
    for m in range(8):
        apre_ref[m] = r_re
        apim_ref[m] = r_im
        r_re, r_im = _cmul(r_re, r_im, q_re, q_im)


def _s5_prep(lam_re, lam_im, log_step, b_re, b_im, c_re, c_im):
    depth, _, g, p = lam_re.shape
    h = b_re.shape[-1]
    lead = (depth, 2)
    b_re_t = jnp.swapaxes(b_re, -1, -2).astype(F32)
    b_im_t = jnp.swapaxes(b_im, -1, -2).astype(F32)

    def spec(*tail):
        zeros = (0,) * len(tail)
        return pl.BlockSpec((None, None) + tail, lambda l, d: (l, d) + zeros)

    th = T_BLK * h
    outs = (jax.ShapeDtypeStruct(lead + (g, th, p), F32),) * 4 + (
        jax.ShapeDtypeStruct(lead + (g, h, th), F32),
        jax.ShapeDtypeStruct(lead + (8, g, 1, p), F32),
        jax.ShapeDtypeStruct(lead + (8, g, 1, p), F32))
    blocks = 4 * _nbytes((g, th, 128), F32) + 8 * _nbytes((g, h, 128), F32)
    return pl.pallas_call(
        _s5_prep_kernel,
        out_shape=outs,
        grid=lead,
        in_specs=[spec(g, 1, p), spec(g, 1, p), spec(g, 1, 1), spec(g, h, p), spec(g, h, p),
                  spec(g, h, p), spec(g, h, p)],
        out_specs=(spec(g, th, p),) * 4 + (spec(g, h, th), spec(8, g, 1, p), spec(8, g, 1, p)),
        compiler_params=_compiler_params(("parallel", "parallel"), blocks, 16 * _nbytes((g, th, 128), F32)),
        name="s5_prep",
    )(lam_re.astype(F32)[:, :, :, None, :], lam_im.astype(F32)[:, :, :, None, :],
      log_step.astype(F32)[..., None, None], b_re_t, b_im_t, c_re.astype(F32), c_im.astype(F32))


def _s5_weights(prep, n_cb, gl, h, p):
    e_re, e_im, ca_re, ca_im, klag, ap_re, ap_im = prep
    depth = e_re.shape[0]
    t = T_BLK
    same_group = np.eye(gl, dtype=bool)

    def block_diag(compact, q_axis, g_axis):
        mask_shape = [1] * (compact.ndim + 1)
        mask_shape[q_axis] = gl
        mask_shape[g_axis] = gl
        return jnp.where(same_group.reshape(mask_shape), jnp.expand_dims(compact, q_axis), 0.0).astype(BF16)

    def split(x):
        return x.reshape(depth, 2, n_cb * gl, t, h, p)

    e = jnp.stack([split(e_re), split(e_im)], axis=2)
    e_sel = jnp.stack([e[:, 0, :, :, ::-1], e[:, 1]], axis=1)
    e_sel = e_sel.reshape(depth, 2, 2, n_cb, gl, t, h, p)
    e_sel = jnp.transpose(e_sel, (0, 3, 5, 6, 1, 2, 4, 7))
    w_s = block_diag(e_sel, 3, 7).reshape(depth, n_cb, t * gl * h, 4 * gl * p)

    def carry(ca):
        ca = split(ca)
        sel = jnp.stack([ca[:, 0], ca[:, 1, :, ::-1]], axis=1)
        sel = sel.reshape(depth, 2, n_cb, gl, t, h, p)
        sel = jnp.transpose(sel, (0, 2, 1, 6, 4, 3, 5))
        return block_diag(sel, 3, 6).reshape(depth, n_cb, 2 * gl * p, t * gl * h)

    w_cre, w_cim = carry(ca_re), carry(ca_im)

    kl = klag.reshape(depth, 2, n_cb * gl, h, t, h)
    sig = np.arange(t)[:, None]
    tau = np.arange(t)[None, :]
    kf = kl[:, 0][:, :, :, np.clip(tau - sig, 0, t - 1), :] * jnp.asarray(tau >= sig, F32)[None, None, None, :, :, None]
    kb = kl[:, 1][:, :, :, np.clip(sig - tau, 0, t - 1), :] * jnp.asarray(sig >= tau, F32)[None, None, None, :, :, None]
    kst = (kf + kb).reshape(depth, n_cb, gl, h, t, t, h)
    kst = jnp.transpose(kst, (0, 1, 4, 6, 5, 2, 3))
    w_i = block_diag(kst, 3, 6).reshape(depth, n_cb, t * gl * h, t * gl * h)

    def lanes(ap):
        ap = jnp.stack([ap[:, 0], ap[:, 1, ::-1]], axis=1)
        ap = ap.reshape(depth, 2, 8, n_cb, gl * p)
        return jnp.transpose(ap, (0, 3, 1, 2, 4))
    apw = jnp.stack([lanes(ap_re), lanes(ap_im)], axis=3)

    return w_s, w_cre, w_cim, w_i, apw


def _fold_time_blocks(u_ref, lhs_ref, r):
    pieces = u_ref.shape[0]
    for tau in range(T_BLK):
        for h in range(pieces):
            col = (tau * pieces + h) * LANES
            lhs_ref[:, col:col + LANES] = u_ref[h, pl.ds(tau, r, stride=T_BLK), :].astype(lhs_ref.dtype)


def _s5_states_kernel(u_ref, w_ref, s_ref, lhs_ref, *, r):
    @pl.when(pl.program_id(2) == 0)
    def _():
        _fold_time_blocks(u_ref, lhs_ref, r)

    s_ref[...] = _dot(lhs_ref[...], w_ref[...])


def _s5_states(u, w_s, r, tn):
    n_cb, kw, ns = w_s.shape
    n = u.shape[1]
    nb = n // T_BLK
    ppc = u.shape[0] // n_cb
    blocks = _nbytes((ppc, T_BLK * r, LANES), F32) + _nbytes((kw, tn), BF16) + _nbytes((r, tn), F32)
    scratch = _nbytes((r, kw), BF16)
    return pl.pallas_call(
        functools.partial(_s5_states_kernel, r=r),
        out_shape=jax.ShapeDtypeStruct((n_cb, nb, ns), F32),
        grid=(n_cb, nb // r, ns // tn),
        in_specs=[pl.BlockSpec((ppc, T_BLK * r, LANES), lambda c, i, j: (c, i, 0)),
                  pl.BlockSpec((None, kw, tn), lambda c, i, j: (c, 0, j))],
        out_specs=pl.BlockSpec((None, r, tn), lambda c, i, j: (c, i, j)),
        scratch_shapes=[pltpu.VMEM((r, kw), BF16)],
        compiler_params=_compiler_params(("parallel", "parallel", "arbitrary"), blocks, scratch),
        name="s5_states",
    )(u, w_s)


def _s5_scan_kernel(sre_ref, sim_ref, apre_ref, apim_ref, xre_ref, xim_ref, *, n_pairs):
    d = pl.program_id(1)
    lw = sre_ref.shape[-1]
    ap_re, ap_im = apre_ref[...], apim_ref[...]
    row = lax.broadcasted_iota(jnp.int32, (8, lw), 0)

    def run(reverse):
        steps = []
        for shift in (1, 2, 4):
            src = (8 - shift) if reverse else (shift - 1)
            keep = (row < 8 - shift) if reverse else (row >= shift)
            steps.append((8 - shift if reverse else shift,
                          jnp.where(keep, ap_re[src:src + 1, :], 0.0),
                          jnp.where(keep, ap_im[src:src + 1, :], 0.0)))
        edge = 7 if reverse else 0
        out_row = 0 if reverse else 7
        nb_shift = 7 if reverse else 1

        def tile(r0, c_re, c_im):
            x_re = sre_ref[pl.ds(r0, 8), :]
            x_im = sim_ref[pl.ds(r0, 8), :]
            for amount, m_re, m_im in steps:
                s_re = pltpu.roll(x_re, amount, 0)
                s_im = pltpu.roll(x_im, amount, 0)
                x_re, x_im = x_re + m_re * s_re - m_im * s_im, x_im + m_re * s_im + m_im * s_re
            inc_re = x_re + ap_re * c_re - ap_im * c_im
            inc_im = x_im + ap_re * c_im + ap_im * c_re
            e_re = jnp.where(row == edge, c_re, pltpu.roll(inc_re, nb_shift, 0))
            e_im = jnp.where(row == edge, c_im, pltpu.roll(inc_im, nb_shift, 0))
            return e_re, e_im, inc_re[out_row:out_row + 1, :], inc_im[out_row:out_row + 1, :]

        def body(k, carry):
            c_re, c_im = carry
            pair = (n_pairs - 1 - k) if reverse else k
            r0 = pl.multiple_of(pair * BF16_ROWS, BF16_ROWS)
            e_re, e_im = [None, None], [None, None]
            for half in ((1, 0) if reverse else (0, 1)):
                e_re[half], e_im[half], c_re, c_im = tile(pl.multiple_of(r0 + 8 * half, 8), c_re, c_im)
            xre_ref[pl.ds(r0, BF16_ROWS), :] = jnp.concatenate(e_re, axis=0).astype(xre_ref.dtype)
            xim_ref[pl.ds(r0, BF16_ROWS), :] = jnp.concatenate(e_im, axis=0).astype(xim_ref.dtype)
            return c_re, c_im

        zero = jnp.zeros((1, lw), F32)
        lax.fori_loop(0, n_pairs, body, (zero, zero))

    @pl.when(d == 0)
    def _():
        run(False)

    @pl.when(d == 1)
    def _():
        run(True)


def _s5_scan(s, apw, lw):
    n_cb, nb, ns = s.shape
    glp = ns // 4
    nq = glp // lw
    blocks = 2 * _nbytes((nb, lw), F32) + 2 * _nbytes((nb, lw), BF16) + 2 * _nbytes((8, lw), F32)
    out = jax.ShapeDtypeStruct((n_cb, nb, 2 * glp), BF16)

    def s_spec(ri):
        return pl.BlockSpec((None, nb, lw), lambda c, d, q: (c, 0, (2 * d + ri) * nq + q))

    def ap_spec(ri):
        return pl.BlockSpec((None, None, None, 8, lw), lambda c, d, q: (c, d, ri, 0, q))

    x_spec = pl.BlockSpec((None, nb, lw), lambda c, d, q: (c, 0, d * nq + q))
    return pl.pallas_call(
        functools.partial(_s5_scan_kernel, n_pairs=nb // BF16_ROWS),
        out_shape=(out, out),
        grid=(n_cb, 2, nq),
        in_specs=[s_spec(0), s_spec(1), ap_spec(0), ap_spec(1)],
        out_specs=(x_spec, x_spec),
        compiler_params=_compiler_params(("parallel", "parallel", "parallel"), blocks),
        name="s5_scan",
    )(s, s, apw, apw)


def _s5_out_kernel(u_ref, xre_ref, xim_ref, wi_ref, wcre_ref, wcim_ref, y_ref, lhs_ref, *, r, tn):
    j = pl.program_id(2)
    pieces = u_ref.shape[0]

    @pl.when(j == 0)
    def _():
        _fold_time_blocks(u_ref, lhs_ref, r)

    y = _dot(lhs_ref[...], wi_ref[...])
    y = y + _dot(xre_ref[...], wcre_ref[...])
    y = y + _dot(xim_ref[...], wcim_ref[...])

    per = tn // (pieces * LANES)
    for jj in range(T_BLK // per):
        @pl.when(j == jj)
        def _():
            for k in range(per):
                for h in range(pieces):
                    col = (k * pieces + h) * LANES
                    y_ref[h, pl.ds(jj * per + k, r, stride=T_BLK), :] = y[:, col:col + LANES]


def _s5_out(u, x_re, x_im, w_i, w_cre, w_cim, r, tn):
    n_cb, kw, _ = w_i.shape
    n = u.shape[1]
    nb = n // T_BLK
    ppc = u.shape[0] // n_cb
    kx = x_re.shape[-1]
    assert tn % (ppc * LANES) == 0 and kw % tn == 0
    blocks = (2 * _nbytes((ppc, T_BLK * r, LANES), F32) + 2 * _nbytes((r, kx), BF16) + _nbytes((kw, tn), BF16)
              + 2 * _nbytes((kx, tn), BF16) + _nbytes((r, tn), F32))
    scratch = _nbytes((r, kw), BF16)
    return pl.pallas_call(
        functools.partial(_s5_out_kernel, r=r, tn=tn),
        out_shape=jax.ShapeDtypeStruct(u.shape, F32),
        grid=(n_cb, nb // r, kw // tn),
        in_specs=[pl.BlockSpec((ppc, T_BLK * r, LANES), lambda c, i, j: (c, i, 0)),
                  pl.BlockSpec((None, r, kx), lambda c, i, j: (c, i, 0)),
                  pl.BlockSpec((None, r, kx), lambda c, i, j: (c, i, 0)),
                  pl.BlockSpec((None, kw, tn), lambda c, i, j: (c, 0, j)),
                  pl.BlockSpec((None, kx, tn), lambda c, i, j: (c, 0, j)),
                  pl.BlockSpec((None, kx, tn), lambda c, i, j: (c, 0, j))],
        out_specs=pl.BlockSpec((ppc, T_BLK * r, LANES), lambda c, i, j: (c, i, 0)),
        scratch_shapes=[pltpu.VMEM((r, kw), BF16)],
        compiler_params=_compiler_params(("parallel", "parallel", "arbitrary"), blocks, scratch),
        name="s5_out",
    )(u, x_re, x_im, w_i, w_cre, w_cim)


def _glu_merge_kernel(y_ref, u_ref, d_ref, w1_ref, w2_ref, pa_ref, gb_ref, m_ref, s_ref):
    @pl.when(pl.program_id(1) == 0)
    def _():
        for c in range(y_ref.shape[0]):
            s = jax.nn.gelu(y_ref[c] + d_ref[c] * u_ref[c])
            s_ref[:, c * LANES:(c + 1) * LANES] = s.astype(s_ref.dtype)

    a = s_ref[...]
    y_b = _dot(a, w1_ref[...]) * jax.nn.sigmoid(_dot(a, w2_ref[...]))
    m_ref[...] = (pa_ref[...].astype(F32) + gb_ref[...].astype(F32) * y_b).astype(m_ref.dtype)


def _glu_merge(y, u, d_skip, w_glu, pa, gates, d_model, tm, tn):
    n_pc, n, _ = y.shape
    k = n_pc * LANES
    nb = d_model // tn
    blocks = (2 * _nbytes((n_pc, tm, LANES), F32) + 2 * _nbytes((k, tn), BF16) + 3 * _nbytes((tm, tn), BF16))
    scratch = _nbytes((tm, k), BF16)
    return pl.pallas_call(
        _glu_merge_kernel,
        out_shape=jax.ShapeDtypeStruct((n, d_model), BF16),
        grid=(n // tm, nb),
        in_specs=[pl.BlockSpec((n_pc, tm, LANES), lambda i, j: (0, i, 0)),
                  pl.BlockSpec((n_pc, tm, LANES), lambda i, j: (0, i, 0)),
                  pl.BlockSpec((n_pc, 1, LANES), lambda i, j: (0, 0, 0)),
                  pl.BlockSpec((k, tn), lambda i, j: (0, j)),
                  pl.BlockSpec((k, tn), lambda i, j: (0, nb + j)),
                  pl.BlockSpec((tm, tn), lambda i, j: (i, j)),
                  pl.BlockSpec((tm, tn), lambda i, j: (i, nb + j))],
        out_specs=pl.BlockSpec((tm, tn), lambda i, j: (i, j)),
        scratch_shapes=[pltpu.VMEM((tm, k), BF16)],
        compiler_params=_compiler_params(("parallel", "arbitrary"), blocks, scratch),
        name="glu_merge",
    )(y, u, d_skip, w_glu, w_glu, pa, gates)


def _ffn_up_kernel(h_ref, hp_ref, hn_ref, wg_ref, wv_ref, cg_ref, cv_ref, act_ref, lhs_ref, *, tm):
    i = pl.program_id(0)
    j = pl.program_id(1)
    last = pl.num_programs(0) - 1
    halo = BF16_ROWS

    @pl.when(j == 0)
    def _():
        lhs_ref[0:halo, :] = jnp.where(i > 0, hp_ref[...], jnp.zeros_like(hp_ref))
        lhs_ref[halo:halo + tm, :] = h_ref[...]
        lhs_ref[halo + tm:, :] = jnp.where(i < last, hn_ref[...], jnp.zeros_like(hn_ref))

    a = lhs_ref[...]
    rows = tm + 2 * halo

    def conv(w_ref, cw_ref):
        u = _dot(a, w_ref[...])
        cw = cw_ref[...]
        up = pltpu.roll(u, 1, 0)[halo:halo + tm]
        dn = pltpu.roll(u, rows - 1, 0)[halo:halo + tm]
        return up * cw[0:1, :] + u[halo:halo + tm] * cw[1:2, :] + dn * cw[2:3, :]

    act_ref[...] = (jax.nn.gelu(conv(wg_ref, cg_ref)) * conv(wv_ref, cv_ref)).astype(act_ref.dtype)


def _ffn_up(hn, w_up, conv_w, d_ff, tm, tn):
    n, k = hn.shape
    nb = d_ff // tn
    hb = tm // BF16_ROWS
    n_hb = n // BF16_ROWS
    blocks = (_nbytes((tm + 2 * BF16_ROWS, k), BF16) + 2 * _nbytes((k, tn), BF16) + 2 * _nbytes((3, tn), F32)
              + _nbytes((tm, tn), BF16))
    scratch = _nbytes((tm + 2 * BF16_ROWS, k), BF16)
    return pl.pallas_call(
        functools.partial(_ffn_up_kernel, tm=tm),
        out_shape=jax.ShapeDtypeStruct((n, d_ff), BF16),
        grid=(n // tm, nb),
        in_specs=[pl.BlockSpec((tm, k), lambda i, j: (i, 0)),
                  pl.BlockSpec((BF16_ROWS, k), lambda i, j: (jnp.maximum(i * hb - 1, 0), 0)),
                  pl.BlockSpec((BF16_ROWS, k), lambda i, j: (jnp.minimum((i + 1) * hb, n_hb - 1), 0)),
                  pl.BlockSpec((k, tn), lambda i, j: (0, j)),
                  pl.BlockSpec((k, tn), lambda i, j: (0, nb + j)),
                  pl.BlockSpec((3, tn), lambda i, j: (0, j)),
                  pl.BlockSpec((3, tn), lambda i, j: (0, nb + j))],
        out_specs=pl.BlockSpec((tm, tn), lambda i, j: (i, j)),
        scratch_shapes=[pltpu.VMEM((tm + 2 * BF16_ROWS, k), BF16)],
        compiler_params=_compiler_params(("parallel", "arbitrary"), blocks, scratch),
        name="ffn_up",
    )(hn, hn, hn, w_up, w_up, conv_w, conv_w)


def kernel(x, meta_tokens, norm_mix, w_in, conv_a_w, w_a, ssm_lambda_re, ssm_lambda_im, ssm_log_step,
           ssm_b_re, ssm_b_im, ssm_c_re, ssm_c_im, ssm_d, w_glu, w_out, norm_ffn, w_up, conv_ffn_w,
           w_down, norm_final):
    bsz, seq, d_model = x.shape
    n_meta = meta_tokens.shape[0]
    depth = w_in.shape[0]
    d_conv = w_a.shape[1]
    d_ssm = ssm_d.shape[1]
    d_ff = w_down.shape[1]
    n_groups, p_state, h_grp = ssm_b_re.shape[2:]
    assert w_in.shape[2] == 3 * d_conv + d_ssm + 2 * d_model and n_groups * h_grp == d_ssm

    cbw = min(S5_CB_WIDTH, d_ssm)
    n_cb = d_ssm // cbw
    gl = cbw // h_grp
    assert n_cb * cbw == d_ssm and gl * h_grp == cbw

    n_true = n_meta + seq
    row_align = T_BLK * BF16_ROWS
    n_pad = _round_up(n_true, 10 * row_align) if n_true >= 40 * row_align else _round_up(n_true, row_align)
    nb = n_pad // T_BLK

    tm = _pick_tile(n_pad, TM_TARGET, BF16_ROWS)
    tm_ew = _pick_tile(n_pad, TM_EW_TARGET, BF16_ROWS)
    r_s5 = _pick_tile(nb, R_S5_TARGET, BF16_ROWS)
    tn = lambda n, target=512: _pick_tile(n, target, 128)

    prep = _s5_prep(ssm_lambda_re, ssm_lambda_im, ssm_log_step, ssm_b_re, ssm_b_im, ssm_c_re, ssm_c_im)
    d_skip = ssm_d.astype(F32).reshape(depth, d_ssm // LANES, 1, LANES)

    residual = functools.partial(_epi_residual, n_valid=n_true)
    tile_spec = lambda t: pl.BlockSpec((tm, t), lambda i, j: (i, j))
    u_off = 3 * d_conv
    gate_off = u_off + d_ssm

    outs = []
    for b in range(bsz):
        h_res = jnp.concatenate([meta_tokens.astype(F32), x[b].astype(F32),
                                 jnp.zeros((n_pad - n_true, d_model), F32)], axis=0)
        for l in range(depth):
            w_in_l, w_a_l, w_glu_l, w_out_l, w_up_l, w_down_l = (
                w[l].astype(BF16) for w in (w_in, w_a, w_glu, w_out, w_up, w_down))
            w_s, w_cre, w_cim, w_i, apw = (
                w[0] for w in _s5_weights([t[l:l + 1] for t in prep], n_cb, gl, h_grp, p_state))

            hn = _rmsnorm(h_res, norm_mix[l], BF16, tm_ew)
            t_c = tn(d_conv)
            bg, cv = _inproj_conv(hn, w_in_l, d_conv, tm, t_c)
            u = _mm(hn, w_in_l, col0=u_off, n_cols=d_ssm, tm=tm, tn=cbw, epilogue=_epi_pieces,
                    out_shape=jax.ShapeDtypeStruct((d_ssm // LANES, n_pad, LANES), F32),
                    out_spec=pl.BlockSpec((cbw // LANES, tm, LANES), lambda i, j: (j, i, 0)), name="inproj_u")
            t_g = tn(2 * d_model)
            gates = _mm(hn, w_in_l, col0=gate_off, n_cols=2 * d_model, tm=tm, tn=t_g,
                        epilogue=_epi_sigmoid, out_shape=jax.ShapeDtypeStruct((n_pad, 2 * d_model), BF16),
                        out_spec=tile_spec(t_g), name="inproj_gates")

            z = _conv_gate(bg, cv, conv_a_w[l].astype(F32), tm, t_c)
            t_d = tn(d_model)
            pa = _mm(z, w_a_l, col0=0, n_cols=d_model, tm=tm, tn=t_d, epilogue=_epi_gate,
                     extras=(gates,), extra_specs=(tile_spec(t_d),),
                     out_shape=jax.ShapeDtypeStruct((n_pad, d_model), BF16), out_spec=tile_spec(t_d),
                     name="branch_a")

            states = _s5_states(u, w_s, r_s5, tn(w_s.shape[-1]))
            x_re, x_im = _s5_scan(states, apw, tn(gl * p_state, SCAN_LANES_TARGET))
            y = _s5_out(u, x_re, x_im, w_i, w_cre, w_cim, r_s5, tn(T_BLK * cbw))

            merged = _glu_merge(y, u, d_skip[l], w_glu_l, pa, gates, d_model, tm, t_d)
            h_res = _mm(merged, w_out_l, col0=0, n_cols=d_model, tm=tm, tn=t_d, epilogue=residual,
                        extras=(h_res,), extra_specs=(tile_spec(t_d),),
                        out_shape=jax.ShapeDtypeStruct((n_pad, d_model), F32), out_spec=tile_spec(t_d),
                        aliases={2: 0}, name="out_proj")

            hn = _rmsnorm(h_res, norm_ffn[l], BF16, tm_ew)
            act = _ffn_up(hn, w_up_l, conv_ffn_w[l].astype(F32), d_ff, tm, tn(d_ff))
            t_o = tn(d_model, 256)
            h_res = _mm(act, w_down_l, col0=0, n_cols=d_model, tm=tm, tn=t_o, epilogue=residual,
                        extras=(h_res,), extra_specs=(tile_spec(t_o),),
                        out_shape=jax.ShapeDtypeStruct((n_pad, d_model), F32), out_spec=tile_spec(t_o),
                        aliases={2: 0}, name="ffn_down")

        out = _rmsnorm(h_res, norm_final, x.dtype, tm_ew)
        outs.append(out[n_meta:n_true])
    return jnp.stack(outs, axis=0)
```

```python
import functools

import numpy as np
import jax
import jax.numpy as jnp
from jax import lax
from jax.experimental import pallas as pl
from jax.experimental.pallas import tpu as pltpu

F32 = jnp.float32
BF16 = jnp.bfloat16
EPS = 1e-6

T_BLK = 8
S5_CB_WIDTH = 256
BF16_ROWS = 16
LANES = 128
VMEM_CAP_BYTES = 60000 * 1024
VMEM_TEMP_BYTES = 12 * 1024 * 1024

TM_TARGET = 640
TM_EW_TARGET = 320
R_S5_TARGET = 416
SCAN_LANES_TARGET = 256


def _round_up(n, m):
    return (n + m - 1) // m * m


def _pick_tile(n, target, align):
    best = 0
    for t in range(align, min(n, target) + 1, align):
        if n % t == 0:
            best = t
    assert best > 0, (n, target, align)
    return best


def _nbytes(shape, dtype):
    return int(np.prod(shape)) * jnp.dtype(dtype).itemsize


def _compiler_params(semantics, block_bytes, scratch_bytes=0):
    est = 2 * block_bytes + scratch_bytes + VMEM_TEMP_BYTES
    return pltpu.CompilerParams(dimension_semantics=semantics,
                                vmem_limit_bytes=int(min(VMEM_CAP_BYTES, est)))


def _dot(a, b):
    return jnp.dot(a, b, preferred_element_type=F32)


def _rmsnorm_kernel(x_ref, g_ref, o_ref):
    x = x_ref[...]
    ms = jnp.mean(x * x, axis=-1, keepdims=True)
    o_ref[...] = (x * lax.rsqrt(ms + EPS) * g_ref[...]).astype(o_ref.dtype)


def _rmsnorm(x, g, out_dtype, tm):
    n, d = x.shape
    blocks = _nbytes((tm, d), F32) + _nbytes((tm, d), out_dtype)
    return pl.pallas_call(
        _rmsnorm_kernel,
        out_shape=jax.ShapeDtypeStruct((n, d), out_dtype),
        grid=(n // tm,),
        in_specs=[pl.BlockSpec((tm, d), lambda i: (i, 0)),
                  pl.BlockSpec((1, d), lambda i: (0, 0))],
        out_specs=pl.BlockSpec((tm, d), lambda i: (i, 0)),
        compiler_params=_compiler_params(("parallel",), blocks),
        name="rmsnorm",
    )(x, g.reshape(1, d).astype(F32))


def _mm_kernel(*refs, n_extra, epilogue, tm):
    a_ref, w_ref = refs[0], refs[1]
    extras = refs[2:2 + n_extra]
    o_ref = refs[2 + n_extra]
    acc = _dot(a_ref[...], w_ref[...])
    row0 = pl.program_id(0) * tm
    o_ref[...] = epilogue(acc, row0, *[e[...] for e in extras]).astype(o_ref.dtype)


def _w_spec(k, tn, layer, jb0):
    return pl.BlockSpec((None, k, tn), lambda i, j: (layer, 0, jb0 + j))


def _mm(a, w, layer, *, col0, n_cols, tm, tn, epilogue, extras=(), extra_specs=(), out_shape, out_spec, name):
    n, k = a.shape
    assert n % tm == 0 and n_cols % tn == 0 and col0 % tn == 0
    blocks = (_nbytes((tm, k), a.dtype) + _nbytes((k, tn), w.dtype) + _nbytes((tm, tn), F32)
              + sum(_nbytes((tm, tn), e.dtype) for e in extras))
    return pl.pallas_call(
        functools.partial(_mm_kernel, n_extra=len(extras), epilogue=epilogue, tm=tm),
        out_shape=out_shape,
        grid=(n // tm, n_cols // tn),
        in_specs=[pl.BlockSpec((tm, k), lambda i, j: (i, 0)),
                  _w_spec(k, tn, layer, col0 // tn),
                  *extra_specs],
        out_specs=out_spec,
        compiler_params=_compiler_params(("parallel", "arbitrary"), blocks),
        name=name,
    )(a, w, *extras)


def _epi_pieces(acc, row0):
    return jnp.stack([acc[:, h * LANES:(h + 1) * LANES] for h in range(acc.shape[1] // LANES)], axis=0)


def _epi_sigmoid(acc, row0):
    return jax.nn.sigmoid(acc)


def _epi_gate(acc, row0, gate):
    return acc * gate.astype(F32)


def _epi_residual(acc, row0, res, *, n_valid):
    rows = row0 + lax.broadcasted_iota(jnp.int32, acc.shape, 0)
    return jnp.where(rows < n_valid, res + acc, 0.0)


def _inproj_conv_kernel(a_ref, wb_ref, wc_ref, wv_ref, bg_ref, cv_ref):
    a = a_ref[...]
    bg_ref[...] = _dot(a, wb_ref[...]).astype(bg_ref.dtype)
    cv_ref[...] = (_dot(a, wc_ref[...]) * _dot(a, wv_ref[...])).astype(cv_ref.dtype)


def _inproj_conv(hn, w_in, layer, d_conv, tm, tn):
    n, k = hn.shape
    nb = d_conv // tn
    blocks = _nbytes((tm, k), BF16) + 3 * _nbytes((k, tn), BF16) + 2 * _nbytes((tm, tn), BF16)
    out = jax.ShapeDtypeStruct((n, d_conv), BF16)
    return pl.pallas_call(
        _inproj_conv_kernel,
        out_shape=(out, out),
        grid=(n // tm, nb),
        in_specs=[pl.BlockSpec((tm, k), lambda i, j: (i, 0)),
                  _w_spec(k, tn, layer, 0),
                  _w_spec(k, tn, layer, nb),
                  _w_spec(k, tn, layer, 2 * nb)],
        out_specs=(pl.BlockSpec((tm, tn), lambda i, j: (i, j)),
                   pl.BlockSpec((tm, tn), lambda i, j: (i, j))),
        compiler_params=_compiler_params(("parallel", "arbitrary"), blocks),
        name="inproj_conv",
    )(hn, w_in, w_in, w_in)


def _shifted_rows(x, prev_row, next_row):
    t = x.shape[0]
    row = lax.broadcasted_iota(jnp.int32, x.shape, 0)
    up = jnp.where(row == 0, prev_row, pltpu.roll(x, 1, 0))
    dn = jnp.where(row == t - 1, next_row, pltpu.roll(x, t - 1, 0))
    return up, dn


def _conv_gate_kernel(bg_ref, cv_ref, cvp_ref, cvn_ref, w_ref, z_ref):
    i = pl.program_id(0)
    last = pl.num_programs(0) - 1
    x = cv_ref[...].astype(F32)
    prev = cvp_ref[...].astype(F32)[BF16_ROWS - 1:BF16_ROWS, :]
    nxt = cvn_ref[...].astype(F32)[0:1, :]
    prev = jnp.where(i > 0, prev, 0.0)
    nxt = jnp.where(i < last, nxt, 0.0)
    up, dn = _shifted_rows(x, prev, nxt)
    w = w_ref[...]
    conv = up * w[0:1, :] + x * w[1:2, :] + dn * w[2:3, :]
    z_ref[...] = (bg_ref[...].astype(F32) * conv).astype(z_ref.dtype)


def _conv_gate(bg, cv, conv_w, tm, tc):
    n, c = bg.shape
    hb = tm // BF16_ROWS
    n_hb = n // BF16_ROWS
    blocks = 3 * _nbytes((tm, tc), BF16) + 2 * _nbytes((BF16_ROWS, tc), BF16) + _nbytes((3, tc), F32)
    return pl.pallas_call(
        _conv_gate_kernel,
        out_shape=jax.ShapeDtypeStruct((n, c), BF16),
        grid=(n // tm, c // tc),
        in_specs=[pl.BlockSpec((tm, tc), lambda i, j: (i, j)),
                  pl.BlockSpec((tm, tc), lambda i, j: (i, j)),
                  pl.BlockSpec((BF16_ROWS, tc), lambda i, j: (jnp.maximum(i * hb - 1, 0), j)),
                  pl.BlockSpec((BF16_ROWS, tc), lambda i, j: (jnp.minimum((i + 1) * hb, n_hb - 1), j)),
                  pl.BlockSpec((3, tc), lambda i, j: (0, j))],
        out_specs=pl.BlockSpec((tm, tc), lambda i, j: (i, j)),
        compiler_params=_compiler_params(("parallel", "parallel"), blocks),
        name="conv_gate",
    )(bg, cv, cv, cv, conv_w)


def _cmul(ar, ai, br, bi):
    return ar * br - ai * bi, ar * bi + ai * br


def _s5_prep_kernel(lre_ref, lim_ref, lst_ref, bre_ref, bim_ref, cre_ref, cim_ref,
                    ere_ref, eim_ref, care_ref, caim_ref, klag_ref, apre_ref, apim_ref):
    lam_re, lam_im = lre_ref[...], lim_ref[...]
    dt = jnp.exp(lst_ref[...])
    mag = jnp.exp(lam_re * dt)
    a_re = mag * jnp.cos(lam_im * dt)
    a_im = mag * jnp.sin(lam_im * dt)
    nr, ni = a_re - 1.0, a_im
    den = lam_re * lam_re + lam_im * lam_im
    f_re = (nr * lam_re + ni * lam_im) / den
    f_im = (ni * lam_re - nr * lam_im) / den
    b_re, b_im = bre_ref[...], bim_ref[...]
    bb_re, bb_im = _cmul(f_re, f_im, b_re, b_im)
    c_re, c_im = cre_ref[...], cim_ref[...]

    pw = [(jnp.ones_like(a_re), jnp.zeros_like(a_im))]
    for _ in range(T_BLK):
        pw.append(_cmul(pw[-1][0], pw[-1][1], a_re, a_im))

    e_re, e_im, ca_re, ca_im = [], [], [], []
    for k in range(T_BLK):
        er, ei = _cmul(pw[k][0], pw[k][1], bb_re, bb_im)
        e_re.append(er)
        e_im.append(ei)
        cr, ci = _cmul(pw[k + 1][0], pw[k + 1][1], c_re, c_im)
        ca_re.append(cr)
        ca_im.append(-ci)
    e_re = jnp.concatenate(e_re, axis=1)
    e_im = jnp.concatenate(e_im, axis=1)
    ere_ref[...] = e_re
    eim_ref[...] = e_im
    care_ref[...] = jnp.concatenate(ca_re, axis=1)
    caim_ref[...] = jnp.concatenate(ca_im, axis=1)

    dn = (((2,), (2,)), ((0,), (0,)))
    klag_ref[...] = (lax.dot_general(c_re, e_re, dn, precision=lax.Precision.HIGHEST,
                                     preferred_element_type=F32)
                     - lax.dot_general(c_im, e_im, dn, precision=lax.Precision.HIGHEST,
                                       preferred_element_type=F32))

    q_re, q_im = pw[T_BLK]
    r_re, r_im = q_re, q_im
    for m in range(8):
        apre_ref[m] = r_re
        apim_ref[m] = r_im
        r_re, r_im = _cmul(r_re, r_im, q_re, q_im)


def _s5_prep(lam_re, lam_im, log_step, b_re, b_im, c_re, c_im):
    depth, _, g, p = lam_re.shape
    h = b_re.shape[-1]
    lead = (depth, 2)
    b_re_t = jnp.swapaxes(b_re, -1, -2).astype(F32)
    b_im_t = jnp.swapaxes(b_im, -1, -2).astype(F32)

    def spec(*tail):
        zeros = (0,) * len(tail)
        return pl.BlockSpec((None, None) + tail, lambda l, d: (l, d) + zeros)

    th = T_BLK * h
    outs = (jax.ShapeDtypeStruct(lead + (g, th, p), F32),) * 4 + (
        jax.ShapeDtypeStruct(lead + (g, h, th), F32),
        jax.ShapeDtypeStruct(lead + (8, g, 1, p), F32),
        jax.ShapeDtypeStruct(lead + (8, g, 1, p), F32))
    blocks = 4 * _nbytes((g, th, 128), F32) + 8 * _nbytes((g, h, 128), F32)
    return pl.pallas_call(
        _s5_prep_kernel,
        out_shape=outs,
        grid=lead,
        in_specs=[spec(g, 1, p), spec(g, 1, p), spec(g, 1, 1), spec(g, h, p), spec(g, h, p),
                  spec(g, h, p), spec(g, h, p)],
        out_specs=(spec(g, th, p),) * 4 + (spec(g, h, th), spec(8, g, 1, p), spec(8, g, 1, p)),
        compiler_params=_compiler_params(("parallel", "parallel"), blocks, 16 * _nbytes((g, th, 128), F32)),
        name="s5_prep",
    )(lam_re.astype(F32)[:, :, :, None, :], lam_im.astype(F32)[:, :, :, None, :],
      log_step.astype(F32)[..., None, None], b_re_t, b_im_t, c_re.astype(F32), c_im.astype(F32))


def _block_diag_kernel(c_ref, e_ref, rg_ref, cg_ref, o_ref):
    spread = _dot(c_ref[...].astype(BF16), e_ref[...])
    o_ref[...] = jnp.where(rg_ref[...] == cg_ref[...], spread, 0.0).astype(o_ref.dtype)


def _block_diag(compact, gl, row_inner, col_inner, name):
    n_cb, rows, kc = compact.shape
    n_out = kc * gl
    tr = _pick_tile(rows, 512, BF16_ROWS)
    col = np.arange(n_out)
    spread = np.arange(kc)[:, None] == (col // (gl * col_inner)) * col_inner + col % col_inner
    row_group = ((np.arange(rows) // row_inner) % gl).astype(np.int32)[:, None]
    col_group = ((col // col_inner) % gl).astype(np.int32)[None, :]
    blocks = (_nbytes((tr, kc), F32) + _nbytes((kc, n_out), BF16) + _nbytes((tr, LANES), jnp.int32)
              + _nbytes((8, n_out), jnp.int32) + _nbytes((tr, n_out), BF16))
    return pl.pallas_call(
        _block_diag_kernel,
        out_shape=jax.ShapeDtypeStruct((n_cb, rows, n_out), BF16),
        grid=(n_cb, rows // tr),
        in_specs=[pl.BlockSpec((None, tr, kc), lambda c, i: (c, i, 0)),
                  pl.BlockSpec((kc, n_out), lambda c, i: (0, 0)),
                  pl.BlockSpec((tr, 1), lambda c, i: (i, 0)),
                  pl.BlockSpec((1, n_out), lambda c, i: (0, 0))],
        out_specs=pl.BlockSpec((None, tr, n_out), lambda c, i: (c, i, 0)),
        compiler_params=_compiler_params(("parallel", "parallel"), blocks),
        name=name,
    )(compact, jnp.asarray(spread, BF16), jnp.asarray(row_group), jnp.asarray(col_group))


def _s5_weights(prep, n_cb, gl, h, p):
    e_re, e_im, ca_re, ca_im, klag, ap_re, ap_im = prep
    t = T_BLK

    def split(x):
        return x.reshape(2, n_cb * gl, t, h, p)

    e = jnp.stack([split(e_re), split(e_im)], axis=1)
    e_sel = jnp.stack([e[0, :, :, ::-1], e[1]], axis=0)
    e_sel = e_sel.reshape(2, 2, n_cb, gl, t, h, p)
    e_sel = jnp.transpose(e_sel, (2, 4, 3, 5, 0, 1, 6))
    w_s = _block_diag(e_sel.reshape(n_cb, t * gl * h, 4 * p), gl, h, p, "s5_w_states")

    def carry(ca, name):
        ca = split(ca)
        sel = jnp.stack([ca[0], ca[1, :, ::-1]], axis=0)
        sel = sel.reshape(2, n_cb, gl, t, h, p)
        sel = jnp.transpose(sel, (1, 0, 2, 5, 3, 4))
        return _block_diag(sel.reshape(n_cb, 2 * gl * p, t * h), gl, p, h, name)

    w_cre, w_cim = carry(ca_re, "s5_w_carry_re"), carry(ca_im, "s5_w_carry_im")

    kl = klag.reshape(2, n_cb * gl, h, t, h)
    sig = np.arange(t)[:, None]
    tau = np.arange(t)[None, :]
    kf = kl[0][:, :, np.clip(tau - sig, 0, t - 1), :] * jnp.asarray(tau >= sig, F32)[None, None, :, :, None]
    kb = kl[1][:, :, np.clip(sig - tau, 0, t - 1), :] * jnp.asarray(sig >= tau, F32)[None, None, :, :, None]
    kst = (kf + kb).reshape(n_cb, gl, h, t, t, h)
    kst = jnp.transpose(kst, (0, 3, 1, 5, 4, 2))
    w_i = _block_diag(kst.reshape(n_cb, t * gl * h, t * h), gl, h, h, "s5_w_intra")

    def lanes(ap):
        ap = jnp.stack([ap[0], ap[1, ::-1]], axis=0)
        ap = ap.reshape(2, 8, n_cb, gl * p)
        return jnp.transpose(ap, (2, 0, 1, 3))
    apw = jnp.stack([lanes(ap_re), lanes(ap_im)], axis=2)

    return w_s, w_cre, w_cim, w_i, apw


def _fold_time_blocks(u_ref, lhs_ref, r):
    pieces = u_ref.shape[0]
    for tau in range(T_BLK):
        for h in range(pieces):
            col = (tau * pieces + h) * LANES
            lhs_ref[:, col:col + LANES] = u_ref[h, pl.ds(tau, r, stride=T_BLK), :].astype(lhs_ref.dtype)


def _s5_states_kernel(u_ref, w_ref, s_ref, lhs_ref, *, r):
    @pl.when(pl.program_id(2) == 0)
    def _():
        _fold_time_blocks(u_ref, lhs_ref, r)

    s_ref[...] = _dot(lhs_ref[...], w_ref[...])


def _s5_states(u, w_s, r, tn):
    n_cb, kw, ns = w_s.shape
    n = u.shape[1]
    nb = n // T_BLK
    ppc = u.shape[0] // n_cb
    blocks = _nbytes((ppc, T_BLK * r, LANES), F32) + _nbytes((kw, tn), BF16) + _nbytes((r, tn), F32)
    scratch = _nbytes((r, kw), BF16)
    return pl.pallas_call(
        functools.partial(_s5_states_kernel, r=r),
        out_shape=jax.ShapeDtypeStruct((n_cb, nb, ns), F32),
        grid=(n_cb, nb // r, ns // tn),
        in_specs=[pl.BlockSpec((ppc, T_BLK * r, LANES), lambda c, i, j: (c, i, 0)),
                  pl.BlockSpec((None, kw, tn), lambda c, i, j: (c, 0, j))],
        out_specs=pl.BlockSpec((None, r, tn), lambda c, i, j: (c, i, j)),
        scratch_shapes=[pltpu.VMEM((r, kw), BF16)],
        compiler_params=_compiler_params(("parallel", "parallel", "arbitrary"), blocks, scratch),
        name="s5_states",
    )(u, w_s)


def _s5_scan_kernel(sre_ref, sim_ref, apre_ref, apim_ref, xre_ref, xim_ref, *, n_pairs):
    d = pl.program_id(1)
    lw = sre_ref.shape[-1]
    ap_re, ap_im = apre_ref[...], apim_ref[...]
    row = lax.broadcasted_iota(jnp.int32, (8, lw), 0)

    def run(reverse):
        steps = []
        for shift in (1, 2, 4):
            src = (8 - shift) if reverse else (shift - 1)
            keep = (row < 8 - shift) if reverse else (row >= shift)
            steps.append((8 - shift if reverse else shift,
                          jnp.where(keep, ap_re[src:src + 1, :], 0.0),
                          jnp.where(keep, ap_im[src:src + 1, :], 0.0)))
        edge = 7 if reverse else 0
        out_row = 0 if reverse else 7
        nb_shift = 7 if reverse else 1

        def tile(r0, c_re, c_im):
            x_re = sre_ref[pl.ds(r0, 8), :]
            x_im = sim_ref[pl.ds(r0, 8), :]
            for amount, m_re, m_im in steps:
                s_re = pltpu.roll(x_re, amount, 0)
                s_im = pltpu.roll(x_im, amount, 0)
                x_re, x_im = x_re + m_re * s_re - m_im * s_im, x_im + m_re * s_im + m_im * s_re
            inc_re = x_re + ap_re * c_re - ap_im * c_im
            inc_im = x_im + ap_re * c_im + ap_im * c_re
            e_re = jnp.where(row == edge, c_re, pltpu.roll(inc_re, nb_shift, 0))
            e_im = jnp.where(row == edge, c_im, pltpu.roll(inc_im, nb_shift, 0))
            return e_re, e_im, inc_re[out_row:out_row + 1, :], inc_im[out_row:out_row + 1, :]

        def body(k, carry):
            c_re, c_im = carry
            pair = (n_pairs - 1 - k) if reverse else k
            r0 = pl.multiple_of(pair * BF16_ROWS, BF16_ROWS)
            e_re, e_im = [None, None], [None, None]
            for half in ((1, 0) if reverse else (0, 1)):
                e_re[half], e_im[half], c_re, c_im = tile(pl.multiple_of(r0 + 8 * half, 8), c_re, c_im)
            xre_ref[pl.ds(r0, BF16_ROWS), :] = jnp.concatenate(e_re, axis=0).astype(xre_ref.dtype)
            xim_ref[pl.ds(r0, BF16_ROWS), :] = jnp.concatenate(e_im, axis=0).astype(xim_ref.dtype)
            return c_re, c_im

        zero = jnp.zeros((1, lw), F32)
        lax.fori_loop(0, n_pairs, body, (zero, zero))

    @pl.when(d == 0)
    def _():
        run(False)

    @pl.when(d == 1)
    def _():
        run(True)


def _s5_scan(s, apw, lw):
    n_cb, nb, ns = s.shape
    glp = ns // 4
    nq = glp // lw
    blocks = 2 * _nbytes((nb, lw), F32) + 2 * _nbytes((nb, lw), BF16) + 2 * _nbytes((8, lw), F32)
    out = jax.ShapeDtypeStruct((n_cb, nb, 2 * glp), BF16)

    def s_spec(ri):
        return pl.BlockSpec((None, nb, lw), lambda c, d, q: (c, 0, (2 * d + ri) * nq + q))

    def ap_spec(ri):
        return pl.BlockSpec((None, None, None, 8, lw), lambda c, d, q: (c, d, ri, 0, q))

    x_spec = pl.BlockSpec((None, nb, lw), lambda c, d, q: (c, 0, d * nq + q))
    return pl.pallas_call(
        functools.partial(_s5_scan_kernel, n_pairs=nb // BF16_ROWS),
        out_shape=(out, out),
        grid=(n_cb, 2, nq),
        in_specs=[s_spec(0), s_spec(1), ap_spec(0), ap_spec(1)],
        out_specs=(x_spec, x_spec),
        compiler_params=_compiler_params(("parallel", "parallel", "parallel"), blocks),
        name="s5_scan",
    )(s, s, apw, apw)


def _s5_out_kernel(u_ref, xre_ref, xim_ref, wi_ref, wcre_ref, wcim_ref, y_ref, lhs_ref, *, r, tn):
    j = pl.program_id(2)
    pieces = u_ref.shape[0]

    @pl.when(j == 0)
    def _():
        _fold_time_blocks(u_ref, lhs_ref, r)

    y = _dot(lhs_ref[...], wi_ref[...])
    y = y + _dot(xre_ref[...], wcre_ref[...])
    y = y + _dot(xim_ref[...], wcim_ref[...])

    per = tn // (pieces * LANES)
    for jj in range(T_BLK // per):
        @pl.when(j == jj)
        def _():
            for k in range(per):
                for h in range(pieces):
                    col = (k * pieces + h) * LANES
                    y_ref[h, pl.ds(jj * per + k, r, stride=T_BLK), :] = y[:, col:col + LANES]


def _s5_out(u, x_re, x_im, w_i, w_cre, w_cim, r, tn):
    n_cb, kw, _ = w_i.shape
    n = u.shape[1]
    nb = n // T_BLK
    ppc = u.shape[0] // n_cb
    kx = x_re.shape[-1]
    assert tn % (ppc * LANES) == 0 and kw % tn == 0
    blocks = (2 * _nbytes((ppc, T_BLK * r, LANES), F32) + 2 * _nbytes((r, kx), BF16) + _nbytes((kw, tn), BF16)
              + 2 * _nbytes((kx, tn), BF16) + _nbytes((r, tn), F32))
    scratch = _nbytes((r, kw), BF16)
    return pl.pallas_call(
        functools.partial(_s5_out_kernel, r=r, tn=tn),
        out_shape=jax.ShapeDtypeStruct(u.shape, F32),
        grid=(n_cb, nb // r, kw // tn),
        in_specs=[pl.BlockSpec((ppc, T_BLK * r, LANES), lambda c, i, j: (c, i, 0)),
                  pl.BlockSpec((None, r, kx), lambda c, i, j: (c, i, 0)),
                  pl.BlockSpec((None, r, kx), lambda c, i, j: (c, i, 0)),
                  pl.BlockSpec((None, kw, tn), lambda c, i, j: (c, 0, j)),
                  pl.BlockSpec((None, kx, tn), lambda c, i, j: (c, 0, j)),
                  pl.BlockSpec((None, kx, tn), lambda c, i, j: (c, 0, j))],
        out_specs=pl.BlockSpec((ppc, T_BLK * r, LANES), lambda c, i, j: (c, i, 0)),
        scratch_shapes=[pltpu.VMEM((r, kw), BF16)],
        compiler_params=_compiler_params(("parallel", "parallel", "arbitrary"), blocks, scratch),
        name="s5_out",
    )(u, x_re, x_im, w_i, w_cre, w_cim)


def _glu_merge_kernel(y_ref, u_ref, d_ref, w1_ref, w2_ref, pa_ref, gb_ref, m_ref, s_ref):
    @pl.when(pl.program_id(1) == 0)
    def _():
        for c in range(y_ref.shape[0]):
            s = jax.nn.gelu(y_ref[c] + d_ref[c] * u_ref[c])
            s_ref[:, c * LANES:(c + 1) * LANES] = s.astype(s_ref.dtype)

    a = s_ref[...]
    y_b = _dot(a, w1_ref[...]) * jax.nn.sigmoid(_dot(a, w2_ref[...]))
    m_ref[...] = (pa_ref[...].astype(F32) + gb_ref[...].astype(F32) * y_b).astype(m_ref.dtype)


def _glu_merge(y, u, d_skip, w_glu, layer, pa, gates, d_model, tm, tn):
    n_pc, n, _ = y.shape
    k = n_pc * LANES
    nb = d_model // tn
    blocks = (2 * _nbytes((n_pc, tm, LANES), F32) + 2 * _nbytes((k, tn), BF16) + 3 * _nbytes((tm, tn), BF16))
    scratch = _nbytes((tm, k), BF16)
    return pl.pallas_call(
        _glu_merge_kernel,
        out_shape=jax.ShapeDtypeStruct((n, d_model), BF16),
        grid=(n // tm, nb),
        in_specs=[pl.BlockSpec((n_pc, tm, LANES), lambda i, j: (0, i, 0)),
                  pl.BlockSpec((n_pc, tm, LANES), lambda i, j: (0, i, 0)),
                  pl.BlockSpec((n_pc, 1, LANES), lambda i, j: (0, 0, 0)),
                  _w_spec(k, tn, layer, 0),
                  _w_spec(k, tn, layer, nb),
                  pl.BlockSpec((tm, tn), lambda i, j: (i, j)),
                  pl.BlockSpec((tm, tn), lambda i, j: (i, nb + j))],
        out_specs=pl.BlockSpec((tm, tn), lambda i, j: (i, j)),
        scratch_shapes=[pltpu.VMEM((tm, k), BF16)],
        compiler_params=_compiler_params(("parallel", "arbitrary"), blocks, scratch),
        name="glu_merge",
    )(y, u, d_skip, w_glu, w_glu, pa, gates)


def _ffn_up_kernel(h_ref, hp_ref, hn_ref, wg_ref, wv_ref, cg_ref, cv_ref, act_ref, lhs_ref, *, tm):
    i = pl.program_id(0)
    j = pl.program_id(1)
    last = pl.num_programs(0) - 1
    halo = BF16_ROWS

    @pl.when(j == 0)
    def _():
        lhs_ref[0:halo, :] = jnp.where(i > 0, hp_ref[...], jnp.zeros_like(hp_ref))
        lhs_ref[halo:halo + tm, :] = h_ref[...]
        lhs_ref[halo + tm:, :] = jnp.where(i < last, hn_ref[...], jnp.zeros_like(hn_ref))

    a = lhs_ref[...]
    rows = tm + 2 * halo

    def conv(w_ref, cw_ref):
        u = _dot(a, w_ref[...])
        cw = cw_ref[...]
        up = pltpu.roll(u, 1, 0)[halo:halo + tm]
        dn = pltpu.roll(u, rows - 1, 0)[halo:halo + tm]
        return up * cw[0:1, :] + u[halo:halo + tm] * cw[1:2, :] + dn * cw[2:3, :]

    act_ref[...] = (jax.nn.gelu(conv(wg_ref, cg_ref)) * conv(wv_ref, cv_ref)).astype(act_ref.dtype)


def _ffn_up(hn, w_up, layer, conv_w, d_ff, tm, tn):
    n, k = hn.shape
    nb = d_ff // tn
    hb = tm // BF16_ROWS
    n_hb = n // BF16_ROWS
    blocks = (_nbytes((tm + 2 * BF16_ROWS, k), BF16) + 2 * _nbytes((k, tn), BF16) + 2 * _nbytes((3, tn), F32)
              + _nbytes((tm, tn), BF16))
    scratch = _nbytes((tm + 2 * BF16_ROWS, k), BF16)
    return pl.pallas_call(
        functools.partial(_ffn_up_kernel, tm=tm),
        out_shape=jax.ShapeDtypeStruct((n, d_ff), BF16),
        grid=(n // tm, nb),
        in_specs=[pl.BlockSpec((tm, k), lambda i, j: (i, 0)),
                  pl.BlockSpec((BF16_ROWS, k), lambda i, j: (jnp.maximum(i * hb - 1, 0), 0)),
                  pl.BlockSpec((BF16_ROWS, k), lambda i, j: (jnp.minimum((i + 1) * hb, n_hb - 1), 0)),
                  _w_spec(k, tn, layer, 0),
                  _w_spec(k, tn, layer, nb),
                  pl.BlockSpec((3, tn), lambda i, j: (0, j)),
                  pl.BlockSpec((3, tn), lambda i, j: (0, nb + j))],
        out_specs=pl.BlockSpec((tm, tn), lambda i, j: (i, j)),
        scratch_shapes=[pltpu.VMEM((tm + 2 * BF16_ROWS, k), BF16)],
        compiler_params=_compiler_params(("parallel", "arbitrary"), blocks, scratch),
        name="ffn_up",
    )(hn, hn, hn, w_up, w_up, conv_w, conv_w)


def kernel(x, meta_tokens, norm_mix, w_in, conv_a_w, w_a, ssm_lambda_re, ssm_lambda_im, ssm_log_step,
           ssm_b_re, ssm_b_im, ssm_c_re, ssm_c_im, ssm_d, w_glu, w_out, norm_ffn, w_up, conv_ffn_w,
           w_down, norm_final):
    bsz, seq, d_model = x.shape
    n_meta = meta_tokens.shape[0]
    depth = w_in.shape[0]
    d_conv = w_a.shape[1]
    d_ssm = ssm_d.shape[1]
    d_ff = w_down.shape[1]
    n_groups, p_state, h_grp = ssm_b_re.shape[2:]
    assert w_in.shape[2] == 3 * d_conv + d_ssm + 2 * d_model and n_groups * h_grp == d_ssm

    cbw = min(S5_CB_WIDTH, d_ssm)
    n_cb = d_ssm // cbw
    gl = cbw // h_grp
    assert n_cb * cbw == d_ssm and gl * h_grp == cbw

    n_true = n_meta + seq
    row_align = T_BLK * BF16_ROWS
    n_pad = _round_up(n_true, 10 * row_align) if n_true >= 40 * row_align else _round_up(n_true, row_align)
    nb = n_pad // T_BLK

    tm = _pick_tile(n_pad, TM_TARGET, BF16_ROWS)
    tm_ew = _pick_tile(n_pad, TM_EW_TARGET, BF16_ROWS)
    r_s5 = _pick_tile(nb, R_S5_TARGET, BF16_ROWS)
    tn = lambda n, target=512: _pick_tile(n, target, 128)

    prep = _s5_prep(ssm_lambda_re, ssm_lambda_im, ssm_log_step, ssm_b_re, ssm_b_im, ssm_c_re, ssm_c_im)
    d_skip = ssm_d.astype(F32).reshape(depth, d_ssm // LANES, 1, LANES)

    residual = functools.partial(_epi_residual, n_valid=n_true)
    tile_spec = lambda t: pl.BlockSpec((tm, t), lambda i, j: (i, j))
    u_off = 3 * d_conv
    gate_off = u_off + d_ssm

    w_in_b, w_a_b, w_glu_b, w_out_b, w_up_b, w_down_b = (
        w.astype(BF16) for w in (w_in, w_a, w_glu, w_out, w_up, w_down))

    outs = []
    for b in range(bsz):
        h_res = jnp.concatenate([meta_tokens.astype(F32), x[b].astype(F32),
                                 jnp.zeros((n_pad - n_true, d_model), F32)], axis=0)
        for l in range(depth):
            w_s, w_cre, w_cim, w_i, apw = _s5_weights([t[l] for t in prep], n_cb, gl, h_grp, p_state)

            hn = _rmsnorm(h_res, norm_mix[l], BF16, tm_ew)
            t_c = tn(d_conv)
            bg, cv = _inproj_conv(hn, w_in_b, l, d_conv, tm, t_c)
            u = _mm(hn, w_in_b, l, col0=u_off, n_cols=d_ssm, tm=tm, tn=cbw, epilogue=_epi_pieces,
                    out_shape=jax.ShapeDtypeStruct((d_ssm // LANES, n_pad, LANES), F32),
                    out_spec=pl.BlockSpec((cbw // LANES, tm, LANES), lambda i, j: (j, i, 0)), name="inproj_u")
            t_g = tn(2 * d_model)
            gates = _mm(hn, w_in_b, l, col0=gate_off, n_cols=2 * d_model, tm=tm, tn=t_g,
                        epilogue=_epi_sigmoid, out_shape=jax.ShapeDtypeStruct((n_pad, 2 * d_model), BF16),
                        out_spec=tile_spec(t_g), name="inproj_gates")

            z = _conv_gate(bg, cv, conv_a_w[l].astype(F32), tm, t_c)
            t_d = tn(d_model)
            pa = _mm(z, w_a_b, l, col0=0, n_cols=d_model, tm=tm, tn=t_d, epilogue=_epi_gate,
                     extras=(gates,), extra_specs=(tile_spec(t_d),),
                     out_shape=jax.ShapeDtypeStruct((n_pad, d_model), BF16), out_spec=tile_spec(t_d),
                     name="branch_a")

            states = _s5_states(u, w_s, r_s5, tn(w_s.shape[-1]))
            x_re, x_im = _s5_scan(states, apw, tn(gl * p_state, SCAN_LANES_TARGET))
            y = _s5_out(u, x_re, x_im, w_i, w_cre, w_cim, r_s5, tn(T_BLK * cbw))

            merged = _glu_merge(y, u, d_skip[l], w_glu_b, l, pa, gates, d_model, tm, t_d)
            h_res = _mm(merged, w_out_b, l, col0=0, n_cols=d_model, tm=tm, tn=t_d, epilogue=residual,
                        extras=(h_res,), extra_specs=(tile_spec(t_d),),
                        out_shape=jax.ShapeDtypeStruct((n_pad, d_model), F32), out_spec=tile_spec(t_d),
                        name="out_proj")

            hn = _rmsnorm(h_res, norm_ffn[l], BF16, tm_ew)
            act = _ffn_up(hn, w_up_b, l, conv_ffn_w[l].astype(F32), d_ff, tm, tn(d_ff))
            t_o = tn(d_model, 256)
            h_res = _mm(act, w_down_b, l, col0=0, n_cols=d_model, tm=tm, tn=t_o, epilogue=residual,
                        extras=(h_res,), extra_specs=(tile_spec(t_o),),
                        out_shape=jax.ShapeDtypeStruct((n_pad, d_model), F32), out_spec=tile_spec(t_o),
                        name="ffn_down")

        out = _rmsnorm(h_res, norm_final, x.dtype, tm_ew)
        outs.append(out[n_meta:n_true])
    return jnp.stack(outs, axis=0)
```

```python
import functools

import numpy as np
import jax
import jax.numpy as jnp
from jax import lax
from jax.experimental import pallas as pl
from jax.experimental.pallas import tpu as pltpu

F32 = jnp.float32
BF16 = jnp.bfloat16
EPS = 1e-6

T_BLK = 8
S5_CB_WIDTH = 256
BF16_ROWS = 16
LANES = 128
VMEM_CAP_BYTES = 60000 * 1024
VMEM_TEMP_BYTES = 20 * 1024 * 1024

TM_TARGET = 1280
TM_FFN_UP_TARGET = 832
TM_FFN_DOWN_TARGET = 640
TM_EW_TARGET = 320
R_S5_TARGET = 416
SCAN_LANES_TARGET = 256


def _round_up(n, m):
    return (n + m - 1) // m * m


def _pick_tile(n, target, align):
    best = 0
    for t in range(align, min(n, target) + 1, align):
        if n % t == 0:
            best = t
    assert best > 0, (n, target, align)
    return best


def _nbytes(shape, dtype):
    return int(np.prod(shape)) * jnp.dtype(dtype).itemsize


def _compiler_params(semantics, block_bytes, scratch_bytes=0):
    est = 2 * block_bytes + scratch_bytes + VMEM_TEMP_BYTES
    return pltpu.CompilerParams(dimension_semantics=semantics,
                                vmem_limit_bytes=int(min(VMEM_CAP_BYTES, est)))


def _dot(a, b):
    return jnp.dot(a, b, preferred_element_type=F32)


def _rmsnorm_kernel(x_ref, g_ref, o_ref):
    x = x_ref[...]
    ms = jnp.mean(x * x, axis=-1, keepdims=True)
    o_ref[...] = (x * lax.rsqrt(ms + EPS) * g_ref[...]).astype(o_ref.dtype)


def _rmsnorm(x, g, out_dtype, tm):
    n, d = x.shape
    blocks = _nbytes((tm, d), F32) + _nbytes((tm, d), out_dtype)
    return pl.pallas_call(
        _rmsnorm_kernel,
        out_shape=jax.ShapeDtypeStruct((n, d), out_dtype),
        grid=(n // tm,),
        in_specs=[pl.BlockSpec((tm, d), lambda i: (i, 0)),
                  pl.BlockSpec((1, d), lambda i: (0, 0))],
        out_specs=pl.BlockSpec((tm, d), lambda i: (i, 0)),
        compiler_params=_compiler_params(("parallel",), blocks),
        name="rmsnorm",
    )(x, g.reshape(1, d).astype(F32))


def _mm_kernel(*refs, n_extra, epilogue, tm):
    a_ref, w_ref = refs[0], refs[1]
    extras = refs[2:2 + n_extra]
    o_ref = refs[2 + n_extra]
    acc = _dot(a_ref[...], w_ref[...])
    row0 = pl.program_id(0) * tm
    o_ref[...] = epilogue(acc, row0, *[e[...] for e in extras]).astype(o_ref.dtype)


def _w_spec(k, tn, layer, jb0):
    return pl.BlockSpec((None, k, tn), lambda i, j: (layer, 0, jb0 + j))


def _mm(a, w, layer, *, col0, n_cols, tm, tn, epilogue, extras=(), extra_specs=(), out_shape, out_spec, name):
    n, k = a.shape
    assert n % tm == 0 and n_cols % tn == 0 and col0 % tn == 0
    blocks = (_nbytes((tm, k), a.dtype) + _nbytes((k, tn), w.dtype) + _nbytes((tm, tn), F32)
              + sum(_nbytes((tm, tn), e.dtype) for e in extras))
    return pl.pallas_call(
        functools.partial(_mm_kernel, n_extra=len(extras), epilogue=epilogue, tm=tm),
        out_shape=out_shape,
        grid=(n // tm, n_cols // tn),
        in_specs=[pl.BlockSpec((tm, k), lambda i, j: (i, 0)),
                  _w_spec(k, tn, layer, col0 // tn),
                  *extra_specs],
        out_specs=out_spec,
        compiler_params=_compiler_params(("parallel", "arbitrary"), blocks),
        name=name,
    )(a, w, *extras)


def _epi_pieces(acc, row0):
    return jnp.stack([acc[:, h * LANES:(h + 1) * LANES] for h in range(acc.shape[1] // LANES)], axis=0)


def _epi_sigmoid(acc, row0):
    return jax.nn.sigmoid(acc)


def _epi_gate(acc, row0, gate):
    return acc * gate.astype(F32)


def _epi_residual(acc, row0, res, *, n_valid):
    rows = row0 + lax.broadcasted_iota(jnp.int32, acc.shape, 0)
    return jnp.where(rows < n_valid, res + acc, 0.0)


def _inproj_conv_kernel(a_ref, wb_ref, wc_ref, wv_ref, bg_ref, cv_ref):
    a = a_ref[...]
    bg_ref[...] = _dot(a, wb_ref[...]).astype(bg_ref.dtype)
    cv_ref[...] = (_dot(a, wc_ref[...]) * _dot(a, wv_ref[...])).astype(cv_ref.dtype)


def _inproj_conv(hn, w_in, layer, d_conv, tm, tn):
    n, k = hn.shape
    nb = d_conv // tn
    blocks = _nbytes((tm, k), BF16) + 3 * _nbytes((k, tn), BF16) + 2 * _nbytes((tm, tn), BF16)
    out = jax.ShapeDtypeStruct((n, d_conv), BF16)
    return pl.pallas_call(
        _inproj_conv_kernel,
        out_shape=(out, out),
        grid=(n // tm, nb),
        in_specs=[pl.BlockSpec((tm, k), lambda i, j: (i, 0)),
                  _w_spec(k, tn, layer, 0),
                  _w_spec(k, tn, layer, nb),
                  _w_spec(k, tn, layer, 2 * nb)],
        out_specs=(pl.BlockSpec((tm, tn), lambda i, j: (i, j)),
                   pl.BlockSpec((tm, tn), lambda i, j: (i, j))),
        compiler_params=_compiler_params(("parallel", "arbitrary"), blocks),
        name="inproj_conv",
    )(hn, w_in, w_in, w_in)


def _shifted_rows(x, prev_row, next_row):
    t = x.shape[0]
    row = lax.broadcasted_iota(jnp.int32, x.shape, 0)
    up = jnp.where(row == 0, prev_row, pltpu.roll(x, 1, 0))
    dn = jnp.where(row == t - 1, next_row, pltpu.roll(x, t - 1, 0))
    return up, dn


def _conv_gate_kernel(bg_ref, cv_ref, cvp_ref, cvn_ref, w_ref, z_ref):
    i = pl.program_id(0)
    last = pl.num_programs(0) - 1
    x = cv_ref[...].astype(F32)
    prev = cvp_ref[...].astype(F32)[BF16_ROWS - 1:BF16_ROWS, :]
    nxt = cvn_ref[...].astype(F32)[0:1, :]
    prev = jnp.where(i > 0, prev, 0.0)
    nxt = jnp.where(i < last, nxt, 0.0)
    up, dn = _shifted_rows(x, prev, nxt)
    w = w_ref[...]
    conv = up * w[0:1, :] + x * w[1:2, :] + dn * w[2:3, :]
    z_ref[...] = (bg_ref[...].astype(F32) * conv).astype(z_ref.dtype)


def _conv_gate(bg, cv, conv_w, tm, tc):
    n, c = bg.shape
    hb = tm // BF16_ROWS
    n_hb = n // BF16_ROWS
    blocks = 3 * _nbytes((tm, tc), BF16) + 2 * _nbytes((BF16_ROWS, tc), BF16) + _nbytes((3, tc), F32)
    return pl.pallas_call(
        _conv_gate_kernel,
        out_shape=jax.ShapeDtypeStruct((n, c), BF16),
        grid=(n // tm, c // tc),
        in_specs=[pl.BlockSpec((tm, tc), lambda i, j: (i, j)),
                  pl.BlockSpec((tm, tc), lambda i, j: (i, j)),
                  pl.BlockSpec((BF16_ROWS, tc), lambda i, j: (jnp.maximum(i * hb - 1, 0), j)),
                  pl.BlockSpec((BF16_ROWS, tc), lambda i, j: (jnp.minimum((i + 1) * hb, n_hb - 1), j)),
                  pl.BlockSpec((3, tc), lambda i, j: (0, j))],
        out_specs=pl.BlockSpec((tm, tc), lambda i, j: (i, j)),
        compiler_params=_compiler_params(("parallel", "parallel"), blocks),
        name="conv_gate",
    )(bg, cv, cv, cv, conv_w)


def _cmul(ar, ai, br, bi):
    return ar * br - ai * bi, ar * bi + ai * br


def _s5_prep_kernel(lre_ref, lim_ref, lst_ref, bre_ref, bim_ref, cre_ref, cim_ref,
                    ere_ref, eim_ref, care_ref, caim_ref, klag_ref, apre_ref, apim_ref):
    lam_re, lam_im = lre_ref[...], lim_ref[...]
    dt = jnp.exp(lst_ref[...])
    mag = jnp.exp(lam_re * dt)
    a_re = mag * jnp.cos(lam_im * dt)
    a_im = mag * jnp.sin(lam_im * dt)
    nr, ni = a_re - 1.0, a_im
    den = lam_re * lam_re + lam_im * lam_im
    f_re = (nr * lam_re + ni * lam_im) / den
    f_im = (ni * lam_re - nr * lam_im) / den
    b_re, b_im = bre_ref[...], bim_ref[...]
    bb_re, bb_im = _cmul(f_re, f_im, b_re, b_im)
    c_re, c_im = cre_ref[...], cim_ref[...]

    pw = [(jnp.ones_like(a_re), jnp.zeros_like(a_im))]
    for _ in range(T_BLK):
        pw.append(_cmul(pw[-1][0], pw[-1][1], a_re, a_im))

    e_re, e_im, ca_re, ca_im = [], [], [], []
    for k in range(T_BLK):
        er, ei = _cmul(pw[k][0], pw[k][1], bb_re, bb_im)
        e_re.append(er)
        e_im.append(ei)
        cr, ci = _cmul(pw[k + 1][0], pw[k + 1][1], c_re, c_im)
        ca_re.append(cr)
        ca_im.append(-ci)
    e_re = jnp.concatenate(e_re, axis=1)
    e_im = jnp.concatenate(e_im, axis=1)
    ere_ref[...] = e_re
    eim_ref[...] = e_im
    care_ref[...] = jnp.concatenate(ca_re, axis=1)
    caim_ref[...] = jnp.concatenate(ca_im, axis=1)

    dn = (((2,), (2,)), ((0,), (0,)))
    klag_ref[...] = (lax.dot_general(c_re, e_re, dn, precision=lax.Precision.HIGHEST,
                                     preferred_element_type=F32)
                     - lax.dot_general(c_im, e_im, dn, precision=lax.Precision.HIGHEST,
                                       preferred_element_type=F32))

    q_re, q_im = pw[T_BLK]
    r_re, r_im = q_re, q_im
    for m in range(8):
        apre_ref[m] = r_re
        apim_ref[m] = r_im
        r_re, r_im = _cmul(r_re, r_im, q_re, q_im)


def _s5_prep(lam_re, lam_im, log_step, b_re, b_im, c_re, c_im):
    depth, _, g, p = lam_re.shape
    h = b_re.shape[-1]
    lead = (depth, 2)
    b_re_t = jnp.swapaxes(b_re, -1, -2).astype(F32)
    b_im_t = jnp.swapaxes(b_im, -1, -2).astype(F32)

    def spec(*tail):
        zeros = (0,) * len(tail)
        return pl.BlockSpec((None, None) + tail, lambda l, d: (l, d) + zeros)

    th = T_BLK * h
    outs = (jax.ShapeDtypeStruct(lead + (g, th, p), F32),) * 4 + (
        jax.ShapeDtypeStruct(lead + (g, h, th), F32),
        jax.ShapeDtypeStruct(lead + (8, g, 1, p), F32),
        jax.ShapeDtypeStruct(lead + (8, g, 1, p), F32))
    blocks = 4 * _nbytes((g, th, 128), F32) + 8 * _nbytes((g, h, 128), F32)
    return pl.pallas_call(
        _s5_prep_kernel,
        out_shape=outs,
        grid=lead,
        in_specs=[spec(g, 1, p), spec(g, 1, p), spec(g, 1, 1), spec(g, h, p), spec(g, h, p),
                  spec(g, h, p), spec(g, h, p)],
        out_specs=(spec(g, th, p),) * 4 + (spec(g, h, th), spec(8, g, 1, p), spec(8, g, 1, p)),
        compiler_params=_compiler_params(("parallel", "parallel"), blocks, 16 * _nbytes((g, th, 128), F32)),
        name="s5_prep",
    )(lam_re.astype(F32)[:, :, :, None, :], lam_im.astype(F32)[:, :, :, None, :],
      log_step.astype(F32)[..., None, None], b_re_t, b_im_t, c_re.astype(F32), c_im.astype(F32))


def _block_diag_kernel(c_ref, e_ref, rg_ref, cg_ref, o_ref):
    spread = _dot(c_ref[...].astype(BF16), e_ref[...])
    o_ref[...] = jnp.where(rg_ref[...] == cg_ref[...], spread, 0.0).astype(o_ref.dtype)


def _block_diag(compact, gl, row_inner, col_inner, name):
    n_cb, rows, kc = compact.shape
    n_out = kc * gl
    tr = _pick_tile(rows, 512, BF16_ROWS)
    col = np.arange(n_out)
    spread = np.arange(kc)[:, None] == (col // (gl * col_inner)) * col_inner + col % col_inner
    row_group = ((np.arange(rows) // row_inner) % gl).astype(np.int32)[:, None]
    col_group = ((col // col_inner) % gl).astype(np.int32)[None, :]
    blocks = (_nbytes((tr, kc), F32) + _nbytes((kc, n_out), BF16) + _nbytes((tr, LANES), jnp.int32)
              + _nbytes((8, n_out), jnp.int32) + _nbytes((tr, n_out), BF16))
    return pl.pallas_call(
        _block_diag_kernel,
        out_shape=jax.ShapeDtypeStruct((n_cb, rows, n_out), BF16),
        grid=(n_cb, rows // tr),
        in_specs=[pl.BlockSpec((None, tr, kc), lambda c, i: (c, i, 0)),
                  pl.BlockSpec((kc, n_out), lambda c, i: (0, 0)),
                  pl.BlockSpec((tr, 1), lambda c, i: (i, 0)),
                  pl.BlockSpec((1, n_out), lambda c, i: (0, 0))],
        out_specs=pl.BlockSpec((None, tr, n_out), lambda c, i: (c, i, 0)),
        compiler_params=_compiler_params(("parallel", "parallel"), blocks),
        name=name,
    )(compact, jnp.asarray(spread, BF16), jnp.asarray(row_group), jnp.asarray(col_group))


def _s5_weights(prep, n_cb, gl, h, p):
    e_re, e_im, ca_re, ca_im, klag, ap_re, ap_im = prep
    t = T_BLK

    def split(x):
        return x.reshape(2, n_cb * gl, t, h, p)

    e = jnp.stack([split(e_re), split(e_im)], axis=1)
    e_sel = jnp.stack([e[0, :, :, ::-1], e[1]], axis=0)
    e_sel = e_sel.reshape(2, 2, n_cb, gl, t, h, p)
    e_sel = jnp.transpose(e_sel, (2, 4, 3, 5, 0, 1, 6))
    w_s = _block_diag(e_sel.reshape(n_cb, t * gl * h, 4 * p), gl, h, p, "s5_w_states")

    def carry(ca, name):
        ca = split(ca)
        sel = jnp.stack([ca[0], ca[1, :, ::-1]], axis=0)
        sel = sel.reshape(2, n_cb, gl, t, h, p)
        sel = jnp.transpose(sel, (1, 0, 2, 5, 3, 4))
        return _block_diag(sel.reshape(n_cb, 2 * gl * p, t * h), gl, p, h, name)

    w_cre, w_cim = carry(ca_re, "s5_w_carry_re"), carry(ca_im, "s5_w_carry_im")

    kl = klag.reshape(2, n_cb * gl, h, t, h)
    sig = np.arange(t)[:, None]
    tau = np.arange(t)[None, :]
    kf = kl[0][:, :, np.clip(tau - sig, 0, t - 1), :] * jnp.asarray(tau >= sig, F32)[None, None, :, :, None]
    kb = kl[1][:, :, np.clip(sig - tau, 0, t - 1), :] * jnp.asarray(sig >= tau, F32)[None, None, :, :, None]
    kst = (kf + kb).reshape(n_cb, gl, h, t, t, h)
    kst = jnp.transpose(kst, (0, 3, 1, 5, 4, 2))
    w_i = _block_diag(kst.reshape(n_cb, t * gl * h, t * h), gl, h, h, "s5_w_intra")

    def lanes(ap):
        ap = jnp.stack([ap[0], ap[1, ::-1]], axis=0)
        ap = ap.reshape(2, 8, n_cb, gl * p)
        return jnp.transpose(ap, (2, 0, 1, 3))
    apw = jnp.stack([lanes(ap_re), lanes(ap_im)], axis=2)

    return w_s, w_cre, w_cim, w_i, apw


def _fold_time_blocks(u_ref, lhs_ref, r):
    pieces = u_ref.shape[0]
    for tau in range(T_BLK):
        for h in range(pieces):
            col = (tau * pieces + h) * LANES
            lhs_ref[:, col:col + LANES] = u_ref[h, pl.ds(tau, r, stride=T_BLK), :].astype(lhs_ref.dtype)


def _s5_states_kernel(u_ref, w_ref, s_ref, lhs_ref, *, r):
    @pl.when(pl.program_id(2) == 0)
    def _():
        _fold_time_blocks(u_ref, lhs_ref, r)

    s_ref[...] = _dot(lhs_ref[...], w_ref[...])


def _s5_states(u, w_s, r, tn):
    n_cb, kw, ns = w_s.shape
    n = u.shape[1]
    nb = n // T_BLK
    ppc = u.shape[0] // n_cb
    blocks = _nbytes((ppc, T_BLK * r, LANES), F32) + _nbytes((kw, tn), BF16) + _nbytes((r, tn), F32)
    scratch = _nbytes((r, kw), BF16)
    return pl.pallas_call(
        functools.partial(_s5_states_kernel, r=r),
        out_shape=jax.ShapeDtypeStruct((n_cb, nb, ns), F32),
        grid=(n_cb, nb // r, ns // tn),
        in_specs=[pl.BlockSpec((ppc, T_BLK * r, LANES), lambda c, i, j: (c, i, 0)),
                  pl.BlockSpec((None, kw, tn), lambda c, i, j: (c, 0, j))],
        out_specs=pl.BlockSpec((None, r, tn), lambda c, i, j: (c, i, j)),
        scratch_shapes=[pltpu.VMEM((r, kw), BF16)],
        compiler_params=_compiler_params(("parallel", "parallel", "arbitrary"), blocks, scratch),
        name="s5_states",
    )(u, w_s)


def _s5_scan_kernel(sre_ref, sim_ref, apre_ref, apim_ref, xre_ref, xim_ref, *, n_pairs):
    d = pl.program_id(1)
    lw = sre_ref.shape[-1]
    ap_re, ap_im = apre_ref[...], apim_ref[...]
    row = lax.broadcasted_iota(jnp.int32, (8, lw), 0)

    def run(reverse):
        steps = []
        for shift in (1, 2, 4):
            src = (8 - shift) if reverse else (shift - 1)
            keep = (row < 8 - shift) if reverse else (row >= shift)
            steps.append((8 - shift if reverse else shift,
                          jnp.where(keep, ap_re[src:src + 1, :], 0.0),
                          jnp.where(keep, ap_im[src:src + 1, :], 0.0)))
        edge = 7 if reverse else 0
        out_row = 0 if reverse else 7
        nb_shift = 7 if reverse else 1

        def tile(r0, c_re, c_im):
            x_re = sre_ref[pl.ds(r0, 8), :]
            x_im = sim_ref[pl.ds(r0, 8), :]
            for amount, m_re, m_im in steps:
                s_re = pltpu.roll(x_re, amount, 0)
                s_im = pltpu.roll(x_im, amount, 0)
                x_re, x_im = x_re + m_re * s_re - m_im * s_im, x_im + m_re * s_im + m_im * s_re
            inc_re = x_re + ap_re * c_re - ap_im * c_im
            inc_im = x_im + ap_re * c_im + ap_im * c_re
            e_re = jnp.where(row == edge, c_re, pltpu.roll(inc_re, nb_shift, 0))
            e_im = jnp.where(row == edge, c_im, pltpu.roll(inc_im, nb_shift, 0))
            return e_re, e_im, inc_re[out_row:out_row + 1, :], inc_im[out_row:out_row + 1, :]

        def body(k, carry):
            c_re, c_im = carry
            pair = (n_pairs - 1 - k) if reverse else k
            r0 = pl.multiple_of(pair * BF16_ROWS, BF16_ROWS)
            e_re, e_im = [None, None], [None, None]
            for half in ((1, 0) if reverse else (0, 1)):
                e_re[half], e_im[half], c_re, c_im = tile(pl.multiple_of(r0 + 8 * half, 8), c_re, c_im)
            xre_ref[pl.ds(r0, BF16_ROWS), :] = jnp.concatenate(e_re, axis=0).astype(xre_ref.dtype)
            xim_ref[pl.ds(r0, BF16_ROWS), :] = jnp.concatenate(e_im, axis=0).astype(xim_ref.dtype)
            return c_re, c_im

        zero = jnp.zeros((1, lw), F32)
        lax.fori_loop(0, n_pairs, body, (zero, zero))

    @pl.when(d == 0)
    def _():
        run(False)

    @pl.when(d == 1)
    def _():
        run(True)


def _s5_scan(s, apw, lw):
    n_cb, nb, ns = s.shape
    glp = ns // 4
    nq = glp // lw
    blocks = 2 * _nbytes((nb, lw), F32) + 2 * _nbytes((nb, lw), BF16) + 2 * _nbytes((8, lw), F32)
    out = jax.ShapeDtypeStruct((n_cb, nb, 2 * glp), BF16)

    def s_spec(ri):
        return pl.BlockSpec((None, nb, lw), lambda c, d, q: (c, 0, (2 * d + ri) * nq + q))

    def ap_spec(ri):
        return pl.BlockSpec((None, None, None, 8, lw), lambda c, d, q: (c, d, ri, 0, q))

    x_spec = pl.BlockSpec((None, nb, lw), lambda c, d, q: (c, 0, d * nq + q))
    return pl.pallas_call(
        functools.partial(_s5_scan_kernel, n_pairs=nb // BF16_ROWS),
        out_shape=(out, out),
        grid=(n_cb, 2, nq),
        in_specs=[s_spec(0), s_spec(1), ap_spec(0), ap_spec(1)],
        out_specs=(x_spec, x_spec),
        compiler_params=_compiler_params(("parallel", "parallel", "parallel"), blocks),
        name="s5_scan",
    )(s, s, apw, apw)


def _s5_out_kernel(u_ref, xre_ref, xim_ref, wi_ref, wcre_ref, wcim_ref, y_ref, lhs_ref, *, r, tn):
    j = pl.program_id(2)
    pieces = u_ref.shape[0]

    @pl.when(j == 0)
    def _():
        _fold_time_blocks(u_ref, lhs_ref, r)

    y = _dot(lhs_ref[...], wi_ref[...])
    y = y + _dot(xre_ref[...], wcre_ref[...])
    y = y + _dot(xim_ref[...], wcim_ref[...])

    per = tn // (pieces * LANES)
    for jj in range(T_BLK // per):
        @pl.when(j == jj)
        def _():
            for k in range(per):
                for h in range(pieces):
                    col = (k * pieces + h) * LANES
                    y_ref[h, pl.ds(jj * per + k, r, stride=T_BLK), :] = y[:, col:col + LANES]


def _s5_out(u, x_re, x_im, w_i, w_cre, w_cim, r, tn):
    n_cb, kw, _ = w_i.shape
    n = u.shape[1]
    nb = n // T_BLK
    ppc = u.shape[0] // n_cb
    kx = x_re.shape[-1]
    assert tn % (ppc * LANES) == 0 and kw % tn == 0
    blocks = (2 * _nbytes((ppc, T_BLK * r, LANES), F32) + 2 * _nbytes((r, kx), BF16) + _nbytes((kw, tn), BF16)
              + 2 * _nbytes((kx, tn), BF16) + _nbytes((r, tn), F32))
    scratch = _nbytes((r, kw), BF16)
    return pl.pallas_call(
        functools.partial(_s5_out_kernel, r=r, tn=tn),
        out_shape=jax.ShapeDtypeStruct(u.shape, F32),
        grid=(n_cb, nb // r, kw // tn),
        in_specs=[pl.BlockSpec((ppc, T_BLK * r, LANES), lambda c, i, j: (c, i, 0)),
                  pl.BlockSpec((None, r, kx), lambda c, i, j: (c, i, 0)),
                  pl.BlockSpec((None, r, kx), lambda c, i, j: (c, i, 0)),
                  pl.BlockSpec((None, kw, tn), lambda c, i, j: (c, 0, j)),
                  pl.BlockSpec((None, kx, tn), lambda c, i, j: (c, 0, j)),
                  pl.BlockSpec((None, kx, tn), lambda c, i, j: (c, 0, j))],
        out_specs=pl.BlockSpec((ppc, T_BLK * r, LANES), lambda c, i, j: (c, i, 0)),
        scratch_shapes=[pltpu.VMEM((r, kw), BF16)],
        compiler_params=_compiler_params(("parallel", "parallel", "arbitrary"), blocks, scratch),
        name="s5_out",
    )(u, x_re, x_im, w_i, w_cre, w_cim)


def _glu_merge_kernel(y_ref, u_ref, d_ref, w1_ref, w2_ref, pa_ref, gb_ref, m_ref, s_ref):
    @pl.when(pl.program_id(1) == 0)
    def _():
        for c in range(y_ref.shape[0]):
            s = jax.nn.gelu(y_ref[c] + d_ref[c] * u_ref[c])
            s_ref[:, c * LANES:(c + 1) * LANES] = s.astype(s_ref.dtype)

    a = s_ref[...]
    y_b = _dot(a, w1_ref[...]) * jax.nn.sigmoid(_dot(a, w2_ref[...]))
    m_ref[...] = (pa_ref[...].astype(F32) + gb_ref[...].astype(F32) * y_b).astype(m_ref.dtype)


def _glu_merge(y, u, d_skip, w_glu, layer, pa, gates, d_model, tm, tn):
    n_pc, n, _ = y.shape
    k = n_pc * LANES
    nb = d_model // tn
    blocks = (2 * _nbytes((n_pc, tm, LANES), F32) + 2 * _nbytes((k, tn), BF16) + 3 * _nbytes((tm, tn), BF16))
    scratch = _nbytes((tm, k), BF16)
    return pl.pallas_call(
        _glu_merge_kernel,
        out_shape=jax.ShapeDtypeStruct((n, d_model), BF16),
        grid=(n // tm, nb),
        in_specs=[pl.BlockSpec((n_pc, tm, LANES), lambda i, j: (0, i, 0)),
                  pl.BlockSpec((n_pc, tm, LANES), lambda i, j: (0, i, 0)),
                  pl.BlockSpec((n_pc, 1, LANES), lambda i, j: (0, 0, 0)),
                  _w_spec(k, tn, layer, 0),
                  _w_spec(k, tn, layer, nb),
                  pl.BlockSpec((tm, tn), lambda i, j: (i, j)),
                  pl.BlockSpec((tm, tn), lambda i, j: (i, nb + j))],
        out_specs=pl.BlockSpec((tm, tn), lambda i, j: (i, j)),
        scratch_shapes=[pltpu.VMEM((tm, k), BF16)],
        compiler_params=_compiler_params(("parallel", "arbitrary"), blocks, scratch),
        name="glu_merge",
    )(y, u, d_skip, w_glu, w_glu, pa, gates)


def _ffn_up_kernel(h_ref, hp_ref, hn_ref, wg_ref, wv_ref, cg_ref, cv_ref, act_ref, lhs_ref, *, tm):
    i = pl.program_id(0)
    j = pl.program_id(1)
    last = pl.num_programs(0) - 1
    halo = BF16_ROWS

    @pl.when(j == 0)
    def _():
        lhs_ref[0:halo, :] = jnp.where(i > 0, hp_ref[...], jnp.zeros_like(hp_ref))
        lhs_ref[halo:halo + tm, :] = h_ref[...]
        lhs_ref[halo + tm:, :] = jnp.where(i < last, hn_ref[...], jnp.zeros_like(hn_ref))

    a = lhs_ref[...]
    rows = tm + 2 * halo

    def conv(w_ref, cw_ref):
        u = _dot(a, w_ref[...])
        cw = cw_ref[...]
        up = pltpu.roll(u, 1, 0)[halo:halo + tm]
        dn = pltpu.roll(u, rows - 1, 0)[halo:halo + tm]
        return up * cw[0:1, :] + u[halo:halo + tm] * cw[1:2, :] + dn * cw[2:3, :]

    act_ref[...] = (jax.nn.gelu(conv(wg_ref, cg_ref)) * conv(wv_ref, cv_ref)).astype(act_ref.dtype)


def _ffn_up(hn, w_up, layer, conv_w, d_ff, tm, tn):
    n, k = hn.shape
    nb = d_ff // tn
    hb = tm // BF16_ROWS
    n_hb = n // BF16_ROWS
    blocks = (_nbytes((tm + 2 * BF16_ROWS, k), BF16) + 2 * _nbytes((k, tn), BF16) + 2 * _nbytes((3, tn), F32)
              + _nbytes((tm, tn), BF16))
    scratch = _nbytes((tm + 2 * BF16_ROWS, k), BF16)
    return pl.pallas_call(
        functools.partial(_ffn_up_kernel, tm=tm),
        out_shape=jax.ShapeDtypeStruct((n, d_ff), BF16),
        grid=(n // tm, nb),
        in_specs=[pl.BlockSpec((tm, k), lambda i, j: (i, 0)),
                  pl.BlockSpec((BF16_ROWS, k), lambda i, j: (jnp.maximum(i * hb - 1, 0), 0)),
                  pl.BlockSpec((BF16_ROWS, k), lambda i, j: (jnp.minimum((i + 1) * hb, n_hb - 1), 0)),
                  _w_spec(k, tn, layer, 0),
                  _w_spec(k, tn, layer, nb),
                  pl.BlockSpec((3, tn), lambda i, j: (0, j)),
                  pl.BlockSpec((3, tn), lambda i, j: (0, nb + j))],
        out_specs=pl.BlockSpec((tm, tn), lambda i, j: (i, j)),
        scratch_shapes=[pltpu.VMEM((tm + 2 * BF16_ROWS, k), BF16)],
        compiler_params=_compiler_params(("parallel", "arbitrary"), blocks, scratch),
        name="ffn_up",
    )(hn, hn, hn, w_up, w_up, conv_w, conv_w)


def kernel(x, meta_tokens, norm_mix, w_in, conv_a_w, w_a, ssm_lambda_re, ssm_lambda_im, ssm_log_step,
           ssm_b_re, ssm_b_im, ssm_c_re, ssm_c_im, ssm_d, w_glu, w_out, norm_ffn, w_up, conv_ffn_w,
           w_down, norm_final):
    bsz, seq, d_model = x.shape
    n_meta = meta_tokens.shape[0]
    depth = w_in.shape[0]
    d_conv = w_a.shape[1]
    d_ssm = ssm_d.shape[1]
    d_ff = w_down.shape[1]
    n_groups, p_state, h_grp = ssm_b_re.shape[2:]
    assert w_in.shape[2] == 3 * d_conv + d_ssm + 2 * d_model and n_groups * h_grp == d_ssm

    cbw = min(S5_CB_WIDTH, d_ssm)
    n_cb = d_ssm // cbw
    gl = cbw // h_grp
    assert n_cb * cbw == d_ssm and gl * h_grp == cbw

    n_true = n_meta + seq
    row_align = T_BLK * BF16_ROWS
    n_pad = _round_up(n_true, 10 * row_align) if n_true >= 40 * row_align else _round_up(n_true, row_align)
    nb = n_pad // T_BLK

    tm = _pick_tile(n_pad, TM_TARGET, BF16_ROWS)
    tm_up = _pick_tile(n_pad, TM_FFN_UP_TARGET, BF16_ROWS)
    tm_dn = _pick_tile(n_pad, TM_FFN_DOWN_TARGET, BF16_ROWS)
    tm_ew = _pick_tile(n_pad, TM_EW_TARGET, BF16_ROWS)
    r_s5 = _pick_tile(nb, R_S5_TARGET, BF16_ROWS)
    tn = lambda n, target=512: _pick_tile(n, target, 128)

    prep = _s5_prep(ssm_lambda_re, ssm_lambda_im, ssm_log_step, ssm_b_re, ssm_b_im, ssm_c_re, ssm_c_im)
    d_skip = ssm_d.astype(F32).reshape(depth, d_ssm // LANES, 1, LANES)

    residual = functools.partial(_epi_residual, n_valid=n_true)
    tile_spec = lambda t, rows=tm: pl.BlockSpec((rows, t), lambda i, j: (i, j))
    u_off = 3 * d_conv
    gate_off = u_off + d_ssm

    w_in_b, w_a_b, w_glu_b, w_out_b, w_up_b, w_down_b = (
        w.astype(BF16) for w in (w_in, w_a, w_glu, w_out, w_up, w_down))

    outs = []
    for b in range(bsz):
        h_res = jnp.concatenate([meta_tokens.astype(F32), x[b].astype(F32),
                                 jnp.zeros((n_pad - n_true, d_model), F32)], axis=0)
        for l in range(depth):
            w_s, w_cre, w_cim, w_i, apw = _s5_weights([t[l] for t in prep], n_cb, gl, h_grp, p_state)

            hn = _rmsnorm(h_res, norm_mix[l], BF16, tm_ew)
            t_c = tn(d_conv)
            bg, cv = _inproj_conv(hn, w_in_b, l, d_conv, tm, tn(d_conv, 256))
            u = _mm(hn, w_in_b, l, col0=u_off, n_cols=d_ssm, tm=tm, tn=cbw, epilogue=_epi_pieces,
                    out_shape=jax.ShapeDtypeStruct((d_ssm // LANES, n_pad, LANES), F32),
                    out_spec=pl.BlockSpec((cbw // LANES, tm, LANES), lambda i, j: (j, i, 0)), name="inproj_u")
            t_g = tn(2 * d_model)
            gates = _mm(hn, w_in_b, l, col0=gate_off, n_cols=2 * d_model, tm=tm, tn=t_g,
                        epilogue=_epi_sigmoid, out_shape=jax.ShapeDtypeStruct((n_pad, 2 * d_model), BF16),
                        out_spec=tile_spec(t_g), name="inproj_gates")

            z = _conv_gate(bg, cv, conv_a_w[l].astype(F32), tm, t_c)
            t_a = tn(d_model, 1024)
            pa = _mm(z, w_a_b, l, col0=0, n_cols=d_model, tm=tm, tn=t_a, epilogue=_epi_gate,
                     extras=(gates,), extra_specs=(tile_spec(t_a),),
                     out_shape=jax.ShapeDtypeStruct((n_pad, d_model), BF16), out_spec=tile_spec(t_a),
                     name="branch_a")
            t_d = tn(d_model)

            states = _s5_states(u, w_s, r_s5, tn(w_s.shape[-1]))
            x_re, x_im = _s5_scan(states, apw, tn(gl * p_state, SCAN_LANES_TARGET))
            y = _s5_out(u, x_re, x_im, w_i, w_cre, w_cim, r_s5, tn(T_BLK * cbw))

            merged = _glu_merge(y, u, d_skip[l], w_glu_b, l, pa, gates, d_model, tm, t_d)
            h_res = _mm(merged, w_out_b, l, col0=0, n_cols=d_model, tm=tm, tn=t_d, epilogue=residual,
                        extras=(h_res,), extra_specs=(tile_spec(t_d),),
                        out_shape=jax.ShapeDtypeStruct((n_pad, d_model), F32), out_spec=tile_spec(t_d),
                        name="out_proj")

            hn = _rmsnorm(h_res, norm_ffn[l], BF16, tm_ew)
            act = _ffn_up(hn, w_up_b, l, conv_ffn_w[l].astype(F32), d_ff, tm_up, tn(d_ff))
            t_o = tn(d_model, 256)
            h_res = _mm(act, w_down_b, l, col0=0, n_cols=d_model, tm=tm_dn, tn=t_o, epilogue=residual,
                        extras=(h_res,), extra_specs=(tile_spec(t_o, tm_dn),),
                        out_shape=jax.ShapeDtypeStruct((n_pad, d_model), F32), out_spec=tile_spec(t_o, tm_dn),
                        name="ffn_down")

        out = _rmsnorm(h_res, norm_final, x.dtype, tm_ew)
        outs.append(out[n_meta:n_true])
    return jnp.stack(outs, axis=0)
```

```python
import functools

import numpy as np
import jax
import jax.numpy as jnp
from jax import lax
from jax.experimental import pallas as pl
from jax.experimental.pallas import tpu as pltpu

F32 = jnp.float32
BF16 = jnp.bfloat16
EPS = 1e-6

T_BLK = 8
S5_CB_WIDTH = 256
BF16_ROWS = 16
LANES = 128
VMEM_CAP_BYTES = 60000 * 1024
VMEM_TEMP_BYTES = 20 * 1024 * 1024

TM_TARGET = 1280
TM_FFN_UP_TARGET = 832
TM_HALF_TARGET = 640
R_S5_STATES_TARGET = 1040
TM_EW_TARGET = 320
R_S5_TARGET = 416
SCAN_LANES_TARGET = 256


def _round_up(n, m):
    return (n + m - 1) // m * m


def _pick_tile(n, target, align):
    best = 0
    for t in range(align, min(n, target) + 1, align):
        if n % t == 0:
            best = t
    assert best > 0, (n, target, align)
    return best


def _nbytes(shape, dtype):
    return int(np.prod(shape)) * jnp.dtype(dtype).itemsize


def _compiler_params(semantics, block_bytes, scratch_bytes=0):
    est = 2 * block_bytes + scratch_bytes + VMEM_TEMP_BYTES
    return pltpu.CompilerParams(dimension_semantics=semantics,
                                vmem_limit_bytes=int(min(VMEM_CAP_BYTES, est)))


def _dot(a, b):
    return jnp.dot(a, b, preferred_element_type=F32)


def _rmsnorm_kernel(x_ref, g_ref, o_ref):
    x = x_ref[...]
    ms = jnp.mean(x * x, axis=-1, keepdims=True)
    o_ref[...] = (x * lax.rsqrt(ms + EPS) * g_ref[...]).astype(o_ref.dtype)


def _rmsnorm(x, g, out_dtype, tm):
    n, d = x.shape
    blocks = _nbytes((tm, d), F32) + _nbytes((tm, d), out_dtype)
    return pl.pallas_call(
        _rmsnorm_kernel,
        out_shape=jax.ShapeDtypeStruct((n, d), out_dtype),
        grid=(n // tm,),
        in_specs=[pl.BlockSpec((tm, d), lambda i: (i, 0)),
                  pl.BlockSpec((1, d), lambda i: (0, 0))],
        out_specs=pl.BlockSpec((tm, d), lambda i: (i, 0)),
        compiler_params=_compiler_params(("parallel",), blocks),
        name="rmsnorm",
    )(x, g.reshape(1, d).astype(F32))


def _mm_kernel(*refs, n_extra, epilogue, tm):
    a_ref, w_ref = refs[0], refs[1]
    extras = refs[2:2 + n_extra]
    o_ref = refs[2 + n_extra]
    acc = _dot(a_ref[...], w_ref[...])
    row0 = pl.program_id(0) * tm
    o_ref[...] = epilogue(acc, row0, *[e[...] for e in extras]).astype(o_ref.dtype)


def _w_spec(k, tn, layer, jb0):
    return pl.BlockSpec((None, k, tn), lambda i, j: (layer, 0, jb0 + j))


def _mm(a, w, layer, *, col0, n_cols, tm, tn, epilogue, extras=(), extra_specs=(), out_shape, out_spec, name):
    n, k = a.shape
    assert n % tm == 0 and n_cols % tn == 0 and col0 % tn == 0
    blocks = (_nbytes((tm, k), a.dtype) + _nbytes((k, tn), w.dtype) + _nbytes((tm, tn), F32)
              + sum(_nbytes((tm, tn), e.dtype) for e in extras))
    return pl.pallas_call(
        functools.partial(_mm_kernel, n_extra=len(extras), epilogue=epilogue, tm=tm),
        out_shape=out_shape,
        grid=(n // tm, n_cols // tn),
        in_specs=[pl.BlockSpec((tm, k), lambda i, j: (i, 0)),
                  _w_spec(k, tn, layer, col0 // tn),
                  *extra_specs],
        out_specs=out_spec,
        compiler_params=_compiler_params(("parallel", "arbitrary"), blocks),
        name=name,
    )(a, w, *extras)


def _epi_pieces(acc, row0):
    return jnp.stack([acc[:, h * LANES:(h + 1) * LANES] for h in range(acc.shape[1] // LANES)], axis=0)


def _epi_sigmoid(acc, row0):
    return jax.nn.sigmoid(acc)


def _epi_residual(acc, row0, res, *, n_valid):
    rows = row0 + lax.broadcasted_iota(jnp.int32, acc.shape, 0)
    return jnp.where(rows < n_valid, res + acc, 0.0)


def _inproj_conv_kernel(a_ref, wb_ref, wc_ref, wv_ref, bg_ref, cv_ref):
    a = a_ref[...]
    bg_ref[...] = _dot(a, wb_ref[...]).astype(bg_ref.dtype)
    cv_ref[...] = (_dot(a, wc_ref[...]) * _dot(a, wv_ref[...])).astype(cv_ref.dtype)


def _inproj_conv(hn, w_in, layer, d_conv, tm, tn):
    n, k = hn.shape
    nb = d_conv // tn
    blocks = _nbytes((tm, k), BF16) + 3 * _nbytes((k, tn), BF16) + 2 * _nbytes((tm, tn), BF16)
    out = jax.ShapeDtypeStruct((n, d_conv), BF16)
    return pl.pallas_call(
        _inproj_conv_kernel,
        out_shape=(out, out),
        grid=(n // tm, nb),
        in_specs=[pl.BlockSpec((tm, k), lambda i, j: (i, 0)),
                  _w_spec(k, tn, layer, 0),
                  _w_spec(k, tn, layer, nb),
                  _w_spec(k, tn, layer, 2 * nb)],
        out_specs=(pl.BlockSpec((tm, tn), lambda i, j: (i, j)),
                   pl.BlockSpec((tm, tn), lambda i, j: (i, j))),
        compiler_params=_compiler_params(("parallel", "arbitrary"), blocks),
        name="inproj_conv",
    )(hn, w_in, w_in, w_in)


def _shifted_rows(x, prev_row, next_row):
    t = x.shape[0]
    row = lax.broadcasted_iota(jnp.int32, x.shape, 0)
    up = jnp.where(row == 0, prev_row, pltpu.roll(x, 1, 0))
    dn = jnp.where(row == t - 1, next_row, pltpu.roll(x, t - 1, 0))
    return up, dn


def _branch_a_kernel(bg_ref, cv_ref, cvp_ref, cvn_ref, cw_ref, w_ref, gate_ref, o_ref, z_ref, *, chunk):
    i = pl.program_id(0)
    last = pl.num_programs(0) - 1

    @pl.when(pl.program_id(1) == 0)
    def _():
        for c0 in range(0, z_ref.shape[1], chunk):
            cols = slice(c0, c0 + chunk)
            x = cv_ref[:, cols].astype(F32)
            prev = cvp_ref[:, cols].astype(F32)[BF16_ROWS - 1:BF16_ROWS, :]
            nxt = cvn_ref[:, cols].astype(F32)[0:1, :]
            prev = jnp.where(i > 0, prev, 0.0)
            nxt = jnp.where(i < last, nxt, 0.0)
            up, dn = _shifted_rows(x, prev, nxt)
            w = cw_ref[:, cols]
            conv = up * w[0:1, :] + x * w[1:2, :] + dn * w[2:3, :]
            z_ref[:, cols] = (bg_ref[:, cols].astype(F32) * conv).astype(z_ref.dtype)

    o_ref[...] = (_dot(z_ref[...], w_ref[...]) * gate_ref[...].astype(F32)).astype(o_ref.dtype)


def _branch_a(bg, cv, conv_w, w_a, layer, gates, d_model, tm, tn):
    n, c = bg.shape
    hb = tm // BF16_ROWS
    n_hb = n // BF16_ROWS
    blocks = (2 * _nbytes((tm, c), BF16) + 2 * _nbytes((BF16_ROWS, c), BF16) + _nbytes((8, c), F32)
              + _nbytes((c, tn), BF16) + 2 * _nbytes((tm, tn), BF16))
    scratch = _nbytes((tm, c), BF16)
    return pl.pallas_call(
        functools.partial(_branch_a_kernel, chunk=_pick_tile(c, 512, LANES)),
        out_shape=jax.ShapeDtypeStruct((n, d_model), BF16),
        grid=(n // tm, d_model // tn),
        in_specs=[pl.BlockSpec((tm, c), lambda i, j: (i, 0)),
                  pl.BlockSpec((tm, c), lambda i, j: (i, 0)),
                  pl.BlockSpec((BF16_ROWS, c), lambda i, j: (jnp.maximum(i * hb - 1, 0), 0)),
                  pl.BlockSpec((BF16_ROWS, c), lambda i, j: (jnp.minimum((i + 1) * hb, n_hb - 1), 0)),
                  pl.BlockSpec((3, c), lambda i, j: (0, 0)),
                  _w_spec(c, tn, layer, 0),
                  pl.BlockSpec((tm, tn), lambda i, j: (i, j))],
        out_specs=pl.BlockSpec((tm, tn), lambda i, j: (i, j)),
        scratch_shapes=[pltpu.VMEM((tm, c), BF16)],
        compiler_params=_compiler_params(("parallel", "arbitrary"), blocks, scratch),
        name="branch_a",
    )(bg, cv, cv, cv, conv_w, w_a, gates)


def _cmul(ar, ai, br, bi):
    return ar * br - ai * bi, ar * bi + ai * br


def _s5_prep_kernel(lre_ref, lim_ref, lst_ref, bre_ref, bim_ref, cre_ref, cim_ref,
                    ere_ref, eim_ref, care_ref, caim_ref, klag_ref, apre_ref, apim_ref):
    lam_re, lam_im = lre_ref[...], lim_ref[...]
    dt = jnp.exp(lst_ref[...])
    mag = jnp.exp(lam_re * dt)
    a_re = mag * jnp.cos(lam_im * dt)
    a_im = mag * jnp.sin(lam_im * dt)
    nr, ni = a_re - 1.0, a_im
    den = lam_re * lam_re + lam_im * lam_im
    f_re = (nr * lam_re + ni * lam_im) / den
    f_im = (ni * lam_re - nr * lam_im) / den
    b_re, b_im = bre_ref[...], bim_ref[...]
    bb_re, bb_im = _cmul(f_re, f_im, b_re, b_im)
    c_re, c_im = cre_ref[...], cim_ref[...]

    pw = [(jnp.ones_like(a_re), jnp.zeros_like(a_im))]
    for _ in range(T_BLK):
        pw.append(_cmul(pw[-1][0], pw[-1][1], a_re, a_im))

    e_re, e_im, ca_re, ca_im = [], [], [], []
    for k in range(T_BLK):
        er, ei = _cmul(pw[k][0], pw[k][1], bb_re, bb_im)
        e_re.append(er)
        e_im.append(ei)
        cr, ci = _cmul(pw[k + 1][0], pw[k + 1][1], c_re, c_im)
        ca_re.append(cr)
        ca_im.append(-ci)
    e_re = jnp.concatenate(e_re, axis=1)
    e_im = jnp.concatenate(e_im, axis=1)
    ere_ref[...] = e_re
    eim_ref[...] = e_im
    care_ref[...] = jnp.concatenate(ca_re, axis=1)
    caim_ref[...] = jnp.concatenate(ca_im, axis=1)

    dn = (((2,), (2,)), ((0,), (0,)))
    klag_ref[...] = (lax.dot_general(c_re, e_re, dn, precision=lax.Precision.HIGHEST,
                                     preferred_element_type=F32)
                     - lax.dot_general(c_im, e_im, dn, precision=lax.Precision.HIGHEST,
                                       preferred_element_type=F32))

    q_re, q_im = pw[T_BLK]
    r_re, r_im = q_re, q_im
    for m in range(8):
        apre_ref[m] = r_re
        apim_ref[m] = r_im
        r_re, r_im = _cmul(r_re, r_im, q_re, q_im)


def _s5_prep(lam_re, lam_im, log_step, b_re, b_im, c_re, c_im):
    depth, _, g, p = lam_re.shape
    h = b_re.shape[-1]
    lead = (depth, 2)
    b_re_t = jnp.swapaxes(b_re, -1, -2).astype(F32)
    b_im_t = jnp.swapaxes(b_im, -1, -2).astype(F32)

    def spec(*tail):
        zeros = (0,) * len(tail)
        return pl.BlockSpec((None, None) + tail, lambda l, d: (l, d) + zeros)

    th = T_BLK * h
    outs = (jax.ShapeDtypeStruct(lead + (g, th, p), F32),) * 4 + (
        jax.ShapeDtypeStruct(lead + (g, h, th), F32),
        jax.ShapeDtypeStruct(lead + (8, g, 1, p), F32),
        jax.ShapeDtypeStruct(lead + (8, g, 1, p), F32))
    blocks = 4 * _nbytes((g, th, 128), F32) + 8 * _nbytes((g, h, 128), F32)
    return pl.pallas_call(
        _s5_prep_kernel,
        out_shape=outs,
        grid=lead,
        in_specs=[spec(g, 1, p), spec(g, 1, p), spec(g, 1, 1), spec(g, h, p), spec(g, h, p),
                  spec(g, h, p), spec(g, h, p)],
        out_specs=(spec(g, th, p),) * 4 + (spec(g, h, th), spec(8, g, 1, p), spec(8, g, 1, p)),
        compiler_params=_compiler_params(("parallel", "parallel"), blocks, 16 * _nbytes((g, th, 128), F32)),
        name="s5_prep",
    )(lam_re.astype(F32)[:, :, :, None, :], lam_im.astype(F32)[:, :, :, None, :],
      log_step.astype(F32)[..., None, None], b_re_t, b_im_t, c_re.astype(F32), c_im.astype(F32))


def _block_diag_kernel(c_ref, e_ref, rg_ref, cg_ref, o_ref):
    spread = _dot(c_ref[...].astype(BF16), e_ref[...])
    o_ref[...] = jnp.where(rg_ref[...] == cg_ref[...], spread, 0.0).astype(o_ref.dtype)


def _block_diag(compact, gl, row_inner, col_inner, name):
    n_cb, rows, kc = compact.shape
    n_out = kc * gl
    tr = _pick_tile(rows, 512, BF16_ROWS)
    col = np.arange(n_out)
    spread = np.arange(kc)[:, None] == (col // (gl * col_inner)) * col_inner + col % col_inner
    row_group = ((np.arange(rows) // row_inner) % gl).astype(np.int32)[:, None]
    col_group = ((col // col_inner) % gl).astype(np.int32)[None, :]
    blocks = (_nbytes((tr, kc), F32) + _nbytes((kc, n_out), BF16) + _nbytes((tr, LANES), jnp.int32)
              + _nbytes((8, n_out), jnp.int32) + _nbytes((tr, n_out), BF16))
    return pl.pallas_call(
        _block_diag_kernel,
        out_shape=jax.ShapeDtypeStruct((n_cb, rows, n_out), BF16),
        grid=(n_cb, rows // tr),
        in_specs=[pl.BlockSpec((None, tr, kc), lambda c, i: (c, i, 0)),
                  pl.BlockSpec((kc, n_out), lambda c, i: (0, 0)),
                  pl.BlockSpec((tr, 1), lambda c, i: (i, 0)),
                  pl.BlockSpec((1, n_out), lambda c, i: (0, 0))],
        out_specs=pl.BlockSpec((None, tr, n_out), lambda c, i: (c, i, 0)),
        compiler_params=_compiler_params(("parallel", "parallel"), blocks),
        name=name,
    )(compact, jnp.asarray(spread, BF16), jnp.asarray(row_group), jnp.asarray(col_group))


def _s5_weights(prep, n_cb, gl, h, p):
    e_re, e_im, ca_re, ca_im, klag, ap_re, ap_im = prep
    t = T_BLK

    def split(x):
        return x.reshape(2, n_cb * gl, t, h, p)

    e = jnp.stack([split(e_re), split(e_im)], axis=1)
    e_sel = jnp.stack([e[0, :, :, ::-1], e[1]], axis=0)
    e_sel = e_sel.reshape(2, 2, n_cb, gl, t, h, p)
    e_sel = jnp.transpose(e_sel, (2, 4, 3, 5, 0, 1, 6))
    w_s = _block_diag(e_sel.reshape(n_cb, t * gl * h, 4 * p), gl, h, p, "s5_w_states")

    def carry(ca, name):
        ca = split(ca)
        sel = jnp.stack([ca[0], ca[1, :, ::-1]], axis=0)
        sel = sel.reshape(2, n_cb, gl, t, h, p)
        sel = jnp.transpose(sel, (1, 0, 2, 5, 3, 4))
        return _block_diag(sel.reshape(n_cb, 2 * gl * p, t * h), gl, p, h, name)

    w_cre, w_cim = carry(ca_re, "s5_w_carry_re"), carry(ca_im, "s5_w_carry_im")

    kl = klag.reshape(2, n_cb * gl, h, t, h)
    sig = np.arange(t)[:, None]
    tau = np.arange(t)[None, :]
    kf = kl[0][:, :, np.clip(tau - sig, 0, t - 1), :] * jnp.asarray(tau >= sig, F32)[None, None, :, :, None]
    kb = kl[1][:, :, np.clip(sig - tau, 0, t - 1), :] * jnp.asarray(sig >= tau, F32)[None, None, :, :, None]
    kst = (kf + kb).reshape(n_cb, gl, h, t, t, h)
    kst = jnp.transpose(kst, (0, 3, 1, 5, 4, 2))
    w_i = _block_diag(kst.reshape(n_cb, t * gl * h, t * h), gl, h, h, "s5_w_intra")

    def lanes(ap):
        ap = jnp.stack([ap[0], ap[1, ::-1]], axis=0)
        ap = ap.reshape(2, 8, n_cb, gl * p)
        return jnp.transpose(ap, (2, 0, 1, 3))
    apw = jnp.stack([lanes(ap_re), lanes(ap_im)], axis=2)

    return w_s, w_cre, w_cim, w_i, apw


def _fold_time_blocks(u_ref, lhs_ref, r):
    pieces = u_ref.shape[0]
    for tau in range(T_BLK):
        for h in range(pieces):
            col = (tau * pieces + h) * LANES
            lhs_ref[:, col:col + LANES] = u_ref[h, pl.ds(tau, r, stride=T_BLK), :].astype(lhs_ref.dtype)


def _s5_states_kernel(u_ref, w_ref, s_ref, lhs_ref, *, r):
    @pl.when(pl.program_id(2) == 0)
    def _():
        _fold_time_blocks(u_ref, lhs_ref, r)

    s_ref[...] = _dot(lhs_ref[...], w_ref[...])


def _s5_states(u, w_s, r, tn):
    n_cb, kw, ns = w_s.shape
    n = u.shape[1]
    nb = n // T_BLK
    ppc = u.shape[0] // n_cb
    blocks = _nbytes((ppc, T_BLK * r, LANES), F32) + _nbytes((kw, tn), BF16) + _nbytes((r, tn), F32)
    scratch = _nbytes((r, kw), BF16)
    return pl.pallas_call(
        functools.partial(_s5_states_kernel, r=r),
        out_shape=jax.ShapeDtypeStruct((n_cb, nb, ns), F32),
        grid=(n_cb, nb // r, ns // tn),
        in_specs=[pl.BlockSpec((ppc, T_BLK * r, LANES), lambda c, i, j: (c, i, 0)),
                  pl.BlockSpec((None, kw, tn), lambda c, i, j: (c, 0, j))],
        out_specs=pl.BlockSpec((None, r, tn), lambda c, i, j: (c, i, j)),
        scratch_shapes=[pltpu.VMEM((r, kw), BF16)],
        compiler_params=_compiler_params(("parallel", "parallel", "arbitrary"), blocks, scratch),
        name="s5_states",
    )(u, w_s)


def _s5_scan_kernel(sre_ref, sim_ref, apre_ref, apim_ref, xre_ref, xim_ref, *, n_pairs):
    d = pl.program_id(1)
    lw = sre_ref.shape[-1]
    ap_re, ap_im = apre_ref[...], apim_ref[...]
    row = lax.broadcasted_iota(jnp.int32, (8, lw), 0)

    def run(reverse):
        steps = []
        for shift in (1, 2, 4):
            src = (8 - shift) if reverse else (shift - 1)
            keep = (row < 8 - shift) if reverse else (row >= shift)
            steps.append((8 - shift if reverse else shift,
                          jnp.where(keep, ap_re[src:src + 1, :], 0.0),
                          jnp.where(keep, ap_im[src:src + 1, :], 0.0)))
        edge = 7 if reverse else 0
        out_row = 0 if reverse else 7
        nb_shift = 7 if reverse else 1

        def tile(r0, c_re, c_im):
            x_re = sre_ref[pl.ds(r0, 8), :]
            x_im = sim_ref[pl.ds(r0, 8), :]
            for amount, m_re, m_im in steps:
                s_re = pltpu.roll(x_re, amount, 0)
                s_im = pltpu.roll(x_im, amount, 0)
                x_re, x_im = x_re + m_re * s_re - m_im * s_im, x_im + m_re * s_im + m_im * s_re
            inc_re = x_re + ap_re * c_re - ap_im * c_im
            inc_im = x_im + ap_re * c_im + ap_im * c_re
            e_re = jnp.where(row == edge, c_re, pltpu.roll(inc_re, nb_shift, 0))
            e_im = jnp.where(row == edge, c_im, pltpu.roll(inc_im, nb_shift, 0))
            return e_re, e_im, inc_re[out_row:out_row + 1, :], inc_im[out_row:out_row + 1, :]

        def body(k, carry):
            c_re, c_im = carry
            pair = (n_pairs - 1 - k) if reverse else k
            r0 = pl.multiple_of(pair * BF16_ROWS, BF16_ROWS)
            e_re, e_im = [None, None], [None, None]
            for half in ((1, 0) if reverse else (0, 1)):
                e_re[half], e_im[half], c_re, c_im = tile(pl.multiple_of(r0 + 8 * half, 8), c_re, c_im)
            xre_ref[pl.ds(r0, BF16_ROWS), :] = jnp.concatenate(e_re, axis=0).astype(xre_ref.dtype)
            xim_ref[pl.ds(r0, BF16_ROWS), :] = jnp.concatenate(e_im, axis=0).astype(xim_ref.dtype)
            return c_re, c_im

        zero = jnp.zeros((1, lw), F32)
        lax.fori_loop(0, n_pairs, body, (zero, zero))

    @pl.when(d == 0)
    def _():
        run(False)

    @pl.when(d == 1)
    def _():
        run(True)


def _s5_scan(s, apw, lw):
    n_cb, nb, ns = s.shape
    glp = ns // 4
    nq = glp // lw
    blocks = 2 * _nbytes((nb, lw), F32) + 2 * _nbytes((nb, lw), BF16) + 2 * _nbytes((8, lw), F32)
    out = jax.ShapeDtypeStruct((n_cb, nb, 2 * glp), BF16)

    def s_spec(ri):
        return pl.BlockSpec((None, nb, lw), lambda c, d, q: (c, 0, (2 * d + ri) * nq + q))

    def ap_spec(ri):
        return pl.BlockSpec((None, None, None, 8, lw), lambda c, d, q: (c, d, ri, 0, q))

    x_spec = pl.BlockSpec((None, nb, lw), lambda c, d, q: (c, 0, d * nq + q))
    return pl.pallas_call(
        functools.partial(_s5_scan_kernel, n_pairs=nb // BF16_ROWS),
        out_shape=(out, out),
        grid=(n_cb, 2, nq),
        in_specs=[s_spec(0), s_spec(1), ap_spec(0), ap_spec(1)],
        out_specs=(x_spec, x_spec),
        compiler_params=_compiler_params(("parallel", "parallel", "parallel"), blocks),
        name="s5_scan",
    )(s, s, apw, apw)


def _s5_out_kernel(u_ref, xre_ref, xim_ref, wi_ref, wcre_ref, wcim_ref, y_ref, lhs_ref, *, r, tn):
    j = pl.program_id(2)
    pieces = u_ref.shape[0]

    @pl.when(j == 0)
    def _():
        _fold_time_blocks(u_ref, lhs_ref, r)

    y = _dot(lhs_ref[...], wi_ref[...])
    y = y + _dot(xre_ref[...], wcre_ref[...])
    y = y + _dot(xim_ref[...], wcim_ref[...])

    per = tn // (pieces * LANES)
    for jj in range(T_BLK // per):
        @pl.when(j == jj)
        def _():
            for k in range(per):
                for h in range(pieces):
                    col = (k * pieces + h) * LANES
                    y_ref[h, pl.ds(jj * per + k, r, stride=T_BLK), :] = y[:, col:col + LANES]


def _s5_out(u, x_re, x_im, w_i, w_cre, w_cim, r, tn):
    n_cb, kw, _ = w_i.shape
    n = u.shape[1]
    nb = n // T_BLK
    ppc = u.shape[0] // n_cb
    kx = x_re.shape[-1]
    assert tn % (ppc * LANES) == 0 and kw % tn == 0
    blocks = (2 * _nbytes((ppc, T_BLK * r, LANES), F32) + 2 * _nbytes((r, kx), BF16) + _nbytes((kw, tn), BF16)
              + 2 * _nbytes((kx, tn), BF16) + _nbytes((r, tn), F32))
    scratch = _nbytes((r, kw), BF16)
    return pl.pallas_call(
        functools.partial(_s5_out_kernel, r=r, tn=tn),
        out_shape=jax.ShapeDtypeStruct(u.shape, F32),
        grid=(n_cb, nb // r, kw // tn),
        in_specs=[pl.BlockSpec((ppc, T_BLK * r, LANES), lambda c, i, j: (c, i, 0)),
                  pl.BlockSpec((None, r, kx), lambda c, i, j: (c, i, 0)),
                  pl.BlockSpec((None, r, kx), lambda c, i, j: (c, i, 0)),
                  pl.BlockSpec((None, kw, tn), lambda c, i, j: (c, 0, j)),
                  pl.BlockSpec((None, kx, tn), lambda c, i, j: (c, 0, j)),
                  pl.BlockSpec((None, kx, tn), lambda c, i, j: (c, 0, j))],
        out_specs=pl.BlockSpec((ppc, T_BLK * r, LANES), lambda c, i, j: (c, i, 0)),
        scratch_shapes=[pltpu.VMEM((r, kw), BF16)],
        compiler_params=_compiler_params(("parallel", "parallel", "arbitrary"), blocks, scratch),
        name="s5_out",
    )(u, x_re, x_im, w_i, w_cre, w_cim)


def _glu_merge_kernel(y_ref, u_ref, d_ref, w1_ref, w2_ref, pa_ref, gb_ref, m_ref, s_ref):
    @pl.when(pl.program_id(1) == 0)
    def _():
        for c in range(y_ref.shape[0]):
            s = jax.nn.gelu(y_ref[c] + d_ref[c] * u_ref[c])
            s_ref[:, c * LANES:(c + 1) * LANES] = s.astype(s_ref.dtype)

    a = s_ref[...]
    y_b = _dot(a, w1_ref[...]) * jax.nn.sigmoid(_dot(a, w2_ref[...]))
    m_ref[...] = (pa_ref[...].astype(F32) + gb_ref[...].astype(F32) * y_b).astype(m_ref.dtype)


def _glu_merge(y, u, d_skip, w_glu, layer, pa, gates, d_model, tm, tn):
    n_pc, n, _ = y.shape
    k = n_pc * LANES
    nb = d_model // tn
    blocks = (2 * _nbytes((n_pc, tm, LANES), F32) + 2 * _nbytes((k, tn), BF16) + 3 * _nbytes((tm, tn), BF16))
    scratch = _nbytes((tm, k), BF16)
    return pl.pallas_call(
        _glu_merge_kernel,
        out_shape=jax.ShapeDtypeStruct((n, d_model), BF16),
        grid=(n // tm, nb),
        in_specs=[pl.BlockSpec((n_pc, tm, LANES), lambda i, j: (0, i, 0)),
                  pl.BlockSpec((n_pc, tm, LANES), lambda i, j: (0, i, 0)),
                  pl.BlockSpec((n_pc, 1, LANES), lambda i, j: (0, 0, 0)),
                  _w_spec(k, tn, layer, 0),
                  _w_spec(k, tn, layer, nb),
                  pl.BlockSpec((tm, tn), lambda i, j: (i, j)),
                  pl.BlockSpec((tm, tn), lambda i, j: (i, nb + j))],
        out_specs=pl.BlockSpec((tm, tn), lambda i, j: (i, j)),
        scratch_shapes=[pltpu.VMEM((tm, k), BF16)],
        compiler_params=_compiler_params(("parallel", "arbitrary"), blocks, scratch),
        name="glu_merge",
    )(y, u, d_skip, w_glu, w_glu, pa, gates)


def _ffn_up_kernel(h_ref, hp_ref, hn_ref, wg_ref, wv_ref, cg_ref, cv_ref, act_ref, lhs_ref, *, tm):
    i = pl.program_id(0)
    j = pl.program_id(1)
    last = pl.num_programs(0) - 1
    halo = BF16_ROWS

    @pl.when(j == 0)
    def _():
        lhs_ref[0:halo, :] = jnp.where(i > 0, hp_ref[...], jnp.zeros_like(hp_ref))
        lhs_ref[halo:halo + tm, :] = h_ref[...]
        lhs_ref[halo + tm:, :] = jnp.where(i < last, hn_ref[...], jnp.zeros_like(hn_ref))

    a = lhs_ref[...]
    rows = tm + 2 * halo

    def conv(w_ref, cw_ref):
        u = _dot(a, w_ref[...])
        cw = cw_ref[...]
        up = pltpu.roll(u, 1, 0)[halo:halo + tm]
        dn = pltpu.roll(u, rows - 1, 0)[halo:halo + tm]
        return up * cw[0:1, :] + u[halo:halo + tm] * cw[1:2, :] + dn * cw[2:3, :]

    act_ref[...] = (jax.nn.gelu(conv(wg_ref, cg_ref)) * conv(wv_ref, cv_ref)).astype(act_ref.dtype)


def _ffn_up(hn, w_up, layer, conv_w, d_ff, tm, tn):
    n, k = hn.shape
    nb = d_ff // tn
    hb = tm // BF16_ROWS
    n_hb = n // BF16_ROWS
    blocks = (_nbytes((tm + 2 * BF16_ROWS, k), BF16) + 2 * _nbytes((k, tn), BF16) + 2 * _nbytes((3, tn), F32)
              + _nbytes((tm, tn), BF16))
    scratch = _nbytes((tm + 2 * BF16_ROWS, k), BF16)
    return pl.pallas_call(
        functools.partial(_ffn_up_kernel, tm=tm),
        out_shape=jax.ShapeDtypeStruct((n, d_ff), BF16),
        grid=(n // tm, nb),
        in_specs=[pl.BlockSpec((tm, k), lambda i, j: (i, 0)),
                  pl.BlockSpec((BF16_ROWS, k), lambda i, j: (jnp.maximum(i * hb - 1, 0), 0)),
                  pl.BlockSpec((BF16_ROWS, k), lambda i, j: (jnp.minimum((i + 1) * hb, n_hb - 1), 0)),
                  _w_spec(k, tn, layer, 0),
                  _w_spec(k, tn, layer, nb),
                  pl.BlockSpec((3, tn), lambda i, j: (0, j)),
                  pl.BlockSpec((3, tn), lambda i, j: (0, nb + j))],
        out_specs=pl.BlockSpec((tm, tn), lambda i, j: (i, j)),
        scratch_shapes=[pltpu.VMEM((tm + 2 * BF16_ROWS, k), BF16)],
        compiler_params=_compiler_params(("parallel", "arbitrary"), blocks, scratch),
        name="ffn_up",
    )(hn, hn, hn, w_up, w_up, conv_w, conv_w)


def kernel(x, meta_tokens, norm_mix, w_in, conv_a_w, w_a, ssm_lambda_re, ssm_lambda_im, ssm_log_step,
           ssm_b_re, ssm_b_im, ssm_c_re, ssm_c_im, ssm_d, w_glu, w_out, norm_ffn, w_up, conv_ffn_w,
           w_down, norm_final):
    bsz, seq, d_model = x.shape
    n_meta = meta_tokens.shape[0]
    depth = w_in.shape[0]
    d_conv = w_a.shape[1]
    d_ssm = ssm_d.shape[1]
    d_ff = w_down.shape[1]
    n_groups, p_state, h_grp = ssm_b_re.shape[2:]
    assert w_in.shape[2] == 3 * d_conv + d_ssm + 2 * d_model and n_groups * h_grp == d_ssm

    cbw = min(S5_CB_WIDTH, d_ssm)
    n_cb = d_ssm // cbw
    gl = cbw // h_grp
    assert n_cb * cbw == d_ssm and gl * h_grp == cbw

    n_true = n_meta + seq
    row_align = T_BLK * BF16_ROWS
    n_pad = _round_up(n_true, 10 * row_align) if n_true >= 40 * row_align else _round_up(n_true, row_align)
    nb = n_pad // T_BLK

    tm = _pick_tile(n_pad, TM_TARGET, BF16_ROWS)
    tm_up = _pick_tile(n_pad, TM_FFN_UP_TARGET, BF16_ROWS)
    tm_half = _pick_tile(n_pad, TM_HALF_TARGET, BF16_ROWS)
    tm_ew = _pick_tile(n_pad, TM_EW_TARGET, BF16_ROWS)
    r_s5 = _pick_tile(nb, R_S5_TARGET, BF16_ROWS)
    tn = lambda n, target=512: _pick_tile(n, target, 128)

    prep = _s5_prep(ssm_lambda_re, ssm_lambda_im, ssm_log_step, ssm_b_re, ssm_b_im, ssm_c_re, ssm_c_im)
    d_skip = ssm_d.astype(F32).reshape(depth, d_ssm // LANES, 1, LANES)

    residual = functools.partial(_epi_residual, n_valid=n_true)
    tile_spec = lambda t, rows=tm: pl.BlockSpec((rows, t), lambda i, j: (i, j))
    u_off = 3 * d_conv
    gate_off = u_off + d_ssm

    w_in_b, w_a_b, w_glu_b, w_out_b, w_up_b, w_down_b = (
        w.astype(BF16) for w in (w_in, w_a, w_glu, w_out, w_up, w_down))

    outs = []
    for b in range(bsz):
        h_res = jnp.concatenate([meta_tokens.astype(F32), x[b].astype(F32),
                                 jnp.zeros((n_pad - n_true, d_model), F32)], axis=0)
        for l in range(depth):
            w_s, w_cre, w_cim, w_i, apw = _s5_weights([t[l] for t in prep], n_cb, gl, h_grp, p_state)

            hn = _rmsnorm(h_res, norm_mix[l], BF16, tm_ew)
            bg, cv = _inproj_conv(hn, w_in_b, l, d_conv, tm_half, tn(d_conv))
            u = _mm(hn, w_in_b, l, col0=u_off, n_cols=d_ssm, tm=tm, tn=cbw, epilogue=_epi_pieces,
                    out_shape=jax.ShapeDtypeStruct((d_ssm // LANES, n_pad, LANES), F32),
                    out_spec=pl.BlockSpec((cbw // LANES, tm, LANES), lambda i, j: (j, i, 0)), name="inproj_u")
            t_g = tn(2 * d_model)
            gates = _mm(hn, w_in_b, l, col0=gate_off, n_cols=2 * d_model, tm=tm, tn=t_g,
                        epilogue=_epi_sigmoid, out_shape=jax.ShapeDtypeStruct((n_pad, 2 * d_model), BF16),
                        out_spec=tile_spec(t_g), name="inproj_gates")

            pa = _branch_a(bg, cv, conv_a_w[l].astype(F32), w_a_b, l, gates, d_model, tm, tn(d_model, 1024))
            t_d = tn(d_model)

            states = _s5_states(u, w_s, _pick_tile(nb, R_S5_STATES_TARGET, BF16_ROWS), tn(w_s.shape[-1]))
            x_re, x_im = _s5_scan(states, apw, tn(gl * p_state, SCAN_LANES_TARGET))
            y = _s5_out(u, x_re, x_im, w_i, w_cre, w_cim, r_s5, tn(T_BLK * cbw))

            merged = _glu_merge(y, u, d_skip[l], w_glu_b, l, pa, gates, d_model, tm_half, tn(d_model, 1024))
            h_res = _mm(merged, w_out_b, l, col0=0, n_cols=d_model, tm=tm, tn=t_d, epilogue=residual,
                        extras=(h_res,), extra_specs=(tile_spec(t_d),),
                        out_shape=jax.ShapeDtypeStruct((n_pad, d_model), F32), out_spec=tile_spec(t_d),
                        name="out_proj")

            hn = _rmsnorm(h_res, norm_ffn[l], BF16, tm_ew)
            act = _ffn_up(hn, w_up_b, l, conv_ffn_w[l].astype(F32), d_ff, tm_up, tn(d_ff))
            t_o = tn(d_model, 256)
            h_res = _mm(act, w_down_b, l, col0=0, n_cols=d_model, tm=tm_half, tn=t_o, epilogue=residual,
                        extras=(h_res,), extra_specs=(tile_spec(t_o, tm_half),),
                        out_shape=jax.ShapeDtypeStruct((n_pad, d_model), F32), out_spec=tile_spec(t_o, tm_half),
                        name="ffn_down")

        out = _rmsnorm(h_res, norm_final, x.dtype, tm_ew)
        outs.append(out[n_meta:n_true])
    return jnp.stack(outs, axis=0)
```

```python
import functools

import numpy as np
import jax
import jax.numpy as jnp
from jax import lax
from jax.experimental import pallas as pl
from jax.experimental.pallas import tpu as pltpu

F32 = jnp.float32
BF16 = jnp.bfloat16
EPS = 1e-6

T_BLK = 8
S5_CB_WIDTH = 256
BF16_ROWS = 16
LANES = 128
VMEM_CAP_BYTES = 60000 * 1024
VMEM_TEMP_BYTES = 20 * 1024 * 1024

TM_TARGET = 1280
TM_FFN_UP_TARGET = 832
TM_HALF_TARGET = 640
R_S5_STATES_TARGET = 1040
TM_EW_TARGET = 320
R_S5_TARGET = 416
SCAN_LANES_TARGET = 256


def _round_up(n, m):
    return (n + m - 1) // m * m


def _pick_tile(n, target, align):
    best = 0
    for t in range(align, min(n, target) + 1, align):
        if n % t == 0:
            best = t
    assert best > 0, (n, target, align)
    return best


def _nbytes(shape, dtype):
    return int(np.prod(shape)) * jnp.dtype(dtype).itemsize


def _compiler_params(semantics, block_bytes, scratch_bytes=0):
    est = 2 * block_bytes + scratch_bytes + VMEM_TEMP_BYTES
    return pltpu.CompilerParams(dimension_semantics=semantics,
                                vmem_limit_bytes=int(min(VMEM_CAP_BYTES, est)))


def _dot(a, b):
    return jnp.dot(a, b, preferred_element_type=F32)


def _rmsnorm_kernel(x_ref, g_ref, o_ref):
    x = x_ref[...]
    ms = jnp.mean(x * x, axis=-1, keepdims=True)
    o_ref[...] = (x * lax.rsqrt(ms + EPS) * g_ref[...]).astype(o_ref.dtype)


def _rmsnorm(x, g, out_dtype, tm):
    n, d = x.shape
    blocks = _nbytes((tm, d), F32) + _nbytes((tm, d), out_dtype)
    return pl.pallas_call(
        _rmsnorm_kernel,
        out_shape=jax.ShapeDtypeStruct((n, d), out_dtype),
        grid=(n // tm,),
        in_specs=[pl.BlockSpec((tm, d), lambda i: (i, 0)),
                  pl.BlockSpec((1, d), lambda i: (0, 0))],
        out_specs=pl.BlockSpec((tm, d), lambda i: (i, 0)),
        compiler_params=_compiler_params(("parallel",), blocks),
        name="rmsnorm",
    )(x, g.reshape(1, d).astype(F32))


def _mm_kernel(*refs, n_extra, epilogue, tm):
    a_ref, w_ref = refs[0], refs[1]
    extras = refs[2:2 + n_extra]
    o_ref = refs[2 + n_extra]
    acc = _dot(a_ref[...], w_ref[...])
    row0 = pl.program_id(0) * tm
    o_ref[...] = epilogue(acc, row0, *[e[...] for e in extras]).astype(o_ref.dtype)


def _w_spec(k, tn, layer, jb0):
    if layer is None:
        return pl.BlockSpec((k, tn), lambda i, j: (0, jb0 + j))
    return pl.BlockSpec((None, k, tn), lambda i, j: (layer, 0, jb0 + j))


def _call_with_cast(kernel_fn, cast, *, grid, in_specs, out_specs, out_shape, operands, scratch_shapes=(),
                    block_bytes, scratch_bytes=0, name):
    if cast is None:
        return pl.pallas_call(
            kernel_fn, out_shape=tuple(out_shape), grid=grid, in_specs=list(in_specs), out_specs=tuple(out_specs),
            scratch_shapes=list(scratch_shapes),
            compiler_params=_compiler_params(("parallel", "arbitrary"), block_bytes, scratch_bytes), name=name,
        )(*operands)

    stack, layer = cast
    _, k, n = stack.shape
    n_steps = grid[0] * grid[1]
    rows = min(r for r in range(BF16_ROWS, k + 1, BF16_ROWS) if k % r == 0 and k // r <= n_steps)
    last_chunk = k // rows - 1
    n_in, n_out = len(in_specs), len(out_specs)

    def chunk(i, j):
        return jnp.minimum(i * grid[1] + j, last_chunk)

    def body(*refs):
        src_ref, dst_ref = refs[n_in], refs[n_in + 1 + n_out]
        dst_ref[...] = src_ref[...].astype(dst_ref.dtype)
        kernel_fn(*refs[:n_in], *refs[n_in + 1:n_in + 1 + n_out], *refs[n_in + 2 + n_out:])

    cast_bytes = _nbytes((rows, n), F32) + _nbytes((rows, n), BF16)
    return pl.pallas_call(
        body,
        out_shape=(*out_shape, jax.ShapeDtypeStruct((k, n), BF16)),
        grid=grid,
        in_specs=[*in_specs, pl.BlockSpec((None, rows, n), lambda i, j: (layer, chunk(i, j), 0))],
        out_specs=(*out_specs, pl.BlockSpec((rows, n), lambda i, j: (chunk(i, j), 0))),
        scratch_shapes=list(scratch_shapes),
        compiler_params=_compiler_params(("arbitrary", "arbitrary"), block_bytes + cast_bytes, scratch_bytes),
        name=name,
    )(*operands, stack)


def _mm(a, w, layer, *, col0, n_cols, tm, tn, epilogue, extras=(), extra_specs=(), out_shape, out_spec, name,
        cast=None):
    n, k = a.shape
    assert n % tm == 0 and n_cols % tn == 0 and col0 % tn == 0
    blocks = (_nbytes((tm, k), a.dtype) + _nbytes((k, tn), w.dtype) + _nbytes((tm, tn), F32)
              + sum(_nbytes((tm, tn), e.dtype) for e in extras))
    outs = _call_with_cast(
        functools.partial(_mm_kernel, n_extra=len(extras), epilogue=epilogue, tm=tm), cast,
        grid=(n // tm, n_cols // tn),
        in_specs=[pl.BlockSpec((tm, k), lambda i, j: (i, 0)),
                  _w_spec(k, tn, layer, col0 // tn),
                  *extra_specs],
        out_specs=[out_spec], out_shape=[out_shape], operands=(a, w, *extras), block_bytes=blocks, name=name)
    return outs[0] if cast is None else outs


def _epi_pieces(acc, row0):
    return jnp.stack([acc[:, h * LANES:(h + 1) * LANES] for h in range(acc.shape[1] // LANES)], axis=0)


def _epi_sigmoid(acc, row0):
    return jax.nn.sigmoid(acc)


def _epi_residual(acc, row0, res, *, n_valid):
    rows = row0 + lax.broadcasted_iota(jnp.int32, acc.shape, 0)
    return jnp.where(rows < n_valid, res + acc, 0.0)


def _inproj_conv_kernel(a_ref, wb_ref, wc_ref, wv_ref, bg_ref, cv_ref):
    a = a_ref[...]
    bg_ref[...] = _dot(a, wb_ref[...]).astype(bg_ref.dtype)
    cv_ref[...] = (_dot(a, wc_ref[...]) * _dot(a, wv_ref[...])).astype(cv_ref.dtype)


def _inproj_conv(hn, w_in, layer, d_conv, tm, tn, cast=None):
    n, k = hn.shape
    nb = d_conv // tn
    blocks = _nbytes((tm, k), BF16) + 3 * _nbytes((k, tn), BF16) + 2 * _nbytes((tm, tn), BF16)
    out = jax.ShapeDtypeStruct((n, d_conv), BF16)
    return _call_with_cast(
        _inproj_conv_kernel, cast,
        grid=(n // tm, nb),
        in_specs=[pl.BlockSpec((tm, k), lambda i, j: (i, 0)),
                  _w_spec(k, tn, layer, 0),
                  _w_spec(k, tn, layer, nb),
                  _w_spec(k, tn, layer, 2 * nb)],
        out_specs=[pl.BlockSpec((tm, tn), lambda i, j: (i, j)),
                   pl.BlockSpec((tm, tn), lambda i, j: (i, j))],
        out_shape=[out, out], operands=(hn, w_in, w_in, w_in), block_bytes=blocks, name="inproj_conv")


def _shifted_rows(x, prev_row, next_row):
    t = x.shape[0]
    row = lax.broadcasted_iota(jnp.int32, x.shape, 0)
    up = jnp.where(row == 0, prev_row, pltpu.roll(x, 1, 0))
    dn = jnp.where(row == t - 1, next_row, pltpu.roll(x, t - 1, 0))
    return up, dn


def _branch_a_kernel(bg_ref, cv_ref, cvp_ref, cvn_ref, cw_ref, w_ref, gate_ref, o_ref, z_ref, *, chunk):
    i = pl.program_id(0)
    last = pl.num_programs(0) - 1

    @pl.when(pl.program_id(1) == 0)
    def _():
        for c0 in range(0, z_ref.shape[1], chunk):
            cols = slice(c0, c0 + chunk)
            x = cv_ref[:, cols].astype(F32)
            prev = cvp_ref[:, cols].astype(F32)[BF16_ROWS - 1:BF16_ROWS, :]
            nxt = cvn_ref[:, cols].astype(F32)[0:1, :]
            prev = jnp.where(i > 0, prev, 0.0)
            nxt = jnp.where(i < last, nxt, 0.0)
            up, dn = _shifted_rows(x, prev, nxt)
            w = cw_ref[:, cols]
            conv = up * w[0:1, :] + x * w[1:2, :] + dn * w[2:3, :]
            z_ref[:, cols] = (bg_ref[:, cols].astype(F32) * conv).astype(z_ref.dtype)

    o_ref[...] = (_dot(z_ref[...], w_ref[...]) * gate_ref[...].astype(F32)).astype(o_ref.dtype)


def _branch_a(bg, cv, conv_w, w_a, layer, gates, d_model, tm, tn):
    n, c = bg.shape
    hb = tm // BF16_ROWS
    n_hb = n // BF16_ROWS
    blocks = (2 * _nbytes((tm, c), BF16) + 2 * _nbytes((BF16_ROWS, c), BF16) + _nbytes((8, c), F32)
              + _nbytes((c, tn), BF16) + 2 * _nbytes((tm, tn), BF16))
    scratch = _nbytes((tm, c), BF16)
    return pl.pallas_call(
        functools.partial(_branch_a_kernel, chunk=_pick_tile(c, 512, LANES)),
        out_shape=jax.ShapeDtypeStruct((n, d_model), BF16),
        grid=(n // tm, d_model // tn),
        in_specs=[pl.BlockSpec((tm, c), lambda i, j: (i, 0)),
                  pl.BlockSpec((tm, c), lambda i, j: (i, 0)),
                  pl.BlockSpec((BF16_ROWS, c), lambda i, j: (jnp.maximum(i * hb - 1, 0), 0)),
                  pl.BlockSpec((BF16_ROWS, c), lambda i, j: (jnp.minimum((i + 1) * hb, n_hb - 1), 0)),
                  pl.BlockSpec((3, c), lambda i, j: (0, 0)),
                  _w_spec(c, tn, layer, 0),
                  pl.BlockSpec((tm, tn), lambda i, j: (i, j))],
        out_specs=pl.BlockSpec((tm, tn), lambda i, j: (i, j)),
        scratch_shapes=[pltpu.VMEM((tm, c), BF16)],
        compiler_params=_compiler_params(("parallel", "arbitrary"), blocks, scratch),
        name="branch_a",
    )(bg, cv, cv, cv, conv_w, w_a, gates)


def _cmul(ar, ai, br, bi):
    return ar * br - ai * bi, ar * bi + ai * br


def _s5_prep_kernel(lre_ref, lim_ref, lst_ref, bre_ref, bim_ref, cre_ref, cim_ref,
                    ere_ref, eim_ref, care_ref, caim_ref, klag_ref, apre_ref, apim_ref):
    lam_re, lam_im = lre_ref[...], lim_ref[...]
    dt = jnp.exp(lst_ref[...])
    mag = jnp.exp(lam_re * dt)
    a_re = mag * jnp.cos(lam_im * dt)
    a_im = mag * jnp.sin(lam_im * dt)
    nr, ni = a_re - 1.0, a_im
    den = lam_re * lam_re + lam_im * lam_im
    f_re = (nr * lam_re + ni * lam_im) / den
    f_im = (ni * lam_re - nr * lam_im) / den
    b_re, b_im = bre_ref[...], bim_ref[...]
    bb_re, bb_im = _cmul(f_re, f_im, b_re, b_im)
    c_re, c_im = cre_ref[...], cim_ref[...]

    pw = [(jnp.ones_like(a_re), jnp.zeros_like(a_im))]
    for _ in range(T_BLK):
        pw.append(_cmul(pw[-1][0], pw[-1][1], a_re, a_im))

    e_re, e_im, ca_re, ca_im = [], [], [], []
    for k in range(T_BLK):
        er, ei = _cmul(pw[k][0], pw[k][1], bb_re, bb_im)
        e_re.append(er)
        e_im.append(ei)
        cr, ci = _cmul(pw[k + 1][0], pw[k + 1][1], c_re, c_im)
        ca_re.append(cr)
        ca_im.append(-ci)
    e_re = jnp.concatenate(e_re, axis=1)
    e_im = jnp.concatenate(e_im, axis=1)
    ere_ref[...] = e_re
    eim_ref[...] = e_im
    care_ref[...] = jnp.concatenate(ca_re, axis=1)
    caim_ref[...] = jnp.concatenate(ca_im, axis=1)

    dn = (((2,), (2,)), ((0,), (0,)))
    klag_ref[...] = (lax.dot_general(c_re, e_re, dn, precision=lax.Precision.HIGHEST,
                                     preferred_element_type=F32)
                     - lax.dot_general(c_im, e_im, dn, precision=lax.Precision.HIGHEST,
                                       preferred_element_type=F32))

    q_re, q_im = pw[T_BLK]
    r_re, r_im = q_re, q_im
    for m in range(8):
        apre_ref[m] = r_re
        apim_ref[m] = r_im
        r_re, r_im = _cmul(r_re, r_im, q_re, q_im)


def _s5_prep(lam_re, lam_im, log_step, b_re, b_im, c_re, c_im):
    depth, _, g, p = lam_re.shape
    h = b_re.shape[-1]
    lead = (depth, 2)
    b_re_t = jnp.swapaxes(b_re, -1, -2).astype(F32)
    b_im_t = jnp.swapaxes(b_im, -1, -2).astype(F32)

    def spec(*tail):
        zeros = (0,) * len(tail)
        return pl.BlockSpec((None, None) + tail, lambda l, d: (l, d) + zeros)

    th = T_BLK * h
    outs = (jax.ShapeDtypeStruct(lead + (g, th, p), F32),) * 4 + (
        jax.ShapeDtypeStruct(lead + (g, h, th), F32),
        jax.ShapeDtypeStruct(lead + (8, g, 1, p), F32),
        jax.ShapeDtypeStruct(lead + (8, g, 1, p), F32))
    blocks = 4 * _nbytes((g, th, 128), F32) + 8 * _nbytes((g, h, 128), F32)
    return pl.pallas_call(
        _s5_prep_kernel,
        out_shape=outs,
        grid=lead,
        in_specs=[spec(g, 1, p), spec(g, 1, p), spec(g, 1, 1), spec(g, h, p), spec(g, h, p),
                  spec(g, h, p), spec(g, h, p)],
        out_specs=(spec(g, th, p),) * 4 + (spec(g, h, th), spec(8, g, 1, p), spec(8, g, 1, p)),
        compiler_params=_compiler_params(("parallel", "parallel"), blocks, 16 * _nbytes((g, th, 128), F32)),
        name="s5_prep",
    )(lam_re.astype(F32)[:, :, :, None, :], lam_im.astype(F32)[:, :, :, None, :],
      log_step.astype(F32)[..., None, None], b_re_t, b_im_t, c_re.astype(F32), c_im.astype(F32))


def _block_diag_kernel(c_ref, e_ref, rg_ref, cg_ref, o_ref):
    spread = _dot(c_ref[...].astype(BF16), e_ref[...])
    o_ref[...] = jnp.where(rg_ref[...] == cg_ref[...], spread, 0.0).astype(o_ref.dtype)


def _block_diag(compact, gl, row_inner, col_inner, name):
    n_cb, rows, kc = compact.shape
    n_out = kc * gl
    tr = _pick_tile(rows, 512, BF16_ROWS)
    col = np.arange(n_out)
    spread = np.arange(kc)[:, None] == (col // (gl * col_inner)) * col_inner + col % col_inner
    row_group = ((np.arange(rows) // row_inner) % gl).astype(np.int32)[:, None]
    col_group = ((col // col_inner) % gl).astype(np.int32)[None, :]
    blocks = (_nbytes((tr, kc), F32) + _nbytes((kc, n_out), BF16) + _nbytes((tr, LANES), jnp.int32)
              + _nbytes((8, n_out), jnp.int32) + _nbytes((tr, n_out), BF16))
    return pl.pallas_call(
        _block_diag_kernel,
        out_shape=jax.ShapeDtypeStruct((n_cb, rows, n_out), BF16),
        grid=(n_cb, rows // tr),
        in_specs=[pl.BlockSpec((None, tr, kc), lambda c, i: (c, i, 0)),
                  pl.BlockSpec((kc, n_out), lambda c, i: (0, 0)),
                  pl.BlockSpec((tr, 1), lambda c, i: (i, 0)),
                  pl.BlockSpec((1, n_out), lambda c, i: (0, 0))],
        out_specs=pl.BlockSpec((None, tr, n_out), lambda c, i: (c, i, 0)),
        compiler_params=_compiler_params(("parallel", "parallel"), blocks),
        name=name,
    )(compact, jnp.asarray(spread, BF16), jnp.asarray(row_group), jnp.asarray(col_group))


def _s5_weights(prep, n_cb, gl, h, p):
    e_re, e_im, ca_re, ca_im, klag, ap_re, ap_im = prep
    t = T_BLK

    def split(x):
        return x.reshape(2, n_cb * gl, t, h, p)

    e = jnp.stack([split(e_re), split(e_im)], axis=1)
    e_sel = jnp.stack([e[0, :, :, ::-1], e[1]], axis=0)
    e_sel = e_sel.reshape(2, 2, n_cb, gl, t, h, p)
    e_sel = jnp.transpose(e_sel, (2, 4, 3, 5, 0, 1, 6))
    w_s = _block_diag(e_sel.reshape(n_cb, t * gl * h, 4 * p), gl, h, p, "s5_w_states")

    def carry(ca, name):
        ca = split(ca)
        sel = jnp.stack([ca[0], ca[1, :, ::-1]], axis=0)
        sel = sel.reshape(2, n_cb, gl, t, h, p)
        sel = jnp.transpose(sel, (1, 0, 2, 5, 3, 4))
        return _block_diag(sel.reshape(n_cb, 2 * gl * p, t * h), gl, p, h, name)

    w_cre, w_cim = carry(ca_re, "s5_w_carry_re"), carry(ca_im, "s5_w_carry_im")

    kl = klag.reshape(2, n_cb * gl, h, t, h)
    sig = np.arange(t)[:, None]
    tau = np.arange(t)[None, :]
    kf = kl[0][:, :, np.clip(tau - sig, 0, t - 1), :] * jnp.asarray(tau >= sig, F32)[None, None, :, :, None]
    kb = kl[1][:, :, np.clip(sig - tau, 0, t - 1), :] * jnp.asarray(sig >= tau, F32)[None, None, :, :, None]
    kst = (kf + kb).reshape(n_cb, gl, h, t, t, h)
    kst = jnp.transpose(kst, (0, 3, 1, 5, 4, 2))
    w_i = _block_diag(kst.reshape(n_cb, t * gl * h, t * h), gl, h, h, "s5_w_intra")

    def lanes(ap):
        ap = jnp.stack([ap[0], ap[1, ::-1]], axis=0)
        ap = ap.reshape(2, 8, n_cb, gl * p)
        return jnp.transpose(ap, (2, 0, 1, 3))
    apw = jnp.stack([lanes(ap_re), lanes(ap_im)], axis=2)

    return w_s, w_cre, w_cim, w_i, apw


def _fold_time_blocks(u_ref, lhs_ref, r):
    pieces = u_ref.shape[0]
    for tau in range(T_BLK):
        for h in range(pieces):
            col = (tau * pieces + h) * LANES
            lhs_ref[:, col:col + LANES] = u_ref[h, pl.ds(tau, r, stride=T_BLK), :].astype(lhs_ref.dtype)


def _s5_states_kernel(u_ref, w_ref, s_ref, lhs_ref, *, r):
    @pl.when(pl.program_id(2) == 0)
    def _():
        _fold_time_blocks(u_ref, lhs_ref, r)

    s_ref[...] = _dot(lhs_ref[...], w_ref[...])


def _s5_states(u, w_s, r, tn):
    n_cb, kw, ns = w_s.shape
    n = u.shape[1]
    nb = n // T_BLK
    ppc = u.shape[0] // n_cb
    blocks = _nbytes((ppc, T_BLK * r, LANES), F32) + _nbytes((kw, tn), BF16) + _nbytes((r, tn), F32)
    scratch = _nbytes((r, kw), BF16)
    return pl.pallas_call(
        functools.partial(_s5_states_kernel, r=r),
        out_shape=jax.ShapeDtypeStruct((n_cb, nb, ns), F32),
        grid=(n_cb, nb // r, ns // tn),
        in_specs=[pl.BlockSpec((ppc, T_BLK * r, LANES), lambda c, i, j: (c, i, 0)),
                  pl.BlockSpec((None, kw, tn), lambda c, i, j: (c, 0, j))],
        out_specs=pl.BlockSpec((None, r, tn), lambda c, i, j: (c, i, j)),
        scratch_shapes=[pltpu.VMEM((r, kw), BF16)],
        compiler_params=_compiler_params(("parallel", "parallel", "arbitrary"), blocks, scratch),
        name="s5_states",
    )(u, w_s)


def _s5_scan_kernel(sre_ref, sim_ref, apre_ref, apim_ref, xre_ref, xim_ref, *, n_pairs):
    d = pl.program_id(1)
    lw = sre_ref.shape[-1]
    ap_re, ap_im = apre_ref[...], apim_ref[...]
    row = lax.broadcasted_iota(jnp.int32, (8, lw), 0)

    def run(reverse):
        steps = []
        for shift in (1, 2, 4):
            src = (8 - shift) if reverse else (shift - 1)
            keep = (row < 8 - shift) if reverse else (row >= shift)
            steps.append((8 - shift if reverse else shift,
                          jnp.where(keep, ap_re[src:src + 1, :], 0.0),
                          jnp.where(keep, ap_im[src:src + 1, :], 0.0)))
        edge = 7 if reverse else 0
        out_row = 0 if reverse else 7
        nb_shift = 7 if reverse else 1

        def tile(r0, c_re, c_im):
            x_re = sre_ref[pl.ds(r0, 8), :]
            x_im = sim_ref[pl.ds(r0, 8), :]
            for amount, m_re, m_im in steps:
                s_re = pltpu.roll(x_re, amount, 0)
                s_im = pltpu.roll(x_im, amount, 0)
                x_re, x_im = x_re + m_re * s_re - m_im * s_im, x_im + m_re * s_im + m_im * s_re
            inc_re = x_re + ap_re * c_re - ap_im * c_im
            inc_im = x_im + ap_re * c_im + ap_im * c_re
            e_re = jnp.where(row == edge, c_re, pltpu.roll(inc_re, nb_shift, 0))
            e_im = jnp.where(row == edge, c_im, pltpu.roll(inc_im, nb_shift, 0))
            return e_re, e_im, inc_re[out_row:out_row + 1, :], inc_im[out_row:out_row + 1, :]

        def body(k, carry):
            c_re, c_im = carry
            pair = (n_pairs - 1 - k) if reverse else k
            r0 = pl.multiple_of(pair * BF16_ROWS, BF16_ROWS)
            e_re, e_im = [None, None], [None, None]
            for half in ((1, 0) if reverse else (0, 1)):
                e_re[half], e_im[half], c_re, c_im = tile(pl.multiple_of(r0 + 8 * half, 8), c_re, c_im)
            xre_ref[pl.ds(r0, BF16_ROWS), :] = jnp.concatenate(e_re, axis=0).astype(xre_ref.dtype)
            xim_ref[pl.ds(r0, BF16_ROWS), :] = jnp.concatenate(e_im, axis=0).astype(xim_ref.dtype)
            return c_re, c_im

        zero = jnp.zeros((1, lw), F32)
        lax.fori_loop(0, n_pairs, body, (zero, zero))

    @pl.when(d == 0)
    def _():
        run(False)

    @pl.when(d == 1)
    def _():
        run(True)


def _s5_scan(s, apw, lw):
    n_cb, nb, ns = s.shape
    glp = ns // 4
    nq = glp // lw
    blocks = 2 * _nbytes((nb, lw), F32) + 2 * _nbytes((nb, lw), BF16) + 2 * _nbytes((8, lw), F32)
    out = jax.ShapeDtypeStruct((n_cb, nb, 2 * glp), BF16)

    def s_spec(ri):
        return pl.BlockSpec((None, nb, lw), lambda c, d, q: (c, 0, (2 * d + ri) * nq + q))

    def ap_spec(ri):
        return pl.BlockSpec((None, None, None, 8, lw), lambda c, d, q: (c, d, ri, 0, q))

    x_spec = pl.BlockSpec((None, nb, lw), lambda c, d, q: (c, 0, d * nq + q))
    return pl.pallas_call(
        functools.partial(_s5_scan_kernel, n_pairs=nb // BF16_ROWS),
        out_shape=(out, out),
        grid=(n_cb, 2, nq),
        in_specs=[s_spec(0), s_spec(1), ap_spec(0), ap_spec(1)],
        out_specs=(x_spec, x_spec),
        compiler_params=_compiler_params(("parallel", "parallel", "parallel"), blocks),
        name="s5_scan",
    )(s, s, apw, apw)


def _s5_out_kernel(u_ref, xre_ref, xim_ref, wi_ref, wcre_ref, wcim_ref, y_ref, lhs_ref, *, r, tn):
    j = pl.program_id(2)
    pieces = u_ref.shape[0]

    @pl.when(j == 0)
    def _():
        _fold_time_blocks(u_ref, lhs_ref, r)

    y = _dot(lhs_ref[...], wi_ref[...])
    y = y + _dot(xre_ref[...], wcre_ref[...])
    y = y + _dot(xim_ref[...], wcim_ref[...])

    per = tn // (pieces * LANES)
    for jj in range(T_BLK // per):
        @pl.when(j == jj)
        def _():
            for k in range(per):
                for h in range(pieces):
                    col = (k * pieces + h) * LANES
                    y_ref[h, pl.ds(jj * per + k, r, stride=T_BLK), :] = y[:, col:col + LANES]


def _s5_out(u, x_re, x_im, w_i, w_cre, w_cim, r, tn):
    n_cb, kw, _ = w_i.shape
    n = u.shape[1]
    nb = n // T_BLK
    ppc = u.shape[0] // n_cb
    kx = x_re.shape[-1]
    assert tn % (ppc * LANES) == 0 and kw % tn == 0
    blocks = (2 * _nbytes((ppc, T_BLK * r, LANES), F32) + 2 * _nbytes((r, kx), BF16) + _nbytes((kw, tn), BF16)
              + 2 * _nbytes((kx, tn), BF16) + _nbytes((r, tn), F32))
    scratch = _nbytes((r, kw), BF16)
    return pl.pallas_call(
        functools.partial(_s5_out_kernel, r=r, tn=tn),
        out_shape=jax.ShapeDtypeStruct(u.shape, F32),
        grid=(n_cb, nb // r, kw // tn),
        in_specs=[pl.BlockSpec((ppc, T_BLK * r, LANES), lambda c, i, j: (c, i, 0)),
                  pl.BlockSpec((None, r, kx), lambda c, i, j: (c, i, 0)),
                  pl.BlockSpec((None, r, kx), lambda c, i, j: (c, i, 0)),
                  pl.BlockSpec((None, kw, tn), lambda c, i, j: (c, 0, j)),
                  pl.BlockSpec((None, kx, tn), lambda c, i, j: (c, 0, j)),
                  pl.BlockSpec((None, kx, tn), lambda c, i, j: (c, 0, j))],
        out_specs=pl.BlockSpec((ppc, T_BLK * r, LANES), lambda c, i, j: (c, i, 0)),
        scratch_shapes=[pltpu.VMEM((r, kw), BF16)],
        compiler_params=_compiler_params(("parallel", "parallel", "arbitrary"), blocks, scratch),
        name="s5_out",
    )(u, x_re, x_im, w_i, w_cre, w_cim)


def _glu_merge_kernel(y_ref, u_ref, d_ref, w1_ref, w2_ref, pa_ref, gb_ref, m_ref, s_ref):
    @pl.when(pl.program_id(1) == 0)
    def _():
        for c in range(y_ref.shape[0]):
            s = jax.nn.gelu(y_ref[c] + d_ref[c] * u_ref[c])
            s_ref[:, c * LANES:(c + 1) * LANES] = s.astype(s_ref.dtype)

    a = s_ref[...]
    y_b = _dot(a, w1_ref[...]) * jax.nn.sigmoid(_dot(a, w2_ref[...]))
    m_ref[...] = (pa_ref[...].astype(F32) + gb_ref[...].astype(F32) * y_b).astype(m_ref.dtype)


def _glu_merge(y, u, d_skip, w_glu, layer, pa, gates, d_model, tm, tn):
    n_pc, n, _ = y.shape
    k = n_pc * LANES
    nb = d_model // tn
    blocks = (2 * _nbytes((n_pc, tm, LANES), F32) + 2 * _nbytes((k, tn), BF16) + 3 * _nbytes((tm, tn), BF16))
    scratch = _nbytes((tm, k), BF16)
    return pl.pallas_call(
        _glu_merge_kernel,
        out_shape=jax.ShapeDtypeStruct((n, d_model), BF16),
        grid=(n // tm, nb),
        in_specs=[pl.BlockSpec((n_pc, tm, LANES), lambda i, j: (0, i, 0)),
                  pl.BlockSpec((n_pc, tm, LANES), lambda i, j: (0, i, 0)),
                  pl.BlockSpec((n_pc, 1, LANES), lambda i, j: (0, 0, 0)),
                  _w_spec(k, tn, layer, 0),
                  _w_spec(k, tn, layer, nb),
                  pl.BlockSpec((tm, tn), lambda i, j: (i, j)),
                  pl.BlockSpec((tm, tn), lambda i, j: (i, nb + j))],
        out_specs=pl.BlockSpec((tm, tn), lambda i, j: (i, j)),
        scratch_shapes=[pltpu.VMEM((tm, k), BF16)],
        compiler_params=_compiler_params(("parallel", "arbitrary"), blocks, scratch),
        name="glu_merge",
    )(y, u, d_skip, w_glu, w_glu, pa, gates)


def _ffn_up_kernel(h_ref, hp_ref, hn_ref, wg_ref, wv_ref, cg_ref, cv_ref, act_ref, lhs_ref, *, tm):
    i = pl.program_id(0)
    j = pl.program_id(1)
    last = pl.num_programs(0) - 1
    halo = BF16_ROWS

    @pl.when(j == 0)
    def _():
        lhs_ref[0:halo, :] = jnp.where(i > 0, hp_ref[...], jnp.zeros_like(hp_ref))
        lhs_ref[halo:halo + tm, :] = h_ref[...]
        lhs_ref[halo + tm:, :] = jnp.where(i < last, hn_ref[...], jnp.zeros_like(hn_ref))

    a = lhs_ref[...]
    rows = tm + 2 * halo

    def conv(w_ref, cw_ref):
        u = _dot(a, w_ref[...])
        cw = cw_ref[...]
        up = pltpu.roll(u, 1, 0)[halo:halo + tm]
        dn = pltpu.roll(u, rows - 1, 0)[halo:halo + tm]
        return up * cw[0:1, :] + u[halo:halo + tm] * cw[1:2, :] + dn * cw[2:3, :]

    act_ref[...] = (jax.nn.gelu(conv(wg_ref, cg_ref)) * conv(wv_ref, cv_ref)).astype(act_ref.dtype)


def _ffn_up(hn, w_up, layer, conv_w, d_ff, tm, tn, cast=None):
    n, k = hn.shape
    nb = d_ff // tn
    hb = tm // BF16_ROWS
    n_hb = n // BF16_ROWS
    blocks = (_nbytes((tm + 2 * BF16_ROWS, k), BF16) + 2 * _nbytes((k, tn), BF16) + 2 * _nbytes((3, tn), F32)
              + _nbytes((tm, tn), BF16))
    scratch = _nbytes((tm + 2 * BF16_ROWS, k), BF16)
    outs = _call_with_cast(
        functools.partial(_ffn_up_kernel, tm=tm), cast,
        grid=(n // tm, nb),
        in_specs=[pl.BlockSpec((tm, k), lambda i, j: (i, 0)),
                  pl.BlockSpec((BF16_ROWS, k), lambda i, j: (jnp.maximum(i * hb - 1, 0), 0)),
                  pl.BlockSpec((BF16_ROWS, k), lambda i, j: (jnp.minimum((i + 1) * hb, n_hb - 1), 0)),
                  _w_spec(k, tn, layer, 0),
                  _w_spec(k, tn, layer, nb),
                  pl.BlockSpec((3, tn), lambda i, j: (0, j)),
                  pl.BlockSpec((3, tn), lambda i, j: (0, nb + j))],
        out_specs=[pl.BlockSpec((tm, tn), lambda i, j: (i, j))],
        out_shape=[jax.ShapeDtypeStruct((n, d_ff), BF16)],
        operands=(hn, hn, hn, w_up, w_up, conv_w, conv_w),
        scratch_shapes=[pltpu.VMEM((tm + 2 * BF16_ROWS, k), BF16)],
        block_bytes=blocks, scratch_bytes=scratch, name="ffn_up")
    return outs[0] if cast is None else outs


def kernel(x, meta_tokens, norm_mix, w_in, conv_a_w, w_a, ssm_lambda_re, ssm_lambda_im, ssm_log_step,
           ssm_b_re, ssm_b_im, ssm_c_re, ssm_c_im, ssm_d, w_glu, w_out, norm_ffn, w_up, conv_ffn_w,
           w_down, norm_final):
    bsz, seq, d_model = x.shape
    n_meta = meta_tokens.shape[0]
    depth = w_in.shape[0]
    d_conv = w_a.shape[1]
    d_ssm = ssm_d.shape[1]
    d_ff = w_down.shape[1]
    n_groups, p_state, h_grp = ssm_b_re.shape[2:]
    assert w_in.shape[2] == 3 * d_conv + d_ssm + 2 * d_model and n_groups * h_grp == d_ssm

    cbw = min(S5_CB_WIDTH, d_ssm)
    n_cb = d_ssm // cbw
    gl = cbw // h_grp
    assert n_cb * cbw == d_ssm and gl * h_grp == cbw

    n_true = n_meta + seq
    row_align = T_BLK * BF16_ROWS
    n_pad = _round_up(n_true, 10 * row_align) if n_true >= 40 * row_align else _round_up(n_true, row_align)
    nb = n_pad // T_BLK

    tm = _pick_tile(n_pad, TM_TARGET, BF16_ROWS)
    tm_up = _pick_tile(n_pad, TM_FFN_UP_TARGET, BF16_ROWS)
    tm_half = _pick_tile(n_pad, TM_HALF_TARGET, BF16_ROWS)
    tm_ew = _pick_tile(n_pad, TM_EW_TARGET, BF16_ROWS)
    r_s5 = _pick_tile(nb, R_S5_TARGET, BF16_ROWS)
    tn = lambda n, target=512: _pick_tile(n, target, 128)

    prep = _s5_prep(ssm_lambda_re, ssm_lambda_im, ssm_log_step, ssm_b_re, ssm_b_im, ssm_c_re, ssm_c_im)
    d_skip = ssm_d.astype(F32).reshape(depth, d_ssm // LANES, 1, LANES)

    residual = functools.partial(_epi_residual, n_valid=n_true)
    tile_spec = lambda t, rows=tm: pl.BlockSpec((rows, t), lambda i, j: (i, j))
    u_off = 3 * d_conv
    gate_off = u_off + d_ssm

    w_a_b, w_glu_b, w_out_b = (w.astype(BF16) for w in (w_a, w_glu, w_out))
    w_in_first = w_in[0].astype(BF16)

    outs = []
    for b in range(bsz):
        h_res = jnp.concatenate([meta_tokens.astype(F32), x[b].astype(F32),
                                 jnp.zeros((n_pad - n_true, d_model), F32)], axis=0)
        w_in_l = w_in_first
        for l in range(depth):
            w_s, w_cre, w_cim, w_i, apw = _s5_weights([t[l] for t in prep], n_cb, gl, h_grp, p_state)

            hn = _rmsnorm(h_res, norm_mix[l], BF16, tm_ew)
            bg, cv, w_down_l = _inproj_conv(hn, w_in_l, None, d_conv, tm_half, tn(d_conv), cast=(w_down, l))
            u = _mm(hn, w_in_l, None, col0=u_off, n_cols=d_ssm, tm=tm, tn=cbw, epilogue=_epi_pieces,
                    out_shape=jax.ShapeDtypeStruct((d_ssm // LANES, n_pad, LANES), F32),
                    out_spec=pl.BlockSpec((cbw // LANES, tm, LANES), lambda i, j: (j, i, 0)), name="inproj_u")
            t_g = tn(2 * d_model)
            gates, w_up_l = _mm(hn, w_in_l, None, col0=gate_off, n_cols=2 * d_model, tm=tm, tn=t_g,
                                epilogue=_epi_sigmoid,
                                out_shape=jax.ShapeDtypeStruct((n_pad, 2 * d_model), BF16),
                                out_spec=tile_spec(t_g), name="inproj_gates", cast=(w_up, l))

            pa = _branch_a(bg, cv, conv_a_w[l].astype(F32), w_a_b, l, gates, d_model, tm, tn(d_model, 1024))
            t_d = tn(d_model)

            states = _s5_states(u, w_s, _pick_tile(nb, R_S5_STATES_TARGET, BF16_ROWS), tn(w_s.shape[-1]))
            x_re, x_im = _s5_scan(states, apw, tn(gl * p_state, SCAN_LANES_TARGET))
            y = _s5_out(u, x_re, x_im, w_i, w_cre, w_cim, r_s5, tn(T_BLK * cbw))

            merged = _glu_merge(y, u, d_skip[l], w_glu_b, l, pa, gates, d_model, tm_half, tn(d_model, 1024))
            h_res = _mm(merged, w_out_b, l, col0=0, n_cols=d_model, tm=tm, tn=t_d, epilogue=residual,
                        extras=(h_res,), extra_specs=(tile_spec(t_d),),
                        out_shape=jax.ShapeDtypeStruct((n_pad, d_model), F32), out_spec=tile_spec(t_d),
                        name="out_proj")

            hn = _rmsnorm(h_res, norm_ffn[l], BF16, tm_ew)
            if l + 1 < depth:
                act, w_in_l = _ffn_up(hn, w_up_l, None, conv_ffn_w[l].astype(F32), d_ff, tm_up, tn(d_ff),
                                      cast=(w_in, l + 1))
            else:
                act = _ffn_up(hn, w_up_l, None, conv_ffn_w[l].astype(F32), d_ff, tm_up, tn(d_ff))
            t_o = tn(d_model, 256)
            h_res = _mm(act, w_down_l, None, col0=0, n_cols=d_model, tm=tm_half, tn=t_o, epilogue=residual,
                        extras=(h_res,), extra_specs=(tile_spec(t_o, tm_half),),
                        out_shape=jax.ShapeDtypeStruct((n_pad, d_model), F32), out_spec=tile_spec(t_o, tm_half),
                        name="ffn_down")

        out = _rmsnorm(h_res, norm_final, x.dtype, tm_ew)
        outs.append(out[n_meta:n_true])
    return jnp.stack(outs, axis=0)
```

```python
import functools

import numpy as np
import jax
import jax.numpy as jnp
from jax import lax
from jax.experimental import pallas as pl
from jax.experimental.pallas import tpu as pltpu

F32 = jnp.float32
BF16 = jnp.bfloat16
EPS = 1e-6

T_BLK = 8
S5_CB_WIDTH = 256
BF16_ROWS = 16
LANES = 128
VMEM_CAP_BYTES = 60000 * 1024
VMEM_TEMP_BYTES = 20 * 1024 * 1024

TM_TARGET = 1280
TM_FFN_UP_TARGET = 832
TM_HALF_TARGET = 640
R_S5_STATES_TARGET = 1040
TM_EW_TARGET = 320
R_S5_TARGET = 416
SCAN_LANES_TARGET = 256


def _round_up(n, m):
    return (n + m - 1) // m * m


def _pick_tile(n, target, align):
    best = 0
    for t in range(align, min(n, target) + 1, align):
        if n % t == 0:
            best = t
    assert best > 0, (n, target, align)
    return best


def _nbytes(shape, dtype):
    return int(np.prod(shape)) * jnp.dtype(dtype).itemsize


def _compiler_params(semantics, block_bytes, scratch_bytes=0):
    est = 2 * block_bytes + scratch_bytes + VMEM_TEMP_BYTES
    return pltpu.CompilerParams(dimension_semantics=semantics,
                                vmem_limit_bytes=int(min(VMEM_CAP_BYTES, est)))


def _dot(a, b):
    return jnp.dot(a, b, preferred_element_type=F32)


def _rmsnorm_kernel(x_ref, g_ref, o_ref):
    x = x_ref[...]
    ms = jnp.mean(x * x, axis=-1, keepdims=True)
    o_ref[...] = (x * lax.rsqrt(ms + EPS) * g_ref[...]).astype(o_ref.dtype)


def _rmsnorm(x, g, out_dtype, tm):
    n, d = x.shape
    blocks = _nbytes((tm, d), F32) + _nbytes((tm, d), out_dtype)
    return pl.pallas_call(
        _rmsnorm_kernel,
        out_shape=jax.ShapeDtypeStruct((n, d), out_dtype),
        grid=(n // tm,),
        in_specs=[pl.BlockSpec((tm, d), lambda i: (i, 0)),
                  pl.BlockSpec((1, d), lambda i: (0, 0))],
        out_specs=pl.BlockSpec((tm, d), lambda i: (i, 0)),
        compiler_params=_compiler_params(("parallel",), blocks),
        name="rmsnorm",
    )(x, g.reshape(1, d).astype(F32))


def _final_norm_kernel(a_ref, b_ref, g_ref, o_ref, *, skip):
    x = jnp.concatenate([a_ref[skip:, :], b_ref[...]], axis=0)
    ms = jnp.mean(x * x, axis=-1, keepdims=True)
    o_ref[...] = (x * lax.rsqrt(ms + EPS) * g_ref[...]).astype(o_ref.dtype)


def _final_norm(x, g, skip, n_out, out_dtype, tm):
    n, d = x.shape
    assert skip % 8 == 0 and tm % skip == 0 and n_out % tm == 0 and n >= n_out + skip
    per = tm // skip
    blocks = _nbytes((tm + skip, d), F32) + _nbytes((tm, d), out_dtype)
    return pl.pallas_call(
        functools.partial(_final_norm_kernel, skip=skip),
        out_shape=jax.ShapeDtypeStruct((n_out, d), out_dtype),
        grid=(n_out // tm,),
        in_specs=[pl.BlockSpec((tm, d), lambda i: (i, 0)),
                  pl.BlockSpec((skip, d), lambda i: ((i + 1) * per, 0)),
                  pl.BlockSpec((1, d), lambda i: (0, 0))],
        out_specs=pl.BlockSpec((tm, d), lambda i: (i, 0)),
        compiler_params=_compiler_params(("parallel",), blocks),
        name="final_norm",
    )(x, x, g.reshape(1, d).astype(F32))


def _mm_kernel(*refs, n_extra, epilogue, tm):
    a_ref, w_ref = refs[0], refs[1]
    extras = refs[2:2 + n_extra]
    o_ref = refs[2 + n_extra]
    acc = _dot(a_ref[...], w_ref[...])
    row0 = pl.program_id(0) * tm
    o_ref[...] = epilogue(acc, row0, *[e[...] for e in extras]).astype(o_ref.dtype)


def _w_spec(k, tn, layer, jb0):
    if layer is None:
        return pl.BlockSpec((k, tn), lambda i, j: (0, jb0 + j))
    return pl.BlockSpec((None, k, tn), lambda i, j: (layer, 0, jb0 + j))


def _call_with_cast(kernel_fn, cast, *, grid, in_specs, out_specs, out_shape, operands, scratch_shapes=(),
                    block_bytes, scratch_bytes=0, name):
    if cast is None:
        return pl.pallas_call(
            kernel_fn, out_shape=tuple(out_shape), grid=grid, in_specs=list(in_specs), out_specs=tuple(out_specs),
            scratch_shapes=list(scratch_shapes),
            compiler_params=_compiler_params(("parallel", "arbitrary"), block_bytes, scratch_bytes), name=name,
        )(*operands)

    stack, layer = cast
    _, k, n = stack.shape
    n_steps = grid[0] * grid[1]
    rows = min(r for r in range(BF16_ROWS, k + 1, BF16_ROWS) if k % r == 0 and k // r <= n_steps)
    last_chunk = k // rows - 1
    n_in, n_out = len(in_specs), len(out_specs)

    def chunk(i, j):
        return jnp.minimum(i * grid[1] + j, last_chunk)

    def body(*refs):
        src_ref, dst_ref = refs[n_in], refs[n_in + 1 + n_out]
        dst_ref[...] = src_ref[...].astype(dst_ref.dtype)
        kernel_fn(*refs[:n_in], *refs[n_in + 1:n_in + 1 + n_out], *refs[n_in + 2 + n_out:])

    cast_bytes = _nbytes((rows, n), F32) + _nbytes((rows, n), BF16)
    return pl.pallas_call(
        body,
        out_shape=(*out_shape, jax.ShapeDtypeStruct((k, n), BF16)),
        grid=grid,
        in_specs=[*in_specs, pl.BlockSpec((None, rows, n), lambda i, j: (layer, chunk(i, j), 0))],
        out_specs=(*out_specs, pl.BlockSpec((rows, n), lambda i, j: (chunk(i, j), 0))),
        scratch_shapes=list(scratch_shapes),
        compiler_params=_compiler_params(("arbitrary", "arbitrary"), block_bytes + cast_bytes, scratch_bytes),
        name=name,
    )(*operands, stack)


def _mm(a, w, layer, *, col0, n_cols, tm, tn, epilogue, extras=(), extra_specs=(), out_shape, out_spec, name,
        cast=None):
    n, k = a.shape
    assert n % tm == 0 and n_cols % tn == 0 and col0 % tn == 0
    blocks = (_nbytes((tm, k), a.dtype) + _nbytes((k, tn), w.dtype) + _nbytes((tm, tn), F32)
              + sum(_nbytes((tm, tn), e.dtype) for e in extras))
    outs = _call_with_cast(
        functools.partial(_mm_kernel, n_extra=len(extras), epilogue=epilogue, tm=tm), cast,
        grid=(n // tm, n_cols // tn),
        in_specs=[pl.BlockSpec((tm, k), lambda i, j: (i, 0)),
                  _w_spec(k, tn, layer, col0 // tn),
                  *extra_specs],
        out_specs=[out_spec], out_shape=[out_shape], operands=(a, w, *extras), block_bytes=blocks, name=name)
    return outs[0] if cast is None else outs


def _epi_pieces(acc, row0):
    return jnp.stack([acc[:, h * LANES:(h + 1) * LANES] for h in range(acc.shape[1] // LANES)], axis=0)


def _epi_sigmoid(acc, row0):
    return jax.nn.sigmoid(acc)


def _epi_residual(acc, row0, res, *, n_valid):
    rows = row0 + lax.broadcasted_iota(jnp.int32, acc.shape, 0)
    return jnp.where(rows < n_valid, res + acc, 0.0)


def _inproj_conv_kernel(a_ref, wb_ref, wc_ref, wv_ref, bg_ref, cv_ref):
    a = a_ref[...]
    bg_ref[...] = _dot(a, wb_ref[...]).astype(bg_ref.dtype)
    cv_ref[...] = (_dot(a, wc_ref[...]) * _dot(a, wv_ref[...])).astype(cv_ref.dtype)


def _inproj_conv(hn, w_in, layer, d_conv, tm, tn, cast=None):
    n, k = hn.shape
    nb = d_conv // tn
    blocks = _nbytes((tm, k), BF16) + 3 * _nbytes((k, tn), BF16) + 2 * _nbytes((tm, tn), BF16)
    out = jax.ShapeDtypeStruct((n, d_conv), BF16)
    return _call_with_cast(
        _inproj_conv_kernel, cast,
        grid=(n // tm, nb),
        in_specs=[pl.BlockSpec((tm, k), lambda i, j: (i, 0)),
                  _w_spec(k, tn, layer, 0),
                  _w_spec(k, tn, layer, nb),
                  _w_spec(k, tn, layer, 2 * nb)],
        out_specs=[pl.BlockSpec((tm, tn), lambda i, j: (i, j)),
                   pl.BlockSpec((tm, tn), lambda i, j: (i, j))],
        out_shape=[out, out], operands=(hn, w_in, w_in, w_in), block_bytes=blocks, name="inproj_conv")


def _shifted_rows(x, prev_row, next_row):
    t = x.shape[0]
    row = lax.broadcasted_iota(jnp.int32, x.shape, 0)
    up = jnp.where(row == 0, prev_row, pltpu.roll(x, 1, 0))
    dn = jnp.where(row == t - 1, next_row, pltpu.roll(x, t - 1, 0))
    return up, dn


def _branch_a_kernel(bg_ref, cv_ref, cvp_ref, cvn_ref, cw_ref, w_ref, gate_ref, o_ref, z_ref, *, chunk):
    i = pl.program_id(0)
    last = pl.num_programs(0) - 1

    @pl.when(pl.program_id(1) == 0)
    def _():
        for c0 in range(0, z_ref.shape[1], chunk):
            cols = slice(c0, c0 + chunk)
            x = cv_ref[:, cols].astype(F32)
            prev = cvp_ref[:, cols].astype(F32)[BF16_ROWS - 1:BF16_ROWS, :]
            nxt = cvn_ref[:, cols].astype(F32)[0:1, :]
            prev = jnp.where(i > 0, prev, 0.0)
            nxt = jnp.where(i < last, nxt, 0.0)
            up, dn = _shifted_rows(x, prev, nxt)
            w = cw_ref[:, cols]
            conv = up * w[0:1, :] + x * w[1:2, :] + dn * w[2:3, :]
            z_ref[:, cols] = (bg_ref[:, cols].astype(F32) * conv).astype(z_ref.dtype)

    o_ref[...] = (_dot(z_ref[...], w_ref[...]) * gate_ref[...].astype(F32)).astype(o_ref.dtype)


def _branch_a(bg, cv, conv_w, w_a, layer, gates, d_model, tm, tn):
    n, c = bg.shape
    hb = tm // BF16_ROWS
    n_hb = n // BF16_ROWS
    blocks = (2 * _nbytes((tm, c), BF16) + 2 * _nbytes((BF16_ROWS, c), BF16) + _nbytes((8, c), F32)
              + _nbytes((c, tn), BF16) + 2 * _nbytes((tm, tn), BF16))
    scratch = _nbytes((tm, c), BF16)
    return pl.pallas_call(
        functools.partial(_branch_a_kernel, chunk=_pick_tile(c, 512, LANES)),
        out_shape=jax.ShapeDtypeStruct((n, d_model), BF16),
        grid=(n // tm, d_model // tn),
        in_specs=[pl.BlockSpec((tm, c), lambda i, j: (i, 0)),
                  pl.BlockSpec((tm, c), lambda i, j: (i, 0)),
                  pl.BlockSpec((BF16_ROWS, c), lambda i, j: (jnp.maximum(i * hb - 1, 0), 0)),
                  pl.BlockSpec((BF16_ROWS, c), lambda i, j: (jnp.minimum((i + 1) * hb, n_hb - 1), 0)),
                  pl.BlockSpec((3, c), lambda i, j: (0, 0)),
                  _w_spec(c, tn, layer, 0),
                  pl.BlockSpec((tm, tn), lambda i, j: (i, j))],
        out_specs=pl.BlockSpec((tm, tn), lambda i, j: (i, j)),
        scratch_shapes=[pltpu.VMEM((tm, c), BF16)],
        compiler_params=_compiler_params(("parallel", "arbitrary"), blocks, scratch),
        name="branch_a",
    )(bg, cv, cv, cv, conv_w, w_a, gates)


def _cmul(ar, ai, br, bi):
    return ar * br - ai * bi, ar * bi + ai * br


def _s5_prep_kernel(lre_ref, lim_ref, lst_ref, bre_ref, bim_ref, cre_ref, cim_ref,
                    ere_ref, eim_ref, care_ref, caim_ref, klag_ref, apre_ref, apim_ref):
    lam_re, lam_im = lre_ref[...], lim_ref[...]
    dt = jnp.exp(lst_ref[...])
    mag = jnp.exp(lam_re * dt)
    a_re = mag * jnp.cos(lam_im * dt)
    a_im = mag * jnp.sin(lam_im * dt)
    nr, ni = a_re - 1.0, a_im
    den = lam_re * lam_re + lam_im * lam_im
    f_re = (nr * lam_re + ni * lam_im) / den
    f_im = (ni * lam_re - nr * lam_im) / den
    b_re, b_im = bre_ref[...], bim_ref[...]
    bb_re, bb_im = _cmul(f_re, f_im, b_re, b_im)
    c_re, c_im = cre_ref[...], cim_ref[...]

    pw = [(jnp.ones_like(a_re), jnp.zeros_like(a_im))]
    for _ in range(T_BLK):
        pw.append(_cmul(pw[-1][0], pw[-1][1], a_re, a_im))

    e_re, e_im, ca_re, ca_im = [], [], [], []
    for k in range(T_BLK):
        er, ei = _cmul(pw[k][0], pw[k][1], bb_re, bb_im)
        e_re.append(er)
        e_im.append(ei)
        cr, ci = _cmul(pw[k + 1][0], pw[k + 1][1], c_re, c_im)
        ca_re.append(cr)
        ca_im.append(-ci)
    e_re = jnp.concatenate(e_re, axis=1)
    e_im = jnp.concatenate(e_im, axis=1)
    ere_ref[...] = e_re
    eim_ref[...] = e_im
    care_ref[...] = jnp.concatenate(ca_re, axis=1)
    caim_ref[...] = jnp.concatenate(ca_im, axis=1)

    dn = (((2,), (2,)), ((0,), (0,)))
    klag_ref[...] = (lax.dot_general(c_re, e_re, dn, precision=lax.Precision.HIGHEST,
                                     preferred_element_type=F32)
                     - lax.dot_general(c_im, e_im, dn, precision=lax.Precision.HIGHEST,
                                       preferred_element_type=F32))

    q_re, q_im = pw[T_BLK]
    r_re, r_im = q_re, q_im
    for m in range(8):
        apre_ref[m] = r_re
        apim_ref[m] = r_im
        r_re, r_im = _cmul(r_re, r_im, q_re, q_im)


def _s5_prep(lam_re, lam_im, log_step, b_re, b_im, c_re, c_im):
    depth, _, g, p = lam_re.shape
    h = b_re.shape[-1]
    lead = (depth, 2)
    b_re_t = jnp.swapaxes(b_re, -1, -2).astype(F32)
    b_im_t = jnp.swapaxes(b_im, -1, -2).astype(F32)

    def spec(*tail):
        zeros = (0,) * len(tail)
        return pl.BlockSpec((None, None) + tail, lambda l, d: (l, d) + zeros)

    th = T_BLK * h
    outs = (jax.ShapeDtypeStruct(lead + (g, th, p), F32),) * 4 + (
        jax.ShapeDtypeStruct(lead + (g, h, th), F32),
        jax.ShapeDtypeStruct(lead + (8, g, 1, p), F32),
        jax.ShapeDtypeStruct(lead + (8, g, 1, p), F32))
    blocks = 4 * _nbytes((g, th, 128), F32) + 8 * _nbytes((g, h, 128), F32)
    return pl.pallas_call(
        _s5_prep_kernel,
        out_shape=outs,
        grid=lead,
        in_specs=[spec(g, 1, p), spec(g, 1, p), spec(g, 1, 1), spec(g, h, p), spec(g, h, p),
                  spec(g, h, p), spec(g, h, p)],
        out_specs=(spec(g, th, p),) * 4 + (spec(g, h, th), spec(8, g, 1, p), spec(8, g, 1, p)),
        compiler_params=_compiler_params(("parallel", "parallel"), blocks, 16 * _nbytes((g, th, 128), F32)),
        name="s5_prep",
    )(lam_re.astype(F32)[:, :, :, None, :], lam_im.astype(F32)[:, :, :, None, :],
      log_step.astype(F32)[..., None, None], b_re_t, b_im_t, c_re.astype(F32), c_im.astype(F32))


def _block_diag_kernel(c_ref, e_ref, rg_ref, cg_ref, o_ref):
    spread = _dot(c_ref[...].astype(BF16), e_ref[...])
    o_ref[...] = jnp.where(rg_ref[...] == cg_ref[...], spread, 0.0).astype(o_ref.dtype)


def _block_diag(compact, gl, row_inner, col_inner, name):
    n_cb, rows, kc = compact.shape
    n_out = kc * gl
    tr = _pick_tile(rows, 512, BF16_ROWS)
    col = np.arange(n_out)
    spread = np.arange(kc)[:, None] == (col // (gl * col_inner)) * col_inner + col % col_inner
    row_group = ((np.arange(rows) // row_inner) % gl).astype(np.int32)[:, None]
    col_group = ((col // col_inner) % gl).astype(np.int32)[None, :]
    blocks = (_nbytes((tr, kc), F32) + _nbytes((kc, n_out), BF16) + _nbytes((tr, LANES), jnp.int32)
              + _nbytes((8, n_out), jnp.int32) + _nbytes((tr, n_out), BF16))
    return pl.pallas_call(
        _block_diag_kernel,
        out_shape=jax.ShapeDtypeStruct((n_cb, rows, n_out), BF16),
        grid=(n_cb, rows // tr),
        in_specs=[pl.BlockSpec((None, tr, kc), lambda c, i: (c, i, 0)),
                  pl.BlockSpec((kc, n_out), lambda c, i: (0, 0)),
                  pl.BlockSpec((tr, 1), lambda c, i: (i, 0)),
                  pl.BlockSpec((1, n_out), lambda c, i: (0, 0))],
        out_specs=pl.BlockSpec((None, tr, n_out), lambda c, i: (c, i, 0)),
        compiler_params=_compiler_params(("parallel", "parallel"), blocks),
        name=name,
    )(compact, jnp.asarray(spread, BF16), jnp.asarray(row_group), jnp.asarray(col_group))


def _s5_weights(prep, n_cb, gl, h, p):
    e_re, e_im, ca_re, ca_im, klag, ap_re, ap_im = prep
    t = T_BLK

    def split(x):
        return x.reshape(2, n_cb * gl, t, h, p)

    e = jnp.stack([split(e_re), split(e_im)], axis=1)
    e_sel = jnp.stack([e[0, :, :, ::-1], e[1]], axis=0)
    e_sel = e_sel.reshape(2, 2, n_cb, gl, t, h, p)
    e_sel = jnp.transpose(e_sel, (2, 4, 3, 5, 0, 1, 6))
    w_s = _block_diag(e_sel.reshape(n_cb, t * gl * h, 4 * p), gl, h, p, "s5_w_states")

    def carry(ca, name):
        ca = split(ca)
        sel = jnp.stack([ca[0], ca[1, :, ::-1]], axis=0)
        sel = sel.reshape(2, n_cb, gl, t, h, p)
        sel = jnp.transpose(sel, (1, 0, 2, 5, 3, 4))
        return _block_diag(sel.reshape(n_cb, 2 * gl * p, t * h), gl, p, h, name)

    w_cre, w_cim = carry(ca_re, "s5_w_carry_re"), carry(ca_im, "s5_w_carry_im")

    kl = klag.reshape(2, n_cb * gl, h, t, h)
    sig = np.arange(t)[:, None]
    tau = np.arange(t)[None, :]
    kf = kl[0][:, :, np.clip(tau - sig, 0, t - 1), :] * jnp.asarray(tau >= sig, F32)[None, None, :, :, None]
    kb = kl[1][:, :, np.clip(sig - tau, 0, t - 1), :] * jnp.asarray(sig >= tau, F32)[None, None, :, :, None]
    kst = (kf + kb).reshape(n_cb, gl, h, t, t, h)
    kst = jnp.transpose(kst, (0, 3, 1, 5, 4, 2))
    w_i = _block_diag(kst.reshape(n_cb, t * gl * h, t * h), gl, h, h, "s5_w_intra")

    def lanes(ap):
        ap = jnp.stack([ap[0], ap[1, ::-1]], axis=0)
        ap = ap.reshape(2, 8, n_cb, gl * p)
        return jnp.transpose(ap, (2, 0, 1, 3))
    apw = jnp.stack([lanes(ap_re), lanes(ap_im)], axis=2)

    return w_s, w_cre, w_cim, w_i, apw


def _fold_time_blocks(u_ref, lhs_ref, r):
    pieces = u_ref.shape[0]
    for tau in range(T_BLK):
        for h in range(pieces):
            col = (tau * pieces + h) * LANES
            lhs_ref[:, col:col + LANES] = u_ref[h, pl.ds(tau, r, stride=T_BLK), :].astype(lhs_ref.dtype)


def _s5_states_kernel(u_ref, w_ref, s_ref, lhs_ref, *, r):
    @pl.when(pl.program_id(2) == 0)
    def _():
        _fold_time_blocks(u_ref, lhs_ref, r)

    s_ref[...] = _dot(lhs_ref[...], w_ref[...])


def _s5_states(u, w_s, r, tn):
    n_cb, kw, ns = w_s.shape
    n = u.shape[1]
    nb = n // T_BLK
    ppc = u.shape[0] // n_cb
    blocks = _nbytes((ppc, T_BLK * r, LANES), F32) + _nbytes((kw, tn), BF16) + _nbytes((r, tn), F32)
    scratch = _nbytes((r, kw), BF16)
    return pl.pallas_call(
        functools.partial(_s5_states_kernel, r=r),
        out_shape=jax.ShapeDtypeStruct((n_cb, nb, ns), F32),
        grid=(n_cb, nb // r, ns // tn),
        in_specs=[pl.BlockSpec((ppc, T_BLK * r, LANES), lambda c, i, j: (c, i, 0)),
                  pl.BlockSpec((None, kw, tn), lambda c, i, j: (c, 0, j))],
        out_specs=pl.BlockSpec((None, r, tn), lambda c, i, j: (c, i, j)),
        scratch_shapes=[pltpu.VMEM((r, kw), BF16)],
        compiler_params=_compiler_params(("parallel", "parallel", "arbitrary"), blocks, scratch),
        name="s5_states",
    )(u, w_s)


def _s5_scan_kernel(sre_ref, sim_ref, apre_ref, apim_ref, xre_ref, xim_ref, *, n_pairs):
    d = pl.program_id(1)
    lw = sre_ref.shape[-1]
    ap_re, ap_im = apre_ref[...], apim_ref[...]
    row = lax.broadcasted_iota(jnp.int32, (8, lw), 0)

    def run(reverse):
        steps = []
        for shift in (1, 2, 4):
            src = (8 - shift) if reverse else (shift - 1)
            keep = (row < 8 - shift) if reverse else (row >= shift)
            steps.append((8 - shift if reverse else shift,
                          jnp.where(keep, ap_re[src:src + 1, :], 0.0),
                          jnp.where(keep, ap_im[src:src + 1, :], 0.0)))
        edge = 7 if reverse else 0
        out_row = 0 if reverse else 7
        nb_shift = 7 if reverse else 1

        def tile(r0, c_re, c_im):
            x_re = sre_ref[pl.ds(r0, 8), :]
            x_im = sim_ref[pl.ds(r0, 8), :]
            for amount, m_re, m_im in steps:
                s_re = pltpu.roll(x_re, amount, 0)
                s_im = pltpu.roll(x_im, amount, 0)
                x_re, x_im = x_re + m_re * s_re - m_im * s_im, x_im + m_re * s_im + m_im * s_re
            inc_re = x_re + ap_re * c_re - ap_im * c_im
            inc_im = x_im + ap_re * c_im + ap_im * c_re
            e_re = jnp.where(row == edge, c_re, pltpu.roll(inc_re, nb_shift, 0))
            e_im = jnp.where(row == edge, c_im, pltpu.roll(inc_im, nb_shift, 0))
            return e_re, e_im, inc_re[out_row:out_row + 1, :], inc_im[out_row:out_row + 1, :]

        def body(k, carry):
            c_re, c_im = carry
            pair = (n_pairs - 1 - k) if reverse else k
            r0 = pl.multiple_of(pair * BF16_ROWS, BF16_ROWS)
            e_re, e_im = [None, None], [None, None]
            for half in ((1, 0) if reverse else (0, 1)):
                e_re[half], e_im[half], c_re, c_im = tile(pl.multiple_of(r0 + 8 * half, 8), c_re, c_im)
            xre_ref[pl.ds(r0, BF16_ROWS), :] = jnp.concatenate(e_re, axis=0).astype(xre_ref.dtype)
            xim_ref[pl.ds(r0, BF16_ROWS), :] = jnp.concatenate(e_im, axis=0).astype(xim_ref.dtype)
            return c_re, c_im

        zero = jnp.zeros((1, lw), F32)
        lax.fori_loop(0, n_pairs, body, (zero, zero))

    @pl.when(d == 0)
    def _():
        run(False)

    @pl.when(d == 1)
    def _():
        run(True)


def _s5_scan(s, apw, lw):
    n_cb, nb, ns = s.shape
    glp = ns // 4
    nq = glp // lw
    blocks = 2 * _nbytes((nb, lw), F32) + 2 * _nbytes((nb, lw), BF16) + 2 * _nbytes((8, lw), F32)
    out = jax.ShapeDtypeStruct((n_cb, nb, 2 * glp), BF16)

    def s_spec(ri):
        return pl.BlockSpec((None, nb, lw), lambda c, d, q: (c, 0, (2 * d + ri) * nq + q))

    def ap_spec(ri):
        return pl.BlockSpec((None, None, None, 8, lw), lambda c, d, q: (c, d, ri, 0, q))

    x_spec = pl.BlockSpec((None, nb, lw), lambda c, d, q: (c, 0, d * nq + q))
    return pl.pallas_call(
        functools.partial(_s5_scan_kernel, n_pairs=nb // BF16_ROWS),
        out_shape=(out, out),
        grid=(n_cb, 2, nq),
        in_specs=[s_spec(0), s_spec(1), ap_spec(0), ap_spec(1)],
        out_specs=(x_spec, x_spec),
        compiler_params=_compiler_params(("parallel", "parallel", "parallel"), blocks),
        name="s5_scan",
    )(s, s, apw, apw)


def _s5_out_kernel(u_ref, xre_ref, xim_ref, wi_ref, wcre_ref, wcim_ref, y_ref, lhs_ref, *, r, tn):
    j = pl.program_id(2)
    pieces = u_ref.shape[0]

    @pl.when(j == 0)
    def _():
        _fold_time_blocks(u_ref, lhs_ref, r)

    y = _dot(lhs_ref[...], wi_ref[...])
    y = y + _dot(xre_ref[...], wcre_ref[...])
    y = y + _dot(xim_ref[...], wcim_ref[...])

    per = tn // (pieces * LANES)
    for jj in range(T_BLK // per):
        @pl.when(j == jj)
        def _():
            for k in range(per):
                for h in range(pieces):
                    col = (k * pieces + h) * LANES
                    y_ref[h, pl.ds(jj * per + k, r, stride=T_BLK), :] = y[:, col:col + LANES]


def _s5_out(u, x_re, x_im, w_i, w_cre, w_cim, r, tn):
    n_cb, kw, _ = w_i.shape
    n = u.shape[1]
    nb = n // T_BLK
    ppc = u.shape[0] // n_cb
    kx = x_re.shape[-1]
    assert tn % (ppc * LANES) == 0 and kw % tn == 0
    blocks = (2 * _nbytes((ppc, T_BLK * r, LANES), F32) + 2 * _nbytes((r, kx), BF16) + _nbytes((kw, tn), BF16)
              + 2 * _nbytes((kx, tn), BF16) + _nbytes((r, tn), F32))
    scratch = _nbytes((r, kw), BF16)
    return pl.pallas_call(
        functools.partial(_s5_out_kernel, r=r, tn=tn),
        out_shape=jax.ShapeDtypeStruct(u.shape, F32),
        grid=(n_cb, nb // r, kw // tn),
        in_specs=[pl.BlockSpec((ppc, T_BLK * r, LANES), lambda c, i, j: (c, i, 0)),
                  pl.BlockSpec((None, r, kx), lambda c, i, j: (c, i, 0)),
                  pl.BlockSpec((None, r, kx), lambda c, i, j: (c, i, 0)),
                  pl.BlockSpec((None, kw, tn), lambda c, i, j: (c, 0, j)),
                  pl.BlockSpec((None, kx, tn), lambda c, i, j: (c, 0, j)),
                  pl.BlockSpec((None, kx, tn), lambda c, i, j: (c, 0, j))],
        out_specs=pl.BlockSpec((ppc, T_BLK * r, LANES), lambda c, i, j: (c, i, 0)),
        scratch_shapes=[pltpu.VMEM((r, kw), BF16)],
        compiler_params=_compiler_params(("parallel", "parallel", "arbitrary"), blocks, scratch),
        name="s5_out",
    )(u, x_re, x_im, w_i, w_cre, w_cim)


def _glu_merge_kernel(y_ref, u_ref, d_ref, w1_ref, w2_ref, pa_ref, gb_ref, m_ref, s_ref):
    @pl.when(pl.program_id(1) == 0)
    def _():
        for c in range(y_ref.shape[0]):
            s = jax.nn.gelu(y_ref[c] + d_ref[c] * u_ref[c])
            s_ref[:, c * LANES:(c + 1) * LANES] = s.astype(s_ref.dtype)

    a = s_ref[...]
    y_b = _dot(a, w1_ref[...]) * jax.nn.sigmoid(_dot(a, w2_ref[...]))
    m_ref[...] = (pa_ref[...].astype(F32) + gb_ref[...].astype(F32) * y_b).astype(m_ref.dtype)


def _glu_merge(y, u, d_skip, w_glu, layer, pa, gates, d_model, tm, tn):
    n_pc, n, _ = y.shape
    k = n_pc * LANES
    nb = d_model // tn
    blocks = (2 * _nbytes((n_pc, tm, LANES), F32) + 2 * _nbytes((k, tn), BF16) + 3 * _nbytes((tm, tn), BF16))
    scratch = _nbytes((tm, k), BF16)
    return pl.pallas_call(
        _glu_merge_kernel,
        out_shape=jax.ShapeDtypeStruct((n, d_model), BF16),
        grid=(n // tm, nb),
        in_specs=[pl.BlockSpec((n_pc, tm, LANES), lambda i, j: (0, i, 0)),
                  pl.BlockSpec((n_pc, tm, LANES), lambda i, j: (0, i, 0)),
                  pl.BlockSpec((n_pc, 1, LANES), lambda i, j: (0, 0, 0)),
                  _w_spec(k, tn, layer, 0),
                  _w_spec(k, tn, layer, nb),
                  pl.BlockSpec((tm, tn), lambda i, j: (i, j)),
                  pl.BlockSpec((tm, tn), lambda i, j: (i, nb + j))],
        out_specs=pl.BlockSpec((tm, tn), lambda i, j: (i, j)),
        scratch_shapes=[pltpu.VMEM((tm, k), BF16)],
        compiler_params=_compiler_params(("parallel", "arbitrary"), blocks, scratch),
        name="glu_merge",
    )(y, u, d_skip, w_glu, w_glu, pa, gates)


def _ffn_up_kernel(h_ref, hp_ref, hn_ref, wg_ref, wv_ref, cg_ref, cv_ref, act_ref, lhs_ref, *, tm):
    i = pl.program_id(0)
    j = pl.program_id(1)
    last = pl.num_programs(0) - 1
    halo = BF16_ROWS

    @pl.when(j == 0)
    def _():
        row = lax.broadcasted_iota(jnp.int32, hp_ref.shape, 0)
        after = jnp.where(jnp.logical_and(row == 0, i < last), hn_ref[...].astype(F32), 0.0)
        edge = jnp.where(jnp.logical_and(row == halo - 1, i > 0), hp_ref[...].astype(F32), after)
        lhs_ref[0:tm, :] = h_ref[...]
        lhs_ref[tm:, :] = edge.astype(lhs_ref.dtype)

    a = lhs_ref[...]
    rows = tm + halo

    def conv(w_ref, cw_ref):
        u = _dot(a, w_ref[...])
        cw = cw_ref[...]
        up = pltpu.roll(u, 1, 0)[0:tm]
        dn = pltpu.roll(u, rows - 1, 0)[0:tm]
        return up * cw[0:1, :] + u[0:tm] * cw[1:2, :] + dn * cw[2:3, :]

    act_ref[...] = (jax.nn.gelu(conv(wg_ref, cg_ref)) * conv(wv_ref, cv_ref)).astype(act_ref.dtype)


def _ffn_up(hn, w_up, layer, conv_w, d_ff, tm, tn, cast=None):
    n, k = hn.shape
    nb = d_ff // tn
    hb = tm // BF16_ROWS
    n_hb = n // BF16_ROWS
    blocks = (_nbytes((tm + 2 * BF16_ROWS, k), BF16) + 2 * _nbytes((k, tn), BF16) + 2 * _nbytes((3, tn), F32)
              + _nbytes((tm, tn), BF16))
    scratch = _nbytes((tm + BF16_ROWS, k), BF16)
    outs = _call_with_cast(
        functools.partial(_ffn_up_kernel, tm=tm), cast,
        grid=(n // tm, nb),
        in_specs=[pl.BlockSpec((tm, k), lambda i, j: (i, 0)),
                  pl.BlockSpec((BF16_ROWS, k), lambda i, j: (jnp.maximum(i * hb - 1, 0), 0)),
                  pl.BlockSpec((BF16_ROWS, k), lambda i, j: (jnp.minimum((i + 1) * hb, n_hb - 1), 0)),
                  _w_spec(k, tn, layer, 0),
                  _w_spec(k, tn, layer, nb),
                  pl.BlockSpec((3, tn), lambda i, j: (0, j)),
                  pl.BlockSpec((3, tn), lambda i, j: (0, nb + j))],
        out_specs=[pl.BlockSpec((tm, tn), lambda i, j: (i, j))],
        out_shape=[jax.ShapeDtypeStruct((n, d_ff), BF16)],
        operands=(hn, hn, hn, w_up, w_up, conv_w, conv_w),
        scratch_shapes=[pltpu.VMEM((tm + BF16_ROWS, k), BF16)],
        block_bytes=blocks, scratch_bytes=scratch, name="ffn_up")
    return outs[0] if cast is None else outs


def kernel(x, meta_tokens, norm_mix, w_in, conv_a_w, w_a, ssm_lambda_re, ssm_lambda_im, ssm_log_step,
           ssm_b_re, ssm_b_im, ssm_c_re, ssm_c_im, ssm_d, w_glu, w_out, norm_ffn, w_up, conv_ffn_w,
           w_down, norm_final):
    bsz, seq, d_model = x.shape
    n_meta = meta_tokens.shape[0]
    depth = w_in.shape[0]
    d_conv = w_a.shape[1]
    d_ssm = ssm_d.shape[1]
    d_ff = w_down.shape[1]
    n_groups, p_state, h_grp = ssm_b_re.shape[2:]
    assert w_in.shape[2] == 3 * d_conv + d_ssm + 2 * d_model and n_groups * h_grp == d_ssm

    cbw = min(S5_CB_WIDTH, d_ssm)
    n_cb = d_ssm // cbw
    gl = cbw // h_grp
    assert n_cb * cbw == d_ssm and gl * h_grp == cbw

    n_true = n_meta + seq
    row_align = T_BLK * BF16_ROWS
    n_pad = _round_up(n_true, 10 * row_align) if n_true >= 40 * row_align else _round_up(n_true, row_align)
    nb = n_pad // T_BLK

    tm = _pick_tile(n_pad, TM_TARGET, BF16_ROWS)
    tm_up = _pick_tile(n_pad, TM_FFN_UP_TARGET, BF16_ROWS)
    tm_half = _pick_tile(n_pad, TM_HALF_TARGET, BF16_ROWS)
    tm_ew = _pick_tile(n_pad, TM_EW_TARGET, BF16_ROWS)
    r_s5 = _pick_tile(nb, R_S5_TARGET, BF16_ROWS)
    tn = lambda n, target=512: _pick_tile(n, target, 128)

    prep = _s5_prep(ssm_lambda_re, ssm_lambda_im, ssm_log_step, ssm_b_re, ssm_b_im, ssm_c_re, ssm_c_im)
    d_skip = ssm_d.astype(F32).reshape(depth, d_ssm // LANES, 1, LANES)

    residual = functools.partial(_epi_residual, n_valid=n_true)
    tile_spec = lambda t, rows=tm: pl.BlockSpec((rows, t), lambda i, j: (i, j))
    u_off = 3 * d_conv
    gate_off = u_off + d_ssm

    w_a_b, w_glu_b, w_out_b = (w.astype(BF16) for w in (w_a, w_glu, w_out))
    w_in_first = w_in[0].astype(BF16)

    outs = []
    for b in range(bsz):
        h_res = jnp.concatenate([meta_tokens.astype(F32), x[b].astype(F32),
                                 jnp.zeros((n_pad - n_true, d_model), F32)], axis=0)
        w_in_l = w_in_first
        for l in range(depth):
            w_s, w_cre, w_cim, w_i, apw = _s5_weights([t[l] for t in prep], n_cb, gl, h_grp, p_state)

            hn = _rmsnorm(h_res, norm_mix[l], BF16, tm_ew)
            bg, cv, w_down_l = _inproj_conv(hn, w_in_l, None, d_conv, tm_half, tn(d_conv), cast=(w_down, l))
            u = _mm(hn, w_in_l, None, col0=u_off, n_cols=d_ssm, tm=tm, tn=cbw, epilogue=_epi_pieces,
                    out_shape=jax.ShapeDtypeStruct((d_ssm // LANES, n_pad, LANES), F32),
                    out_spec=pl.BlockSpec((cbw // LANES, tm, LANES), lambda i, j: (j, i, 0)), name="inproj_u")
            t_g = tn(2 * d_model)
            gates, w_up_l = _mm(hn, w_in_l, None, col0=gate_off, n_cols=2 * d_model, tm=tm, tn=t_g,
                                epilogue=_epi_sigmoid,
                                out_shape=jax.ShapeDtypeStruct((n_pad, 2 * d_model), BF16),
                                out_spec=tile_spec(t_g), name="inproj_gates", cast=(w_up, l))

            pa = _branch_a(bg, cv, conv_a_w[l].astype(F32), w_a_b, l, gates, d_model, tm, tn(d_model, 1024))
            t_d = tn(d_model)

            states = _s5_states(u, w_s, _pick_tile(nb, R_S5_STATES_TARGET, BF16_ROWS), tn(w_s.shape[-1]))
            x_re, x_im = _s5_scan(states, apw, tn(gl * p_state, SCAN_LANES_TARGET))
            y = _s5_out(u, x_re, x_im, w_i, w_cre, w_cim, r_s5, tn(T_BLK * cbw))

            merged = _glu_merge(y, u, d_skip[l], w_glu_b, l, pa, gates, d_model, tm_half, tn(d_model, 1024))
            h_res = _mm(merged, w_out_b, l, col0=0, n_cols=d_model, tm=tm, tn=t_d, epilogue=residual,
                        extras=(h_res,), extra_specs=(tile_spec(t_d),),
                        out_shape=jax.ShapeDtypeStruct((n_pad, d_model), F32), out_spec=tile_spec(t_d),
                        name="out_proj")

            hn = _rmsnorm(h_res, norm_ffn[l], BF16, tm_ew)
            if l + 1 < depth:
                act, w_in_l = _ffn_up(hn, w_up_l, None, conv_ffn_w[l].astype(F32), d_ff, tm_up, tn(d_ff),
                                      cast=(w_in, l + 1))
            else:
                act = _ffn_up(hn, w_up_l, None, conv_ffn_w[l].astype(F32), d_ff, tm_up, tn(d_ff))
            t_o = tn(d_model, 256)
            h_res = _mm(act, w_down_l, None, col0=0, n_cols=d_model, tm=tm_half, tn=t_o, epilogue=residual,
                        extras=(h_res,), extra_specs=(tile_spec(t_o, tm_half),),
                        out_shape=jax.ShapeDtypeStruct((n_pad, d_model), F32), out_spec=tile_spec(t_o, tm_half),
                        name="ffn_down")

        outs.append(_final_norm(h_res, norm_final, n_meta, seq, x.dtype, _pick_tile(seq, TM_EW_TARGET, n_meta)))
    return jnp.stack(outs, axis=0)
```

```python
import functools

import numpy as np
import jax
import jax.numpy as jnp
from jax import lax
from jax.experimental import pallas as pl
from jax.experimental.pallas import tpu as pltpu

F32 = jnp.float32
BF16 = jnp.bfloat16
EPS = 1e-6

T_BLK = 8
S5_CB_WIDTH = 256
BF16_ROWS = 16
LANES = 128
VMEM_CAP_BYTES = 60000 * 1024
VMEM_TEMP_BYTES = 20 * 1024 * 1024

TM_TARGET = 1280
TM_FFN_UP_TARGET = 832
TM_HALF_TARGET = 640
R_S5_STATES_TARGET = 1040
TM_EW_TARGET = 320
R_S5_TARGET = 416
SCAN_LANES_TARGET = 256


def _round_up(n, m):
    return (n + m - 1) // m * m


def _pick_tile(n, target, align):
    best = 0
    for t in range(align, min(n, target) + 1, align):
        if n % t == 0:
            best = t
    assert best > 0, (n, target, align)
    return best


def _nbytes(shape, dtype):
    return int(np.prod(shape)) * jnp.dtype(dtype).itemsize


def _compiler_params(semantics, block_bytes, scratch_bytes=0):
    est = 2 * block_bytes + scratch_bytes + VMEM_TEMP_BYTES
    return pltpu.CompilerParams(dimension_semantics=semantics,
                                vmem_limit_bytes=int(min(VMEM_CAP_BYTES, est)))


def _dot(a, b):
    return jnp.dot(a, b, preferred_element_type=F32)


def _sigmoid(x):
    return 0.5 * (1.0 + jnp.tanh(0.5 * x))


def _rmsnorm_kernel(x_ref, g_ref, o_ref):
    x = x_ref[...]
    ms = jnp.mean(x * x, axis=-1, keepdims=True)
    o_ref[...] = (x * lax.rsqrt(ms + EPS) * g_ref[...]).astype(o_ref.dtype)


def _rmsnorm(x, g, out_dtype, tm):
    n, d = x.shape
    blocks = _nbytes((tm, d), F32) + _nbytes((tm, d), out_dtype)
    return pl.pallas_call(
        _rmsnorm_kernel,
        out_shape=jax.ShapeDtypeStruct((n, d), out_dtype),
        grid=(n // tm,),
        in_specs=[pl.BlockSpec((tm, d), lambda i: (i, 0)),
                  pl.BlockSpec((1, d), lambda i: (0, 0))],
        out_specs=pl.BlockSpec((tm, d), lambda i: (i, 0)),
        compiler_params=_compiler_params(("parallel",), blocks),
        name="rmsnorm",
    )(x, g.reshape(1, d).astype(F32))


def _assemble_norm_kernel(meta_ref, prev_ref, x_ref, g_ref, h_ref, hn_ref, *, n_valid):
    i = pl.program_id(0)
    tm = h_ref.shape[0]
    n_meta = meta_ref.shape[0]
    head = jnp.where(i == 0, meta_ref[...], prev_ref[...])
    tile = jnp.concatenate([head, x_ref[0:tm - n_meta, :]], axis=0)
    rows = i * tm + lax.broadcasted_iota(jnp.int32, tile.shape, 0)
    tile = jnp.where(rows < n_valid, tile, 0.0)
    h_ref[...] = tile
    ms = jnp.mean(tile * tile, axis=-1, keepdims=True)
    hn_ref[...] = (tile * lax.rsqrt(ms + EPS) * g_ref[...]).astype(hn_ref.dtype)


def _assemble_norm(meta, x, g, n_pad, tm):
    n_meta, d = meta.shape
    seq = x.shape[0]
    assert n_meta % 8 == 0 and tm % n_meta == 0 and seq % n_meta == 0 and n_pad % tm == 0
    per = tm // n_meta
    last_x = pl.cdiv(seq, tm) - 1
    last_prev = seq // n_meta - 1
    blocks = _nbytes((tm + 2 * n_meta, d), F32) + _nbytes((tm, d), F32) + _nbytes((tm, d), BF16)
    return pl.pallas_call(
        functools.partial(_assemble_norm_kernel, n_valid=n_meta + seq),
        out_shape=(jax.ShapeDtypeStruct((n_pad, d), F32), jax.ShapeDtypeStruct((n_pad, d), BF16)),
        grid=(n_pad // tm,),
        in_specs=[pl.BlockSpec((n_meta, d), lambda i: (0, 0)),
                  pl.BlockSpec((n_meta, d), lambda i: (jnp.clip(i * per - 1, 0, last_prev), 0)),
                  pl.BlockSpec((tm, d), lambda i: (jnp.minimum(i, last_x), 0)),
                  pl.BlockSpec((1, d), lambda i: (0, 0))],
        out_specs=(pl.BlockSpec((tm, d), lambda i: (i, 0)), pl.BlockSpec((tm, d), lambda i: (i, 0))),
        compiler_params=_compiler_params(("parallel",), blocks),
        name="assemble_norm",
    )(meta.astype(F32), x.astype(F32), x.astype(F32), g.reshape(1, d).astype(F32))


def _final_norm_kernel(a_ref, b_ref, g_ref, o_ref, *, skip):
    x = jnp.concatenate([a_ref[skip:, :], b_ref[...]], axis=0)
    ms = jnp.mean(x * x, axis=-1, keepdims=True)
    o_ref[...] = (x * lax.rsqrt(ms + EPS) * g_ref[...]).astype(o_ref.dtype)


def _final_norm(x, g, skip, n_out, out_dtype, tm):
    n, d = x.shape
    assert skip % 8 == 0 and tm % skip == 0 and n_out % tm == 0 and n >= n_out + skip
    per = tm // skip
    blocks = _nbytes((tm + skip, d), F32) + _nbytes((tm, d), out_dtype)
    return pl.pallas_call(
        functools.partial(_final_norm_kernel, skip=skip),
        out_shape=jax.ShapeDtypeStruct((n_out, d), out_dtype),
        grid=(n_out // tm,),
        in_specs=[pl.BlockSpec((tm, d), lambda i: (i, 0)),
                  pl.BlockSpec((skip, d), lambda i: ((i + 1) * per, 0)),
                  pl.BlockSpec((1, d), lambda i: (0, 0))],
        out_specs=pl.BlockSpec((tm, d), lambda i: (i, 0)),
        compiler_params=_compiler_params(("parallel",), blocks),
        name="final_norm",
    )(x, x, g.reshape(1, d).astype(F32))


def _mm_kernel(*refs, n_extra, epilogue, tm):
    a_ref, w_ref = refs[0], refs[1]
    extras = refs[2:2 + n_extra]
    o_ref = refs[2 + n_extra]
    acc = _dot(a_ref[...], w_ref[...])
    row0 = pl.program_id(0) * tm
    o_ref[...] = epilogue(acc, row0, *[e[...] for e in extras]).astype(o_ref.dtype)


def _w_spec(k, tn, layer, jb0):
    if layer is None:
        return pl.BlockSpec((k, tn), lambda i, j: (0, jb0 + j))
    return pl.BlockSpec((None, k, tn), lambda i, j: (layer, 0, jb0 + j))


def _call_with_cast(kernel_fn, cast, *, grid, in_specs, out_specs, out_shape, operands, scratch_shapes=(),
                    block_bytes, scratch_bytes=0, name):
    if cast is None:
        return pl.pallas_call(
            kernel_fn, out_shape=tuple(out_shape), grid=grid, in_specs=list(in_specs), out_specs=tuple(out_specs),
            scratch_shapes=list(scratch_shapes),
            compiler_params=_compiler_params(("parallel", "arbitrary"), block_bytes, scratch_bytes), name=name,
        )(*operands)

    stack, layer = cast
    _, k, n = stack.shape
    n_steps = grid[0] * grid[1]
    options = []
    for rows in range(BF16_ROWS, k + 1, BF16_ROWS):
        if k % rows == 0 and k // rows <= n_steps:
            splits = max(c for c in range(1, n_steps // (k // rows) + 1) if n % (c * LANES) == 0)
            options.append((rows * (n // splits), rows, splits))
    _, rows, splits = min(options)
    cols = n // splits
    last_chunk = (k // rows) * splits - 1
    n_in, n_out = len(in_specs), len(out_specs)

    def chunk(i, j):
        q = jnp.minimum(i * grid[1] + j, last_chunk)
        return q // splits, q % splits

    def body(*refs):
        src_ref, dst_ref = refs[n_in], refs[n_in + 1 + n_out]
        dst_ref[...] = src_ref[...].astype(dst_ref.dtype)
        kernel_fn(*refs[:n_in], *refs[n_in + 1:n_in + 1 + n_out], *refs[n_in + 2 + n_out:])

    cast_bytes = _nbytes((rows, cols), F32) + _nbytes((rows, cols), BF16)
    return pl.pallas_call(
        body,
        out_shape=(*out_shape, jax.ShapeDtypeStruct((k, n), BF16)),
        grid=grid,
        in_specs=[*in_specs, pl.BlockSpec((None, rows, cols), lambda i, j: (layer, *chunk(i, j)))],
        out_specs=(*out_specs, pl.BlockSpec((rows, cols), lambda i, j: chunk(i, j))),
        scratch_shapes=list(scratch_shapes),
        compiler_params=_compiler_params(("arbitrary", "arbitrary"), block_bytes + cast_bytes, scratch_bytes),
        name=name,
    )(*operands, stack)


def _mm(a, w, layer, *, col0, n_cols, tm, tn, epilogue, extras=(), extra_specs=(), out_shape, out_spec, name,
        cast=None):
    n, k = a.shape
    assert n % tm == 0 and n_cols % tn == 0 and col0 % tn == 0
    blocks = (_nbytes((tm, k), a.dtype) + _nbytes((k, tn), w.dtype) + _nbytes((tm, tn), F32)
              + sum(_nbytes((tm, tn), e.dtype) for e in extras))
    outs = _call_with_cast(
        functools.partial(_mm_kernel, n_extra=len(extras), epilogue=epilogue, tm=tm), cast,
        grid=(n // tm, n_cols // tn),
        in_specs=[pl.BlockSpec((tm, k), lambda i, j: (i, 0)),
                  _w_spec(k, tn, layer, col0 // tn),
                  *extra_specs],
        out_specs=[out_spec], out_shape=[out_shape], operands=(a, w, *extras), block_bytes=blocks, name=name)
    return outs[0] if cast is None else outs


def _epi_pieces(acc, row0):
    return jnp.stack([acc[:, h * LANES:(h + 1) * LANES] for h in range(acc.shape[1] // LANES)], axis=0)


def _epi_sigmoid(acc, row0):
    return _sigmoid(acc)


def _epi_residual(acc, row0, res, *, n_valid):
    rows = row0 + lax.broadcasted_iota(jnp.int32, acc.shape, 0)
    return jnp.where(rows < n_valid, res + acc, 0.0)


def _inproj_conv_kernel(a_ref, wb_ref, wc_ref, wv_ref, bg_ref, cv_ref):
    a = a_ref[...]
    bg_ref[...] = _dot(a, wb_ref[...]).astype(bg_ref.dtype)
    cv_ref[...] = (_dot(a, wc_ref[...]) * _dot(a, wv_ref[...])).astype(cv_ref.dtype)


def _inproj_conv(hn, w_in, layer, d_conv, tm, tn, cast=None):
    n, k = hn.shape
    nb = d_conv // tn
    blocks = _nbytes((tm, k), BF16) + 3 * _nbytes((k, tn), BF16) + 2 * _nbytes((tm, tn), BF16)
    out = jax.ShapeDtypeStruct((n, d_conv), BF16)
    return _call_with_cast(
        _inproj_conv_kernel, cast,
        grid=(n // tm, nb),
        in_specs=[pl.BlockSpec((tm, k), lambda i, j: (i, 0)),
                  _w_spec(k, tn, layer, 0),
                  _w_spec(k, tn, layer, nb),
                  _w_spec(k, tn, layer, 2 * nb)],
        out_specs=[pl.BlockSpec((tm, tn), lambda i, j: (i, j)),
                   pl.BlockSpec((tm, tn), lambda i, j: (i, j))],
        out_shape=[out, out], operands=(hn, w_in, w_in, w_in), block_bytes=blocks, name="inproj_conv")


def _shifted_rows(x, prev_row, next_row):
    t = x.shape[0]
    row = lax.broadcasted_iota(jnp.int32, x.shape, 0)
    up = jnp.where(row == 0, prev_row, pltpu.roll(x, 1, 0))
    dn = jnp.where(row == t - 1, next_row, pltpu.roll(x, t - 1, 0))
    return up, dn


def _branch_a_kernel(bg_ref, cv_ref, cvp_ref, cvn_ref, cw_ref, w_ref, gate_ref, o_ref, z_ref, *, chunk):
    i = pl.program_id(0)
    last = pl.num_programs(0) - 1

    @pl.when(pl.program_id(1) == 0)
    def _():
        for c0 in range(0, z_ref.shape[1], chunk):
            cols = slice(c0, c0 + chunk)
            x = cv_ref[:, cols].astype(F32)
            prev = cvp_ref[:, cols].astype(F32)[BF16_ROWS - 1:BF16_ROWS, :]
            nxt = cvn_ref[:, cols].astype(F32)[0:1, :]
            prev = jnp.where(i > 0, prev, 0.0)
            nxt = jnp.where(i < last, nxt, 0.0)
            up, dn = _shifted_rows(x, prev, nxt)
            w = cw_ref[:, cols]
            conv = up * w[0:1, :] + x * w[1:2, :] + dn * w[2:3, :]
            z_ref[:, cols] = (bg_ref[:, cols].astype(F32) * conv).astype(z_ref.dtype)

    o_ref[...] = (_dot(z_ref[...], w_ref[...]) * gate_ref[...].astype(F32)).astype(o_ref.dtype)


def _branch_a(bg, cv, conv_w, w_a, layer, gates, d_model, tm, tn):
    n, c = bg.shape
    hb = tm // BF16_ROWS
    n_hb = n // BF16_ROWS
    blocks = (2 * _nbytes((tm, c), BF16) + 2 * _nbytes((BF16_ROWS, c), BF16) + _nbytes((8, c), F32)
              + _nbytes((c, tn), BF16) + 2 * _nbytes((tm, tn), BF16))
    scratch = _nbytes((tm, c), BF16)
    return pl.pallas_call(
        functools.partial(_branch_a_kernel, chunk=_pick_tile(c, 512, LANES)),
        out_shape=jax.ShapeDtypeStruct((n, d_model), BF16),
        grid=(n // tm, d_model // tn),
        in_specs=[pl.BlockSpec((tm, c), lambda i, j: (i, 0)),
                  pl.BlockSpec((tm, c), lambda i, j: (i, 0)),
                  pl.BlockSpec((BF16_ROWS, c), lambda i, j: (jnp.maximum(i * hb - 1, 0), 0)),
                  pl.BlockSpec((BF16_ROWS, c), lambda i, j: (jnp.minimum((i + 1) * hb, n_hb - 1), 0)),
                  pl.BlockSpec((3, c), lambda i, j: (0, 0)),
                  _w_spec(c, tn, layer, 0),
                  pl.BlockSpec((tm, tn), lambda i, j: (i, j))],
        out_specs=pl.BlockSpec((tm, tn), lambda i, j: (i, j)),
        scratch_shapes=[pltpu.VMEM((tm, c), BF16)],
        compiler_params=_compiler_params(("parallel", "arbitrary"), blocks, scratch),
        name="branch_a",
    )(bg, cv, cv, cv, conv_w, w_a, gates)


def _cmul(ar, ai, br, bi):
    return ar * br - ai * bi, ar * bi + ai * br


def _s5_prep_kernel(lre_ref, lim_ref, lst_ref, bre_ref, bim_ref, cre_ref, cim_ref,
                    ere_ref, eim_ref, care_ref, caim_ref, klag_ref, apre_ref, apim_ref):
    lam_re, lam_im = lre_ref[...], lim_ref[...]
    dt = jnp.exp(lst_ref[...])
    mag = jnp.exp(lam_re * dt)
    a_re = mag * jnp.cos(lam_im * dt)
    a_im = mag * jnp.sin(lam_im * dt)
    nr, ni = a_re - 1.0, a_im
    den = lam_re * lam_re + lam_im * lam_im
    f_re = (nr * lam_re + ni * lam_im) / den
    f_im = (ni * lam_re - nr * lam_im) / den
    b_re, b_im = bre_ref[...], bim_ref[...]
    bb_re, bb_im = _cmul(f_re, f_im, b_re, b_im)
    c_re, c_im = cre_ref[...], cim_ref[...]

    pw = [(jnp.ones_like(a_re), jnp.zeros_like(a_im))]
    for _ in range(T_BLK):
        pw.append(_cmul(pw[-1][0], pw[-1][1], a_re, a_im))

    e_re, e_im, ca_re, ca_im = [], [], [], []
    for k in range(T_BLK):
        er, ei = _cmul(pw[k][0], pw[k][1], bb_re, bb_im)
        e_re.append(er)
        e_im.append(ei)
        cr, ci = _cmul(pw[k + 1][0], pw[k + 1][1], c_re, c_im)
        ca_re.append(cr)
        ca_im.append(-ci)
    e_re = jnp.concatenate(e_re, axis=1)
    e_im = jnp.concatenate(e_im, axis=1)
    ere_ref[...] = e_re
    eim_ref[...] = e_im
    care_ref[...] = jnp.concatenate(ca_re, axis=1)
    caim_ref[...] = jnp.concatenate(ca_im, axis=1)

    dn = (((2,), (2,)), ((0,), (0,)))
    klag_ref[...] = (lax.dot_general(c_re, e_re, dn, precision=lax.Precision.HIGHEST,
                                     preferred_element_type=F32)
                     - lax.dot_general(c_im, e_im, dn, precision=lax.Precision.HIGHEST,
                                       preferred_element_type=F32))

    q_re, q_im = pw[T_BLK]
    r_re, r_im = q_re, q_im
    for m in range(8):
        apre_ref[m] = r_re
        apim_ref[m] = r_im
        r_re, r_im = _cmul(r_re, r_im, q_re, q_im)


def _s5_prep(lam_re, lam_im, log_step, b_re, b_im, c_re, c_im):
    depth, _, g, p = lam_re.shape
    h = b_re.shape[-1]
    lead = (depth, 2)
    b_re_t = jnp.swapaxes(b_re, -1, -2).astype(F32)
    b_im_t = jnp.swapaxes(b_im, -1, -2).astype(F32)

    def spec(*tail):
        zeros = (0,) * len(tail)
        return pl.BlockSpec((None, None) + tail, lambda l, d: (l, d) + zeros)

    th = T_BLK * h
    outs = (jax.ShapeDtypeStruct(lead + (g, th, p), F32),) * 4 + (
        jax.ShapeDtypeStruct(lead + (g, h, th), F32),
        jax.ShapeDtypeStruct(lead + (8, g, 1, p), F32),
        jax.ShapeDtypeStruct(lead + (8, g, 1, p), F32))
    blocks = 4 * _nbytes((g, th, 128), F32) + 8 * _nbytes((g, h, 128), F32)
    return pl.pallas_call(
        _s5_prep_kernel,
        out_shape=outs,
        grid=lead,
        in_specs=[spec(g, 1, p), spec(g, 1, p), spec(g, 1, 1), spec(g, h, p), spec(g, h, p),
                  spec(g, h, p), spec(g, h, p)],
        out_specs=(spec(g, th, p),) * 4 + (spec(g, h, th), spec(8, g, 1, p), spec(8, g, 1, p)),
        compiler_params=_compiler_params(("parallel", "parallel"), blocks, 16 * _nbytes((g, th, 128), F32)),
        name="s5_prep",
    )(lam_re.astype(F32)[:, :, :, None, :], lam_im.astype(F32)[:, :, :, None, :],
      log_step.astype(F32)[..., None, None], b_re_t, b_im_t, c_re.astype(F32), c_im.astype(F32))


def _block_diag_kernel(c_ref, e_ref, rg_ref, cg_ref, o_ref):
    spread = _dot(c_ref[...].astype(BF16), e_ref[...])
    o_ref[...] = jnp.where(rg_ref[...] == cg_ref[...], spread, 0.0).astype(o_ref.dtype)


def _block_diag(compact, gl, row_inner, col_inner, name):
    n_cb, rows, kc = compact.shape
    n_out = kc * gl
    tr = _pick_tile(rows, 512, BF16_ROWS)
    col = np.arange(n_out)
    spread = np.arange(kc)[:, None] == (col // (gl * col_inner)) * col_inner + col % col_inner
    row_group = ((np.arange(rows) // row_inner) % gl).astype(np.int32)[:, None]
    col_group = ((col // col_inner) % gl).astype(np.int32)[None, :]
    blocks = (_nbytes((tr, kc), F32) + _nbytes((kc, n_out), BF16) + _nbytes((tr, LANES), jnp.int32)
              + _nbytes((8, n_out), jnp.int32) + _nbytes((tr, n_out), BF16))
    return pl.pallas_call(
        _block_diag_kernel,
        out_shape=jax.ShapeDtypeStruct((n_cb, rows, n_out), BF16),
        grid=(n_cb, rows // tr),
        in_specs=[pl.BlockSpec((None, tr, kc), lambda c, i: (c, i, 0)),
                  pl.BlockSpec((kc, n_out), lambda c, i: (0, 0)),
                  pl.BlockSpec((tr, 1), lambda c, i: (i, 0)),
                  pl.BlockSpec((1, n_out), lambda c, i: (0, 0))],
        out_specs=pl.BlockSpec((None, tr, n_out), lambda c, i: (c, i, 0)),
        compiler_params=_compiler_params(("parallel", "parallel"), blocks),
        name=name,
    )(compact, jnp.asarray(spread, BF16), jnp.asarray(row_group), jnp.asarray(col_group))


def _s5_weights(prep, n_cb, gl, h, p):
    e_re, e_im, ca_re, ca_im, klag, ap_re, ap_im = prep
    t = T_BLK

    def split(x):
        return x.reshape(2, n_cb * gl, t, h, p)

    e = jnp.stack([split(e_re), split(e_im)], axis=1)
    e_sel = jnp.stack([e[0, :, :, ::-1], e[1]], axis=0)
    e_sel = e_sel.reshape(2, 2, n_cb, gl, t, h, p)
    e_sel = jnp.transpose(e_sel, (2, 4, 3, 5, 0, 1, 6))
    w_s = _block_diag(e_sel.reshape(n_cb, t * gl * h, 4 * p), gl, h, p, "s5_w_states")

    def carry(ca, name):
        ca = split(ca)
        sel = jnp.stack([ca[0], ca[1, :, ::-1]], axis=0)
        sel = sel.reshape(2, n_cb, gl, t, h, p)
        sel = jnp.transpose(sel, (1, 0, 2, 5, 3, 4))
        return _block_diag(sel.reshape(n_cb, 2 * gl * p, t * h), gl, p, h, name)

    w_cre, w_cim = carry(ca_re, "s5_w_carry_re"), carry(ca_im, "s5_w_carry_im")

    kl = klag.reshape(2, n_cb * gl, h, t, h)
    sig = np.arange(t)[:, None]
    tau = np.arange(t)[None, :]
    kf = kl[0][:, :, np.clip(tau - sig, 0, t - 1), :] * jnp.asarray(tau >= sig, F32)[None, None, :, :, None]
    kb = kl[1][:, :, np.clip(sig - tau, 0, t - 1), :] * jnp.asarray(sig >= tau, F32)[None, None, :, :, None]
    kst = (kf + kb).reshape(n_cb, gl, h, t, t, h)
    kst = jnp.transpose(kst, (0, 3, 1, 5, 4, 2))
    w_i = _block_diag(kst.reshape(n_cb, t * gl * h, t * h), gl, h, h, "s5_w_intra")

    def lanes(ap):
        ap = jnp.stack([ap[0], ap[1, ::-1]], axis=0)
        ap = ap.reshape(2, 8, n_cb, gl * p)
        return jnp.transpose(ap, (2, 0, 1, 3))
    apw = jnp.stack([lanes(ap_re), lanes(ap_im)], axis=2)

    return w_s, w_cre, w_cim, w_i, apw


def _fold_time_blocks(u_ref, lhs_ref, r):
    pieces = u_ref.shape[0]
    for tau in range(T_BLK):
        for h in range(pieces):
            col = (tau * pieces + h) * LANES
            lhs_ref[:, col:col + LANES] = u_ref[h, pl.ds(tau, r, stride=T_BLK), :].astype(lhs_ref.dtype)


def _s5_states_kernel(u_ref, w_ref, s_ref, lhs_ref, *, r):
    @pl.when(pl.program_id(2) == 0)
    def _():
        _fold_time_blocks(u_ref, lhs_ref, r)

    s_ref[...] = _dot(lhs_ref[...], w_ref[...])


def _s5_states(u, w_s, r, tn):
    n_cb, kw, ns = w_s.shape
    n = u.shape[1]
    nb = n // T_BLK
    ppc = u.shape[0] // n_cb
    blocks = _nbytes((ppc, T_BLK * r, LANES), F32) + _nbytes((kw, tn), BF16) + _nbytes((r, tn), F32)
    scratch = _nbytes((r, kw), BF16)
    return pl.pallas_call(
        functools.partial(_s5_states_kernel, r=r),
        out_shape=jax.ShapeDtypeStruct((n_cb, nb, ns), F32),
        grid=(n_cb, nb // r, ns // tn),
        in_specs=[pl.BlockSpec((ppc, T_BLK * r, LANES), lambda c, i, j: (c, i, 0)),
                  pl.BlockSpec((None, kw, tn), lambda c, i, j: (c, 0, j))],
        out_specs=pl.BlockSpec((None, r, tn), lambda c, i, j: (c, i, j)),
        scratch_shapes=[pltpu.VMEM((r, kw), BF16)],
        compiler_params=_compiler_params(("parallel", "parallel", "arbitrary"), blocks, scratch),
        name="s5_states",
    )(u, w_s)


def _s5_scan_kernel(sre_ref, sim_ref, apre_ref, apim_ref, xre_ref, xim_ref, *, n_pairs):
    d = pl.program_id(1)
    lw = sre_ref.shape[-1]
    ap_re, ap_im = apre_ref[...], apim_ref[...]
    row = lax.broadcasted_iota(jnp.int32, (8, lw), 0)

    def run(reverse):
        steps = []
        for shift in (1, 2, 4):
            src = (8 - shift) if reverse else (shift - 1)
            keep = (row < 8 - shift) if reverse else (row >= shift)
            steps.append((8 - shift if reverse else shift,
                          jnp.where(keep, ap_re[src:src + 1, :], 0.0),
                          jnp.where(keep, ap_im[src:src + 1, :], 0.0)))
        edge = 7 if reverse else 0
        out_row = 0 if reverse else 7
        nb_shift = 7 if reverse else 1

        def tile(r0, c_re, c_im):
            x_re = sre_ref[pl.ds(r0, 8), :]
            x_im = sim_ref[pl.ds(r0, 8), :]
            for amount, m_re, m_im in steps:
                s_re = pltpu.roll(x_re, amount, 0)
                s_im = pltpu.roll(x_im, amount, 0)
                x_re, x_im = x_re + m_re * s_re - m_im * s_im, x_im + m_re * s_im + m_im * s_re
            inc_re = x_re + ap_re * c_re - ap_im * c_im
            inc_im = x_im + ap_re * c_im + ap_im * c_re
            e_re = jnp.where(row == edge, c_re, pltpu.roll(inc_re, nb_shift, 0))
            e_im = jnp.where(row == edge, c_im, pltpu.roll(inc_im, nb_shift, 0))
            return e_re, e_im, inc_re[out_row:out_row + 1, :], inc_im[out_row:out_row + 1, :]

        def body(k, carry):
            c_re, c_im = carry
            pair = (n_pairs - 1 - k) if reverse else k
            r0 = pl.multiple_of(pair * BF16_ROWS, BF16_ROWS)
            e_re, e_im = [None, None], [None, None]
            for half in ((1, 0) if reverse else (0, 1)):
                e_re[half], e_im[half], c_re, c_im = tile(pl.multiple_of(r0 + 8 * half, 8), c_re, c_im)
            xre_ref[pl.ds(r0, BF16_ROWS), :] = jnp.concatenate(e_re, axis=0).astype(xre_ref.dtype)
            xim_ref[pl.ds(r0, BF16_ROWS), :] = jnp.concatenate(e_im, axis=0).astype(xim_ref.dtype)
            return c_re, c_im

        zero = jnp.zeros((1, lw), F32)
        lax.fori_loop(0, n_pairs, body, (zero, zero))

    @pl.when(d == 0)
    def _():
        run(False)

    @pl.when(d == 1)
    def _():
        run(True)


def _s5_scan(s, apw, lw):
    n_cb, nb, ns = s.shape
    glp = ns // 4
    nq = glp // lw
    blocks = 2 * _nbytes((nb, lw), F32) + 2 * _nbytes((nb, lw), BF16) + 2 * _nbytes((8, lw), F32)
    out = jax.ShapeDtypeStruct((n_cb, nb, 2 * glp), BF16)

    def s_spec(ri):
        return pl.BlockSpec((None, nb, lw), lambda c, d, q: (c, 0, (2 * d + ri) * nq + q))

    def ap_spec(ri):
        return pl.BlockSpec((None, None, None, 8, lw), lambda c, d, q: (c, d, ri, 0, q))

    x_spec = pl.BlockSpec((None, nb, lw), lambda c, d, q: (c, 0, d * nq + q))
    return pl.pallas_call(
        functools.partial(_s5_scan_kernel, n_pairs=nb // BF16_ROWS),
        out_shape=(out, out),
        grid=(n_cb, 2, nq),
        in_specs=[s_spec(0), s_spec(1), ap_spec(0), ap_spec(1)],
        out_specs=(x_spec, x_spec),
        compiler_params=_compiler_params(("parallel", "parallel", "parallel"), blocks),
        name="s5_scan",
    )(s, s, apw, apw)


def _s5_out_kernel(u_ref, xre_ref, xim_ref, wi_ref, wcre_ref, wcim_ref, y_ref, lhs_ref, *, r, tn):
    j = pl.program_id(2)
    pieces = u_ref.shape[0]

    @pl.when(j == 0)
    def _():
        _fold_time_blocks(u_ref, lhs_ref, r)

    y = _dot(lhs_ref[...], wi_ref[...])
    y = y + _dot(xre_ref[...], wcre_ref[...])
    y = y + _dot(xim_ref[...], wcim_ref[...])

    per = tn // (pieces * LANES)
    for jj in range(T_BLK // per):
        @pl.when(j == jj)
        def _():
            for k in range(per):
                for h in range(pieces):
                    col = (k * pieces + h) * LANES
                    y_ref[h, pl.ds(jj * per + k, r, stride=T_BLK), :] = y[:, col:col + LANES]


def _s5_out(u, x_re, x_im, w_i, w_cre, w_cim, r, tn):
    n_cb, kw, _ = w_i.shape
    n = u.shape[1]
    nb = n // T_BLK
    ppc = u.shape[0] // n_cb
    kx = x_re.shape[-1]
    assert tn % (ppc * LANES) == 0 and kw % tn == 0
    blocks = (2 * _nbytes((ppc, T_BLK * r, LANES), F32) + 2 * _nbytes((r, kx), BF16) + _nbytes((kw, tn), BF16)
              + 2 * _nbytes((kx, tn), BF16) + _nbytes((r, tn), F32))
    scratch = _nbytes((r, kw), BF16)
    return pl.pallas_call(
        functools.partial(_s5_out_kernel, r=r, tn=tn),
        out_shape=jax.ShapeDtypeStruct(u.shape, F32),
        grid=(n_cb, nb // r, kw // tn),
        in_specs=[pl.BlockSpec((ppc, T_BLK * r, LANES), lambda c, i, j: (c, i, 0)),
                  pl.BlockSpec((None, r, kx), lambda c, i, j: (c, i, 0)),
                  pl.BlockSpec((None, r, kx), lambda c, i, j: (c, i, 0)),
                  pl.BlockSpec((None, kw, tn), lambda c, i, j: (c, 0, j)),
                  pl.BlockSpec((None, kx, tn), lambda c, i, j: (c, 0, j)),
                  pl.BlockSpec((None, kx, tn), lambda c, i, j: (c, 0, j))],
        out_specs=pl.BlockSpec((ppc, T_BLK * r, LANES), lambda c, i, j: (c, i, 0)),
        scratch_shapes=[pltpu.VMEM((r, kw), BF16)],
        compiler_params=_compiler_params(("parallel", "parallel", "arbitrary"), blocks, scratch),
        name="s5_out",
    )(u, x_re, x_im, w_i, w_cre, w_cim)


def _glu_merge_kernel(y_ref, u_ref, d_ref, w1_ref, w2_ref, pa_ref, gb_ref, m_ref, s_ref):
    @pl.when(pl.program_id(1) == 0)
    def _():
        for c in range(y_ref.shape[0]):
            s = jax.nn.gelu(y_ref[c] + d_ref[c] * u_ref[c])
            s_ref[:, c * LANES:(c + 1) * LANES] = s.astype(s_ref.dtype)

    a = s_ref[...]
    y_b = _dot(a, w1_ref[...]) * _sigmoid(_dot(a, w2_ref[...]))
    m_ref[...] = (pa_ref[...].astype(F32) + gb_ref[...].astype(F32) * y_b).astype(m_ref.dtype)


def _glu_merge(y, u, d_skip, w_glu, layer, pa, gates, d_model, tm, tn):
    n_pc, n, _ = y.shape
    k = n_pc * LANES
    nb = d_model // tn
    blocks = (2 * _nbytes((n_pc, tm, LANES), F32) + 2 * _nbytes((k, tn), BF16) + 3 * _nbytes((tm, tn), BF16))
    scratch = _nbytes((tm, k), BF16)
    return pl.pallas_call(
        _glu_merge_kernel,
        out_shape=jax.ShapeDtypeStruct((n, d_model), BF16),
        grid=(n // tm, nb),
        in_specs=[pl.BlockSpec((n_pc, tm, LANES), lambda i, j: (0, i, 0)),
                  pl.BlockSpec((n_pc, tm, LANES), lambda i, j: (0, i, 0)),
                  pl.BlockSpec((n_pc, 1, LANES), lambda i, j: (0, 0, 0)),
                  _w_spec(k, tn, layer, 0),
                  _w_spec(k, tn, layer, nb),
                  pl.BlockSpec((tm, tn), lambda i, j: (i, j)),
                  pl.BlockSpec((tm, tn), lambda i, j: (i, nb + j))],
        out_specs=pl.BlockSpec((tm, tn), lambda i, j: (i, j)),
        scratch_shapes=[pltpu.VMEM((tm, k), BF16)],
        compiler_params=_compiler_params(("parallel", "arbitrary"), blocks, scratch),
        name="glu_merge",
    )(y, u, d_skip, w_glu, w_glu, pa, gates)


def _ffn_up_kernel(h_ref, hp_ref, hn_ref, wg_ref, wv_ref, cg_ref, cv_ref, act_ref, lhs_ref, *, tm):
    i = pl.program_id(0)
    j = pl.program_id(1)
    last = pl.num_programs(0) - 1
    halo = BF16_ROWS

    @pl.when(j == 0)
    def _():
        row = lax.broadcasted_iota(jnp.int32, hp_ref.shape, 0)
        after = jnp.where(jnp.logical_and(row == 0, i < last), hn_ref[...].astype(F32), 0.0)
        edge = jnp.where(jnp.logical_and(row == halo - 1, i > 0), hp_ref[...].astype(F32), after)
        lhs_ref[0:tm, :] = h_ref[...]
        lhs_ref[tm:, :] = edge.astype(lhs_ref.dtype)

    a = lhs_ref[...]
    rows = tm + halo

    def conv(w_ref, cw_ref):
        u = _dot(a, w_ref[...])
        cw = cw_ref[...]
        up = pltpu.roll(u, 1, 0)[0:tm]
        dn = pltpu.roll(u, rows - 1, 0)[0:tm]
        return up * cw[0:1, :] + u[0:tm] * cw[1:2, :] + dn * cw[2:3, :]

    act_ref[...] = (jax.nn.gelu(conv(wg_ref, cg_ref)) * conv(wv_ref, cv_ref)).astype(act_ref.dtype)


def _ffn_up(hn, w_up, layer, conv_w, d_ff, tm, tn, cast=None):
    n, k = hn.shape
    nb = d_ff // tn
    hb = tm // BF16_ROWS
    n_hb = n // BF16_ROWS
    blocks = (_nbytes((tm + 2 * BF16_ROWS, k), BF16) + 2 * _nbytes((k, tn), BF16) + 2 * _nbytes((3, tn), F32)
              + _nbytes((tm, tn), BF16))
    scratch = _nbytes((tm + BF16_ROWS, k), BF16)
    outs = _call_with_cast(
        functools.partial(_ffn_up_kernel, tm=tm), cast,
        grid=(n // tm, nb),
        in_specs=[pl.BlockSpec((tm, k), lambda i, j: (i, 0)),
                  pl.BlockSpec((BF16_ROWS, k), lambda i, j: (jnp.maximum(i * hb - 1, 0), 0)),
                  pl.BlockSpec((BF16_ROWS, k), lambda i, j: (jnp.minimum((i + 1) * hb, n_hb - 1), 0)),
                  _w_spec(k, tn, layer, 0),
                  _w_spec(k, tn, layer, nb),
                  pl.BlockSpec((3, tn), lambda i, j: (0, j)),
                  pl.BlockSpec((3, tn), lambda i, j: (0, nb + j))],
        out_specs=[pl.BlockSpec((tm, tn), lambda i, j: (i, j))],
        out_shape=[jax.ShapeDtypeStruct((n, d_ff), BF16)],
        operands=(hn, hn, hn, w_up, w_up, conv_w, conv_w),
        scratch_shapes=[pltpu.VMEM((tm + BF16_ROWS, k), BF16)],
        block_bytes=blocks, scratch_bytes=scratch, name="ffn_up")
    return outs[0] if cast is None else outs


def kernel(x, meta_tokens, norm_mix, w_in, conv_a_w, w_a, ssm_lambda_re, ssm_lambda_im, ssm_log_step,
           ssm_b_re, ssm_b_im, ssm_c_re, ssm_c_im, ssm_d, w_glu, w_out, norm_ffn, w_up, conv_ffn_w,
           w_down, norm_final):
    bsz, seq, d_model = x.shape
    n_meta = meta_tokens.shape[0]
    depth = w_in.shape[0]
    d_conv = w_a.shape[1]
    d_ssm = ssm_d.shape[1]
    d_ff = w_down.shape[1]
    n_groups, p_state, h_grp = ssm_b_re.shape[2:]
    assert w_in.shape[2] == 3 * d_conv + d_ssm + 2 * d_model and n_groups * h_grp == d_ssm

    cbw = min(S5_CB_WIDTH, d_ssm)
    n_cb = d_ssm // cbw
    gl = cbw // h_grp
    assert n_cb * cbw == d_ssm and gl * h_grp == cbw

    n_true = n_meta + seq
    row_align = T_BLK * BF16_ROWS
    n_pad = _round_up(n_true, 10 * row_align) if n_true >= 40 * row_align else _round_up(n_true, row_align)
    nb = n_pad // T_BLK

    tm = _pick_tile(n_pad, TM_TARGET, BF16_ROWS)
    tm_up = _pick_tile(n_pad, TM_FFN_UP_TARGET, BF16_ROWS)
    tm_half = _pick_tile(n_pad, TM_HALF_TARGET, BF16_ROWS)
    tm_ew = _pick_tile(n_pad, TM_EW_TARGET, BF16_ROWS)
    r_s5 = _pick_tile(nb, R_S5_TARGET, BF16_ROWS)
    tn = lambda n, target=512: _pick_tile(n, target, 128)

    prep = _s5_prep(ssm_lambda_re, ssm_lambda_im, ssm_log_step, ssm_b_re, ssm_b_im, ssm_c_re, ssm_c_im)
    d_skip = ssm_d.astype(F32).reshape(depth, d_ssm // LANES, 1, LANES)

    residual = functools.partial(_epi_residual, n_valid=n_true)
    tile_spec = lambda t, rows=tm: pl.BlockSpec((rows, t), lambda i, j: (i, j))
    u_off = 3 * d_conv
    gate_off = u_off + d_ssm

    w_a_b, w_glu_b, w_out_b = (w.astype(BF16) for w in (w_a, w_glu, w_out))
    w_in_first = w_in[0].astype(BF16)

    outs = []
    for b in range(bsz):
        w_in_l = w_in_first
        for l in range(depth):
            w_s, w_cre, w_cim, w_i, apw = _s5_weights([t[l] for t in prep], n_cb, gl, h_grp, p_state)

            if l == 0:
                h_res, hn = _assemble_norm(meta_tokens, x[b], norm_mix[0], n_pad, tm_ew)
            else:
                hn = _rmsnorm(h_res, norm_mix[l], BF16, tm_ew)
            bg, cv, w_down_l = _inproj_conv(hn, w_in_l, None, d_conv, tm_half, tn(d_conv), cast=(w_down, l))
            u = _mm(hn, w_in_l, None, col0=u_off, n_cols=d_ssm, tm=tm, tn=cbw, epilogue=_epi_pieces,
                    out_shape=jax.ShapeDtypeStruct((d_ssm // LANES, n_pad, LANES), F32),
                    out_spec=pl.BlockSpec((cbw // LANES, tm, LANES), lambda i, j: (j, i, 0)), name="inproj_u")
            t_g = tn(2 * d_model)
            gates, w_up_l = _mm(hn, w_in_l, None, col0=gate_off, n_cols=2 * d_model, tm=tm, tn=t_g,
                                epilogue=_epi_sigmoid,
                                out_shape=jax.ShapeDtypeStruct((n_pad, 2 * d_model), BF16),
                                out_spec=tile_spec(t_g), name="inproj_gates", cast=(w_up, l))

            pa = _branch_a(bg, cv, conv_a_w[l].astype(F32), w_a_b, l, gates, d_model, tm, tn(d_model, 1024))
            t_d = tn(d_model)

            states = _s5_states(u, w_s, _pick_tile(nb, R_S5_STATES_TARGET, BF16_ROWS), tn(w_s.shape[-1]))
            x_re, x_im = _s5_scan(states, apw, tn(gl * p_state, SCAN_LANES_TARGET))
            y = _s5_out(u, x_re, x_im, w_i, w_cre, w_cim, r_s5, tn(T_BLK * cbw))

            merged = _glu_merge(y, u, d_skip[l], w_glu_b, l, pa, gates, d_model, tm_half, tn(d_model, 1024))
            h_res = _mm(merged, w_out_b, l, col0=0, n_cols=d_model, tm=tm, tn=t_d, epilogue=residual,
                        extras=(h_res,), extra_specs=(tile_spec(t_d),),
                        out_shape=jax.ShapeDtypeStruct((n_pad, d_model), F32), out_spec=tile_spec(t_d),
                        name="out_proj")

            hn = _rmsnorm(h_res, norm_ffn[l], BF16, tm_ew)
            if l + 1 < depth:
                act, w_in_l = _ffn_up(hn, w_up_l, None, conv_ffn_w[l].astype(F32), d_ff, tm_up, tn(d_ff),
                                      cast=(w_in, l + 1))
            else:
                act = _ffn_up(hn, w_up_l, None, conv_ffn_w[l].astype(F32), d_ff, tm_up, tn(d_ff))
            t_o = tn(d_model, 256)
            h_res = _mm(act, w_down_l, None, col0=0, n_cols=d_model, tm=tm_half, tn=t_o, epilogue=residual,
                        extras=(h_res,), extra_specs=(tile_spec(t_o, tm_half),),
                        out_shape=jax.ShapeDtypeStruct((n_pad, d_model), F32), out_spec=tile_spec(t_o, tm_half),
                        name="ffn_down")

        outs.append(_final_norm(h_res, norm_final, n_meta, seq, x.dtype, _pick_tile(seq, TM_EW_TARGET, n_meta)))
    return jnp.stack(outs, axis=0)
```

```python
import functools

import numpy as np
import jax
import jax.numpy as jnp
from jax import lax
from jax.experimental import pallas as pl
from jax.experimental.pallas import tpu as pltpu

F32 = jnp.float32
BF16 = jnp.bfloat16
EPS = 1e-6

T_BLK = 16
S5_CB_WIDTH = 128
PAD_ROWS = 1280
BF16_ROWS = 16
LANES = 128
VMEM_CAP_BYTES = 60000 * 1024
VMEM_TEMP_BYTES = 20 * 1024 * 1024

TM_TARGET = 1280
TM_FFN_UP_TARGET = 832
TM_HALF_TARGET = 640
R_S5_STATES_TARGET = 1040
TM_EW_TARGET = 320
R_S5_TARGET = 1040
SCAN_LANES_TARGET = 256


def _round_up(n, m):
    return (n + m - 1) // m * m


def _pick_tile(n, target, align):
    best = 0
    for t in range(align, min(n, target) + 1, align):
        if n % t == 0:
            best = t
    assert best > 0, (n, target, align)
    return best


def _nbytes(shape, dtype):
    return int(np.prod(shape)) * jnp.dtype(dtype).itemsize


def _compiler_params(semantics, block_bytes, scratch_bytes=0):
    est = 2 * block_bytes + scratch_bytes + VMEM_TEMP_BYTES
    return pltpu.CompilerParams(dimension_semantics=semantics,
                                vmem_limit_bytes=int(min(VMEM_CAP_BYTES, est)))


def _dot(a, b):
    return jnp.dot(a, b, preferred_element_type=F32)


def _sigmoid(x):
    return 0.5 * (1.0 + jnp.tanh(0.5 * x))


def _rmsnorm_kernel(x_ref, g_ref, o_ref):
    x = x_ref[...]
    ms = jnp.mean(x * x, axis=-1, keepdims=True)
    o_ref[...] = (x * lax.rsqrt(ms + EPS) * g_ref[...]).astype(o_ref.dtype)


def _rmsnorm(x, g, out_dtype, tm):
    n, d = x.shape
    blocks = _nbytes((tm, d), F32) + _nbytes((tm, d), out_dtype)
    return pl.pallas_call(
        _rmsnorm_kernel,
        out_shape=jax.ShapeDtypeStruct((n, d), out_dtype),
        grid=(n // tm,),
        in_specs=[pl.BlockSpec((tm, d), lambda i: (i, 0)),
                  pl.BlockSpec((1, d), lambda i: (0, 0))],
        out_specs=pl.BlockSpec((tm, d), lambda i: (i, 0)),
        compiler_params=_compiler_params(("parallel",), blocks),
        name="rmsnorm",
    )(x, g.reshape(1, d).astype(F32))


def _assemble_norm_kernel(meta_ref, prev_ref, x_ref, g_ref, h_ref, hn_ref, *, n_valid):
    i = pl.program_id(0)
    tm = h_ref.shape[0]
    n_meta = meta_ref.shape[0]
    head = jnp.where(i == 0, meta_ref[...], prev_ref[...])
    tile = jnp.concatenate([head, x_ref[0:tm - n_meta, :]], axis=0)
    rows = i * tm + lax.broadcasted_iota(jnp.int32, tile.shape, 0)
    tile = jnp.where(rows < n_valid, tile, 0.0)
    h_ref[...] = tile
    ms = jnp.mean(tile * tile, axis=-1, keepdims=True)
    hn_ref[...] = (tile * lax.rsqrt(ms + EPS) * g_ref[...]).astype(hn_ref.dtype)


def _assemble_norm(meta, x, g, n_pad, tm):
    n_meta, d = meta.shape
    seq = x.shape[0]
    assert n_meta % 8 == 0 and tm % n_meta == 0 and seq % n_meta == 0 and n_pad % tm == 0
    per = tm // n_meta
    last_x = pl.cdiv(seq, tm) - 1
    last_prev = seq // n_meta - 1
    blocks = _nbytes((tm + 2 * n_meta, d), F32) + _nbytes((tm, d), F32) + _nbytes((tm, d), BF16)
    return pl.pallas_call(
        functools.partial(_assemble_norm_kernel, n_valid=n_meta + seq),
        out_shape=(jax.ShapeDtypeStruct((n_pad, d), F32), jax.ShapeDtypeStruct((n_pad, d), BF16)),
        grid=(n_pad // tm,),
        in_specs=[pl.BlockSpec((n_meta, d), lambda i: (0, 0)),
                  pl.BlockSpec((n_meta, d), lambda i: (jnp.clip(i * per - 1, 0, last_prev), 0)),
                  pl.BlockSpec((tm, d), lambda i: (jnp.minimum(i, last_x), 0)),
                  pl.BlockSpec((1, d), lambda i: (0, 0))],
        out_specs=(pl.BlockSpec((tm, d), lambda i: (i, 0)), pl.BlockSpec((tm, d), lambda i: (i, 0))),
        compiler_params=_compiler_params(("parallel",), blocks),
        name="assemble_norm",
    )(meta.astype(F32), x.astype(F32), x.astype(F32), g.reshape(1, d).astype(F32))


def _final_norm_kernel(a_ref, b_ref, g_ref, o_ref, *, skip):
    x = jnp.concatenate([a_ref[skip:, :], b_ref[...]], axis=0)
    ms = jnp.mean(x * x, axis=-1, keepdims=True)
    o_ref[...] = (x * lax.rsqrt(ms + EPS) * g_ref[...]).astype(o_ref.dtype)


def _final_norm(x, g, skip, n_out, out_dtype, tm):
    n, d = x.shape
    assert skip % 8 == 0 and tm % skip == 0 and n_out % tm == 0 and n >= n_out + skip
    per = tm // skip
    blocks = _nbytes((tm + skip, d), F32) + _nbytes((tm, d), out_dtype)
    return pl.pallas_call(
        functools.partial(_final_norm_kernel, skip=skip),
        out_shape=jax.ShapeDtypeStruct((n_out, d), out_dtype),
        grid=(n_out // tm,),
        in_specs=[pl.BlockSpec((tm, d), lambda i: (i, 0)),
                  pl.BlockSpec((skip, d), lambda i: ((i + 1) * per, 0)),
                  pl.BlockSpec((1, d), lambda i: (0, 0))],
        out_specs=pl.BlockSpec((tm, d), lambda i: (i, 0)),
        compiler_params=_compiler_params(("parallel",), blocks),
        name="final_norm",
    )(x, x, g.reshape(1, d).astype(F32))


def _mm_kernel(*refs, n_extra, epilogue, tm):
    a_ref, w_ref = refs[0], refs[1]
    extras = refs[2:2 + n_extra]
    o_ref = refs[2 + n_extra]
    acc = _dot(a_ref[...], w_ref[...])
    row0 = pl.program_id(0) * tm
    o_ref[...] = epilogue(acc, row0, *[e[...] for e in extras]).astype(o_ref.dtype)


def _w_spec(k, tn, layer, jb0):
    if layer is None:
        return pl.BlockSpec((k, tn), lambda i, j: (0, jb0 + j))
    return pl.BlockSpec((None, k, tn), lambda i, j: (layer, 0, jb0 + j))


def _call_with_cast(kernel_fn, cast, *, grid, in_specs, out_specs, out_shape, operands, scratch_shapes=(),
                    block_bytes, scratch_bytes=0, name):
    if cast is None:
        return pl.pallas_call(
            kernel_fn, out_shape=tuple(out_shape), grid=grid, in_specs=list(in_specs), out_specs=tuple(out_specs),
            scratch_shapes=list(scratch_shapes),
            compiler_params=_compiler_params(("parallel", "arbitrary"), block_bytes, scratch_bytes), name=name,
        )(*operands)

    stack, layer = cast
    _, k, n = stack.shape
    n_steps = grid[0] * grid[1]
    options = []
    for rows in range(BF16_ROWS, k + 1, BF16_ROWS):
        if k % rows == 0 and k // rows <= n_steps:
            splits = max(c for c in range(1, n_steps // (k // rows) + 1) if n % (c * LANES) == 0)
            options.append((rows * (n // splits), rows, splits))
    _, rows, splits = min(options)
    cols = n // splits
    last_chunk = (k // rows) * splits - 1
    n_in, n_out = len(in_specs), len(out_specs)

    def chunk(i, j):
        q = jnp.minimum(i * grid[1] + j, last_chunk)
        return q // splits, q % splits

    def body(*refs):
        src_ref, dst_ref = refs[n_in], refs[n_in + 1 + n_out]
        dst_ref[...] = src_ref[...].astype(dst_ref.dtype)
        kernel_fn(*refs[:n_in], *refs[n_in + 1:n_in + 1 + n_out], *refs[n_in + 2 + n_out:])

    cast_bytes = _nbytes((rows, cols), F32) + _nbytes((rows, cols), BF16)
    return pl.pallas_call(
        body,
        out_shape=(*out_shape, jax.ShapeDtypeStruct((k, n), BF16)),
        grid=grid,
        in_specs=[*in_specs, pl.BlockSpec((None, rows, cols), lambda i, j: (layer, *chunk(i, j)))],
        out_specs=(*out_specs, pl.BlockSpec((rows, cols), lambda i, j: chunk(i, j))),
        scratch_shapes=list(scratch_shapes),
        compiler_params=_compiler_params(("arbitrary", "arbitrary"), block_bytes + cast_bytes, scratch_bytes),
        name=name,
    )(*operands, stack)


def _mm(a, w, layer, *, col0, n_cols, tm, tn, epilogue, extras=(), extra_specs=(), out_shape, out_spec, name,
        cast=None):
    n, k = a.shape
    assert n % tm == 0 and n_cols % tn == 0 and col0 % tn == 0
    blocks = (_nbytes((tm, k), a.dtype) + _nbytes((k, tn), w.dtype) + _nbytes((tm, tn), F32)
              + sum(_nbytes((tm, tn), e.dtype) for e in extras))
    outs = _call_with_cast(
        functools.partial(_mm_kernel, n_extra=len(extras), epilogue=epilogue, tm=tm), cast,
        grid=(n // tm, n_cols // tn),
        in_specs=[pl.BlockSpec((tm, k), lambda i, j: (i, 0)),
                  _w_spec(k, tn, layer, col0 // tn),
                  *extra_specs],
        out_specs=[out_spec], out_shape=[out_shape], operands=(a, w, *extras), block_bytes=blocks, name=name)
    return outs[0] if cast is None else outs


def _epi_pieces(acc, row0):
    return jnp.stack([acc[:, h * LANES:(h + 1) * LANES] for h in range(acc.shape[1] // LANES)], axis=0)


def _epi_sigmoid(acc, row0):
    return _sigmoid(acc)


def _epi_residual(acc, row0, res, *, n_valid):
    rows = row0 + lax.broadcasted_iota(jnp.int32, acc.shape, 0)
    return jnp.where(rows < n_valid, res + acc, 0.0)


def _inproj_conv_kernel(a_ref, wb_ref, wc_ref, wv_ref, bg_ref, cv_ref):
    a = a_ref[...]
    bg_ref[...] = _dot(a, wb_ref[...]).astype(bg_ref.dtype)
    cv_ref[...] = (_dot(a, wc_ref[...]) * _dot(a, wv_ref[...])).astype(cv_ref.dtype)


def _inproj_conv(hn, w_in, layer, d_conv, tm, tn, cast=None):
    n, k = hn.shape
    nb = d_conv // tn
    blocks = _nbytes((tm, k), BF16) + 3 * _nbytes((k, tn), BF16) + 2 * _nbytes((tm, tn), BF16)
    out = jax.ShapeDtypeStruct((n, d_conv), BF16)
    return _call_with_cast(
        _inproj_conv_kernel, cast,
        grid=(n // tm, nb),
        in_specs=[pl.BlockSpec((tm, k), lambda i, j: (i, 0)),
                  _w_spec(k, tn, layer, 0),
                  _w_spec(k, tn, layer, nb),
                  _w_spec(k, tn, layer, 2 * nb)],
        out_specs=[pl.BlockSpec((tm, tn), lambda i, j: (i, j)),
                   pl.BlockSpec((tm, tn), lambda i, j: (i, j))],
        out_shape=[out, out], operands=(hn, w_in, w_in, w_in), block_bytes=blocks, name="inproj_conv")


def _shifted_rows(x, prev_row, next_row):
    t = x.shape[0]
    row = lax.broadcasted_iota(jnp.int32, x.shape, 0)
    up = jnp.where(row == 0, prev_row, pltpu.roll(x, 1, 0))
    dn = jnp.where(row == t - 1, next_row, pltpu.roll(x, t - 1, 0))
    return up, dn


def _branch_a_kernel(bg_ref, cv_ref, cvp_ref, cvn_ref, cw_ref, w_ref, gate_ref, o_ref, z_ref, *, chunk):
    i = pl.program_id(0)
    last = pl.num_programs(0) - 1

    @pl.when(pl.program_id(1) == 0)
    def _():
        for c0 in range(0, z_ref.shape[1], chunk):
            cols = slice(c0, c0 + chunk)
            x = cv_ref[:, cols].astype(F32)
            prev = cvp_ref[:, cols].astype(F32)[BF16_ROWS - 1:BF16_ROWS, :]
            nxt = cvn_ref[:, cols].astype(F32)[0:1, :]
            prev = jnp.where(i > 0, prev, 0.0)
            nxt = jnp.where(i < last, nxt, 0.0)
            up, dn = _shifted_rows(x, prev, nxt)
            w = cw_ref[:, cols]
            conv = up * w[0:1, :] + x * w[1:2, :] + dn * w[2:3, :]
            z_ref[:, cols] = (bg_ref[:, cols].astype(F32) * conv).astype(z_ref.dtype)

    o_ref[...] = (_dot(z_ref[...], w_ref[...]) * gate_ref[...].astype(F32)).astype(o_ref.dtype)


def _branch_a(bg, cv, conv_w, w_a, layer, gates, d_model, tm, tn):
    n, c = bg.shape
    hb = tm // BF16_ROWS
    n_hb = n // BF16_ROWS
    blocks = (2 * _nbytes((tm, c), BF16) + 2 * _nbytes((BF16_ROWS, c), BF16) + _nbytes((8, c), F32)
              + _nbytes((c, tn), BF16) + 2 * _nbytes((tm, tn), BF16))
    scratch = _nbytes((tm, c), BF16)
    return pl.pallas_call(
        functools.partial(_branch_a_kernel, chunk=_pick_tile(c, 512, LANES)),
        out_shape=jax.ShapeDtypeStruct((n, d_model), BF16),
        grid=(n // tm, d_model // tn),
        in_specs=[pl.BlockSpec((tm, c), lambda i, j: (i, 0)),
                  pl.BlockSpec((tm, c), lambda i, j: (i, 0)),
                  pl.BlockSpec((BF16_ROWS, c), lambda i, j: (jnp.maximum(i * hb - 1, 0), 0)),
                  pl.BlockSpec((BF16_ROWS, c), lambda i, j: (jnp.minimum((i + 1) * hb, n_hb - 1), 0)),
                  pl.BlockSpec((3, c), lambda i, j: (0, 0)),
                  _w_spec(c, tn, layer, 0),
                  pl.BlockSpec((tm, tn), lambda i, j: (i, j))],
        out_specs=pl.BlockSpec((tm, tn), lambda i, j: (i, j)),
        scratch_shapes=[pltpu.VMEM((tm, c), BF16)],
        compiler_params=_compiler_params(("parallel", "arbitrary"), blocks, scratch),
        name="branch_a",
    )(bg, cv, cv, cv, conv_w, w_a, gates)


def _cmul(ar, ai, br, bi):
    return ar * br - ai * bi, ar * bi + ai * br


def _s5_prep_kernel(lre_ref, lim_ref, lst_ref, bre_ref, bim_ref, cre_ref, cim_ref,
                    ere_ref, eim_ref, care_ref, caim_ref, klag_ref, apre_ref, apim_ref):
    lam_re, lam_im = lre_ref[...], lim_ref[...]
    dt = jnp.exp(lst_ref[...])
    mag = jnp.exp(lam_re * dt)
    a_re = mag * jnp.cos(lam_im * dt)
    a_im = mag * jnp.sin(lam_im * dt)
    nr, ni = a_re - 1.0, a_im
    den = lam_re * lam_re + lam_im * lam_im
    f_re = (nr * lam_re + ni * lam_im) / den
    f_im = (ni * lam_re - nr * lam_im) / den
    b_re, b_im = bre_ref[...], bim_ref[...]
    bb_re, bb_im = _cmul(f_re, f_im, b_re, b_im)
    c_re, c_im = cre_ref[...], cim_ref[...]

    pw = [(jnp.ones_like(a_re), jnp.zeros_like(a_im))]
    for _ in range(T_BLK):
        pw.append(_cmul(pw[-1][0], pw[-1][1], a_re, a_im))

    e_re, e_im, ca_re, ca_im = [], [], [], []
    for k in range(T_BLK):
        er, ei = _cmul(pw[k][0], pw[k][1], bb_re, bb_im)
        e_re.append(er)
        e_im.append(ei)
        cr, ci = _cmul(pw[k + 1][0], pw[k + 1][1], c_re, c_im)
        ca_re.append(cr)
        ca_im.append(-ci)
    e_re = jnp.concatenate(e_re, axis=1)
    e_im = jnp.concatenate(e_im, axis=1)
    ere_ref[...] = e_re
    eim_ref[...] = e_im
    care_ref[...] = jnp.concatenate(ca_re, axis=1)
    caim_ref[...] = jnp.concatenate(ca_im, axis=1)

    dn = (((2,), (2,)), ((0,), (0,)))
    klag_ref[...] = (lax.dot_general(c_re, e_re, dn, precision=lax.Precision.HIGHEST,
                                     preferred_element_type=F32)
                     - lax.dot_general(c_im, e_im, dn, precision=lax.Precision.HIGHEST,
                                       preferred_element_type=F32))

    q_re, q_im = pw[T_BLK]
    r_re, r_im = q_re, q_im
    for m in range(8):
        apre_ref[m] = r_re
        apim_ref[m] = r_im
        r_re, r_im = _cmul(r_re, r_im, q_re, q_im)


def _s5_prep(lam_re, lam_im, log_step, b_re, b_im, c_re, c_im):
    depth, _, g, p = lam_re.shape
    h = b_re.shape[-1]
    lead = (depth, 2)
    b_re_t = jnp.swapaxes(b_re, -1, -2).astype(F32)
    b_im_t = jnp.swapaxes(b_im, -1, -2).astype(F32)

    gc = _pick_tile(g, 16, 1)

    def spec(*tail):
        zeros = (0,) * (len(tail) - 1)
        return pl.BlockSpec((None, None) + tail, lambda l, d, q: (l, d, q) + zeros)

    ap_spec = pl.BlockSpec((None, None, 8, gc, 1, p), lambda l, d, q: (l, d, 0, q, 0, 0))
    th = T_BLK * h
    outs = (jax.ShapeDtypeStruct(lead + (g, th, p), F32),) * 4 + (
        jax.ShapeDtypeStruct(lead + (g, h, th), F32),
        jax.ShapeDtypeStruct(lead + (8, g, 1, p), F32),
        jax.ShapeDtypeStruct(lead + (8, g, 1, p), F32))
    blocks = 4 * _nbytes((gc, th, LANES), F32) + 8 * _nbytes((gc, h, LANES), F32)
    return pl.pallas_call(
        _s5_prep_kernel,
        out_shape=outs,
        grid=lead + (g // gc,),
        in_specs=[spec(gc, 1, p), spec(gc, 1, p), spec(gc, 1, 1), spec(gc, h, p), spec(gc, h, p),
                  spec(gc, h, p), spec(gc, h, p)],
        out_specs=(spec(gc, th, p),) * 4 + (spec(gc, h, th), ap_spec, ap_spec),
        compiler_params=_compiler_params(("parallel", "parallel", "parallel"), blocks,
                                         16 * _nbytes((gc, th, LANES), F32)),
        name="s5_prep",
    )(lam_re.astype(F32)[:, :, :, None, :], lam_im.astype(F32)[:, :, :, None, :],
      log_step.astype(F32)[..., None, None], b_re_t, b_im_t, c_re.astype(F32), c_im.astype(F32))


def _block_diag_kernel(c_ref, e_ref, rg_ref, cg_ref, o_ref):
    spread = _dot(c_ref[...].astype(BF16), e_ref[...])
    o_ref[...] = jnp.where(rg_ref[...] == cg_ref[...], spread, 0.0).astype(o_ref.dtype)


def _block_diag(compact, gl, row_inner, col_inner, name):
    n_cb, rows, kc = compact.shape
    n_out = kc * gl
    tr = _pick_tile(rows, 512, BF16_ROWS)
    col = np.arange(n_out)
    spread = np.arange(kc)[:, None] == (col // (gl * col_inner)) * col_inner + col % col_inner
    row_group = ((np.arange(rows) // row_inner) % gl).astype(np.int32)[:, None]
    col_group = ((col // col_inner) % gl).astype(np.int32)[None, :]
    blocks = (_nbytes((tr, kc), F32) + _nbytes((kc, n_out), BF16) + _nbytes((tr, LANES), jnp.int32)
              + _nbytes((8, n_out), jnp.int32) + _nbytes((tr, n_out), BF16))
    return pl.pallas_call(
        _block_diag_kernel,
        out_shape=jax.ShapeDtypeStruct((n_cb, rows, n_out), BF16),
        grid=(n_cb, rows // tr),
        in_specs=[pl.BlockSpec((None, tr, kc), lambda c, i: (c, i, 0)),
                  pl.BlockSpec((kc, n_out), lambda c, i: (0, 0)),
                  pl.BlockSpec((tr, 1), lambda c, i: (i, 0)),
                  pl.BlockSpec((1, n_out), lambda c, i: (0, 0))],
        out_specs=pl.BlockSpec((None, tr, n_out), lambda c, i: (c, i, 0)),
        compiler_params=_compiler_params(("parallel", "parallel"), blocks),
        name=name,
    )(compact, jnp.asarray(spread, BF16), jnp.asarray(row_group), jnp.asarray(col_group))


def _s5_weights(prep, n_cb, gl, h, p):
    e_re, e_im, ca_re, ca_im, klag, ap_re, ap_im = prep
    t = T_BLK

    def split(x):
        return x.reshape(2, n_cb * gl, t, h, p)

    e = jnp.stack([split(e_re), split(e_im)], axis=1)
    e_sel = jnp.stack([e[0, :, :, ::-1], e[1]], axis=0)
    e_sel = e_sel.reshape(2, 2, n_cb, gl, t, h, p)
    e_sel = jnp.transpose(e_sel, (2, 4, 3, 5, 0, 1, 6))
    w_s = _block_diag(e_sel.reshape(n_cb, t * gl * h, 4 * p), gl, h, p, "s5_w_states")

    def carry(ca, name):
        ca = split(ca)
        sel = jnp.stack([ca[0], ca[1, :, ::-1]], axis=0)
        sel = sel.reshape(2, n_cb, gl, t, h, p)
        sel = jnp.transpose(sel, (1, 0, 2, 5, 3, 4))
        return _block_diag(sel.reshape(n_cb, 2 * gl * p, t * h), gl, p, h, name)

    w_cre, w_cim = carry(ca_re, "s5_w_carry_re"), carry(ca_im, "s5_w_carry_im")

    kl = klag.reshape(2, n_cb * gl, h, t, h)
    sig = np.arange(t)[:, None]
    tau = np.arange(t)[None, :]
    kf = kl[0][:, :, np.clip(tau - sig, 0, t - 1), :] * jnp.asarray(tau >= sig, F32)[None, None, :, :, None]
    kb = kl[1][:, :, np.clip(sig - tau, 0, t - 1), :] * jnp.asarray(sig >= tau, F32)[None, None, :, :, None]
    kst = (kf + kb).reshape(n_cb, gl, h, t, t, h)
    kst = jnp.transpose(kst, (0, 3, 1, 5, 4, 2))
    w_i = _block_diag(kst.reshape(n_cb, t * gl * h, t * h), gl, h, h, "s5_w_intra")

    def lanes(ap):
        ap = jnp.stack([ap[0], ap[1, ::-1]], axis=0)
        ap = ap.reshape(2, 8, n_cb, gl * p)
        return jnp.transpose(ap, (2, 0, 1, 3))
    apw = jnp.stack([lanes(ap_re), lanes(ap_im)], axis=2)

    return w_s, w_cre, w_cim, w_i, apw


def _fold_time_blocks(u_ref, lhs_ref, r):
    pieces = u_ref.shape[0]
    for tau in range(T_BLK):
        for h in range(pieces):
            col = (tau * pieces + h) * LANES
            lhs_ref[:, col:col + LANES] = u_ref[h, pl.ds(tau, r, stride=T_BLK), :].astype(lhs_ref.dtype)


def _s5_states_kernel(u_ref, w_ref, s_ref, lhs_ref, *, r):
    @pl.when(pl.program_id(2) == 0)
    def _():
        _fold_time_blocks(u_ref, lhs_ref, r)

    s_ref[...] = _dot(lhs_ref[...], w_ref[...])


def _s5_states(u, w_s, r, tn):
    n_cb, kw, ns = w_s.shape
    n = u.shape[1]
    nb = n // T_BLK
    ppc = u.shape[0] // n_cb
    blocks = _nbytes((ppc, T_BLK * r, LANES), F32) + _nbytes((kw, tn), BF16) + _nbytes((r, tn), F32)
    scratch = _nbytes((r, kw), BF16)
    return pl.pallas_call(
        functools.partial(_s5_states_kernel, r=r),
        out_shape=jax.ShapeDtypeStruct((n_cb, nb, ns), F32),
        grid=(n_cb, nb // r, ns // tn),
        in_specs=[pl.BlockSpec((ppc, T_BLK * r, LANES), lambda c, i, j: (c, i, 0)),
                  pl.BlockSpec((None, kw, tn), lambda c, i, j: (c, 0, j))],
        out_specs=pl.BlockSpec((None, r, tn), lambda c, i, j: (c, i, j)),
        scratch_shapes=[pltpu.VMEM((r, kw), BF16)],
        compiler_params=_compiler_params(("parallel", "parallel", "arbitrary"), blocks, scratch),
        name="s5_states",
    )(u, w_s)


def _s5_scan_kernel(sre_ref, sim_ref, apre_ref, apim_ref, xre_ref, xim_ref, *, n_pairs):
    d = pl.program_id(1)
    lw = sre_ref.shape[-1]
    ap_re, ap_im = apre_ref[...], apim_ref[...]
    row = lax.broadcasted_iota(jnp.int32, (8, lw), 0)

    def run(reverse):
        steps = []
        for shift in (1, 2, 4):
            src = (8 - shift) if reverse else (shift - 1)
            keep = (row < 8 - shift) if reverse else (row >= shift)
            steps.append((8 - shift if reverse else shift,
                          jnp.where(keep, ap_re[src:src + 1, :], 0.0),
                          jnp.where(keep, ap_im[src:src + 1, :], 0.0)))
        edge = 7 if reverse else 0
        out_row = 0 if reverse else 7
        nb_shift = 7 if reverse else 1

        def tile(r0, c_re, c_im):
            x_re = sre_ref[pl.ds(r0, 8), :]
            x_im = sim_ref[pl.ds(r0, 8), :]
            for amount, m_re, m_im in steps:
                s_re = pltpu.roll(x_re, amount, 0)
                s_im = pltpu.roll(x_im, amount, 0)
                x_re, x_im = x_re + m_re * s_re - m_im * s_im, x_im + m_re * s_im + m_im * s_re
            inc_re = x_re + ap_re * c_re - ap_im * c_im
            inc_im = x_im + ap_re * c_im + ap_im * c_re
            e_re = jnp.where(row == edge, c_re, pltpu.roll(inc_re, nb_shift, 0))
            e_im = jnp.where(row == edge, c_im, pltpu.roll(inc_im, nb_shift, 0))
            return e_re, e_im, inc_re[out_row:out_row + 1, :], inc_im[out_row:out_row + 1, :]

        def body(k, carry):
            c_re, c_im = carry
            pair = (n_pairs - 1 - k) if reverse else k
            r0 = pl.multiple_of(pair * BF16_ROWS, BF16_ROWS)
            e_re, e_im = [None, None], [None, None]
            for half in ((1, 0) if reverse else (0, 1)):
                e_re[half], e_im[half], c_re, c_im = tile(pl.multiple_of(r0 + 8 * half, 8), c_re, c_im)
            xre_ref[pl.ds(r0, BF16_ROWS), :] = jnp.concatenate(e_re, axis=0).astype(xre_ref.dtype)
            xim_ref[pl.ds(r0, BF16_ROWS), :] = jnp.concatenate(e_im, axis=0).astype(xim_ref.dtype)
            return c_re, c_im

        zero = jnp.zeros((1, lw), F32)
        lax.fori_loop(0, n_pairs, body, (zero, zero))

    @pl.when(d == 0)
    def _():
        run(False)

    @pl.when(d == 1)
    def _():
        run(True)


def _s5_scan(s, apw, lw):
    n_cb, nb, ns = s.shape
    glp = ns // 4
    nq = glp // lw
    blocks = 2 * _nbytes((nb, lw), F32) + 2 * _nbytes((nb, lw), BF16) + 2 * _nbytes((8, lw), F32)
    out = jax.ShapeDtypeStruct((n_cb, nb, 2 * glp), BF16)

    def s_spec(ri):
        return pl.BlockSpec((None, nb, lw), lambda c, d, q: (c, 0, (2 * d + ri) * nq + q))

    def ap_spec(ri):
        return pl.BlockSpec((None, None, None, 8, lw), lambda c, d, q: (c, d, ri, 0, q))

    x_spec = pl.BlockSpec((None, nb, lw), lambda c, d, q: (c, 0, d * nq + q))
    return pl.pallas_call(
        functools.partial(_s5_scan_kernel, n_pairs=nb // BF16_ROWS),
        out_shape=(out, out),
        grid=(n_cb, 2, nq),
        in_specs=[s_spec(0), s_spec(1), ap_spec(0), ap_spec(1)],
        out_specs=(x_spec, x_spec),
        compiler_params=_compiler_params(("parallel", "parallel", "parallel"), blocks),
        name="s5_scan",
    )(s, s, apw, apw)


def _s5_out_kernel(u_ref, xre_ref, xim_ref, wi_ref, wcre_ref, wcim_ref, y_ref, lhs_ref, *, r, tn):
    j = pl.program_id(2)
    pieces = u_ref.shape[0]

    @pl.when(j == 0)
    def _():
        _fold_time_blocks(u_ref, lhs_ref, r)

    y = _dot(lhs_ref[...], wi_ref[...])
    y = y + _dot(xre_ref[...], wcre_ref[...])
    y = y + _dot(xim_ref[...], wcim_ref[...])

    per = tn // (pieces * LANES)
    for jj in range(T_BLK // per):
        @pl.when(j == jj)
        def _():
            for k in range(per):
                for h in range(pieces):
                    col = (k * pieces + h) * LANES
                    y_ref[h, pl.ds(jj * per + k, r, stride=T_BLK), :] = y[:, col:col + LANES]


def _s5_out(u, x_re, x_im, w_i, w_cre, w_cim, r, tn):
    n_cb, kw, _ = w_i.shape
    n = u.shape[1]
    nb = n // T_BLK
    ppc = u.shape[0] // n_cb
    kx = x_re.shape[-1]
    assert tn % (ppc * LANES) == 0 and kw % tn == 0
    blocks = (2 * _nbytes((ppc, T_BLK * r, LANES), F32) + 2 * _nbytes((r, kx), BF16) + _nbytes((kw, tn), BF16)
              + 2 * _nbytes((kx, tn), BF16) + _nbytes((r, tn), F32))
    scratch = _nbytes((r, kw), BF16)
    return pl.pallas_call(
        functools.partial(_s5_out_kernel, r=r, tn=tn),
        out_shape=jax.ShapeDtypeStruct(u.shape, F32),
        grid=(n_cb, nb // r, kw // tn),
        in_specs=[pl.BlockSpec((ppc, T_BLK * r, LANES), lambda c, i, j: (c, i, 0)),
                  pl.BlockSpec((None, r, kx), lambda c, i, j: (c, i, 0)),
                  pl.BlockSpec((None, r, kx), lambda c, i, j: (c, i, 0)),
                  pl.BlockSpec((None, kw, tn), lambda c, i, j: (c, 0, j)),
                  pl.BlockSpec((None, kx, tn), lambda c, i, j: (c, 0, j)),
                  pl.BlockSpec((None, kx, tn), lambda c, i, j: (c, 0, j))],
        out_specs=pl.BlockSpec((ppc, T_BLK * r, LANES), lambda c, i, j: (c, i, 0)),
        scratch_shapes=[pltpu.VMEM((r, kw), BF16)],
        compiler_params=_compiler_params(("parallel", "parallel", "arbitrary"), blocks, scratch),
        name="s5_out",
    )(u, x_re, x_im, w_i, w_cre, w_cim)


def _glu_merge_kernel(y_ref, u_ref, d_ref, w1_ref, w2_ref, pa_ref, gb_ref, m_ref, s_ref):
    @pl.when(pl.program_id(1) == 0)
    def _():
        for c in range(y_ref.shape[0]):
            s = jax.nn.gelu(y_ref[c] + d_ref[c] * u_ref[c])
            s_ref[:, c * LANES:(c + 1) * LANES] = s.astype(s_ref.dtype)

    a = s_ref[...]
    y_b = _dot(a, w1_ref[...]) * _sigmoid(_dot(a, w2_ref[...]))
    m_ref[...] = (pa_ref[...].astype(F32) + gb_ref[...].astype(F32) * y_b).astype(m_ref.dtype)


def _glu_merge(y, u, d_skip, w_glu, layer, pa, gates, d_model, tm, tn):
    n_pc, n, _ = y.shape
    k = n_pc * LANES
    nb = d_model // tn
    blocks = (2 * _nbytes((n_pc, tm, LANES), F32) + 2 * _nbytes((k, tn), BF16) + 3 * _nbytes((tm, tn), BF16))
    scratch = _nbytes((tm, k), BF16)
    return pl.pallas_call(
        _glu_merge_kernel,
        out_shape=jax.ShapeDtypeStruct((n, d_model), BF16),
        grid=(n // tm, nb),
        in_specs=[pl.BlockSpec((n_pc, tm, LANES), lambda i, j: (0, i, 0)),
                  pl.BlockSpec((n_pc, tm, LANES), lambda i, j: (0, i, 0)),
                  pl.BlockSpec((n_pc, 1, LANES), lambda i, j: (0, 0, 0)),
                  _w_spec(k, tn, layer, 0),
                  _w_spec(k, tn, layer, nb),
                  pl.BlockSpec((tm, tn), lambda i, j: (i, j)),
                  pl.BlockSpec((tm, tn), lambda i, j: (i, nb + j))],
        out_specs=pl.BlockSpec((tm, tn), lambda i, j: (i, j)),
        scratch_shapes=[pltpu.VMEM((tm, k), BF16)],
        compiler_params=_compiler_params(("parallel", "arbitrary"), blocks, scratch),
        name="glu_merge",
    )(y, u, d_skip, w_glu, w_glu, pa, gates)


def _ffn_up_kernel(h_ref, hp_ref, hn_ref, wg_ref, wv_ref, cg_ref, cv_ref, act_ref, lhs_ref, *, tm):
    i = pl.program_id(0)
    j = pl.program_id(1)
    last = pl.num_programs(0) - 1
    halo = BF16_ROWS

    @pl.when(j == 0)
    def _():
        row = lax.broadcasted_iota(jnp.int32, hp_ref.shape, 0)
        after = jnp.where(jnp.logical_and(row == 0, i < last), hn_ref[...].astype(F32), 0.0)
        edge = jnp.where(jnp.logical_and(row == halo - 1, i > 0), hp_ref[...].astype(F32), after)
        lhs_ref[0:tm, :] = h_ref[...]
        lhs_ref[tm:, :] = edge.astype(lhs_ref.dtype)

    a = lhs_ref[...]
    rows = tm + halo

    def conv(w_ref, cw_ref):
        u = _dot(a, w_ref[...])
        cw = cw_ref[...]
        up = pltpu.roll(u, 1, 0)[0:tm]
        dn = pltpu.roll(u, rows - 1, 0)[0:tm]
        return up * cw[0:1, :] + u[0:tm] * cw[1:2, :] + dn * cw[2:3, :]

    act_ref[...] = (jax.nn.gelu(conv(wg_ref, cg_ref)) * conv(wv_ref, cv_ref)).astype(act_ref.dtype)


def _ffn_up(hn, w_up, layer, conv_w, d_ff, tm, tn, cast=None):
    n, k = hn.shape
    nb = d_ff // tn
    hb = tm // BF16_ROWS
    n_hb = n // BF16_ROWS
    blocks = (_nbytes((tm + 2 * BF16_ROWS, k), BF16) + 2 * _nbytes((k, tn), BF16) + 2 * _nbytes((3, tn), F32)
              + _nbytes((tm, tn), BF16))
    scratch = _nbytes((tm + BF16_ROWS, k), BF16)
    outs = _call_with_cast(
        functools.partial(_ffn_up_kernel, tm=tm), cast,
        grid=(n // tm, nb),
        in_specs=[pl.BlockSpec((tm, k), lambda i, j: (i, 0)),
                  pl.BlockSpec((BF16_ROWS, k), lambda i, j: (jnp.maximum(i * hb - 1, 0), 0)),
                  pl.BlockSpec((BF16_ROWS, k), lambda i, j: (jnp.minimum((i + 1) * hb, n_hb - 1), 0)),
                  _w_spec(k, tn, layer, 0),
                  _w_spec(k, tn, layer, nb),
                  pl.BlockSpec((3, tn), lambda i, j: (0, j)),
                  pl.BlockSpec((3, tn), lambda i, j: (0, nb + j))],
        out_specs=[pl.BlockSpec((tm, tn), lambda i, j: (i, j))],
        out_shape=[jax.ShapeDtypeStruct((n, d_ff), BF16)],
        operands=(hn, hn, hn, w_up, w_up, conv_w, conv_w),
        scratch_shapes=[pltpu.VMEM((tm + BF16_ROWS, k), BF16)],
        block_bytes=blocks, scratch_bytes=scratch, name="ffn_up")
    return outs[0] if cast is None else outs


def kernel(x, meta_tokens, norm_mix, w_in, conv_a_w, w_a, ssm_lambda_re, ssm_lambda_im, ssm_log_step,
           ssm_b_re, ssm_b_im, ssm_c_re, ssm_c_im, ssm_d, w_glu, w_out, norm_ffn, w_up, conv_ffn_w,
           w_down, norm_final):
    bsz, seq, d_model = x.shape
    n_meta = meta_tokens.shape[0]
    depth = w_in.shape[0]
    d_conv = w_a.shape[1]
    d_ssm = ssm_d.shape[1]
    d_ff = w_down.shape[1]
    n_groups, p_state, h_grp = ssm_b_re.shape[2:]
    assert w_in.shape[2] == 3 * d_conv + d_ssm + 2 * d_model and n_groups * h_grp == d_ssm

    cbw = min(S5_CB_WIDTH, d_ssm)
    n_cb = d_ssm // cbw
    gl = cbw // h_grp
    assert n_cb * cbw == d_ssm and gl * h_grp == cbw

    n_true = n_meta + seq
    row_align = T_BLK * BF16_ROWS
    n_pad = _round_up(n_true, PAD_ROWS if n_true >= 4 * PAD_ROWS else row_align)
    assert n_pad % row_align == 0
    nb = n_pad // T_BLK

    tm = _pick_tile(n_pad, TM_TARGET, BF16_ROWS)
    tm_up = _pick_tile(n_pad, TM_FFN_UP_TARGET, BF16_ROWS)
    tm_half = _pick_tile(n_pad, TM_HALF_TARGET, BF16_ROWS)
    tm_ew = _pick_tile(n_pad, TM_EW_TARGET, BF16_ROWS)
    r_s5 = _pick_tile(nb, R_S5_TARGET, BF16_ROWS)
    tn = lambda n, target=512: _pick_tile(n, target, 128)

    prep = _s5_prep(ssm_lambda_re, ssm_lambda_im, ssm_log_step, ssm_b_re, ssm_b_im, ssm_c_re, ssm_c_im)
    d_skip = ssm_d.astype(F32).reshape(depth, d_ssm // LANES, 1, LANES)

    residual = functools.partial(_epi_residual, n_valid=n_true)
    tile_spec = lambda t, rows=tm: pl.BlockSpec((rows, t), lambda i, j: (i, j))
    u_off = 3 * d_conv
    gate_off = u_off + d_ssm

    w_a_b, w_glu_b, w_out_b = (w.astype(BF16) for w in (w_a, w_glu, w_out))
    w_in_first = w_in[0].astype(BF16)

    outs = []
    for b in range(bsz):
        w_in_l = w_in_first
        for l in range(depth):
            w_s, w_cre, w_cim, w_i, apw = _s5_weights([t[l] for t in prep], n_cb, gl, h_grp, p_state)

            if l == 0:
                h_res, hn = _assemble_norm(meta_tokens, x[b], norm_mix[0], n_pad, tm_ew)
            else:
                hn = _rmsnorm(h_res, norm_mix[l], BF16, tm_ew)
            bg, cv, w_down_l = _inproj_conv(hn, w_in_l, None, d_conv, tm_half, tn(d_conv), cast=(w_down, l))
            t_u = tn(d_ssm, 256)
            u = _mm(hn, w_in_l, None, col0=u_off, n_cols=d_ssm, tm=tm, tn=t_u, epilogue=_epi_pieces,
                    out_shape=jax.ShapeDtypeStruct((d_ssm // LANES, n_pad, LANES), F32),
                    out_spec=pl.BlockSpec((t_u // LANES, tm, LANES), lambda i, j: (j, i, 0)), name="inproj_u")
            t_g = tn(2 * d_model)
            gates, w_up_l = _mm(hn, w_in_l, None, col0=gate_off, n_cols=2 * d_model, tm=tm, tn=t_g,
                                epilogue=_epi_sigmoid,
                                out_shape=jax.ShapeDtypeStruct((n_pad, 2 * d_model), BF16),
                                out_spec=tile_spec(t_g), name="inproj_gates", cast=(w_up, l))

            pa = _branch_a(bg, cv, conv_a_w[l].astype(F32), w_a_b, l, gates, d_model, tm, tn(d_model, 1024))
            t_d = tn(d_model)

            states = _s5_states(u, w_s, _pick_tile(nb, R_S5_STATES_TARGET, BF16_ROWS), tn(w_s.shape[-1]))
            x_re, x_im = _s5_scan(states, apw, tn(gl * p_state, SCAN_LANES_TARGET))
            y = _s5_out(u, x_re, x_im, w_i, w_cre, w_cim, r_s5, tn(T_BLK * cbw))

            merged = _glu_merge(y, u, d_skip[l], w_glu_b, l, pa, gates, d_model, tm_half, tn(d_model, 1024))
            h_res = _mm(merged, w_out_b, l, col0=0, n_cols=d_model, tm=tm, tn=t_d, epilogue=residual,
                        extras=(h_res,), extra_specs=(tile_spec(t_d),),
                        out_shape=jax.ShapeDtypeStruct((n_pad, d_model), F32), out_spec=tile_spec(t_d),
                        name="out_proj")

            hn = _rmsnorm(h_res, norm_ffn[l], BF16, tm_ew)
            if l + 1 < depth:
                act, w_in_l = _ffn_up(hn, w_up_l, None, conv_ffn_w[l].astype(F32), d_ff, tm_up, tn(d_ff),
                                      cast=(w_in, l + 1))
            else:
                act = _ffn_up(hn, w_up_l, None, conv_ffn_w[l].astype(F32), d_ff, tm_up, tn(d_ff))
            t_o = tn(d_model, 256)
            h_res = _mm(act, w_down_l, None, col0=0, n_cols=d_model, tm=tm_half, tn=t_o, epilogue=residual,
                        extras=(h_res,), extra_specs=(tile_spec(t_o, tm_half),),
                        out_shape=jax.ShapeDtypeStruct((n_pad, d_model), F32), out_spec=tile_spec(t_o, tm_half),
                        name="ffn_down")

        outs.append(_final_norm(h_res, norm_final, n_meta, seq, x.dtype, _pick_tile(seq, TM_EW_TARGET, n_meta)))
    return jnp.stack(outs, axis=0)
```

```python
import functools

import numpy as np
import jax
import jax.numpy as jnp
from jax import lax
from jax.experimental import pallas as pl
from jax.experimental.pallas import tpu as pltpu

F32 = jnp.float32
BF16 = jnp.bfloat16
EPS = 1e-6

T_BLK = 16
S5_CB_WIDTH = 128
PAD_ROWS = 1280
BF16_ROWS = 16
LANES = 128
VMEM_CAP_BYTES = 60000 * 1024
VMEM_TEMP_BYTES = 20 * 1024 * 1024

TM_TARGET = 1280
TM_FFN_UP_TARGET = 832
TM_HALF_TARGET = 640
R_S5_STATES_TARGET = 1040
TM_EW_TARGET = 320
R_S5_TARGET = 1040
SCAN_LANES_TARGET = 256


def _round_up(n, m):
    return (n + m - 1) // m * m


def _pick_tile(n, target, align):
    best = 0
    for t in range(align, min(n, target) + 1, align):
        if n % t == 0:
            best = t
    assert best > 0, (n, target, align)
    return best


def _nbytes(shape, dtype):
    return int(np.prod(shape)) * jnp.dtype(dtype).itemsize


def _compiler_params(semantics, block_bytes, scratch_bytes=0):
    est = 2 * block_bytes + scratch_bytes + VMEM_TEMP_BYTES
    return pltpu.CompilerParams(dimension_semantics=semantics,
                                vmem_limit_bytes=int(min(VMEM_CAP_BYTES, est)))


def _dot(a, b):
    return jnp.dot(a, b, preferred_element_type=F32)


def _sigmoid(x):
    return 0.5 * (1.0 + jnp.tanh(0.5 * x))


def _rmsnorm_kernel(x_ref, g_ref, o_ref):
    x = x_ref[...]
    ms = jnp.mean(x * x, axis=-1, keepdims=True)
    o_ref[...] = (x * lax.rsqrt(ms + EPS) * g_ref[...]).astype(o_ref.dtype)


def _rmsnorm(x, g, out_dtype, tm):
    n, d = x.shape
    blocks = _nbytes((tm, d), F32) + _nbytes((tm, d), out_dtype)
    return pl.pallas_call(
        _rmsnorm_kernel,
        out_shape=jax.ShapeDtypeStruct((n, d), out_dtype),
        grid=(n // tm,),
        in_specs=[pl.BlockSpec((tm, d), lambda i: (i, 0)),
                  pl.BlockSpec((1, d), lambda i: (0, 0))],
        out_specs=pl.BlockSpec((tm, d), lambda i: (i, 0)),
        compiler_params=_compiler_params(("parallel",), blocks),
        name="rmsnorm",
    )(x, g.reshape(1, d).astype(F32))


def _assemble_norm_kernel(meta_ref, prev_ref, x_ref, g_ref, h_ref, hn_ref, *, n_valid):
    i = pl.program_id(0)
    tm = h_ref.shape[0]
    n_meta = meta_ref.shape[0]
    head = jnp.where(i == 0, meta_ref[...], prev_ref[...])
    tile = jnp.concatenate([head, x_ref[0:tm - n_meta, :]], axis=0)
    rows = i * tm + lax.broadcasted_iota(jnp.int32, tile.shape, 0)
    tile = jnp.where(rows < n_valid, tile, 0.0)
    h_ref[...] = tile
    ms = jnp.mean(tile * tile, axis=-1, keepdims=True)
    hn_ref[...] = (tile * lax.rsqrt(ms + EPS) * g_ref[...]).astype(hn_ref.dtype)


def _assemble_norm(meta, x, g, n_pad, tm):
    n_meta, d = meta.shape
    seq = x.shape[0]
    assert n_meta % 8 == 0 and tm % n_meta == 0 and seq % n_meta == 0 and n_pad % tm == 0
    per = tm // n_meta
    last_x = pl.cdiv(seq, tm) - 1
    last_prev = seq // n_meta - 1
    blocks = _nbytes((tm + 2 * n_meta, d), F32) + _nbytes((tm, d), F32) + _nbytes((tm, d), BF16)
    return pl.pallas_call(
        functools.partial(_assemble_norm_kernel, n_valid=n_meta + seq),
        out_shape=(jax.ShapeDtypeStruct((n_pad, d), F32), jax.ShapeDtypeStruct((n_pad, d), BF16)),
        grid=(n_pad // tm,),
        in_specs=[pl.BlockSpec((n_meta, d), lambda i: (0, 0)),
                  pl.BlockSpec((n_meta, d), lambda i: (jnp.clip(i * per - 1, 0, last_prev), 0)),
                  pl.BlockSpec((tm, d), lambda i: (jnp.minimum(i, last_x), 0)),
                  pl.BlockSpec((1, d), lambda i: (0, 0))],
        out_specs=(pl.BlockSpec((tm, d), lambda i: (i, 0)), pl.BlockSpec((tm, d), lambda i: (i, 0))),
        compiler_params=_compiler_params(("parallel",), blocks),
        name="assemble_norm",
    )(meta.astype(F32), x.astype(F32), x.astype(F32), g.reshape(1, d).astype(F32))


def _final_norm_kernel(a_ref, b_ref, g_ref, o_ref, *, skip):
    x = jnp.concatenate([a_ref[skip:, :], b_ref[...]], axis=0)
    ms = jnp.mean(x * x, axis=-1, keepdims=True)
    o_ref[...] = (x * lax.rsqrt(ms + EPS) * g_ref[...]).astype(o_ref.dtype)


def _final_norm(x, g, skip, n_out, out_dtype, tm):
    n, d = x.shape
    assert skip % 8 == 0 and tm % skip == 0 and n_out % tm == 0 and n >= n_out + skip
    per = tm // skip
    blocks = _nbytes((tm + skip, d), F32) + _nbytes((tm, d), out_dtype)
    return pl.pallas_call(
        functools.partial(_final_norm_kernel, skip=skip),
        out_shape=jax.ShapeDtypeStruct((n_out, d), out_dtype),
        grid=(n_out // tm,),
        in_specs=[pl.BlockSpec((tm, d), lambda i: (i, 0)),
                  pl.BlockSpec((skip, d), lambda i: ((i + 1) * per, 0)),
                  pl.BlockSpec((1, d), lambda i: (0, 0))],
        out_specs=pl.BlockSpec((tm, d), lambda i: (i, 0)),
        compiler_params=_compiler_params(("parallel",), blocks),
        name="final_norm",
    )(x, x, g.reshape(1, d).astype(F32))


def _mm_kernel(*refs, n_extra, epilogue, tm):
    a_ref, w_ref = refs[0], refs[1]
    extras = refs[2:2 + n_extra]
    o_ref = refs[2 + n_extra]
    acc = _dot(a_ref[...], w_ref[...])
    row0 = pl.program_id(0) * tm
    o_ref[...] = epilogue(acc, row0, *[e[...] for e in extras]).astype(o_ref.dtype)


def _w_spec(k, tn, layer, jb0):
    if layer is None:
        return pl.BlockSpec((k, tn), lambda i, j: (0, jb0 + j))
    return pl.BlockSpec((None, k, tn), lambda i, j: (layer, 0, jb0 + j))


def _call_with_cast(kernel_fn, cast, *, grid, in_specs, out_specs, out_shape, operands, scratch_shapes=(),
                    block_bytes, scratch_bytes=0, name):
    if cast is None:
        return pl.pallas_call(
            kernel_fn, out_shape=tuple(out_shape), grid=grid, in_specs=list(in_specs), out_specs=tuple(out_specs),
            scratch_shapes=list(scratch_shapes),
            compiler_params=_compiler_params(("parallel", "arbitrary"), block_bytes, scratch_bytes), name=name,
        )(*operands)

    stack, layer = cast
    _, k, n = stack.shape
    n_steps = grid[0] * grid[1]
    options = []
    for rows in range(BF16_ROWS, k + 1, BF16_ROWS):
        if k % rows == 0 and k // rows <= n_steps:
            splits = max(c for c in range(1, n_steps // (k // rows) + 1) if n % (c * LANES) == 0)
            options.append((rows * (n // splits), rows, splits))
    _, rows, splits = min(options)
    cols = n // splits
    last_chunk = (k // rows) * splits - 1
    n_in, n_out = len(in_specs), len(out_specs)

    def chunk(i, j):
        q = jnp.minimum(i * grid[1] + j, last_chunk)
        return q // splits, q % splits

    def body(*refs):
        src_ref, dst_ref = refs[n_in], refs[n_in + 1 + n_out]
        dst_ref[...] = src_ref[...].astype(dst_ref.dtype)
        kernel_fn(*refs[:n_in], *refs[n_in + 1:n_in + 1 + n_out], *refs[n_in + 2 + n_out:])

    cast_bytes = _nbytes((rows, cols), F32) + _nbytes((rows, cols), BF16)
    return pl.pallas_call(
        body,
        out_shape=(*out_shape, jax.ShapeDtypeStruct((k, n), BF16)),
        grid=grid,
        in_specs=[*in_specs, pl.BlockSpec((None, rows, cols), lambda i, j: (layer, *chunk(i, j)))],
        out_specs=(*out_specs, pl.BlockSpec((rows, cols), lambda i, j: chunk(i, j))),
        scratch_shapes=list(scratch_shapes),
        compiler_params=_compiler_params(("arbitrary", "arbitrary"), block_bytes + cast_bytes, scratch_bytes),
        name=name,
    )(*operands, stack)


def _mm(a, w, layer, *, col0, n_cols, tm, tn, epilogue, extras=(), extra_specs=(), out_shape, out_spec, name,
        cast=None):
    n, k = a.shape
    assert n % tm == 0 and n_cols % tn == 0 and col0 % tn == 0
    blocks = (_nbytes((tm, k), a.dtype) + _nbytes((k, tn), w.dtype) + _nbytes((tm, tn), F32)
              + sum(_nbytes((tm, tn), e.dtype) for e in extras))
    outs = _call_with_cast(
        functools.partial(_mm_kernel, n_extra=len(extras), epilogue=epilogue, tm=tm), cast,
        grid=(n // tm, n_cols // tn),
        in_specs=[pl.BlockSpec((tm, k), lambda i, j: (i, 0)),
                  _w_spec(k, tn, layer, col0 // tn),
                  *extra_specs],
        out_specs=[out_spec], out_shape=[out_shape], operands=(a, w, *extras), block_bytes=blocks, name=name)
    return outs[0] if cast is None else outs


def _epi_pieces(acc, row0):
    return jnp.stack([acc[:, h * LANES:(h + 1) * LANES] for h in range(acc.shape[1] // LANES)], axis=0)


def _epi_sigmoid(acc, row0):
    return _sigmoid(acc)


def _epi_residual(acc, row0, res, *, n_valid):
    rows = row0 + lax.broadcasted_iota(jnp.int32, acc.shape, 0)
    return jnp.where(rows < n_valid, res + acc, 0.0)


def _inproj_conv_kernel(a_ref, wb_ref, wc_ref, wv_ref, bg_ref, cv_ref):
    a = a_ref[...]
    bg_ref[...] = _dot(a, wb_ref[...]).astype(bg_ref.dtype)
    cv_ref[...] = (_dot(a, wc_ref[...]) * _dot(a, wv_ref[...])).astype(cv_ref.dtype)


def _inproj_conv(hn, w_in, layer, d_conv, tm, tn, cast=None):
    n, k = hn.shape
    nb = d_conv // tn
    blocks = _nbytes((tm, k), BF16) + 3 * _nbytes((k, tn), BF16) + 2 * _nbytes((tm, tn), BF16)
    out = jax.ShapeDtypeStruct((n, d_conv), BF16)
    return _call_with_cast(
        _inproj_conv_kernel, cast,
        grid=(n // tm, nb),
        in_specs=[pl.BlockSpec((tm, k), lambda i, j: (i, 0)),
                  _w_spec(k, tn, layer, 0),
                  _w_spec(k, tn, layer, nb),
                  _w_spec(k, tn, layer, 2 * nb)],
        out_specs=[pl.BlockSpec((tm, tn), lambda i, j: (i, j)),
                   pl.BlockSpec((tm, tn), lambda i, j: (i, j))],
        out_shape=[out, out], operands=(hn, w_in, w_in, w_in), block_bytes=blocks, name="inproj_conv")


def _shifted_rows(x, prev_row, next_row):
    t = x.shape[0]
    row = lax.broadcasted_iota(jnp.int32, x.shape, 0)
    up = jnp.where(row == 0, prev_row, pltpu.roll(x, 1, 0))
    dn = jnp.where(row == t - 1, next_row, pltpu.roll(x, t - 1, 0))
    return up, dn


def _branch_a_kernel(bg_ref, cv_ref, cvp_ref, cvn_ref, cw_ref, w_ref, gate_ref, o_ref, z_ref, *, chunk):
    i = pl.program_id(0)
    last = pl.num_programs(0) - 1

    @pl.when(pl.program_id(1) == 0)
    def _():
        for c0 in range(0, z_ref.shape[1], chunk):
            cols = slice(c0, c0 + chunk)
            x = cv_ref[:, cols].astype(F32)
            prev = cvp_ref[:, cols].astype(F32)[BF16_ROWS - 1:BF16_ROWS, :]
            nxt = cvn_ref[:, cols].astype(F32)[0:1, :]
            prev = jnp.where(i > 0, prev, 0.0)
            nxt = jnp.where(i < last, nxt, 0.0)
            up, dn = _shifted_rows(x, prev, nxt)
            w = cw_ref[:, cols]
            conv = up * w[0:1, :] + x * w[1:2, :] + dn * w[2:3, :]
            z_ref[:, cols] = (bg_ref[:, cols].astype(F32) * conv).astype(z_ref.dtype)

    o_ref[...] = (_dot(z_ref[...], w_ref[...]) * gate_ref[...].astype(F32)).astype(o_ref.dtype)


def _branch_a(bg, cv, conv_w, w_a, layer, gates, d_model, tm, tn):
    n, c = bg.shape
    hb = tm // BF16_ROWS
    n_hb = n // BF16_ROWS
    blocks = (2 * _nbytes((tm, c), BF16) + 2 * _nbytes((BF16_ROWS, c), BF16) + _nbytes((8, c), F32)
              + _nbytes((c, tn), BF16) + 2 * _nbytes((tm, tn), BF16))
    scratch = _nbytes((tm, c), BF16)
    return pl.pallas_call(
        functools.partial(_branch_a_kernel, chunk=_pick_tile(c, 512, LANES)),
        out_shape=jax.ShapeDtypeStruct((n, d_model), BF16),
        grid=(n // tm, d_model // tn),
        in_specs=[pl.BlockSpec((tm, c), lambda i, j: (i, 0)),
                  pl.BlockSpec((tm, c), lambda i, j: (i, 0)),
                  pl.BlockSpec((BF16_ROWS, c), lambda i, j: (jnp.maximum(i * hb - 1, 0), 0)),
                  pl.BlockSpec((BF16_ROWS, c), lambda i, j: (jnp.minimum((i + 1) * hb, n_hb - 1), 0)),
                  pl.BlockSpec((3, c), lambda i, j: (0, 0)),
                  _w_spec(c, tn, layer, 0),
                  pl.BlockSpec((tm, tn), lambda i, j: (i, j))],
        out_specs=pl.BlockSpec((tm, tn), lambda i, j: (i, j)),
        scratch_shapes=[pltpu.VMEM((tm, c), BF16)],
        compiler_params=_compiler_params(("parallel", "arbitrary"), blocks, scratch),
        name="branch_a",
    )(bg, cv, cv, cv, conv_w, w_a, gates)


def _cmul(ar, ai, br, bi):
    return ar * br - ai * bi, ar * bi + ai * br


def _s5_prep_kernel(lre_ref, lim_ref, lst_ref, bre_ref, bim_ref, cre_ref, cim_ref,
                    ere_ref, eim_ref, care_ref, caim_ref, klag_ref, apre_ref, apim_ref):
    lam_re, lam_im = lre_ref[...], lim_ref[...]
    dt = jnp.exp(lst_ref[...])
    mag = jnp.exp(lam_re * dt)
    a_re = mag * jnp.cos(lam_im * dt)
    a_im = mag * jnp.sin(lam_im * dt)
    nr, ni = a_re - 1.0, a_im
    den = lam_re * lam_re + lam_im * lam_im
    f_re = (nr * lam_re + ni * lam_im) / den
    f_im = (ni * lam_re - nr * lam_im) / den
    b_re, b_im = bre_ref[...], bim_ref[...]
    bb_re, bb_im = _cmul(f_re, f_im, b_re, b_im)
    c_re, c_im = cre_ref[...], cim_ref[...]

    pw = [(jnp.ones_like(a_re), jnp.zeros_like(a_im))]
    for _ in range(T_BLK):
        pw.append(_cmul(pw[-1][0], pw[-1][1], a_re, a_im))

    e_re, e_im, ck_re, ck_im = [], [], [], []
    for k in range(T_BLK + 1):
        cr, ci = _cmul(pw[k][0], pw[k][1], c_re, c_im)
        ck_re.append(cr)
        ck_im.append(ci)
    for k in range(T_BLK):
        er, ei = _cmul(pw[k][0], pw[k][1], bb_re, bb_im)
        e_re.append(er)
        e_im.append(ei)
    ere_ref[...] = jnp.concatenate(e_re, axis=1)
    eim_ref[...] = jnp.concatenate(e_im, axis=1)
    care_ref[...] = jnp.concatenate(ck_re[1:], axis=1)
    caim_ref[...] = jnp.concatenate([-ci for ci in ck_im[1:]], axis=1)

    dn = (((2,), (2,)), ((0,), (0,)))
    klag_ref[...] = (lax.dot_general(bb_re, jnp.concatenate(ck_re[:-1], axis=1), dn,
                                     precision=lax.Precision.HIGHEST, preferred_element_type=F32)
                     - lax.dot_general(bb_im, jnp.concatenate(ck_im[:-1], axis=1), dn,
                                       precision=lax.Precision.HIGHEST, preferred_element_type=F32))

    q_re, q_im = pw[T_BLK]
    r_re, r_im = q_re, q_im
    for m in range(8):
        apre_ref[m] = r_re
        apim_ref[m] = r_im
        r_re, r_im = _cmul(r_re, r_im, q_re, q_im)


def _s5_prep(lam_re, lam_im, log_step, b_re, b_im, c_re, c_im):
    depth, _, g, p = lam_re.shape
    h = b_re.shape[-1]
    lead = (depth, 2)
    b_re_t = jnp.swapaxes(b_re, -1, -2).astype(F32)
    b_im_t = jnp.swapaxes(b_im, -1, -2).astype(F32)

    gc = _pick_tile(g, 16, 1)

    def spec(*tail):
        zeros = (0,) * (len(tail) - 1)
        return pl.BlockSpec((None, None) + tail, lambda l, d, q: (l, d, q) + zeros)

    ap_spec = pl.BlockSpec((None, None, 8, gc, 1, p), lambda l, d, q: (l, d, 0, q, 0, 0))
    th = T_BLK * h
    outs = (jax.ShapeDtypeStruct(lead + (g, th, p), F32),) * 4 + (
        jax.ShapeDtypeStruct(lead + (g, h, th), F32),
        jax.ShapeDtypeStruct(lead + (8, g, 1, p), F32),
        jax.ShapeDtypeStruct(lead + (8, g, 1, p), F32))
    blocks = 4 * _nbytes((gc, th, LANES), F32) + 8 * _nbytes((gc, h, LANES), F32)
    return pl.pallas_call(
        _s5_prep_kernel,
        out_shape=outs,
        grid=lead + (g // gc,),
        in_specs=[spec(gc, 1, p), spec(gc, 1, p), spec(gc, 1, 1), spec(gc, h, p), spec(gc, h, p),
                  spec(gc, h, p), spec(gc, h, p)],
        out_specs=(spec(gc, th, p),) * 4 + (spec(gc, h, th), ap_spec, ap_spec),
        compiler_params=_compiler_params(("parallel", "parallel", "parallel"), blocks,
                                         16 * _nbytes((gc, th, LANES), F32)),
        name="s5_prep",
    )(lam_re.astype(F32)[:, :, :, None, :], lam_im.astype(F32)[:, :, :, None, :],
      log_step.astype(F32)[..., None, None], b_re_t, b_im_t, c_re.astype(F32), c_im.astype(F32))


def _block_diag_kernel(c_ref, e_ref, rg_ref, cg_ref, o_ref):
    spread = _dot(c_ref[...].astype(BF16), e_ref[...])
    o_ref[...] = jnp.where(rg_ref[...] == cg_ref[...], spread, 0.0).astype(o_ref.dtype)


def _block_diag(compact, gl, row_inner, col_inner, name):
    n_cb, rows, kc = compact.shape
    n_out = kc * gl
    tr = _pick_tile(rows, 512, BF16_ROWS)
    col = np.arange(n_out)
    spread = np.arange(kc)[:, None] == (col // (gl * col_inner)) * col_inner + col % col_inner
    row_group = ((np.arange(rows) // row_inner) % gl).astype(np.int32)[:, None]
    col_group = ((col // col_inner) % gl).astype(np.int32)[None, :]
    blocks = (_nbytes((tr, kc), F32) + _nbytes((kc, n_out), BF16) + _nbytes((tr, LANES), jnp.int32)
              + _nbytes((8, n_out), jnp.int32) + _nbytes((tr, n_out), BF16))
    return pl.pallas_call(
        _block_diag_kernel,
        out_shape=jax.ShapeDtypeStruct((n_cb, rows, n_out), BF16),
        grid=(n_cb, rows // tr),
        in_specs=[pl.BlockSpec((None, tr, kc), lambda c, i: (c, i, 0)),
                  pl.BlockSpec((kc, n_out), lambda c, i: (0, 0)),
                  pl.BlockSpec((tr, 1), lambda c, i: (i, 0)),
                  pl.BlockSpec((1, n_out), lambda c, i: (0, 0))],
        out_specs=pl.BlockSpec((None, tr, n_out), lambda c, i: (c, i, 0)),
        compiler_params=_compiler_params(("parallel", "parallel"), blocks),
        name=name,
    )(compact, jnp.asarray(spread, BF16), jnp.asarray(row_group), jnp.asarray(col_group))


def _s5_weights(prep, n_cb, gl, h, p):
    e_re, e_im, ca_re, ca_im, klag, ap_re, ap_im = prep
    t = T_BLK

    def split(x):
        return x.reshape(2, n_cb * gl, t, h, p)

    e = jnp.stack([split(e_re), split(e_im)], axis=1)
    e_sel = jnp.stack([e[0, :, :, ::-1], e[1]], axis=0)
    e_sel = e_sel.reshape(2, 2, n_cb, gl, t, h, p)
    e_sel = jnp.transpose(e_sel, (2, 4, 3, 5, 0, 1, 6))
    w_s = _block_diag(e_sel.reshape(n_cb, t * gl * h, 4 * p), gl, h, p, "s5_w_states")

    def carry(ca, name):
        ca = split(ca)
        sel = jnp.stack([ca[0], ca[1, :, ::-1]], axis=0)
        sel = sel.reshape(2, n_cb, gl, t, h, p)
        sel = jnp.transpose(sel, (1, 0, 2, 5, 3, 4))
        return _block_diag(sel.reshape(n_cb, 2 * gl * p, t * h), gl, p, h, name)

    w_cre, w_cim = carry(ca_re, "s5_w_carry_re"), carry(ca_im, "s5_w_carry_im")

    kl = klag.reshape(2, n_cb * gl, h, t, h)
    pad_k = lambda x, before, after: jnp.pad(x, ((0, 0), (0, 0), (before, after), (0, 0)))
    kst = jnp.stack([pad_k(kl[0][:, :, :t - s], s, 0) + pad_k(kl[1][:, :, s::-1], 0, t - 1 - s)
                     for s in range(t)], axis=1)
    kst = kst.reshape(n_cb, gl, t, h, t * h)
    kst = jnp.transpose(kst, (0, 2, 1, 3, 4))
    w_i = _block_diag(kst.reshape(n_cb, t * gl * h, t * h), gl, h, h, "s5_w_intra")

    def lanes(ap):
        ap = jnp.stack([ap[0], ap[1, ::-1]], axis=0)
        ap = ap.reshape(2, 8, n_cb, gl * p)
        return jnp.transpose(ap, (2, 0, 1, 3))
    apw = jnp.stack([lanes(ap_re), lanes(ap_im)], axis=2)

    return w_s, w_cre, w_cim, w_i, apw


def _fold_time_blocks(u_ref, lhs_ref, r):
    pieces = u_ref.shape[0]
    for tau in range(T_BLK):
        for h in range(pieces):
            col = (tau * pieces + h) * LANES
            lhs_ref[:, col:col + LANES] = u_ref[h, pl.ds(tau, r, stride=T_BLK), :].astype(lhs_ref.dtype)


def _s5_states_kernel(u_ref, w_ref, s_ref, lhs_ref, *, r):
    @pl.when(pl.program_id(2) == 0)
    def _():
        _fold_time_blocks(u_ref, lhs_ref, r)

    s_ref[...] = _dot(lhs_ref[...], w_ref[...])


def _s5_states(u, w_s, r, tn):
    n_cb, kw, ns = w_s.shape
    n = u.shape[1]
    nb = n // T_BLK
    ppc = u.shape[0] // n_cb
    blocks = _nbytes((ppc, T_BLK * r, LANES), F32) + _nbytes((kw, tn), BF16) + _nbytes((r, tn), F32)
    scratch = _nbytes((r, kw), BF16)
    return pl.pallas_call(
        functools.partial(_s5_states_kernel, r=r),
        out_shape=jax.ShapeDtypeStruct((n_cb, nb, ns), F32),
        grid=(n_cb, nb // r, ns // tn),
        in_specs=[pl.BlockSpec((ppc, T_BLK * r, LANES), lambda c, i, j: (c, i, 0)),
                  pl.BlockSpec((None, kw, tn), lambda c, i, j: (c, 0, j))],
        out_specs=pl.BlockSpec((None, r, tn), lambda c, i, j: (c, i, j)),
        scratch_shapes=[pltpu.VMEM((r, kw), BF16)],
        compiler_params=_compiler_params(("parallel", "parallel", "arbitrary"), blocks, scratch),
        name="s5_states",
    )(u, w_s)


def _s5_scan_kernel(sre_ref, sim_ref, apre_ref, apim_ref, xre_ref, xim_ref, *, n_pairs):
    d = pl.program_id(1)
    lw = sre_ref.shape[-1]
    ap_re, ap_im = apre_ref[...], apim_ref[...]
    row = lax.broadcasted_iota(jnp.int32, (8, lw), 0)

    def run(reverse):
        steps = []
        for shift in (1, 2, 4):
            src = (8 - shift) if reverse else (shift - 1)
            keep = (row < 8 - shift) if reverse else (row >= shift)
            steps.append((8 - shift if reverse else shift,
                          jnp.where(keep, ap_re[src:src + 1, :], 0.0),
                          jnp.where(keep, ap_im[src:src + 1, :], 0.0)))
        edge = 7 if reverse else 0
        out_row = 0 if reverse else 7
        nb_shift = 7 if reverse else 1

        def tile(r0, c_re, c_im):
            x_re = sre_ref[pl.ds(r0, 8), :]
            x_im = sim_ref[pl.ds(r0, 8), :]
            for amount, m_re, m_im in steps:
                s_re = pltpu.roll(x_re, amount, 0)
                s_im = pltpu.roll(x_im, amount, 0)
                x_re, x_im = x_re + m_re * s_re - m_im * s_im, x_im + m_re * s_im + m_im * s_re
            inc_re = x_re + ap_re * c_re - ap_im * c_im
            inc_im = x_im + ap_re * c_im + ap_im * c_re
            e_re = jnp.where(row == edge, c_re, pltpu.roll(inc_re, nb_shift, 0))
            e_im = jnp.where(row == edge, c_im, pltpu.roll(inc_im, nb_shift, 0))
            return e_re, e_im, inc_re[out_row:out_row + 1, :], inc_im[out_row:out_row + 1, :]

        def body(k, carry):
            c_re, c_im = carry
            pair = (n_pairs - 1 - k) if reverse else k
            r0 = pl.multiple_of(pair * BF16_ROWS, BF16_ROWS)
            e_re, e_im = [None, None], [None, None]
            for half in ((1, 0) if reverse else (0, 1)):
                e_re[half], e_im[half], c_re, c_im = tile(pl.multiple_of(r0 + 8 * half, 8), c_re, c_im)
            xre_ref[pl.ds(r0, BF16_ROWS), :] = jnp.concatenate(e_re, axis=0).astype(xre_ref.dtype)
            xim_ref[pl.ds(r0, BF16_ROWS), :] = jnp.concatenate(e_im, axis=0).astype(xim_ref.dtype)
            return c_re, c_im

        zero = jnp.zeros((1, lw), F32)
        lax.fori_loop(0, n_pairs, body, (zero, zero))

    @pl.when(d == 0)
    def _():
        run(False)

    @pl.when(d == 1)
    def _():
        run(True)


def _s5_scan(s, apw, lw):
    n_cb, nb, ns = s.shape
    glp = ns // 4
    nq = glp // lw
    blocks = 2 * _nbytes((nb, lw), F32) + 2 * _nbytes((nb, lw), BF16) + 2 * _nbytes((8, lw), F32)
    out = jax.ShapeDtypeStruct((n_cb, nb, 2 * glp), BF16)

    def s_spec(ri):
        return pl.BlockSpec((None, nb, lw), lambda c, d, q: (c, 0, (2 * d + ri) * nq + q))

    def ap_spec(ri):
        return pl.BlockSpec((None, None, None, 8, lw), lambda c, d, q: (c, d, ri, 0, q))

    x_spec = pl.BlockSpec((None, nb, lw), lambda c, d, q: (c, 0, d * nq + q))
    return pl.pallas_call(
        functools.partial(_s5_scan_kernel, n_pairs=nb // BF16_ROWS),
        out_shape=(out, out),
        grid=(n_cb, 2, nq),
        in_specs=[s_spec(0), s_spec(1), ap_spec(0), ap_spec(1)],
        out_specs=(x_spec, x_spec),
        compiler_params=_compiler_params(("parallel", "parallel", "parallel"), blocks),
        name="s5_scan",
    )(s, s, apw, apw)


def _s5_out_kernel(u_ref, xre_ref, xim_ref, wi_ref, wcre_ref, wcim_ref, y_ref, lhs_ref, *, r, tn):
    j = pl.program_id(2)
    pieces = u_ref.shape[0]

    @pl.when(j == 0)
    def _():
        _fold_time_blocks(u_ref, lhs_ref, r)

    y = _dot(lhs_ref[...], wi_ref[...])
    y = y + _dot(xre_ref[...], wcre_ref[...])
    y = y + _dot(xim_ref[...], wcim_ref[...])

    per = tn // (pieces * LANES)
    for jj in range(T_BLK // per):
        @pl.when(j == jj)
        def _():
            for k in range(per):
                for h in range(pieces):
                    col = (k * pieces + h) * LANES
                    y_ref[h, pl.ds(jj * per + k, r, stride=T_BLK), :] = y[:, col:col + LANES]


def _s5_out(u, x_re, x_im, w_i, w_cre, w_cim, r, tn):
    n_cb, kw, _ = w_i.shape
    n = u.shape[1]
    nb = n // T_BLK
    ppc = u.shape[0] // n_cb
    kx = x_re.shape[-1]
    assert tn % (ppc * LANES) == 0 and kw % tn == 0
    blocks = (2 * _nbytes((ppc, T_BLK * r, LANES), F32) + 2 * _nbytes((r, kx), BF16) + _nbytes((kw, tn), BF16)
              + 2 * _nbytes((kx, tn), BF16) + _nbytes((r, tn), F32))
    scratch = _nbytes((r, kw), BF16)
    return pl.pallas_call(
        functools.partial(_s5_out_kernel, r=r, tn=tn),
        out_shape=jax.ShapeDtypeStruct(u.shape, F32),
        grid=(n_cb, nb // r, kw // tn),
        in_specs=[pl.BlockSpec((ppc, T_BLK * r, LANES), lambda c, i, j: (c, i, 0)),
                  pl.BlockSpec((None, r, kx), lambda c, i, j: (c, i, 0)),
                  pl.BlockSpec((None, r, kx), lambda c, i, j: (c, i, 0)),
                  pl.BlockSpec((None, kw, tn), lambda c, i, j: (c, 0, j)),
                  pl.BlockSpec((None, kx, tn), lambda c, i, j: (c, 0, j)),
                  pl.BlockSpec((None, kx, tn), lambda c, i, j: (c, 0, j))],
        out_specs=pl.BlockSpec((ppc, T_BLK * r, LANES), lambda c, i, j: (c, i, 0)),
        scratch_shapes=[pltpu.VMEM((r, kw), BF16)],
        compiler_params=_compiler_params(("parallel", "parallel", "arbitrary"), blocks, scratch),
        name="s5_out",
    )(u, x_re, x_im, w_i, w_cre, w_cim)


def _glu_merge_kernel(y_ref, u_ref, d_ref, w1_ref, w2_ref, pa_ref, gb_ref, m_ref, s_ref):
    @pl.when(pl.program_id(1) == 0)
    def _():
        for c in range(y_ref.shape[0]):
            s = jax.nn.gelu(y_ref[c] + d_ref[c] * u_ref[c])
            s_ref[:, c * LANES:(c + 1) * LANES] = s.astype(s_ref.dtype)

    a = s_ref[...]
    y_b = _dot(a, w1_ref[...]) * _sigmoid(_dot(a, w2_ref[...]))
    m_ref[...] = (pa_ref[...].astype(F32) + gb_ref[...].astype(F32) * y_b).astype(m_ref.dtype)


def _glu_merge(y, u, d_skip, w_glu, layer, pa, gates, d_model, tm, tn):
    n_pc, n, _ = y.shape
    k = n_pc * LANES
    nb = d_model // tn
    blocks = (2 * _nbytes((n_pc, tm, LANES), F32) + 2 * _nbytes((k, tn), BF16) + 3 * _nbytes((tm, tn), BF16))
    scratch = _nbytes((tm, k), BF16)
    return pl.pallas_call(
        _glu_merge_kernel,
        out_shape=jax.ShapeDtypeStruct((n, d_model), BF16),
        grid=(n // tm, nb),
        in_specs=[pl.BlockSpec((n_pc, tm, LANES), lambda i, j: (0, i, 0)),
                  pl.BlockSpec((n_pc, tm, LANES), lambda i, j: (0, i, 0)),
                  pl.BlockSpec((n_pc, 1, LANES), lambda i, j: (0, 0, 0)),
                  _w_spec(k, tn, layer, 0),
                  _w_spec(k, tn, layer, nb),
                  pl.BlockSpec((tm, tn), lambda i, j: (i, j)),
                  pl.BlockSpec((tm, tn), lambda i, j: (i, nb + j))],
        out_specs=pl.BlockSpec((tm, tn), lambda i, j: (i, j)),
        scratch_shapes=[pltpu.VMEM((tm, k), BF16)],
        compiler_params=_compiler_params(("parallel", "arbitrary"), blocks, scratch),
        name="glu_merge",
    )(y, u, d_skip, w_glu, w_glu, pa, gates)


def _ffn_up_kernel(h_ref, hp_ref, hn_ref, wg_ref, wv_ref, cg_ref, cv_ref, act_ref, lhs_ref, *, tm):
    i = pl.program_id(0)
    j = pl.program_id(1)
    last = pl.num_programs(0) - 1
    halo = BF16_ROWS

    @pl.when(j == 0)
    def _():
        row = lax.broadcasted_iota(jnp.int32, hp_ref.shape, 0)
        after = jnp.where(jnp.logical_and(row == 0, i < last), hn_ref[...].astype(F32), 0.0)
        edge = jnp.where(jnp.logical_and(row == halo - 1, i > 0), hp_ref[...].astype(F32), after)
        lhs_ref[0:tm, :] = h_ref[...]
        lhs_ref[tm:, :] = edge.astype(lhs_ref.dtype)

    a = lhs_ref[...]
    rows = tm + halo

    def conv(w_ref, cw_ref):
        u = _dot(a, w_ref[...])
        cw = cw_ref[...]
        up = pltpu.roll(u, 1, 0)[0:tm]
        dn = pltpu.roll(u, rows - 1, 0)[0:tm]
        return up * cw[0:1, :] + u[0:tm] * cw[1:2, :] + dn * cw[2:3, :]

    act_ref[...] = (jax.nn.gelu(conv(wg_ref, cg_ref)) * conv(wv_ref, cv_ref)).astype(act_ref.dtype)


def _ffn_up(hn, w_up, layer, conv_w, d_ff, tm, tn, cast=None):
    n, k = hn.shape
    nb = d_ff // tn
    hb = tm // BF16_ROWS
    n_hb = n // BF16_ROWS
    blocks = (_nbytes((tm + 2 * BF16_ROWS, k), BF16) + 2 * _nbytes((k, tn), BF16) + 2 * _nbytes((3, tn), F32)
              + _nbytes((tm, tn), BF16))
    scratch = _nbytes((tm + BF16_ROWS, k), BF16)
    outs = _call_with_cast(
        functools.partial(_ffn_up_kernel, tm=tm), cast,
        grid=(n // tm, nb),
        in_specs=[pl.BlockSpec((tm, k), lambda i, j: (i, 0)),
                  pl.BlockSpec((BF16_ROWS, k), lambda i, j: (jnp.maximum(i * hb - 1, 0), 0)),
                  pl.BlockSpec((BF16_ROWS, k), lambda i, j: (jnp.minimum((i + 1) * hb, n_hb - 1), 0)),
                  _w_spec(k, tn, layer, 0),
                  _w_spec(k, tn, layer, nb),
                  pl.BlockSpec((3, tn), lambda i, j: (0, j)),
                  pl.BlockSpec((3, tn), lambda i, j: (0, nb + j))],
        out_specs=[pl.BlockSpec((tm, tn), lambda i, j: (i, j))],
        out_shape=[jax.ShapeDtypeStruct((n, d_ff), BF16)],
        operands=(hn, hn, hn, w_up, w_up, conv_w, conv_w),
        scratch_shapes=[pltpu.VMEM((tm + BF16_ROWS, k), BF16)],
        block_bytes=blocks, scratch_bytes=scratch, name="ffn_up")
    return outs[0] if cast is None else outs


def kernel(x, meta_tokens, norm_mix, w_in, conv_a_w, w_a, ssm_lambda_re, ssm_lambda_im, ssm_log_step,
           ssm_b_re, ssm_b_im, ssm_c_re, ssm_c_im, ssm_d, w_glu, w_out, norm_ffn, w_up, conv_ffn_w,
           w_down, norm_final):
    bsz, seq, d_model = x.shape
    n_meta = meta_tokens.shape[0]
    depth = w_in.shape[0]
    d_conv = w_a.shape[1]
    d_ssm = ssm_d.shape[1]
    d_ff = w_down.shape[1]
    n_groups, p_state, h_grp = ssm_b_re.shape[2:]
    assert w_in.shape[2] == 3 * d_conv + d_ssm + 2 * d_model and n_groups * h_grp == d_ssm

    cbw = min(S5_CB_WIDTH, d_ssm)
    n_cb = d_ssm // cbw
    gl = cbw // h_grp
    assert n_cb * cbw == d_ssm and gl * h_grp == cbw

    n_true = n_meta + seq
    row_align = T_BLK * BF16_ROWS
    n_pad = _round_up(n_true, PAD_ROWS if n_true >= 4 * PAD_ROWS else row_align)
    assert n_pad % row_align == 0
    nb = n_pad // T_BLK

    tm = _pick_tile(n_pad, TM_TARGET, BF16_ROWS)
    tm_up = _pick_tile(n_pad, TM_FFN_UP_TARGET, BF16_ROWS)
    tm_half = _pick_tile(n_pad, TM_HALF_TARGET, BF16_ROWS)
    tm_ew = _pick_tile(n_pad, TM_EW_TARGET, BF16_ROWS)
    r_s5 = _pick_tile(nb, R_S5_TARGET, BF16_ROWS)
    tn = lambda n, target=512: _pick_tile(n, target, 128)

    prep = _s5_prep(ssm_lambda_re, ssm_lambda_im, ssm_log_step, ssm_b_re, ssm_b_im, ssm_c_re, ssm_c_im)
    d_skip = ssm_d.astype(F32).reshape(depth, d_ssm // LANES, 1, LANES)

    residual = functools.partial(_epi_residual, n_valid=n_true)
    tile_spec = lambda t, rows=tm: pl.BlockSpec((rows, t), lambda i, j: (i, j))
    u_off = 3 * d_conv
    gate_off = u_off + d_ssm

    w_a_b, w_glu_b, w_out_b = (w.astype(BF16) for w in (w_a, w_glu, w_out))
    w_in_first = w_in[0].astype(BF16)

    outs = []
    for b in range(bsz):
        w_in_l = w_in_first
        for l in range(depth):
            w_s, w_cre, w_cim, w_i, apw = _s5_weights([t[l] for t in prep], n_cb, gl, h_grp, p_state)

            if l == 0:
                h_res, hn = _assemble_norm(meta_tokens, x[b], norm_mix[0], n_pad, tm_ew)
            else:
                hn = _rmsnorm(h_res, norm_mix[l], BF16, tm_ew)
            bg, cv, w_down_l = _inproj_conv(hn, w_in_l, None, d_conv, tm_half, tn(d_conv), cast=(w_down, l))
            t_u = tn(d_ssm, 256)
            u = _mm(hn, w_in_l, None, col0=u_off, n_cols=d_ssm, tm=tm, tn=t_u, epilogue=_epi_pieces,
                    out_shape=jax.ShapeDtypeStruct((d_ssm // LANES, n_pad, LANES), F32),
                    out_spec=pl.BlockSpec((t_u // LANES, tm, LANES), lambda i, j: (j, i, 0)), name="inproj_u")
            t_g = tn(2 * d_model)
            gates, w_up_l = _mm(hn, w_in_l, None, col0=gate_off, n_cols=2 * d_model, tm=tm, tn=t_g,
                                epilogue=_epi_sigmoid,
                                out_shape=jax.ShapeDtypeStruct((n_pad, 2 * d_model), BF16),
                                out_spec=tile_spec(t_g), name="inproj_gates", cast=(w_up, l))

            pa = _branch_a(bg, cv, conv_a_w[l].astype(F32), w_a_b, l, gates, d_model, tm, tn(d_model, 1024))
            t_d = tn(d_model)

            states = _s5_states(u, w_s, _pick_tile(nb, R_S5_STATES_TARGET, BF16_ROWS), tn(w_s.shape[-1]))
            x_re, x_im = _s5_scan(states, apw, tn(gl * p_state, SCAN_LANES_TARGET))
            y = _s5_out(u, x_re, x_im, w_i, w_cre, w_cim, r_s5, tn(T_BLK * cbw))

            merged = _glu_merge(y, u, d_skip[l], w_glu_b, l, pa, gates, d_model, tm_half, tn(d_model, 1024))
            h_res = _mm(merged, w_out_b, l, col0=0, n_cols=d_model, tm=tm, tn=t_d, epilogue=residual,
                        extras=(h_res,), extra_specs=(tile_spec(t_d),),
                        out_shape=jax.ShapeDtypeStruct((n_pad, d_model), F32), out_spec=tile_spec(t_d),
                        name="out_proj")

            hn = _rmsnorm(h_res, norm_ffn[l], BF16, tm_ew)
            if l + 1 < depth:
                act, w_in_l = _ffn_up(hn, w_up_l, None, conv_ffn_w[l].astype(F32), d_ff, tm_up, tn(d_ff),
                                      cast=(w_in, l + 1))
            else:
                act = _ffn_up(hn, w_up_l, None, conv_ffn_w[l].astype(F32), d_ff, tm_up, tn(d_ff))
            t_o = tn(d_model, 256)
            h_res = _mm(act, w_down_l, None, col0=0, n_cols=d_model, tm=tm_half, tn=t_o, epilogue=residual,
                        extras=(h_res,), extra_specs=(tile_spec(t_o, tm_half),),
                        out_shape=jax.ShapeDtypeStruct((n_pad, d_model), F32), out_spec=tile_spec(t_o, tm_half),
                        name="ffn_down")

        outs.append(_final_norm(h_res, norm_final, n_meta, seq, x.dtype, _pick_tile(seq, TM_EW_TARGET, n_meta)))
    return jnp.stack(outs, axis=0)
```

```python
import functools

import numpy as np
import jax
import jax.numpy as jnp
from jax import lax
from jax.experimental import pallas as pl
from jax.experimental.pallas import tpu as pltpu

F32 = jnp.float32
BF16 = jnp.bfloat16
EPS = 1e-6

T_BLK = 16
S5_CB_WIDTH = 128
PAD_ROWS = 1280
BF16_ROWS = 16
LANES = 128
VMEM_CAP_BYTES = 60000 * 1024
VMEM_TEMP_BYTES = 20 * 1024 * 1024

TM_TARGET = 1280
TM_FFN_UP_TARGET = 832
TM_HALF_TARGET = 640
R_S5_STATES_TARGET = 1040
TM_EW_TARGET = 320
R_S5_TARGET = 1040
SCAN_LANES_TARGET = 256


def _round_up(n, m):
    return (n + m - 1) // m * m


def _pick_tile(n, target, align):
    best = 0
    for t in range(align, min(n, target) + 1, align):
        if n % t == 0:
            best = t
    assert best > 0, (n, target, align)
    return best


def _nbytes(shape, dtype):
    return int(np.prod(shape)) * jnp.dtype(dtype).itemsize


def _compiler_params(semantics, block_bytes, scratch_bytes=0):
    est = 2 * block_bytes + scratch_bytes + VMEM_TEMP_BYTES
    return pltpu.CompilerParams(dimension_semantics=semantics,
                                vmem_limit_bytes=int(min(VMEM_CAP_BYTES, est)))


def _dot(a, b):
    return jnp.dot(a, b, preferred_element_type=F32)


def _sigmoid(x):
    return 0.5 * (1.0 + jnp.tanh(0.5 * x))


def _rmsnorm_kernel(x_ref, g_ref, o_ref):
    x = x_ref[...]
    ms = jnp.mean(x * x, axis=-1, keepdims=True)
    o_ref[...] = (x * lax.rsqrt(ms + EPS) * g_ref[...]).astype(o_ref.dtype)


def _rmsnorm(x, g, out_dtype, tm):
    n, d = x.shape
    blocks = _nbytes((tm, d), F32) + _nbytes((tm, d), out_dtype)
    return pl.pallas_call(
        _rmsnorm_kernel,
        out_shape=jax.ShapeDtypeStruct((n, d), out_dtype),
        grid=(n // tm,),
        in_specs=[pl.BlockSpec((tm, d), lambda i: (i, 0)),
                  pl.BlockSpec((1, d), lambda i: (0, 0))],
        out_specs=pl.BlockSpec((tm, d), lambda i: (i, 0)),
        compiler_params=_compiler_params(("parallel",), blocks),
        name="rmsnorm",
    )(x, g.reshape(1, d).astype(F32))


def _assemble_norm_kernel(meta_ref, prev_ref, x_ref, g_ref, h_ref, hn_ref, *, n_valid):
    i = pl.program_id(0)
    tm = h_ref.shape[0]
    n_meta = meta_ref.shape[0]
    head = jnp.where(i == 0, meta_ref[...], prev_ref[...])
    tile = jnp.concatenate([head, x_ref[0:tm - n_meta, :]], axis=0)
    rows = i * tm + lax.broadcasted_iota(jnp.int32, tile.shape, 0)
    tile = jnp.where(rows < n_valid, tile, 0.0)
    h_ref[...] = tile
    ms = jnp.mean(tile * tile, axis=-1, keepdims=True)
    hn_ref[...] = (tile * lax.rsqrt(ms + EPS) * g_ref[...]).astype(hn_ref.dtype)


def _assemble_norm(meta, x, g, n_pad, tm):
    n_meta, d = meta.shape
    seq = x.shape[0]
    assert n_meta % 8 == 0 and tm % n_meta == 0 and seq % n_meta == 0 and n_pad % tm == 0
    per = tm // n_meta
    last_x = pl.cdiv(seq, tm) - 1
    last_prev = seq // n_meta - 1
    blocks = _nbytes((tm + 2 * n_meta, d), F32) + _nbytes((tm, d), F32) + _nbytes((tm, d), BF16)
    return pl.pallas_call(
        functools.partial(_assemble_norm_kernel, n_valid=n_meta + seq),
        out_shape=(jax.ShapeDtypeStruct((n_pad, d), F32), jax.ShapeDtypeStruct((n_pad, d), BF16)),
        grid=(n_pad // tm,),
        in_specs=[pl.BlockSpec((n_meta, d), lambda i: (0, 0)),
                  pl.BlockSpec((n_meta, d), lambda i: (jnp.clip(i * per - 1, 0, last_prev), 0)),
                  pl.BlockSpec((tm, d), lambda i: (jnp.minimum(i, last_x), 0)),
                  pl.BlockSpec((1, d), lambda i: (0, 0))],
        out_specs=(pl.BlockSpec((tm, d), lambda i: (i, 0)), pl.BlockSpec((tm, d), lambda i: (i, 0))),
        compiler_params=_compiler_params(("parallel",), blocks),
        name="assemble_norm",
    )(meta.astype(F32), x.astype(F32), x.astype(F32), g.reshape(1, d).astype(F32))


def _final_norm_kernel(a_ref, b_ref, g_ref, o_ref, *, skip):
    x = jnp.concatenate([a_ref[skip:, :], b_ref[...]], axis=0)
    ms = jnp.mean(x * x, axis=-1, keepdims=True)
    o_ref[...] = (x * lax.rsqrt(ms + EPS) * g_ref[...]).astype(o_ref.dtype)


def _final_norm(x, g, skip, n_out, out_dtype, tm):
    n, d = x.shape
    assert skip % 8 == 0 and tm % skip == 0 and n_out % tm == 0 and n >= n_out + skip
    per = tm // skip
    blocks = _nbytes((tm + skip, d), F32) + _nbytes((tm, d), out_dtype)
    return pl.pallas_call(
        functools.partial(_final_norm_kernel, skip=skip),
        out_shape=jax.ShapeDtypeStruct((n_out, d), out_dtype),
        grid=(n_out // tm,),
        in_specs=[pl.BlockSpec((tm, d), lambda i: (i, 0)),
                  pl.BlockSpec((skip, d), lambda i: ((i + 1) * per, 0)),
                  pl.BlockSpec((1, d), lambda i: (0, 0))],
        out_specs=pl.BlockSpec((tm, d), lambda i: (i, 0)),
        compiler_params=_compiler_params(("parallel",), blocks),
        name="final_norm",
    )(x, x, g.reshape(1, d).astype(F32))


def _mm_kernel(*refs, n_extra, epilogue, tm):
    a_ref, w_ref = refs[0], refs[1]
    extras = refs[2:2 + n_extra]
    o_ref = refs[2 + n_extra]
    acc = _dot(a_ref[...], w_ref[...])
    row0 = pl.program_id(0) * tm
    o_ref[...] = epilogue(acc, row0, *[e[...] for e in extras]).astype(o_ref.dtype)


def _w_spec(k, tn, layer, jb0):
    if layer is None:
        return pl.BlockSpec((k, tn), lambda i, j: (0, jb0 + j))
    return pl.BlockSpec((None, k, tn), lambda i, j: (layer, 0, jb0 + j))


def _call_with_cast(kernel_fn, cast, *, grid, in_specs, out_specs, out_shape, operands, scratch_shapes=(),
                    block_bytes, scratch_bytes=0, name):
    if cast is None:
        return pl.pallas_call(
            kernel_fn, out_shape=tuple(out_shape), grid=grid, in_specs=list(in_specs), out_specs=tuple(out_specs),
            scratch_shapes=list(scratch_shapes),
            compiler_params=_compiler_params(("parallel", "arbitrary"), block_bytes, scratch_bytes), name=name,
        )(*operands)

    stack, layer = cast
    _, k, n = stack.shape
    n_steps = grid[0] * grid[1]
    options = []
    for rows in range(BF16_ROWS, k + 1, BF16_ROWS):
        if k % rows == 0 and k // rows <= n_steps:
            splits = max(c for c in range(1, n_steps // (k // rows) + 1) if n % (c * LANES) == 0)
            options.append((rows * (n // splits), rows, splits))
    _, rows, splits = min(options)
    cols = n // splits
    last_chunk = (k // rows) * splits - 1
    n_in, n_out = len(in_specs), len(out_specs)

    def chunk(i, j):
        q = jnp.minimum(i * grid[1] + j, last_chunk)
        return q // splits, q % splits

    def body(*refs):
        src_ref, dst_ref = refs[n_in], refs[n_in + 1 + n_out]
        dst_ref[...] = src_ref[...].astype(dst_ref.dtype)
        kernel_fn(*refs[:n_in], *refs[n_in + 1:n_in + 1 + n_out], *refs[n_in + 2 + n_out:])

    cast_bytes = _nbytes((rows, cols), F32) + _nbytes((rows, cols), BF16)
    return pl.pallas_call(
        body,
        out_shape=(*out_shape, jax.ShapeDtypeStruct((k, n), BF16)),
        grid=grid,
        in_specs=[*in_specs, pl.BlockSpec((None, rows, cols), lambda i, j: (layer, *chunk(i, j)))],
        out_specs=(*out_specs, pl.BlockSpec((rows, cols), lambda i, j: chunk(i, j))),
        scratch_shapes=list(scratch_shapes),
        compiler_params=_compiler_params(("arbitrary", "arbitrary"), block_bytes + cast_bytes, scratch_bytes),
        name=name,
    )(*operands, stack)


def _mm(a, w, layer, *, col0, n_cols, tm, tn, epilogue, extras=(), extra_specs=(), out_shape, out_spec, name,
        cast=None):
    n, k = a.shape
    assert n % tm == 0 and n_cols % tn == 0 and col0 % tn == 0
    blocks = (_nbytes((tm, k), a.dtype) + _nbytes((k, tn), w.dtype) + _nbytes((tm, tn), F32)
              + sum(_nbytes((tm, tn), e.dtype) for e in extras))
    outs = _call_with_cast(
        functools.partial(_mm_kernel, n_extra=len(extras), epilogue=epilogue, tm=tm), cast,
        grid=(n // tm, n_cols // tn),
        in_specs=[pl.BlockSpec((tm, k), lambda i, j: (i, 0)),
                  _w_spec(k, tn, layer, col0 // tn),
                  *extra_specs],
        out_specs=[out_spec], out_shape=[out_shape], operands=(a, w, *extras), block_bytes=blocks, name=name)
    return outs[0] if cast is None else outs


def _epi_pieces(acc, row0):
    return jnp.stack([acc[:, h * LANES:(h + 1) * LANES] for h in range(acc.shape[1] // LANES)], axis=0)


def _epi_sigmoid(acc, row0):
    return _sigmoid(acc)


def _epi_residual(acc, row0, res, *, n_valid):
    rows = row0 + lax.broadcasted_iota(jnp.int32, acc.shape, 0)
    return jnp.where(rows < n_valid, res + acc, 0.0)


def _inproj_conv_kernel(a_ref, wb_ref, wc_ref, wv_ref, bg_ref, cv_ref):
    a = a_ref[...]
    bg_ref[...] = _dot(a, wb_ref[...]).astype(bg_ref.dtype)
    cv_ref[...] = (_dot(a, wc_ref[...]) * _dot(a, wv_ref[...])).astype(cv_ref.dtype)


def _inproj_conv(hn, w_in, layer, d_conv, tm, tn, cast=None):
    n, k = hn.shape
    nb = d_conv // tn
    blocks = _nbytes((tm, k), BF16) + 3 * _nbytes((k, tn), BF16) + 2 * _nbytes((tm, tn), BF16)
    out = jax.ShapeDtypeStruct((n, d_conv), BF16)
    return _call_with_cast(
        _inproj_conv_kernel, cast,
        grid=(n // tm, nb),
        in_specs=[pl.BlockSpec((tm, k), lambda i, j: (i, 0)),
                  _w_spec(k, tn, layer, 0),
                  _w_spec(k, tn, layer, nb),
                  _w_spec(k, tn, layer, 2 * nb)],
        out_specs=[pl.BlockSpec((tm, tn), lambda i, j: (i, j)),
                   pl.BlockSpec((tm, tn), lambda i, j: (i, j))],
        out_shape=[out, out], operands=(hn, w_in, w_in, w_in), block_bytes=blocks, name="inproj_conv")


def _shifted_rows(x, prev_row, next_row):
    t = x.shape[0]
    row = lax.broadcasted_iota(jnp.int32, x.shape, 0)
    up = jnp.where(row == 0, prev_row, pltpu.roll(x, 1, 0))
    dn = jnp.where(row == t - 1, next_row, pltpu.roll(x, t - 1, 0))
    return up, dn


def _branch_a_kernel(bg_ref, cv_ref, cvp_ref, cvn_ref, cw_ref, w_ref, gate_ref, o_ref, z_ref, *, chunk):
    i = pl.program_id(0)
    last = pl.num_programs(0) - 1

    @pl.when(pl.program_id(1) == 0)
    def _():
        for c0 in range(0, z_ref.shape[1], chunk):
            cols = slice(c0, c0 + chunk)
            x = cv_ref[:, cols].astype(F32)
            prev = cvp_ref[:, cols].astype(F32)[BF16_ROWS - 1:BF16_ROWS, :]
            nxt = cvn_ref[:, cols].astype(F32)[0:1, :]
            prev = jnp.where(i > 0, prev, 0.0)
            nxt = jnp.where(i < last, nxt, 0.0)
            up, dn = _shifted_rows(x, prev, nxt)
            w = cw_ref[:, cols]
            conv = up * w[0:1, :] + x * w[1:2, :] + dn * w[2:3, :]
            z_ref[:, cols] = (bg_ref[:, cols].astype(F32) * conv).astype(z_ref.dtype)

    o_ref[...] = (_dot(z_ref[...], w_ref[...]) * gate_ref[...].astype(F32)).astype(o_ref.dtype)


def _branch_a(bg, cv, conv_w, w_a, layer, gates, d_model, tm, tn):
    n, c = bg.shape
    hb = tm // BF16_ROWS
    n_hb = n // BF16_ROWS
    blocks = (2 * _nbytes((tm, c), BF16) + 2 * _nbytes((BF16_ROWS, c), BF16) + _nbytes((8, c), F32)
              + _nbytes((c, tn), BF16) + 2 * _nbytes((tm, tn), BF16))
    scratch = _nbytes((tm, c), BF16)
    return pl.pallas_call(
        functools.partial(_branch_a_kernel, chunk=_pick_tile(c, 512, LANES)),
        out_shape=jax.ShapeDtypeStruct((n, d_model), BF16),
        grid=(n // tm, d_model // tn),
        in_specs=[pl.BlockSpec((tm, c), lambda i, j: (i, 0)),
                  pl.BlockSpec((tm, c), lambda i, j: (i, 0)),
                  pl.BlockSpec((BF16_ROWS, c), lambda i, j: (jnp.maximum(i * hb - 1, 0), 0)),
                  pl.BlockSpec((BF16_ROWS, c), lambda i, j: (jnp.minimum((i + 1) * hb, n_hb - 1), 0)),
                  pl.BlockSpec((3, c), lambda i, j: (0, 0)),
                  _w_spec(c, tn, layer, 0),
                  pl.BlockSpec((tm, tn), lambda i, j: (i, j))],
        out_specs=pl.BlockSpec((tm, tn), lambda i, j: (i, j)),
        scratch_shapes=[pltpu.VMEM((tm, c), BF16)],
        compiler_params=_compiler_params(("parallel", "arbitrary"), blocks, scratch),
        name="branch_a",
    )(bg, cv, cv, cv, conv_w, w_a, gates)


def _cmul(ar, ai, br, bi):
    return ar * br - ai * bi, ar * bi + ai * br


def _s5_prep_kernel(lre_ref, lim_ref, lst_ref, bre_ref, bim_ref, cre_ref, cim_ref,
                    ere_ref, eim_ref, care_ref, caim_ref, klag_ref, apre_ref, apim_ref):
    lam_re, lam_im = lre_ref[...], lim_ref[...]
    dt = jnp.exp(lst_ref[...])
    mag = jnp.exp(lam_re * dt)
    a_re = mag * jnp.cos(lam_im * dt)
    a_im = mag * jnp.sin(lam_im * dt)
    nr, ni = a_re - 1.0, a_im
    den = lam_re * lam_re + lam_im * lam_im
    f_re = (nr * lam_re + ni * lam_im) / den
    f_im = (ni * lam_re - nr * lam_im) / den
    b_re, b_im = bre_ref[...], bim_ref[...]
    bb_re, bb_im = _cmul(f_re, f_im, b_re, b_im)
    c_re, c_im = cre_ref[...], cim_ref[...]

    pw = [(jnp.ones_like(a_re), jnp.zeros_like(a_im))]
    for _ in range(T_BLK):
        pw.append(_cmul(pw[-1][0], pw[-1][1], a_re, a_im))

    e_re, e_im, ck_re, ck_im = [], [], [], []
    for k in range(T_BLK + 1):
        cr, ci = _cmul(pw[k][0], pw[k][1], c_re, c_im)
        ck_re.append(cr)
        ck_im.append(ci)
    for k in range(T_BLK):
        er, ei = _cmul(pw[k][0], pw[k][1], bb_re, bb_im)
        e_re.append(er)
        e_im.append(ei)
    ere_ref[...] = jnp.concatenate(e_re, axis=1)
    eim_ref[...] = jnp.concatenate(e_im, axis=1)
    care_ref[...] = jnp.concatenate(ck_re[1:], axis=1)
    caim_ref[...] = jnp.concatenate([-ci for ci in ck_im[1:]], axis=1)

    dn = (((2,), (2,)), ((0,), (0,)))
    k_t = (lax.dot_general(bb_re, jnp.concatenate(ck_re[:-1], axis=1), dn,
                           precision=lax.Precision.HIGHEST, preferred_element_type=F32)
           - lax.dot_general(bb_im, jnp.concatenate(ck_im[:-1], axis=1), dn,
                             precision=lax.Precision.HIGHEST, preferred_element_type=F32))
    n_grp, h_out, th = k_t.shape
    h_in = th // T_BLK
    k_t = k_t.reshape(n_grp * h_out, th)
    src = lax.broadcasted_iota(jnp.int32, (th, th), 0)
    dst = lax.broadcasted_iota(jnp.int32, (th, th), 1)
    assert h_in & (h_in - 1) == 0, "group width must be a power of two"
    shift = h_in.bit_length() - 1
    same_h = (src & (h_in - 1)) == (dst & (h_in - 1))
    lag, tau = src >> shift, dst >> shift
    backward = pl.program_id(1) == 1
    for sigma in range(T_BLK):
        wanted = jnp.where(backward, sigma - lag, sigma + lag)
        place = jnp.where(jnp.logical_and(same_h, tau == wanted), 1.0, 0.0)
        block = jnp.dot(k_t, place, precision=lax.Precision.HIGHEST, preferred_element_type=F32)
        klag_ref[:, sigma, :, :] = block.reshape(n_grp, h_out, th)

    q_re, q_im = pw[T_BLK]
    r_re, r_im = q_re, q_im
    for m in range(8):
        apre_ref[m] = r_re
        apim_ref[m] = r_im
        r_re, r_im = _cmul(r_re, r_im, q_re, q_im)


def _s5_prep(lam_re, lam_im, log_step, b_re, b_im, c_re, c_im):
    depth, _, g, p = lam_re.shape
    h = b_re.shape[-1]
    lead = (depth, 2)
    b_re_t = jnp.swapaxes(b_re, -1, -2).astype(F32)
    b_im_t = jnp.swapaxes(b_im, -1, -2).astype(F32)

    gc = _pick_tile(g, 16, 1)

    def spec(*tail):
        zeros = (0,) * (len(tail) - 1)
        return pl.BlockSpec((None, None) + tail, lambda l, d, q: (l, d, q) + zeros)

    ap_spec = pl.BlockSpec((None, None, 8, gc, 1, p), lambda l, d, q: (l, d, 0, q, 0, 0))
    th = T_BLK * h
    outs = (jax.ShapeDtypeStruct(lead + (g, th, p), F32),) * 4 + (
        jax.ShapeDtypeStruct(lead + (g, T_BLK, h, th), F32),
        jax.ShapeDtypeStruct(lead + (8, g, 1, p), F32),
        jax.ShapeDtypeStruct(lead + (8, g, 1, p), F32))
    blocks = 4 * _nbytes((gc, th, LANES), F32) + 8 * _nbytes((gc, h, LANES), F32)
    return pl.pallas_call(
        _s5_prep_kernel,
        out_shape=outs,
        grid=lead + (g // gc,),
        in_specs=[spec(gc, 1, p), spec(gc, 1, p), spec(gc, 1, 1), spec(gc, h, p), spec(gc, h, p),
                  spec(gc, h, p), spec(gc, h, p)],
        out_specs=(spec(gc, th, p),) * 4 + (spec(gc, T_BLK, h, th), ap_spec, ap_spec),
        compiler_params=_compiler_params(("parallel", "parallel", "parallel"), blocks,
                                         16 * _nbytes((gc, th, LANES), F32)),
        name="s5_prep",
    )(lam_re.astype(F32)[:, :, :, None, :], lam_im.astype(F32)[:, :, :, None, :],
      log_step.astype(F32)[..., None, None], b_re_t, b_im_t, c_re.astype(F32), c_im.astype(F32))


def _block_diag_kernel(c_ref, e_ref, rg_ref, cg_ref, o_ref):
    spread = _dot(c_ref[...].astype(BF16), e_ref[...])
    o_ref[...] = jnp.where(rg_ref[...] == cg_ref[...], spread, 0.0).astype(o_ref.dtype)


def _block_diag(compact, gl, row_inner, col_inner, name):
    n_cb, rows, kc = compact.shape
    n_out = kc * gl
    tr = _pick_tile(rows, 512, BF16_ROWS)
    col = np.arange(n_out)
    spread = np.arange(kc)[:, None] == (col // (gl * col_inner)) * col_inner + col % col_inner
    row_group = ((np.arange(rows) // row_inner) % gl).astype(np.int32)[:, None]
    col_group = ((col // col_inner) % gl).astype(np.int32)[None, :]
    blocks = (_nbytes((tr, kc), F32) + _nbytes((kc, n_out), BF16) + _nbytes((tr, LANES), jnp.int32)
              + _nbytes((8, n_out), jnp.int32) + _nbytes((tr, n_out), BF16))
    return pl.pallas_call(
        _block_diag_kernel,
        out_shape=jax.ShapeDtypeStruct((n_cb, rows, n_out), BF16),
        grid=(n_cb, rows // tr),
        in_specs=[pl.BlockSpec((None, tr, kc), lambda c, i: (c, i, 0)),
                  pl.BlockSpec((kc, n_out), lambda c, i: (0, 0)),
                  pl.BlockSpec((tr, 1), lambda c, i: (i, 0)),
                  pl.BlockSpec((1, n_out), lambda c, i: (0, 0))],
        out_specs=pl.BlockSpec((None, tr, n_out), lambda c, i: (c, i, 0)),
        compiler_params=_compiler_params(("parallel", "parallel"), blocks),
        name=name,
    )(compact, jnp.asarray(spread, BF16), jnp.asarray(row_group), jnp.asarray(col_group))


def _s5_weights(prep, n_cb, gl, h, p):
    e_re, e_im, ca_re, ca_im, klag, ap_re, ap_im = prep
    t = T_BLK

    def split(x):
        return x.reshape(2, n_cb * gl, t, h, p)

    e = jnp.stack([split(e_re), split(e_im)], axis=1)
    e_sel = jnp.stack([e[0, :, :, ::-1], e[1]], axis=0)
    e_sel = e_sel.reshape(2, 2, n_cb, gl, t, h, p)
    e_sel = jnp.transpose(e_sel, (2, 4, 3, 5, 0, 1, 6))
    w_s = _block_diag(e_sel.reshape(n_cb, t * gl * h, 4 * p), gl, h, p, "s5_w_states")

    def carry(ca, name):
        ca = split(ca)
        sel = jnp.stack([ca[0], ca[1, :, ::-1]], axis=0)
        sel = sel.reshape(2, n_cb, gl, t, h, p)
        sel = jnp.transpose(sel, (1, 0, 2, 5, 3, 4))
        return _block_diag(sel.reshape(n_cb, 2 * gl * p, t * h), gl, p, h, name)

    w_cre, w_cim = carry(ca_re, "s5_w_carry_re"), carry(ca_im, "s5_w_carry_im")

    kst = (klag[0] + klag[1]).reshape(n_cb, gl, t, h, t * h)
    kst = jnp.transpose(kst, (0, 2, 1, 3, 4))
    w_i = _block_diag(kst.reshape(n_cb, t * gl * h, t * h), gl, h, h, "s5_w_intra")

    def lanes(ap):
        ap = jnp.stack([ap[0], ap[1, ::-1]], axis=0)
        ap = ap.reshape(2, 8, n_cb, gl * p)
        return jnp.transpose(ap, (2, 0, 1, 3))
    apw = jnp.stack([lanes(ap_re), lanes(ap_im)], axis=2)

    return w_s, w_cre, w_cim, w_i, apw


def _fold_time_blocks(u_ref, lhs_ref, r):
    pieces = u_ref.shape[0]
    for tau in range(T_BLK):
        for h in range(pieces):
            col = (tau * pieces + h) * LANES
            lhs_ref[:, col:col + LANES] = u_ref[h, pl.ds(tau, r, stride=T_BLK), :].astype(lhs_ref.dtype)


def _s5_states_kernel(u_ref, w_ref, s_ref, lhs_ref, *, r):
    @pl.when(pl.program_id(2) == 0)
    def _():
        _fold_time_blocks(u_ref, lhs_ref, r)

    s_ref[...] = _dot(lhs_ref[...], w_ref[...])


def _s5_states(u, w_s, r, tn):
    n_cb, kw, ns = w_s.shape
    n = u.shape[1]
    nb = n // T_BLK
    ppc = u.shape[0] // n_cb
    blocks = _nbytes((ppc, T_BLK * r, LANES), F32) + _nbytes((kw, tn), BF16) + _nbytes((r, tn), F32)
    scratch = _nbytes((r, kw), BF16)
    return pl.pallas_call(
        functools.partial(_s5_states_kernel, r=r),
        out_shape=jax.ShapeDtypeStruct((n_cb, nb, ns), F32),
        grid=(n_cb, nb // r, ns // tn),
        in_specs=[pl.BlockSpec((ppc, T_BLK * r, LANES), lambda c, i, j: (c, i, 0)),
                  pl.BlockSpec((None, kw, tn), lambda c, i, j: (c, 0, j))],
        out_specs=pl.BlockSpec((None, r, tn), lambda c, i, j: (c, i, j)),
        scratch_shapes=[pltpu.VMEM((r, kw), BF16)],
        compiler_params=_compiler_params(("parallel", "parallel", "arbitrary"), blocks, scratch),
        name="s5_states",
    )(u, w_s)


def _s5_scan_kernel(sre_ref, sim_ref, apre_ref, apim_ref, xre_ref, xim_ref, *, n_pairs):
    d = pl.program_id(1)
    lw = sre_ref.shape[-1]
    ap_re, ap_im = apre_ref[...], apim_ref[...]
    row = lax.broadcasted_iota(jnp.int32, (8, lw), 0)

    def run(reverse):
        steps = []
        for shift in (1, 2, 4):
            src = (8 - shift) if reverse else (shift - 1)
            keep = (row < 8 - shift) if reverse else (row >= shift)
            steps.append((8 - shift if reverse else shift,
                          jnp.where(keep, ap_re[src:src + 1, :], 0.0),
                          jnp.where(keep, ap_im[src:src + 1, :], 0.0)))
        edge = 7 if reverse else 0
        out_row = 0 if reverse else 7
        nb_shift = 7 if reverse else 1

        def tile(r0, c_re, c_im):
            x_re = sre_ref[pl.ds(r0, 8), :]
            x_im = sim_ref[pl.ds(r0, 8), :]
            for amount, m_re, m_im in steps:
                s_re = pltpu.roll(x_re, amount, 0)
                s_im = pltpu.roll(x_im, amount, 0)
                x_re, x_im = x_re + m_re * s_re - m_im * s_im, x_im + m_re * s_im + m_im * s_re
            inc_re = x_re + ap_re * c_re - ap_im * c_im
            inc_im = x_im + ap_re * c_im + ap_im * c_re
            e_re = jnp.where(row == edge, c_re, pltpu.roll(inc_re, nb_shift, 0))
            e_im = jnp.where(row == edge, c_im, pltpu.roll(inc_im, nb_shift, 0))
            return e_re, e_im, inc_re[out_row:out_row + 1, :], inc_im[out_row:out_row + 1, :]

        def body(k, carry):
            c_re, c_im = carry
            pair = (n_pairs - 1 - k) if reverse else k
            r0 = pl.multiple_of(pair * BF16_ROWS, BF16_ROWS)
            e_re, e_im = [None, None], [None, None]
            for half in ((1, 0) if reverse else (0, 1)):
                e_re[half], e_im[half], c_re, c_im = tile(pl.multiple_of(r0 + 8 * half, 8), c_re, c_im)
            xre_ref[pl.ds(r0, BF16_ROWS), :] = jnp.concatenate(e_re, axis=0).astype(xre_ref.dtype)
            xim_ref[pl.ds(r0, BF16_ROWS), :] = jnp.concatenate(e_im, axis=0).astype(xim_ref.dtype)
            return c_re, c_im

        zero = jnp.zeros((1, lw), F32)
        lax.fori_loop(0, n_pairs, body, (zero, zero))

    @pl.when(d == 0)
    def _():
        run(False)

    @pl.when(d == 1)
    def _():
        run(True)


def _s5_scan(s, apw, lw):
    n_cb, nb, ns = s.shape
    glp = ns // 4
    nq = glp // lw
    blocks = 2 * _nbytes((nb, lw), F32) + 2 * _nbytes((nb, lw), BF16) + 2 * _nbytes((8, lw), F32)
    out = jax.ShapeDtypeStruct((n_cb, nb, 2 * glp), BF16)

    def s_spec(ri):
        return pl.BlockSpec((None, nb, lw), lambda c, d, q: (c, 0, (2 * d + ri) * nq + q))

    def ap_spec(ri):
        return pl.BlockSpec((None, None, None, 8, lw), lambda c, d, q: (c, d, ri, 0, q))

    x_spec = pl.BlockSpec((None, nb, lw), lambda c, d, q: (c, 0, d * nq + q))
    return pl.pallas_call(
        functools.partial(_s5_scan_kernel, n_pairs=nb // BF16_ROWS),
        out_shape=(out, out),
        grid=(n_cb, 2, nq),
        in_specs=[s_spec(0), s_spec(1), ap_spec(0), ap_spec(1)],
        out_specs=(x_spec, x_spec),
        compiler_params=_compiler_params(("parallel", "parallel", "parallel"), blocks),
        name="s5_scan",
    )(s, s, apw, apw)


def _s5_out_kernel(u_ref, xre_ref, xim_ref, wi_ref, wcre_ref, wcim_ref, y_ref, lhs_ref, *, r, tn):
    j = pl.program_id(2)
    pieces = u_ref.shape[0]

    @pl.when(j == 0)
    def _():
        _fold_time_blocks(u_ref, lhs_ref, r)

    y = _dot(lhs_ref[...], wi_ref[...])
    y = y + _dot(xre_ref[...], wcre_ref[...])
    y = y + _dot(xim_ref[...], wcim_ref[...])

    per = tn // (pieces * LANES)
    for jj in range(T_BLK // per):
        @pl.when(j == jj)
        def _():
            for k in range(per):
                for h in range(pieces):
                    col = (k * pieces + h) * LANES
                    y_ref[h, pl.ds(jj * per + k, r, stride=T_BLK), :] = y[:, col:col + LANES]


def _s5_out(u, x_re, x_im, w_i, w_cre, w_cim, r, tn):
    n_cb, kw, _ = w_i.shape
    n = u.shape[1]
    nb = n // T_BLK
    ppc = u.shape[0] // n_cb
    kx = x_re.shape[-1]
    assert tn % (ppc * LANES) == 0 and kw % tn == 0
    blocks = (2 * _nbytes((ppc, T_BLK * r, LANES), F32) + 2 * _nbytes((r, kx), BF16) + _nbytes((kw, tn), BF16)
              + 2 * _nbytes((kx, tn), BF16) + _nbytes((r, tn), F32))
    scratch = _nbytes((r, kw), BF16)
    return pl.pallas_call(
        functools.partial(_s5_out_kernel, r=r, tn=tn),
        out_shape=jax.ShapeDtypeStruct(u.shape, F32),
        grid=(n_cb, nb // r, kw // tn),
        in_specs=[pl.BlockSpec((ppc, T_BLK * r, LANES), lambda c, i, j: (c, i, 0)),
                  pl.BlockSpec((None, r, kx), lambda c, i, j: (c, i, 0)),
                  pl.BlockSpec((None, r, kx), lambda c, i, j: (c, i, 0)),
                  pl.BlockSpec((None, kw, tn), lambda c, i, j: (c, 0, j)),
                  pl.BlockSpec((None, kx, tn), lambda c, i, j: (c, 0, j)),
                  pl.BlockSpec((None, kx, tn), lambda c, i, j: (c, 0, j))],
        out_specs=pl.BlockSpec((ppc, T_BLK * r, LANES), lambda c, i, j: (c, i, 0)),
        scratch_shapes=[pltpu.VMEM((r, kw), BF16)],
        compiler_params=_compiler_params(("parallel", "parallel", "arbitrary"), blocks, scratch),
        name="s5_out",
    )(u, x_re, x_im, w_i, w_cre, w_cim)


def _glu_merge_kernel(y_ref, u_ref, d_ref, w1_ref, w2_ref, pa_ref, gb_ref, m_ref, s_ref):
    @pl.when(pl.program_id(1) == 0)
    def _():
        for c in range(y_ref.shape[0]):
            s = jax.nn.gelu(y_ref[c] + d_ref[c] * u_ref[c])
            s_ref[:, c * LANES:(c + 1) * LANES] = s.astype(s_ref.dtype)

    a = s_ref[...]
    y_b = _dot(a, w1_ref[...]) * _sigmoid(_dot(a, w2_ref[...]))
    m_ref[...] = (pa_ref[...].astype(F32) + gb_ref[...].astype(F32) * y_b).astype(m_ref.dtype)


def _glu_merge(y, u, d_skip, w_glu, layer, pa, gates, d_model, tm, tn):
    n_pc, n, _ = y.shape
    k = n_pc * LANES
    nb = d_model // tn
    blocks = (2 * _nbytes((n_pc, tm, LANES), F32) + 2 * _nbytes((k, tn), BF16) + 3 * _nbytes((tm, tn), BF16))
    scratch = _nbytes((tm, k), BF16)
    return pl.pallas_call(
        _glu_merge_kernel,
        out_shape=jax.ShapeDtypeStruct((n, d_model), BF16),
        grid=(n // tm, nb),
        in_specs=[pl.BlockSpec((n_pc, tm, LANES), lambda i, j: (0, i, 0)),
                  pl.BlockSpec((n_pc, tm, LANES), lambda i, j: (0, i, 0)),
                  pl.BlockSpec((n_pc, 1, LANES), lambda i, j: (0, 0, 0)),
                  _w_spec(k, tn, layer, 0),
                  _w_spec(k, tn, layer, nb),
                  pl.BlockSpec((tm, tn), lambda i, j: (i, j)),
                  pl.BlockSpec((tm, tn), lambda i, j: (i, nb + j))],
        out_specs=pl.BlockSpec((tm, tn), lambda i, j: (i, j)),
        scratch_shapes=[pltpu.VMEM((tm, k), BF16)],
        compiler_params=_compiler_params(("parallel", "arbitrary"), blocks, scratch),
        name="glu_merge",
    )(y, u, d_skip, w_glu, w_glu, pa, gates)


def _ffn_up_kernel(h_ref, hp_ref, hn_ref, wg_ref, wv_ref, cg_ref, cv_ref, act_ref, lhs_ref, *, tm):
    i = pl.program_id(0)
    j = pl.program_id(1)
    last = pl.num_programs(0) - 1
    halo = BF16_ROWS

    @pl.when(j == 0)
    def _():
        row = lax.broadcasted_iota(jnp.int32, hp_ref.shape, 0)
        after = jnp.where(jnp.logical_and(row == 0, i < last), hn_ref[...].astype(F32), 0.0)
        edge = jnp.where(jnp.logical_and(row == halo - 1, i > 0), hp_ref[...].astype(F32), after)
        lhs_ref[0:tm, :] = h_ref[...]
        lhs_ref[tm:, :] = edge.astype(lhs_ref.dtype)

    a = lhs_ref[...]
    rows = tm + halo

    def conv(w_ref, cw_ref):
        u = _dot(a, w_ref[...])
        cw = cw_ref[...]
        up = pltpu.roll(u, 1, 0)[0:tm]
        dn = pltpu.roll(u, rows - 1, 0)[0:tm]
        return up * cw[0:1, :] + u[0:tm] * cw[1:2, :] + dn * cw[2:3, :]

    act_ref[...] = (jax.nn.gelu(conv(wg_ref, cg_ref)) * conv(wv_ref, cv_ref)).astype(act_ref.dtype)


def _ffn_up(hn, w_up, layer, conv_w, d_ff, tm, tn, cast=None):
    n, k = hn.shape
    nb = d_ff // tn
    hb = tm // BF16_ROWS
    n_hb = n // BF16_ROWS
    blocks = (_nbytes((tm + 2 * BF16_ROWS, k), BF16) + 2 * _nbytes((k, tn), BF16) + 2 * _nbytes((3, tn), F32)
              + _nbytes((tm, tn), BF16))
    scratch = _nbytes((tm + BF16_ROWS, k), BF16)
    outs = _call_with_cast(
        functools.partial(_ffn_up_kernel, tm=tm), cast,
        grid=(n // tm, nb),
        in_specs=[pl.BlockSpec((tm, k), lambda i, j: (i, 0)),
                  pl.BlockSpec((BF16_ROWS, k), lambda i, j: (jnp.maximum(i * hb - 1, 0), 0)),
                  pl.BlockSpec((BF16_ROWS, k), lambda i, j: (jnp.minimum((i + 1) * hb, n_hb - 1), 0)),
                  _w_spec(k, tn, layer, 0),
                  _w_spec(k, tn, layer, nb),
                  pl.BlockSpec((3, tn), lambda i, j: (0, j)),
                  pl.BlockSpec((3, tn), lambda i, j: (0, nb + j))],
        out_specs=[pl.BlockSpec((tm, tn), lambda i, j: (i, j))],
        out_shape=[jax.ShapeDtypeStruct((n, d_ff), BF16)],
        operands=(hn, hn, hn, w_up, w_up, conv_w, conv_w),
        scratch_shapes=[pltpu.VMEM((tm + BF16_ROWS, k), BF16)],
        block_bytes=blocks, scratch_bytes=scratch, name="ffn_up")
    return outs[0] if cast is None else outs


def kernel(x, meta_tokens, norm_mix, w_in, conv_a_w, w_a, ssm_lambda_re, ssm_lambda_im, ssm_log_step,
           ssm_b_re, ssm_b_im, ssm_c_re, ssm_c_im, ssm_d, w_glu, w_out, norm_ffn, w_up, conv_ffn_w,
           w_down, norm_final):
    bsz, seq, d_model = x.shape
    n_meta = meta_tokens.shape[0]
    depth = w_in.shape[0]
    d_conv = w_a.shape[1]
    d_ssm = ssm_d.shape[1]
    d_ff = w_down.shape[1]
    n_groups, p_state, h_grp = ssm_b_re.shape[2:]
    assert w_in.shape[2] == 3 * d_conv + d_ssm + 2 * d_model and n_groups * h_grp == d_ssm

    cbw = min(S5_CB_WIDTH, d_ssm)
    n_cb = d_ssm // cbw
    gl = cbw // h_grp
    assert n_cb * cbw == d_ssm and gl * h_grp == cbw

    n_true = n_meta + seq
    row_align = T_BLK * BF16_ROWS
    n_pad = _round_up(n_true, PAD_ROWS if n_true >= 4 * PAD_ROWS else row_align)
    assert n_pad % row_align == 0
    nb = n_pad // T_BLK

    tm = _pick_tile(n_pad, TM_TARGET, BF16_ROWS)
    tm_up = _pick_tile(n_pad, TM_FFN_UP_TARGET, BF16_ROWS)
    tm_half = _pick_tile(n_pad, TM_HALF_TARGET, BF16_ROWS)
    tm_ew = _pick_tile(n_pad, TM_EW_TARGET, BF16_ROWS)
    r_s5 = _pick_tile(nb, R_S5_TARGET, BF16_ROWS)
    tn = lambda n, target=512: _pick_tile(n, target, 128)

    prep = _s5_prep(ssm_lambda_re, ssm_lambda_im, ssm_log_step, ssm_b_re, ssm_b_im, ssm_c_re, ssm_c_im)
    d_skip = ssm_d.astype(F32).reshape(depth, d_ssm // LANES, 1, LANES)

    residual = functools.partial(_epi_residual, n_valid=n_true)
    tile_spec = lambda t, rows=tm: pl.BlockSpec((rows, t), lambda i, j: (i, j))
    u_off = 3 * d_conv
    gate_off = u_off + d_ssm

    w_a_b, w_glu_b, w_out_b = (w.astype(BF16) for w in (w_a, w_glu, w_out))
    w_in_first = w_in[0].astype(BF16)

    outs = []
    for b in range(bsz):
        w_in_l = w_in_first
        for l in range(depth):
            w_s, w_cre, w_cim, w_i, apw = _s5_weights([t[l] for t in prep], n_cb, gl, h_grp, p_state)

            if l == 0:
                h_res, hn = _assemble_norm(meta_tokens, x[b], norm_mix[0], n_pad, tm_ew)
            else:
                hn = _rmsnorm(h_res, norm_mix[l], BF16, tm_ew)
            bg, cv, w_down_l = _inproj_conv(hn, w_in_l, None, d_conv, tm_half, tn(d_conv), cast=(w_down, l))
            t_u = tn(d_ssm, 256)
            u = _mm(hn, w_in_l, None, col0=u_off, n_cols=d_ssm, tm=tm, tn=t_u, epilogue=_epi_pieces,
                    out_shape=jax.ShapeDtypeStruct((d_ssm // LANES, n_pad, LANES), F32),
                    out_spec=pl.BlockSpec((t_u // LANES, tm, LANES), lambda i, j: (j, i, 0)), name="inproj_u")
            t_g = tn(2 * d_model)
            gates, w_up_l = _mm(hn, w_in_l, None, col0=gate_off, n_cols=2 * d_model, tm=tm, tn=t_g,
                                epilogue=_epi_sigmoid,
                                out_shape=jax.ShapeDtypeStruct((n_pad, 2 * d_model), BF16),
                                out_spec=tile_spec(t_g), name="inproj_gates", cast=(w_up, l))

            pa = _branch_a(bg, cv, conv_a_w[l].astype(F32), w_a_b, l, gates, d_model, tm, tn(d_model, 1024))
            t_d = tn(d_model)

            states = _s5_states(u, w_s, _pick_tile(nb, R_S5_STATES_TARGET, BF16_ROWS), tn(w_s.shape[-1]))
            x_re, x_im = _s5_scan(states, apw, tn(gl * p_state, SCAN_LANES_TARGET))
            y = _s5_out(u, x_re, x_im, w_i, w_cre, w_cim, r_s5, tn(T_BLK * cbw))

            merged = _glu_merge(y, u, d_skip[l], w_glu_b, l, pa, gates, d_model, tm_half, tn(d_model, 1024))
            h_res = _mm(merged, w_out_b, l, col0=0, n_cols=d_model, tm=tm, tn=t_d, epilogue=residual,
                        extras=(h_res,), extra_specs=(tile_spec(t_d),),
                        out_shape=jax.ShapeDtypeStruct((n_pad, d_model), F32), out_spec=tile_spec(t_d),
                        name="out_proj")

            hn = _rmsnorm(h_res, norm_ffn[l], BF16, tm_ew)
            if l + 1 < depth:
                act, w_in_l = _ffn_up(hn, w_up_l, None, conv_ffn_w[l].astype(F32), d_ff, tm_up, tn(d_ff),
                                      cast=(w_in, l + 1))
            else:
                act = _ffn_up(hn, w_up_l, None, conv_ffn_w[l].astype(F32), d_ff, tm_up, tn(d_ff))
            t_o = tn(d_model, 256)
            h_res = _mm(act, w_down_l, None, col0=0, n_cols=d_model, tm=tm_half, tn=t_o, epilogue=residual,
                        extras=(h_res,), extra_specs=(tile_spec(t_o, tm_half),),
                        out_shape=jax.ShapeDtypeStruct((n_pad, d_model), F32), out_spec=tile_spec(t_o, tm_half),
                        name="ffn_down")

        outs.append(_final_norm(h_res, norm_final, n_meta, seq, x.dtype, _pick_tile(seq, TM_EW_TARGET, n_meta)))
    return jnp.stack(outs, axis=0)
```

```python
import functools

import numpy as np
import jax
import jax.numpy as jnp
from jax import lax
from jax.experimental import pallas as pl
from jax.experimental.pallas import tpu as pltpu

F32 = jnp.float32
BF16 = jnp.bfloat16
EPS = 1e-6

T_BLK = 16
S5_CB_WIDTH = 128
PAD_ROWS = 1280
BF16_ROWS = 16
LANES = 128
VMEM_CAP_BYTES = 60000 * 1024
VMEM_TEMP_BYTES = 20 * 1024 * 1024

TM_TARGET = 1280
TM_FFN_UP_TARGET = 832
TM_HALF_TARGET = 640
R_S5_STATES_TARGET = 1040
TM_EW_TARGET = 320
R_S5_TARGET = 1040
SCAN_LANES_TARGET = 256


def _round_up(n, m):
    return (n + m - 1) // m * m


def _pick_tile(n, target, align):
    best = 0
    for t in range(align, min(n, target) + 1, align):
        if n % t == 0:
            best = t
    assert best > 0, (n, target, align)
    return best


def _nbytes(shape, dtype):
    return int(np.prod(shape)) * jnp.dtype(dtype).itemsize


def _compiler_params(semantics, block_bytes, scratch_bytes=0):
    est = 2 * block_bytes + scratch_bytes + VMEM_TEMP_BYTES
    return pltpu.CompilerParams(dimension_semantics=semantics,
                                vmem_limit_bytes=int(min(VMEM_CAP_BYTES, est)))


def _dot(a, b):
    return jnp.dot(a, b, preferred_element_type=F32)


def _sigmoid(x):
    return 0.5 * (1.0 + jnp.tanh(0.5 * x))


def _rmsnorm_kernel(x_ref, g_ref, o_ref):
    x = x_ref[...]
    ms = jnp.mean(x * x, axis=-1, keepdims=True)
    o_ref[...] = (x * lax.rsqrt(ms + EPS) * g_ref[...]).astype(o_ref.dtype)


def _rmsnorm(x, g, out_dtype, tm):
    n, d = x.shape
    blocks = _nbytes((tm, d), F32) + _nbytes((tm, d), out_dtype)
    return pl.pallas_call(
        _rmsnorm_kernel,
        out_shape=jax.ShapeDtypeStruct((n, d), out_dtype),
        grid=(n // tm,),
        in_specs=[pl.BlockSpec((tm, d), lambda i: (i, 0)),
                  pl.BlockSpec((1, d), lambda i: (0, 0))],
        out_specs=pl.BlockSpec((tm, d), lambda i: (i, 0)),
        compiler_params=_compiler_params(("parallel",), blocks),
        name="rmsnorm",
    )(x, g.reshape(1, d).astype(F32))


def _assemble_norm_kernel(meta_ref, prev_ref, x_ref, g_ref, h_ref, hn_ref, *, n_valid):
    i = pl.program_id(0)
    tm = h_ref.shape[0]
    n_meta = meta_ref.shape[0]
    head = jnp.where(i == 0, meta_ref[...], prev_ref[...])
    tile = jnp.concatenate([head, x_ref[0:tm - n_meta, :]], axis=0)
    rows = i * tm + lax.broadcasted_iota(jnp.int32, tile.shape, 0)
    tile = jnp.where(rows < n_valid, tile, 0.0)
    h_ref[...] = tile
    ms = jnp.mean(tile * tile, axis=-1, keepdims=True)
    hn_ref[...] = (tile * lax.rsqrt(ms + EPS) * g_ref[...]).astype(hn_ref.dtype)


def _assemble_norm(meta, x, g, n_pad, tm):
    n_meta, d = meta.shape
    seq = x.shape[0]
    assert n_meta % 8 == 0 and tm % n_meta == 0 and seq % n_meta == 0 and n_pad % tm == 0
    per = tm // n_meta
    last_x = pl.cdiv(seq, tm) - 1
    last_prev = seq // n_meta - 1
    blocks = _nbytes((tm + 2 * n_meta, d), F32) + _nbytes((tm, d), F32) + _nbytes((tm, d), BF16)
    return pl.pallas_call(
        functools.partial(_assemble_norm_kernel, n_valid=n_meta + seq),
        out_shape=(jax.ShapeDtypeStruct((n_pad, d), F32), jax.ShapeDtypeStruct((n_pad, d), BF16)),
        grid=(n_pad // tm,),
        in_specs=[pl.BlockSpec((n_meta, d), lambda i: (0, 0)),
                  pl.BlockSpec((n_meta, d), lambda i: (jnp.clip(i * per - 1, 0, last_prev), 0)),
                  pl.BlockSpec((tm, d), lambda i: (jnp.minimum(i, last_x), 0)),
                  pl.BlockSpec((1, d), lambda i: (0, 0))],
        out_specs=(pl.BlockSpec((tm, d), lambda i: (i, 0)), pl.BlockSpec((tm, d), lambda i: (i, 0))),
        compiler_params=_compiler_params(("parallel",), blocks),
        name="assemble_norm",
    )(meta.astype(F32), x.astype(F32), x.astype(F32), g.reshape(1, d).astype(F32))


def _final_norm_kernel(a_ref, b_ref, g_ref, o_ref, *, skip):
    x = jnp.concatenate([a_ref[skip:, :], b_ref[...]], axis=0)
    ms = jnp.mean(x * x, axis=-1, keepdims=True)
    o_ref[...] = (x * lax.rsqrt(ms + EPS) * g_ref[...]).astype(o_ref.dtype)


def _final_norm(x, g, skip, n_out, out_dtype, tm):
    n, d = x.shape
    assert skip % 8 == 0 and tm % skip == 0 and n_out % tm == 0 and n >= n_out + skip
    per = tm // skip
    blocks = _nbytes((tm + skip, d), F32) + _nbytes((tm, d), out_dtype)
    return pl.pallas_call(
        functools.partial(_final_norm_kernel, skip=skip),
        out_shape=jax.ShapeDtypeStruct((n_out, d), out_dtype),
        grid=(n_out // tm,),
        in_specs=[pl.BlockSpec((tm, d), lambda i: (i, 0)),
                  pl.BlockSpec((skip, d), lambda i: ((i + 1) * per, 0)),
                  pl.BlockSpec((1, d), lambda i: (0, 0))],
        out_specs=pl.BlockSpec((tm, d), lambda i: (i, 0)),
        compiler_params=_compiler_params(("parallel",), blocks),
        name="final_norm",
    )(x, x, g.reshape(1, d).astype(F32))


def _mm_kernel(*refs, n_extra, epilogue, tm):
    a_ref, w_ref = refs[0], refs[1]
    extras = refs[2:2 + n_extra]
    o_ref = refs[2 + n_extra]
    acc = _dot(a_ref[...], w_ref[...])
    row0 = pl.program_id(0) * tm
    o_ref[...] = epilogue(acc, row0, *[e[...] for e in extras]).astype(o_ref.dtype)


def _w_spec(k, tn, layer, jb0):
    if layer is None:
        return pl.BlockSpec((k, tn), lambda i, j: (0, jb0 + j))
    return pl.BlockSpec((None, k, tn), lambda i, j: (layer, 0, jb0 + j))


def _call_with_cast(kernel_fn, cast, *, grid, in_specs, out_specs, out_shape, operands, scratch_shapes=(),
                    block_bytes, scratch_bytes=0, name):
    if cast is None:
        return pl.pallas_call(
            kernel_fn, out_shape=tuple(out_shape), grid=grid, in_specs=list(in_specs), out_specs=tuple(out_specs),
            scratch_shapes=list(scratch_shapes),
            compiler_params=_compiler_params(("parallel", "arbitrary"), block_bytes, scratch_bytes), name=name,
        )(*operands)

    stack, layer = cast
    _, k, n = stack.shape
    n_steps = grid[0] * grid[1]
    options = []
    for rows in range(BF16_ROWS, k + 1, BF16_ROWS):
        if k % rows == 0 and k // rows <= n_steps:
            splits = max(c for c in range(1, n_steps // (k // rows) + 1) if n % (c * LANES) == 0)
            options.append((rows * (n // splits), rows, splits))
    _, rows, splits = min(options)
    cols = n // splits
    last_chunk = (k // rows) * splits - 1
    n_in, n_out = len(in_specs), len(out_specs)

    def chunk(i, j):
        q = jnp.minimum(i * grid[1] + j, last_chunk)
        return q // splits, q % splits

    def body(*refs):
        src_ref, dst_ref = refs[n_in], refs[n_in + 1 + n_out]
        dst_ref[...] = src_ref[...].astype(dst_ref.dtype)
        kernel_fn(*refs[:n_in], *refs[n_in + 1:n_in + 1 + n_out], *refs[n_in + 2 + n_out:])

    cast_bytes = _nbytes((rows, cols), F32) + _nbytes((rows, cols), BF16)
    return pl.pallas_call(
        body,
        out_shape=(*out_shape, jax.ShapeDtypeStruct((k, n), BF16)),
        grid=grid,
        in_specs=[*in_specs, pl.BlockSpec((None, rows, cols), lambda i, j: (layer, *chunk(i, j)))],
        out_specs=(*out_specs, pl.BlockSpec((rows, cols), lambda i, j: chunk(i, j))),
        scratch_shapes=list(scratch_shapes),
        compiler_params=_compiler_params(("arbitrary", "arbitrary"), block_bytes + cast_bytes, scratch_bytes),
        name=name,
    )(*operands, stack)


def _mm(a, w, layer, *, col0, n_cols, tm, tn, epilogue, extras=(), extra_specs=(), out_shape, out_spec, name,
        cast=None):
    n, k = a.shape
    assert n % tm == 0 and n_cols % tn == 0 and col0 % tn == 0
    blocks = (_nbytes((tm, k), a.dtype) + _nbytes((k, tn), w.dtype) + _nbytes((tm, tn), F32)
              + sum(_nbytes((tm, tn), e.dtype) for e in extras))
    outs = _call_with_cast(
        functools.partial(_mm_kernel, n_extra=len(extras), epilogue=epilogue, tm=tm), cast,
        grid=(n // tm, n_cols // tn),
        in_specs=[pl.BlockSpec((tm, k), lambda i, j: (i, 0)),
                  _w_spec(k, tn, layer, col0 // tn),
                  *extra_specs],
        out_specs=[out_spec], out_shape=[out_shape], operands=(a, w, *extras), block_bytes=blocks, name=name)
    return outs[0] if cast is None else outs


def _epi_pieces(acc, row0):
    return jnp.stack([acc[:, h * LANES:(h + 1) * LANES] for h in range(acc.shape[1] // LANES)], axis=0)


def _epi_sigmoid(acc, row0):
    return _sigmoid(acc)


def _epi_residual(acc, row0, res, *, n_valid):
    rows = row0 + lax.broadcasted_iota(jnp.int32, acc.shape, 0)
    return jnp.where(rows < n_valid, res + acc, 0.0)


def _inproj_conv_kernel(a_ref, wb_ref, wc_ref, wv_ref, bg_ref, cv_ref):
    a = a_ref[...]
    bg_ref[...] = _dot(a, wb_ref[...]).astype(bg_ref.dtype)
    cv_ref[...] = (_dot(a, wc_ref[...]) * _dot(a, wv_ref[...])).astype(cv_ref.dtype)


def _inproj_conv(hn, w_in, layer, d_conv, tm, tn, cast=None):
    n, k = hn.shape
    nb = d_conv // tn
    blocks = _nbytes((tm, k), BF16) + 3 * _nbytes((k, tn), BF16) + 2 * _nbytes((tm, tn), BF16)
    out = jax.ShapeDtypeStruct((n, d_conv), BF16)
    return _call_with_cast(
        _inproj_conv_kernel, cast,
        grid=(n // tm, nb),
        in_specs=[pl.BlockSpec((tm, k), lambda i, j: (i, 0)),
                  _w_spec(k, tn, layer, 0),
                  _w_spec(k, tn, layer, nb),
                  _w_spec(k, tn, layer, 2 * nb)],
        out_specs=[pl.BlockSpec((tm, tn), lambda i, j: (i, j)),
                   pl.BlockSpec((tm, tn), lambda i, j: (i, j))],
        out_shape=[out, out], operands=(hn, w_in, w_in, w_in), block_bytes=blocks, name="inproj_conv")


def _shifted_rows(x, prev_row, next_row):
    t = x.shape[0]
    row = lax.broadcasted_iota(jnp.int32, x.shape, 0)
    up = jnp.where(row == 0, prev_row, pltpu.roll(x, 1, 0))
    dn = jnp.where(row == t - 1, next_row, pltpu.roll(x, t - 1, 0))
    return up, dn


def _branch_a_kernel(bg_ref, cv_ref, cvp_ref, cvn_ref, cw_ref, w_ref, gate_ref, o_ref, z_ref, *, chunk):
    i = pl.program_id(0)
    last = pl.num_programs(0) - 1

    @pl.when(pl.program_id(1) == 0)
    def _():
        for c0 in range(0, z_ref.shape[1], chunk):
            cols = slice(c0, c0 + chunk)
            x = cv_ref[:, cols].astype(F32)
            prev = cvp_ref[:, cols].astype(F32)[BF16_ROWS - 1:BF16_ROWS, :]
            nxt = cvn_ref[:, cols].astype(F32)[0:1, :]
            prev = jnp.where(i > 0, prev, 0.0)
            nxt = jnp.where(i < last, nxt, 0.0)
            up, dn = _shifted_rows(x, prev, nxt)
            w = cw_ref[:, cols]
            conv = up * w[0:1, :] + x * w[1:2, :] + dn * w[2:3, :]
            z_ref[:, cols] = (bg_ref[:, cols].astype(F32) * conv).astype(z_ref.dtype)

    o_ref[...] = (_dot(z_ref[...], w_ref[...]) * gate_ref[...].astype(F32)).astype(o_ref.dtype)


def _branch_a(bg, cv, conv_w, w_a, layer, gates, d_model, tm, tn):
    n, c = bg.shape
    hb = tm // BF16_ROWS
    n_hb = n // BF16_ROWS
    blocks = (2 * _nbytes((tm, c), BF16) + 2 * _nbytes((BF16_ROWS, c), BF16) + _nbytes((8, c), F32)
              + _nbytes((c, tn), BF16) + 2 * _nbytes((tm, tn), BF16))
    scratch = _nbytes((tm, c), BF16)
    return pl.pallas_call(
        functools.partial(_branch_a_kernel, chunk=_pick_tile(c, 512, LANES)),
        out_shape=jax.ShapeDtypeStruct((n, d_model), BF16),
        grid=(n // tm, d_model // tn),
        in_specs=[pl.BlockSpec((tm, c), lambda i, j: (i, 0)),
                  pl.BlockSpec((tm, c), lambda i, j: (i, 0)),
                  pl.BlockSpec((BF16_ROWS, c), lambda i, j: (jnp.maximum(i * hb - 1, 0), 0)),
                  pl.BlockSpec((BF16_ROWS, c), lambda i, j: (jnp.minimum((i + 1) * hb, n_hb - 1), 0)),
                  pl.BlockSpec((3, c), lambda i, j: (0, 0)),
                  _w_spec(c, tn, layer, 0),
                  pl.BlockSpec((tm, tn), lambda i, j: (i, j))],
        out_specs=pl.BlockSpec((tm, tn), lambda i, j: (i, j)),
        scratch_shapes=[pltpu.VMEM((tm, c), BF16)],
        compiler_params=_compiler_params(("parallel", "arbitrary"), blocks, scratch),
        name="branch_a",
    )(bg, cv, cv, cv, conv_w, w_a, gates)


def _cmul(ar, ai, br, bi):
    return ar * br - ai * bi, ar * bi + ai * br


def _s5_prep_kernel(lre_ref, lim_ref, lst_ref, bre_ref, bim_ref, cre_ref, cim_ref,
                    ere_ref, eim_ref, care_ref, caim_ref, klag_ref, apre_ref, apim_ref):
    lam_re, lam_im = lre_ref[...], lim_ref[...]
    dt = jnp.exp(lst_ref[...])
    mag = jnp.exp(lam_re * dt)
    a_re = mag * jnp.cos(lam_im * dt)
    a_im = mag * jnp.sin(lam_im * dt)
    nr, ni = a_re - 1.0, a_im
    den = lam_re * lam_re + lam_im * lam_im
    f_re = (nr * lam_re + ni * lam_im) / den
    f_im = (ni * lam_re - nr * lam_im) / den
    b_re, b_im = bre_ref[...], bim_ref[...]
    bb_re, bb_im = _cmul(f_re, f_im, b_re, b_im)
    c_re, c_im = cre_ref[...], cim_ref[...]

    pw = [(jnp.ones_like(a_re), jnp.zeros_like(a_im))]
    for _ in range(T_BLK):
        pw.append(_cmul(pw[-1][0], pw[-1][1], a_re, a_im))

    e_re, e_im, ck_re, ck_im = [], [], [], []
    for k in range(T_BLK + 1):
        cr, ci = _cmul(pw[k][0], pw[k][1], c_re, c_im)
        ck_re.append(cr)
        ck_im.append(ci)
    for k in range(T_BLK):
        er, ei = _cmul(pw[k][0], pw[k][1], bb_re, bb_im)
        e_re.append(er)
        e_im.append(ei)
    for k in range(T_BLK):
        ere_ref[k] = e_re[k]
        eim_ref[k] = e_im[k]
        care_ref[k] = ck_re[k + 1]
        caim_ref[k] = -ck_im[k + 1]

    dn = (((2,), (2,)), ((0,), (0,)))
    k_t = (lax.dot_general(bb_re, jnp.concatenate(ck_re[:-1], axis=1), dn,
                           precision=lax.Precision.HIGHEST, preferred_element_type=F32)
           - lax.dot_general(bb_im, jnp.concatenate(ck_im[:-1], axis=1), dn,
                             precision=lax.Precision.HIGHEST, preferred_element_type=F32))
    n_grp, h_out, th = k_t.shape
    h_in = th // T_BLK
    k_t = k_t.reshape(n_grp * h_out, th)
    src = lax.broadcasted_iota(jnp.int32, (th, th), 0)
    dst = lax.broadcasted_iota(jnp.int32, (th, th), 1)
    assert h_in & (h_in - 1) == 0, "group width must be a power of two"
    shift = h_in.bit_length() - 1
    same_h = (src & (h_in - 1)) == (dst & (h_in - 1))
    lag, tau = src >> shift, dst >> shift
    backward = pl.program_id(1) == 1
    for sigma in range(T_BLK):
        wanted = jnp.where(backward, sigma - lag, sigma + lag)
        place = jnp.where(jnp.logical_and(same_h, tau == wanted), 1.0, 0.0)
        block = jnp.dot(k_t, place, precision=lax.Precision.HIGHEST, preferred_element_type=F32)
        klag_ref[:, sigma, :, :] = block.reshape(n_grp, h_out, th)

    q_re, q_im = pw[T_BLK]
    r_re, r_im = q_re, q_im
    for m in range(8):
        apre_ref[m] = r_re
        apim_ref[m] = r_im
        r_re, r_im = _cmul(r_re, r_im, q_re, q_im)


def _s5_prep(lam_re, lam_im, log_step, b_re, b_im, c_re, c_im):
    depth, _, g, p = lam_re.shape
    h = b_re.shape[-1]
    lead = (depth, 2)
    b_re_t = jnp.swapaxes(b_re, -1, -2).astype(F32)
    b_im_t = jnp.swapaxes(b_im, -1, -2).astype(F32)

    gc = _pick_tile(g, 16, 1)

    def spec(*tail):
        zeros = (0,) * (len(tail) - 1)
        return pl.BlockSpec((None, None) + tail, lambda l, d, q: (l, d, q) + zeros)

    ap_spec = pl.BlockSpec((None, None, 8, gc, 1, p), lambda l, d, q: (l, d, 0, q, 0, 0))
    pw_spec = pl.BlockSpec((None, None, T_BLK, gc, h, p), lambda l, d, q: (l, d, 0, q, 0, 0))
    th = T_BLK * h
    outs = (jax.ShapeDtypeStruct(lead + (T_BLK, g, h, p), F32),) * 4 + (
        jax.ShapeDtypeStruct(lead + (g, T_BLK, h, th), F32),
        jax.ShapeDtypeStruct(lead + (8, g, 1, p), F32),
        jax.ShapeDtypeStruct(lead + (8, g, 1, p), F32))
    blocks = 4 * _nbytes((gc, th, LANES), F32) + 8 * _nbytes((gc, h, LANES), F32)
    return pl.pallas_call(
        _s5_prep_kernel,
        out_shape=outs,
        grid=lead + (g // gc,),
        in_specs=[spec(gc, 1, p), spec(gc, 1, p), spec(gc, 1, 1), spec(gc, h, p), spec(gc, h, p),
                  spec(gc, h, p), spec(gc, h, p)],
        out_specs=(pw_spec,) * 4 + (spec(gc, T_BLK, h, th), ap_spec, ap_spec),
        compiler_params=_compiler_params(("parallel", "parallel", "parallel"), blocks,
                                         16 * _nbytes((gc, th, LANES), F32)),
        name="s5_prep",
    )(lam_re.astype(F32)[:, :, :, None, :], lam_im.astype(F32)[:, :, :, None, :],
      log_step.astype(F32)[..., None, None], b_re_t, b_im_t, c_re.astype(F32), c_im.astype(F32))


def _block_diag_kernel(*refs, n_parts):
    c_refs = refs[:n_parts]
    e_ref, rg_ref, cg_ref, o_ref = refs[n_parts:]
    same_group = rg_ref[...] == cg_ref[...]
    width = e_ref.shape[1]
    for q, c_ref in enumerate(c_refs):
        spread = _dot(c_ref[...].astype(BF16), e_ref[...])
        o_ref[:, q * width:(q + 1) * width] = jnp.where(same_group, spread, 0.0).astype(o_ref.dtype)


def _block_diag(compacts, gl, row_inner, col_inner, name):
    n_cb, rows, kc = compacts[0].shape
    width = kc * gl
    tr = _pick_tile(rows, 512, BF16_ROWS)
    col = np.arange(width)
    spread = np.arange(kc)[:, None] == (col // (gl * col_inner)) * col_inner + col % col_inner
    row_group = ((np.arange(rows) // row_inner) % gl).astype(np.int32)[:, None]
    col_group = ((col // col_inner) % gl).astype(np.int32)[None, :]
    n_parts = len(compacts)
    blocks = (n_parts * _nbytes((tr, LANES), F32) + _nbytes((kc, width), BF16) + _nbytes((tr, LANES), jnp.int32)
              + _nbytes((8, width), jnp.int32) + _nbytes((tr, n_parts * width), BF16))
    return pl.pallas_call(
        functools.partial(_block_diag_kernel, n_parts=n_parts),
        out_shape=jax.ShapeDtypeStruct((n_cb, rows, n_parts * width), BF16),
        grid=(n_cb, rows // tr),
        in_specs=[pl.BlockSpec((None, tr, kc), lambda c, i: (c, i, 0))] * n_parts
                 + [pl.BlockSpec((kc, width), lambda c, i: (0, 0)),
                    pl.BlockSpec((tr, 1), lambda c, i: (i, 0)),
                    pl.BlockSpec((1, width), lambda c, i: (0, 0))],
        out_specs=pl.BlockSpec((None, tr, n_parts * width), lambda c, i: (c, i, 0)),
        compiler_params=_compiler_params(("parallel", "parallel"), blocks),
        name=name,
    )(*compacts, jnp.asarray(spread, BF16), jnp.asarray(row_group), jnp.asarray(col_group))


def _s5_weights(prep, n_cb, gl, h, p):
    e_re, e_im, ca_re, ca_im, klag, ap_re, ap_im = prep
    t = T_BLK

    def rows_by_step(x, reverse):
        x = x[::-1] if reverse else x
        x = jnp.transpose(x.reshape(t, n_cb, gl, h, p), (1, 0, 2, 3, 4))
        return x.reshape(n_cb, t * gl * h, p)

    w_s = _block_diag([rows_by_step(e_re[0], True), rows_by_step(e_im[0], True),
                       rows_by_step(e_re[1], False), rows_by_step(e_im[1], False)], gl, h, p, "s5_w_states")

    w_cre = _block_diag([rows_by_step(ca_re[0], False), rows_by_step(ca_re[1], True)], gl, h, p, "s5_w_carry_re")
    w_cim = _block_diag([rows_by_step(ca_im[0], False), rows_by_step(ca_im[1], True)], gl, h, p, "s5_w_carry_im")

    kst = (klag[0] + klag[1]).reshape(n_cb, gl, t, h, t * h)
    kst = jnp.transpose(kst, (0, 2, 1, 3, 4))
    w_i = _block_diag([kst.reshape(n_cb, t * gl * h, t * h)], gl, h, h, "s5_w_intra")

    def lanes(ap):
        ap = jnp.stack([ap[0], ap[1, ::-1]], axis=0)
        ap = ap.reshape(2, 8, n_cb, gl * p)
        return jnp.transpose(ap, (2, 0, 1, 3))
    apw = jnp.stack([lanes(ap_re), lanes(ap_im)], axis=2)

    return w_s, w_cre, w_cim, w_i, apw


def _fold_time_blocks(u_ref, lhs_ref, r):
    pieces = u_ref.shape[0]
    for tau in range(T_BLK):
        for h in range(pieces):
            col = (tau * pieces + h) * LANES
            lhs_ref[:, col:col + LANES] = u_ref[h, pl.ds(tau, r, stride=T_BLK), :].astype(lhs_ref.dtype)


def _s5_states_kernel(u_ref, w_ref, s_ref, lhs_ref, *, r):
    @pl.when(pl.program_id(2) == 0)
    def _():
        _fold_time_blocks(u_ref, lhs_ref, r)

    s_ref[...] = _dot(lhs_ref[...], w_ref[...])


def _s5_states(u, w_s, r, tn):
    n_cb, kw, ns = w_s.shape
    n = u.shape[1]
    nb = n // T_BLK
    ppc = u.shape[0] // n_cb
    blocks = _nbytes((ppc, T_BLK * r, LANES), F32) + _nbytes((kw, tn), BF16) + _nbytes((r, tn), F32)
    scratch = _nbytes((r, kw), BF16)
    return pl.pallas_call(
        functools.partial(_s5_states_kernel, r=r),
        out_shape=jax.ShapeDtypeStruct((n_cb, nb, ns), F32),
        grid=(n_cb, nb // r, ns // tn),
        in_specs=[pl.BlockSpec((ppc, T_BLK * r, LANES), lambda c, i, j: (c, i, 0)),
                  pl.BlockSpec((None, kw, tn), lambda c, i, j: (c, 0, j))],
        out_specs=pl.BlockSpec((None, r, tn), lambda c, i, j: (c, i, j)),
        scratch_shapes=[pltpu.VMEM((r, kw), BF16)],
        compiler_params=_compiler_params(("parallel", "parallel", "arbitrary"), blocks, scratch),
        name="s5_states",
    )(u, w_s)


def _s5_scan_kernel(sre_ref, sim_ref, apre_ref, apim_ref, xre_ref, xim_ref, *, n_pairs):
    d = pl.program_id(1)
    lw = sre_ref.shape[-1]
    ap_re, ap_im = apre_ref[...], apim_ref[...]
    row = lax.broadcasted_iota(jnp.int32, (8, lw), 0)

    def run(reverse):
        steps = []
        for shift in (1, 2, 4):
            src = (8 - shift) if reverse else (shift - 1)
            keep = (row < 8 - shift) if reverse else (row >= shift)
            steps.append((8 - shift if reverse else shift,
                          jnp.where(keep, ap_re[src:src + 1, :], 0.0),
                          jnp.where(keep, ap_im[src:src + 1, :], 0.0)))
        edge = 7 if reverse else 0
        out_row = 0 if reverse else 7
        nb_shift = 7 if reverse else 1

        def tile(r0, c_re, c_im):
            x_re = sre_ref[pl.ds(r0, 8), :]
            x_im = sim_ref[pl.ds(r0, 8), :]
            for amount, m_re, m_im in steps:
                s_re = pltpu.roll(x_re, amount, 0)
                s_im = pltpu.roll(x_im, amount, 0)
                x_re, x_im = x_re + m_re * s_re - m_im * s_im, x_im + m_re * s_im + m_im * s_re
            inc_re = x_re + ap_re * c_re - ap_im * c_im
            inc_im = x_im + ap_re * c_im + ap_im * c_re
            e_re = jnp.where(row == edge, c_re, pltpu.roll(inc_re, nb_shift, 0))
            e_im = jnp.where(row == edge, c_im, pltpu.roll(inc_im, nb_shift, 0))
            return e_re, e_im, inc_re[out_row:out_row + 1, :], inc_im[out_row:out_row + 1, :]

        def body(k, carry):
            c_re, c_im = carry
            pair = (n_pairs - 1 - k) if reverse else k
            r0 = pl.multiple_of(pair * BF16_ROWS, BF16_ROWS)
            e_re, e_im = [None, None], [None, None]
            for half in ((1, 0) if reverse else (0, 1)):
                e_re[half], e_im[half], c_re, c_im = tile(pl.multiple_of(r0 + 8 * half, 8), c_re, c_im)
            xre_ref[pl.ds(r0, BF16_ROWS), :] = jnp.concatenate(e_re, axis=0).astype(xre_ref.dtype)
            xim_ref[pl.ds(r0, BF16_ROWS), :] = jnp.concatenate(e_im, axis=0).astype(xim_ref.dtype)
            return c_re, c_im

        zero = jnp.zeros((1, lw), F32)
        lax.fori_loop(0, n_pairs, body, (zero, zero))

    @pl.when(d == 0)
    def _():
        run(False)

    @pl.when(d == 1)
    def _():
        run(True)


def _s5_scan(s, apw, lw):
    n_cb, nb, ns = s.shape
    glp = ns // 4
    nq = glp // lw
    blocks = 2 * _nbytes((nb, lw), F32) + 2 * _nbytes((nb, lw), BF16) + 2 * _nbytes((8, lw), F32)
    out = jax.ShapeDtypeStruct((n_cb, nb, 2 * glp), BF16)

    def s_spec(ri):
        return pl.BlockSpec((None, nb, lw), lambda c, d, q: (c, 0, (2 * d + ri) * nq + q))

    def ap_spec(ri):
        return pl.BlockSpec((None, None, None, 8, lw), lambda c, d, q: (c, d, ri, 0, q))

    x_spec = pl.BlockSpec((None, nb, lw), lambda c, d, q: (c, 0, d * nq + q))
    return pl.pallas_call(
        functools.partial(_s5_scan_kernel, n_pairs=nb // BF16_ROWS),
        out_shape=(out, out),
        grid=(n_cb, 2, nq),
        in_specs=[s_spec(0), s_spec(1), ap_spec(0), ap_spec(1)],
        out_specs=(x_spec, x_spec),
        compiler_params=_compiler_params(("parallel", "parallel", "parallel"), blocks),
        name="s5_scan",
    )(s, s, apw, apw)


def _s5_out_kernel(u_ref, xre_ref, xim_ref, wi_ref, wcre_ref, wcim_ref, y_ref, lhs_ref, *, r, tn):
    j = pl.program_id(2)
    pieces = u_ref.shape[0]

    @pl.when(j == 0)
    def _():
        _fold_time_blocks(u_ref, lhs_ref, r)

    y = _dot(lhs_ref[...], wi_ref[...])
    nt = (((1,), (1,)), ((), ()))
    y = y + lax.dot_general(xre_ref[...], wcre_ref[...], nt, preferred_element_type=F32)
    y = y + lax.dot_general(xim_ref[...], wcim_ref[...], nt, preferred_element_type=F32)

    per = tn // (pieces * LANES)
    for jj in range(T_BLK // per):
        @pl.when(j == jj)
        def _():
            for k in range(per):
                for h in range(pieces):
                    col = (k * pieces + h) * LANES
                    y_ref[h, pl.ds(jj * per + k, r, stride=T_BLK), :] = y[:, col:col + LANES]


def _s5_out(u, x_re, x_im, w_i, w_cre, w_cim, r, tn):
    n_cb, kw, _ = w_i.shape
    n = u.shape[1]
    nb = n // T_BLK
    ppc = u.shape[0] // n_cb
    kx = x_re.shape[-1]
    assert tn % (ppc * LANES) == 0 and kw % tn == 0
    blocks = (2 * _nbytes((ppc, T_BLK * r, LANES), F32) + 2 * _nbytes((r, kx), BF16) + _nbytes((kw, tn), BF16)
              + 2 * _nbytes((kx, tn), BF16) + _nbytes((r, tn), F32))
    scratch = _nbytes((r, kw), BF16)
    return pl.pallas_call(
        functools.partial(_s5_out_kernel, r=r, tn=tn),
        out_shape=jax.ShapeDtypeStruct(u.shape, F32),
        grid=(n_cb, nb // r, kw // tn),
        in_specs=[pl.BlockSpec((ppc, T_BLK * r, LANES), lambda c, i, j: (c, i, 0)),
                  pl.BlockSpec((None, r, kx), lambda c, i, j: (c, i, 0)),
                  pl.BlockSpec((None, r, kx), lambda c, i, j: (c, i, 0)),
                  pl.BlockSpec((None, kw, tn), lambda c, i, j: (c, 0, j)),
                  pl.BlockSpec((None, tn, kx), lambda c, i, j: (c, j, 0)),
                  pl.BlockSpec((None, tn, kx), lambda c, i, j: (c, j, 0))],
        out_specs=pl.BlockSpec((ppc, T_BLK * r, LANES), lambda c, i, j: (c, i, 0)),
        scratch_shapes=[pltpu.VMEM((r, kw), BF16)],
        compiler_params=_compiler_params(("parallel", "parallel", "arbitrary"), blocks, scratch),
        name="s5_out",
    )(u, x_re, x_im, w_i, w_cre, w_cim)


def _glu_merge_kernel(y_ref, u_ref, d_ref, w1_ref, w2_ref, pa_ref, gb_ref, m_ref, s_ref):
    @pl.when(pl.program_id(1) == 0)
    def _():
        for c in range(y_ref.shape[0]):
            s = jax.nn.gelu(y_ref[c] + d_ref[c] * u_ref[c])
            s_ref[:, c * LANES:(c + 1) * LANES] = s.astype(s_ref.dtype)

    a = s_ref[...]
    y_b = _dot(a, w1_ref[...]) * _sigmoid(_dot(a, w2_ref[...]))
    m_ref[...] = (pa_ref[...].astype(F32) + gb_ref[...].astype(F32) * y_b).astype(m_ref.dtype)


def _glu_merge(y, u, d_skip, w_glu, layer, pa, gates, d_model, tm, tn):
    n_pc, n, _ = y.shape
    k = n_pc * LANES
    nb = d_model // tn
    blocks = (2 * _nbytes((n_pc, tm, LANES), F32) + 2 * _nbytes((k, tn), BF16) + 3 * _nbytes((tm, tn), BF16))
    scratch = _nbytes((tm, k), BF16)
    return pl.pallas_call(
        _glu_merge_kernel,
        out_shape=jax.ShapeDtypeStruct((n, d_model), BF16),
        grid=(n // tm, nb),
        in_specs=[pl.BlockSpec((n_pc, tm, LANES), lambda i, j: (0, i, 0)),
                  pl.BlockSpec((n_pc, tm, LANES), lambda i, j: (0, i, 0)),
                  pl.BlockSpec((n_pc, 1, LANES), lambda i, j: (0, 0, 0)),
                  _w_spec(k, tn, layer, 0),
                  _w_spec(k, tn, layer, nb),
                  pl.BlockSpec((tm, tn), lambda i, j: (i, j)),
                  pl.BlockSpec((tm, tn), lambda i, j: (i, nb + j))],
        out_specs=pl.BlockSpec((tm, tn), lambda i, j: (i, j)),
        scratch_shapes=[pltpu.VMEM((tm, k), BF16)],
        compiler_params=_compiler_params(("parallel", "arbitrary"), blocks, scratch),
        name="glu_merge",
    )(y, u, d_skip, w_glu, w_glu, pa, gates)


def _ffn_up_kernel(h_ref, hp_ref, hn_ref, wg_ref, wv_ref, cg_ref, cv_ref, act_ref, lhs_ref, *, tm):
    i = pl.program_id(0)
    j = pl.program_id(1)
    last = pl.num_programs(0) - 1
    halo = BF16_ROWS

    @pl.when(j == 0)
    def _():
        row = lax.broadcasted_iota(jnp.int32, hp_ref.shape, 0)
        after = jnp.where(jnp.logical_and(row == 0, i < last), hn_ref[...].astype(F32), 0.0)
        edge = jnp.where(jnp.logical_and(row == halo - 1, i > 0), hp_ref[...].astype(F32), after)
        lhs_ref[0:tm, :] = h_ref[...]
        lhs_ref[tm:, :] = edge.astype(lhs_ref.dtype)

    a = lhs_ref[...]
    rows = tm + halo

    def conv(w_ref, cw_ref):
        u = _dot(a, w_ref[...])
        cw = cw_ref[...]
        up = pltpu.roll(u, 1, 0)[0:tm]
        dn = pltpu.roll(u, rows - 1, 0)[0:tm]
        return up * cw[0:1, :] + u[0:tm] * cw[1:2, :] + dn * cw[2:3, :]

    act_ref[...] = (jax.nn.gelu(conv(wg_ref, cg_ref)) * conv(wv_ref, cv_ref)).astype(act_ref.dtype)


def _ffn_up(hn, w_up, layer, conv_w, d_ff, tm, tn, cast=None):
    n, k = hn.shape
    nb = d_ff // tn
    hb = tm // BF16_ROWS
    n_hb = n // BF16_ROWS
    blocks = (_nbytes((tm + 2 * BF16_ROWS, k), BF16) + 2 * _nbytes((k, tn), BF16) + 2 * _nbytes((3, tn), F32)
              + _nbytes((tm, tn), BF16))
    scratch = _nbytes((tm + BF16_ROWS, k), BF16)
    outs = _call_with_cast(
        functools.partial(_ffn_up_kernel, tm=tm), cast,
        grid=(n // tm, nb),
        in_specs=[pl.BlockSpec((tm, k), lambda i, j: (i, 0)),
                  pl.BlockSpec((BF16_ROWS, k), lambda i, j: (jnp.maximum(i * hb - 1, 0), 0)),
                  pl.BlockSpec((BF16_ROWS, k), lambda i, j: (jnp.minimum((i + 1) * hb, n_hb - 1), 0)),
                  _w_spec(k, tn, layer, 0),
                  _w_spec(k, tn, layer, nb),
                  pl.BlockSpec((3, tn), lambda i, j: (0, j)),
                  pl.BlockSpec((3, tn), lambda i, j: (0, nb + j))],
        out_specs=[pl.BlockSpec((tm, tn), lambda i, j: (i, j))],
        out_shape=[jax.ShapeDtypeStruct((n, d_ff), BF16)],
        operands=(hn, hn, hn, w_up, w_up, conv_w, conv_w),
        scratch_shapes=[pltpu.VMEM((tm + BF16_ROWS, k), BF16)],
        block_bytes=blocks, scratch_bytes=scratch, name="ffn_up")
    return outs[0] if cast is None else outs


def kernel(x, meta_tokens, norm_mix, w_in, conv_a_w, w_a, ssm_lambda_re, ssm_lambda_im, ssm_log_step,
           ssm_b_re, ssm_b_im, ssm_c_re, ssm_c_im, ssm_d, w_glu, w_out, norm_ffn, w_up, conv_ffn_w,
           w_down, norm_final):
    bsz, seq, d_model = x.shape
    n_meta = meta_tokens.shape[0]
    depth = w_in.shape[0]
    d_conv = w_a.shape[1]
    d_ssm = ssm_d.shape[1]
    d_ff = w_down.shape[1]
    n_groups, p_state, h_grp = ssm_b_re.shape[2:]
    assert w_in.shape[2] == 3 * d_conv + d_ssm + 2 * d_model and n_groups * h_grp == d_ssm

    cbw = min(S5_CB_WIDTH, d_ssm)
    n_cb = d_ssm // cbw
    gl = cbw // h_grp
    assert n_cb * cbw == d_ssm and gl * h_grp == cbw

    n_true = n_meta + seq
    row_align = T_BLK * BF16_ROWS
    n_pad = _round_up(n_true, PAD_ROWS if n_true >= 4 * PAD_ROWS else row_align)
    assert n_pad % row_align == 0
    nb = n_pad // T_BLK

    tm = _pick_tile(n_pad, TM_TARGET, BF16_ROWS)
    tm_up = _pick_tile(n_pad, TM_FFN_UP_TARGET, BF16_ROWS)
    tm_half = _pick_tile(n_pad, TM_HALF_TARGET, BF16_ROWS)
    tm_ew = _pick_tile(n_pad, TM_EW_TARGET, BF16_ROWS)
    r_s5 = _pick_tile(nb, R_S5_TARGET, BF16_ROWS)
    tn = lambda n, target=512: _pick_tile(n, target, 128)

    prep = _s5_prep(ssm_lambda_re, ssm_lambda_im, ssm_log_step, ssm_b_re, ssm_b_im, ssm_c_re, ssm_c_im)
    d_skip = ssm_d.astype(F32).reshape(depth, d_ssm // LANES, 1, LANES)

    residual = functools.partial(_epi_residual, n_valid=n_true)
    tile_spec = lambda t, rows=tm: pl.BlockSpec((rows, t), lambda i, j: (i, j))
    u_off = 3 * d_conv
    gate_off = u_off + d_ssm

    w_a_b, w_glu_b, w_out_b = (w.astype(BF16) for w in (w_a, w_glu, w_out))
    w_in_first = w_in[0].astype(BF16)

    outs = []
    for b in range(bsz):
        w_in_l = w_in_first
        for l in range(depth):
            w_s, w_cre, w_cim, w_i, apw = _s5_weights([t[l] for t in prep], n_cb, gl, h_grp, p_state)

            if l == 0:
                h_res, hn = _assemble_norm(meta_tokens, x[b], norm_mix[0], n_pad, tm_ew)
            else:
                hn = _rmsnorm(h_res, norm_mix[l], BF16, tm_ew)
            bg, cv, w_down_l = _inproj_conv(hn, w_in_l, None, d_conv, tm_half, tn(d_conv), cast=(w_down, l))
            t_u = tn(d_ssm, 256)
            u = _mm(hn, w_in_l, None, col0=u_off, n_cols=d_ssm, tm=tm, tn=t_u, epilogue=_epi_pieces,
                    out_shape=jax.ShapeDtypeStruct((d_ssm // LANES, n_pad, LANES), F32),
                    out_spec=pl.BlockSpec((t_u // LANES, tm, LANES), lambda i, j: (j, i, 0)), name="inproj_u")
            t_g = tn(2 * d_model)
            gates, w_up_l = _mm(hn, w_in_l, None, col0=gate_off, n_cols=2 * d_model, tm=tm, tn=t_g,
                                epilogue=_epi_sigmoid,
                                out_shape=jax.ShapeDtypeStruct((n_pad, 2 * d_model), BF16),
                                out_spec=tile_spec(t_g), name="inproj_gates", cast=(w_up, l))

            pa = _branch_a(bg, cv, conv_a_w[l].astype(F32), w_a_b, l, gates, d_model, tm, tn(d_model, 1024))
            t_d = tn(d_model)

            states = _s5_states(u, w_s, _pick_tile(nb, R_S5_STATES_TARGET, BF16_ROWS), tn(w_s.shape[-1]))
            x_re, x_im = _s5_scan(states, apw, tn(gl * p_state, SCAN_LANES_TARGET))
            y = _s5_out(u, x_re, x_im, w_i, w_cre, w_cim, r_s5, tn(T_BLK * cbw))

            merged = _glu_merge(y, u, d_skip[l], w_glu_b, l, pa, gates, d_model, tm_half, tn(d_model, 1024))
            h_res = _mm(merged, w_out_b, l, col0=0, n_cols=d_model, tm=tm, tn=t_d, epilogue=residual,
                        extras=(h_res,), extra_specs=(tile_spec(t_d),),
                        out_shape=jax.ShapeDtypeStruct((n_pad, d_model), F32), out_spec=tile_spec(t_d),
                        name="out_proj")

            hn = _rmsnorm(h_res, norm_ffn[l], BF16, tm_ew)
            if l + 1 < depth:
                act, w_in_l = _ffn_up(hn, w_up_l, None, conv_ffn_w[l].astype(F32), d_ff, tm_up, tn(d_ff),
                                      cast=(w_in, l + 1))
            else:
                act = _ffn_up(hn, w_up_l, None, conv_ffn_w[l].astype(F32), d_ff, tm_up, tn(d_ff))
            t_o = tn(d_model, 256)
            h_res = _mm(act, w_down_l, None, col0=0, n_cols=d_model, tm=tm_half, tn=t_o, epilogue=residual,
                        extras=(h_res,), extra_specs=(tile_spec(t_o, tm_half),),
                        out_shape=jax.ShapeDtypeStruct((n_pad, d_model), F32), out_spec=tile_spec(t_o, tm_half),
                        name="ffn_down")

        outs.append(_final_norm(h_res, norm_final, n_meta, seq, x.dtype, _pick_tile(seq, TM_EW_TARGET, n_meta)))
    return jnp.stack(outs, axis=0)
```

```python
import functools

import numpy as np
import jax
import jax.numpy as jnp
from jax import lax
from jax.experimental import pallas as pl
from jax.experimental.pallas import tpu as pltpu

F32 = jnp.float32
BF16 = jnp.bfloat16
EPS = 1e-6

T_BLK = 16
S5_CB_WIDTH = 128
PAD_ROWS = 1280
BF16_ROWS = 16
LANES = 128
VMEM_CAP_BYTES = 60000 * 1024
VMEM_TEMP_BYTES = 20 * 1024 * 1024

TM_TARGET = 1280
TM_FFN_UP_TARGET = 832
TM_HALF_TARGET = 640
R_S5_STATES_TARGET = 1040
TM_EW_TARGET = 320
R_S5_TARGET = 1040
SCAN_LANES_TARGET = 256


def _round_up(n, m):
    return (n + m - 1) // m * m


def _pick_tile(n, target, align):
    best = 0
    for t in range(align, min(n, target) + 1, align):
        if n % t == 0:
            best = t
    assert best > 0, (n, target, align)
    return best


def _nbytes(shape, dtype):
    return int(np.prod(shape)) * jnp.dtype(dtype).itemsize


def _compiler_params(semantics, block_bytes, scratch_bytes=0):
    est = 2 * block_bytes + scratch_bytes + VMEM_TEMP_BYTES
    return pltpu.CompilerParams(dimension_semantics=semantics,
                                vmem_limit_bytes=int(min(VMEM_CAP_BYTES, est)))


def _dot(a, b):
    return jnp.dot(a, b, preferred_element_type=F32)


def _sigmoid(x):
    return 0.5 * (1.0 + jnp.tanh(0.5 * x))


def _rmsnorm_kernel(x_ref, g_ref, o_ref):
    x = x_ref[...]
    ms = jnp.mean(x * x, axis=-1, keepdims=True)
    o_ref[...] = (x * lax.rsqrt(ms + EPS) * g_ref[...]).astype(o_ref.dtype)


def _rmsnorm(x, g, out_dtype, tm):
    n, d = x.shape
    blocks = _nbytes((tm, d), F32) + _nbytes((tm, d), out_dtype)
    return pl.pallas_call(
        _rmsnorm_kernel,
        out_shape=jax.ShapeDtypeStruct((n, d), out_dtype),
        grid=(n // tm,),
        in_specs=[pl.BlockSpec((tm, d), lambda i: (i, 0)),
                  pl.BlockSpec((1, d), lambda i: (0, 0))],
        out_specs=pl.BlockSpec((tm, d), lambda i: (i, 0)),
        compiler_params=_compiler_params(("parallel",), blocks),
        name="rmsnorm",
    )(x, g.reshape(1, d).astype(F32))


def _assemble_norm_kernel(meta_ref, prev_ref, x_ref, g_ref, h_ref, hn_ref, *, n_valid):
    i = pl.program_id(0)
    tm = h_ref.shape[0]
    n_meta = meta_ref.shape[0]
    head = jnp.where(i == 0, meta_ref[...], prev_ref[...])
    tile = jnp.concatenate([head, x_ref[0:tm - n_meta, :]], axis=0)
    rows = i * tm + lax.broadcasted_iota(jnp.int32, tile.shape, 0)
    tile = jnp.where(rows < n_valid, tile, 0.0)
    h_ref[...] = tile
    ms = jnp.mean(tile * tile, axis=-1, keepdims=True)
    hn_ref[...] = (tile * lax.rsqrt(ms + EPS) * g_ref[...]).astype(hn_ref.dtype)


def _assemble_norm(meta, x, g, n_pad, tm):
    n_meta, d = meta.shape
    seq = x.shape[0]
    assert n_meta % 8 == 0 and tm % n_meta == 0 and seq % n_meta == 0 and n_pad % tm == 0
    per = tm // n_meta
    last_x = pl.cdiv(seq, tm) - 1
    last_prev = seq // n_meta - 1
    blocks = _nbytes((tm + 2 * n_meta, d), F32) + _nbytes((tm, d), F32) + _nbytes((tm, d), BF16)
    return pl.pallas_call(
        functools.partial(_assemble_norm_kernel, n_valid=n_meta + seq),
        out_shape=(jax.ShapeDtypeStruct((n_pad, d), F32), jax.ShapeDtypeStruct((n_pad, d), BF16)),
        grid=(n_pad // tm,),
        in_specs=[pl.BlockSpec((n_meta, d), lambda i: (0, 0)),
                  pl.BlockSpec((n_meta, d), lambda i: (jnp.clip(i * per - 1, 0, last_prev), 0)),
                  pl.BlockSpec((tm, d), lambda i: (jnp.minimum(i, last_x), 0)),
                  pl.BlockSpec((1, d), lambda i: (0, 0))],
        out_specs=(pl.BlockSpec((tm, d), lambda i: (i, 0)), pl.BlockSpec((tm, d), lambda i: (i, 0))),
        compiler_params=_compiler_params(("parallel",), blocks),
        name="assemble_norm",
    )(meta.astype(F32), x.astype(F32), x.astype(F32), g.reshape(1, d).astype(F32))


def _final_norm_kernel(a_ref, b_ref, g_ref, o_ref, *, skip):
    x = jnp.concatenate([a_ref[skip:, :], b_ref[...]], axis=0)
    ms = jnp.mean(x * x, axis=-1, keepdims=True)
    o_ref[...] = (x * lax.rsqrt(ms + EPS) * g_ref[...]).astype(o_ref.dtype)


def _final_norm(x, g, skip, n_out, out_dtype, tm):
    n, d = x.shape
    assert skip % 8 == 0 and tm % skip == 0 and n_out % tm == 0 and n >= n_out + skip
    per = tm // skip
    blocks = _nbytes((tm + skip, d), F32) + _nbytes((tm, d), out_dtype)
    return pl.pallas_call(
        functools.partial(_final_norm_kernel, skip=skip),
        out_shape=jax.ShapeDtypeStruct((n_out, d), out_dtype),
        grid=(n_out // tm,),
        in_specs=[pl.BlockSpec((tm, d), lambda i: (i, 0)),
                  pl.BlockSpec((skip, d), lambda i: ((i + 1) * per, 0)),
                  pl.BlockSpec((1, d), lambda i: (0, 0))],
        out_specs=pl.BlockSpec((tm, d), lambda i: (i, 0)),
        compiler_params=_compiler_params(("parallel",), blocks),
        name="final_norm",
    )(x, x, g.reshape(1, d).astype(F32))


def _mm_kernel(*refs, n_extra, epilogue, tm):
    a_ref, w_ref = refs[0], refs[1]
    extras = refs[2:2 + n_extra]
    o_ref = refs[2 + n_extra]
    acc = _dot(a_ref[...], w_ref[...])
    row0 = pl.program_id(0) * tm
    o_ref[...] = epilogue(acc, row0, *[e[...] for e in extras]).astype(o_ref.dtype)


def _w_spec(k, tn, layer, jb0):
    if layer is None:
        return pl.BlockSpec((k, tn), lambda i, j: (0, jb0 + j))
    return pl.BlockSpec((None, k, tn), lambda i, j: (layer, 0, jb0 + j))


def _call_with_cast(kernel_fn, cast, *, grid, in_specs, out_specs, out_shape, operands, scratch_shapes=(),
                    block_bytes, scratch_bytes=0, name):
    if cast is None:
        return pl.pallas_call(
            kernel_fn, out_shape=tuple(out_shape), grid=grid, in_specs=list(in_specs), out_specs=tuple(out_specs),
            scratch_shapes=list(scratch_shapes),
            compiler_params=_compiler_params(("parallel", "arbitrary"), block_bytes, scratch_bytes), name=name,
        )(*operands)

    stack, layer = cast
    _, k, n = stack.shape
    n_steps = grid[0] * grid[1]
    options = []
    for rows in range(BF16_ROWS, k + 1, BF16_ROWS):
        if k % rows == 0 and k // rows <= n_steps:
            splits = max(c for c in range(1, n_steps // (k // rows) + 1) if n % (c * LANES) == 0)
            options.append((rows * (n // splits), rows, splits))
    _, rows, splits = min(options)
    cols = n // splits
    last_chunk = (k // rows) * splits - 1
    n_in, n_out = len(in_specs), len(out_specs)

    def chunk(i, j):
        q = jnp.minimum(i * grid[1] + j, last_chunk)
        return q // splits, q % splits

    def body(*refs):
        src_ref, dst_ref = refs[n_in], refs[n_in + 1 + n_out]
        dst_ref[...] = src_ref[...].astype(dst_ref.dtype)
        kernel_fn(*refs[:n_in], *refs[n_in + 1:n_in + 1 + n_out], *refs[n_in + 2 + n_out:])

    cast_bytes = _nbytes((rows, cols), F32) + _nbytes((rows, cols), BF16)
    return pl.pallas_call(
        body,
        out_shape=(*out_shape, jax.ShapeDtypeStruct((k, n), BF16)),
        grid=grid,
        in_specs=[*in_specs, pl.BlockSpec((None, rows, cols), lambda i, j: (layer, *chunk(i, j)))],
        out_specs=(*out_specs, pl.BlockSpec((rows, cols), lambda i, j: chunk(i, j))),
        scratch_shapes=list(scratch_shapes),
        compiler_params=_compiler_params(("arbitrary", "arbitrary"), block_bytes + cast_bytes, scratch_bytes),
        name=name,
    )(*operands, stack)


def _mm(a, w, layer, *, col0, n_cols, tm, tn, epilogue, extras=(), extra_specs=(), out_shape, out_spec, name,
        cast=None):
    n, k = a.shape
    assert n % tm == 0 and n_cols % tn == 0 and col0 % tn == 0
    blocks = (_nbytes((tm, k), a.dtype) + _nbytes((k, tn), w.dtype) + _nbytes((tm, tn), F32)
              + sum(_nbytes((tm, tn), e.dtype) for e in extras))
    outs = _call_with_cast(
        functools.partial(_mm_kernel, n_extra=len(extras), epilogue=epilogue, tm=tm), cast,
        grid=(n // tm, n_cols // tn),
        in_specs=[pl.BlockSpec((tm, k), lambda i, j: (i, 0)),
                  _w_spec(k, tn, layer, col0 // tn),
                  *extra_specs],
        out_specs=[out_spec], out_shape=[out_shape], operands=(a, w, *extras), block_bytes=blocks, name=name)
    return outs[0] if cast is None else outs


def _epi_pieces(acc, row0):
    return jnp.stack([acc[:, h * LANES:(h + 1) * LANES] for h in range(acc.shape[1] // LANES)], axis=0)


def _epi_sigmoid(acc, row0):
    return _sigmoid(acc)


def _epi_residual(acc, row0, res, *, n_valid):
    rows = row0 + lax.broadcasted_iota(jnp.int32, acc.shape, 0)
    return jnp.where(rows < n_valid, res + acc, 0.0)


def _inproj_conv_kernel(a_ref, wb_ref, wc_ref, wv_ref, wu_ref, bg_ref, cv_ref, u_ref):
    a = a_ref[...]
    bg_ref[...] = _dot(a, wb_ref[...]).astype(bg_ref.dtype)
    cv_ref[...] = (_dot(a, wc_ref[...]) * _dot(a, wv_ref[...])).astype(cv_ref.dtype)
    u_ref[...] = _epi_pieces(_dot(a, wu_ref[...]), 0)


def _inproj_conv(hn, w_in, layer, d_conv, d_ssm, tm, tn, cast=None):
    n, k = hn.shape
    nb = d_conv // tn
    tn_u = d_ssm // nb
    assert tn_u % LANES == 0 and (3 * d_conv) % tn_u == 0
    blocks = (_nbytes((tm, k), BF16) + 3 * _nbytes((k, tn), BF16) + _nbytes((k, tn_u), BF16)
              + 2 * _nbytes((tm, tn), BF16) + _nbytes((tm, tn_u), F32))
    out = jax.ShapeDtypeStruct((n, d_conv), BF16)
    return _call_with_cast(
        _inproj_conv_kernel, cast,
        grid=(n // tm, nb),
        in_specs=[pl.BlockSpec((tm, k), lambda i, j: (i, 0)),
                  _w_spec(k, tn, layer, 0),
                  _w_spec(k, tn, layer, nb),
                  _w_spec(k, tn, layer, 2 * nb),
                  _w_spec(k, tn_u, layer, 3 * d_conv // tn_u)],
        out_specs=[pl.BlockSpec((tm, tn), lambda i, j: (i, j)),
                   pl.BlockSpec((tm, tn), lambda i, j: (i, j)),
                   pl.BlockSpec((tn_u // LANES, tm, LANES), lambda i, j: (j, i, 0))],
        out_shape=[out, out, jax.ShapeDtypeStruct((d_ssm // LANES, n, LANES), F32)],
        operands=(hn, w_in, w_in, w_in, w_in), block_bytes=blocks, name="inproj_conv")


def _shifted_rows(x, prev_row, next_row):
    t = x.shape[0]
    row = lax.broadcasted_iota(jnp.int32, x.shape, 0)
    up = jnp.where(row == 0, prev_row, pltpu.roll(x, 1, 0))
    dn = jnp.where(row == t - 1, next_row, pltpu.roll(x, t - 1, 0))
    return up, dn


def _branch_a_kernel(bg_ref, cv_ref, cvp_ref, cvn_ref, cw_ref, w_ref, gate_ref, o_ref, z_ref, *, chunk):
    i = pl.program_id(0)
    last = pl.num_programs(0) - 1

    @pl.when(pl.program_id(1) == 0)
    def _():
        for c0 in range(0, z_ref.shape[1], chunk):
            cols = slice(c0, c0 + chunk)
            x = cv_ref[:, cols].astype(F32)
            prev = cvp_ref[:, cols].astype(F32)[BF16_ROWS - 1:BF16_ROWS, :]
            nxt = cvn_ref[:, cols].astype(F32)[0:1, :]
            prev = jnp.where(i > 0, prev, 0.0)
            nxt = jnp.where(i < last, nxt, 0.0)
            up, dn = _shifted_rows(x, prev, nxt)
            w = cw_ref[:, cols]
            conv = up * w[0:1, :] + x * w[1:2, :] + dn * w[2:3, :]
            z_ref[:, cols] = (bg_ref[:, cols].astype(F32) * conv).astype(z_ref.dtype)

    o_ref[...] = (_dot(z_ref[...], w_ref[...]) * gate_ref[...].astype(F32)).astype(o_ref.dtype)


def _branch_a(bg, cv, conv_w, w_a, layer, gates, d_model, tm, tn):
    n, c = bg.shape
    hb = tm // BF16_ROWS
    n_hb = n // BF16_ROWS
    blocks = (2 * _nbytes((tm, c), BF16) + 2 * _nbytes((BF16_ROWS, c), BF16) + _nbytes((8, c), F32)
              + _nbytes((c, tn), BF16) + 2 * _nbytes((tm, tn), BF16))
    scratch = _nbytes((tm, c), BF16)
    return pl.pallas_call(
        functools.partial(_branch_a_kernel, chunk=_pick_tile(c, 512, LANES)),
        out_shape=jax.ShapeDtypeStruct((n, d_model), BF16),
        grid=(n // tm, d_model // tn),
        in_specs=[pl.BlockSpec((tm, c), lambda i, j: (i, 0)),
                  pl.BlockSpec((tm, c), lambda i, j: (i, 0)),
                  pl.BlockSpec((BF16_ROWS, c), lambda i, j: (jnp.maximum(i * hb - 1, 0), 0)),
                  pl.BlockSpec((BF16_ROWS, c), lambda i, j: (jnp.minimum((i + 1) * hb, n_hb - 1), 0)),
                  pl.BlockSpec((3, c), lambda i, j: (0, 0)),
                  _w_spec(c, tn, layer, 0),
                  pl.BlockSpec((tm, tn), lambda i, j: (i, j))],
        out_specs=pl.BlockSpec((tm, tn), lambda i, j: (i, j)),
        scratch_shapes=[pltpu.VMEM((tm, c), BF16)],
        compiler_params=_compiler_params(("parallel", "arbitrary"), blocks, scratch),
        name="branch_a",
    )(bg, cv, cv, cv, conv_w, w_a, gates)


def _cmul(ar, ai, br, bi):
    return ar * br - ai * bi, ar * bi + ai * br


def _s5_prep_kernel(lre_ref, lim_ref, lst_ref, bre_ref, bim_ref, cre_ref, cim_ref,
                    ere_ref, eim_ref, care_ref, caim_ref, klag_ref, apre_ref, apim_ref):
    lam_re, lam_im = lre_ref[...], lim_ref[...]
    dt = jnp.exp(lst_ref[...])
    mag = jnp.exp(lam_re * dt)
    a_re = mag * jnp.cos(lam_im * dt)
    a_im = mag * jnp.sin(lam_im * dt)
    nr, ni = a_re - 1.0, a_im
    den = lam_re * lam_re + lam_im * lam_im
    f_re = (nr * lam_re + ni * lam_im) / den
    f_im = (ni * lam_re - nr * lam_im) / den
    b_re, b_im = bre_ref[...], bim_ref[...]
    bb_re, bb_im = _cmul(f_re, f_im, b_re, b_im)
    c_re, c_im = cre_ref[...], cim_ref[...]

    pw = [(jnp.ones_like(a_re), jnp.zeros_like(a_im))]
    for _ in range(T_BLK):
        pw.append(_cmul(pw[-1][0], pw[-1][1], a_re, a_im))

    e_re, e_im, ck_re, ck_im = [], [], [], []
    for k in range(T_BLK + 1):
        cr, ci = _cmul(pw[k][0], pw[k][1], c_re, c_im)
        ck_re.append(cr)
        ck_im.append(ci)
    for k in range(T_BLK):
        er, ei = _cmul(pw[k][0], pw[k][1], bb_re, bb_im)
        e_re.append(er)
        e_im.append(ei)
    for k in range(T_BLK):
        ere_ref[k] = e_re[k]
        eim_ref[k] = e_im[k]
        care_ref[k] = ck_re[k + 1]
        caim_ref[k] = -ck_im[k + 1]

    dn = (((2,), (2,)), ((0,), (0,)))
    k_t = (lax.dot_general(bb_re, jnp.concatenate(ck_re[:-1], axis=1), dn,
                           precision=lax.Precision.HIGHEST, preferred_element_type=F32)
           - lax.dot_general(bb_im, jnp.concatenate(ck_im[:-1], axis=1), dn,
                             precision=lax.Precision.HIGHEST, preferred_element_type=F32))
    n_grp, h_out, th = k_t.shape
    h_in = th // T_BLK
    k_t = k_t.reshape(n_grp * h_out, th)
    src = lax.broadcasted_iota(jnp.int32, (th, th), 0)
    dst = lax.broadcasted_iota(jnp.int32, (th, th), 1)
    assert h_in & (h_in - 1) == 0, "group width must be a power of two"
    shift = h_in.bit_length() - 1
    same_h = (src & (h_in - 1)) == (dst & (h_in - 1))
    lag, tau = src >> shift, dst >> shift
    backward = pl.program_id(1) == 1
    for sigma in range(T_BLK):
        wanted = jnp.where(backward, sigma - lag, sigma + lag)
        place = jnp.where(jnp.logical_and(same_h, tau == wanted), 1.0, 0.0)
        block = jnp.dot(k_t, place, precision=lax.Precision.HIGHEST, preferred_element_type=F32)
        klag_ref[:, sigma, :, :] = block.reshape(n_grp, h_out, th)

    q_re, q_im = pw[T_BLK]
    r_re, r_im = q_re, q_im
    for m in range(8):
        apre_ref[m] = r_re
        apim_ref[m] = r_im
        r_re, r_im = _cmul(r_re, r_im, q_re, q_im)


def _s5_prep(lam_re, lam_im, log_step, b_re, b_im, c_re, c_im):
    depth, _, g, p = lam_re.shape
    h = b_re.shape[-1]
    lead = (depth, 2)
    b_re_t = jnp.swapaxes(b_re, -1, -2).astype(F32)
    b_im_t = jnp.swapaxes(b_im, -1, -2).astype(F32)

    gc = _pick_tile(g, 16, 1)

    def spec(*tail):
        zeros = (0,) * (len(tail) - 1)
        return pl.BlockSpec((None, None) + tail, lambda l, d, q: (l, d, q) + zeros)

    ap_spec = pl.BlockSpec((None, None, 8, gc, 1, p), lambda l, d, q: (l, d, 0, q, 0, 0))
    pw_spec = pl.BlockSpec((None, None, T_BLK, gc, h, p), lambda l, d, q: (l, d, 0, q, 0, 0))
    th = T_BLK * h
    outs = (jax.ShapeDtypeStruct(lead + (T_BLK, g, h, p), F32),) * 4 + (
        jax.ShapeDtypeStruct(lead + (g, T_BLK, h, th), F32),
        jax.ShapeDtypeStruct(lead + (8, g, 1, p), F32),
        jax.ShapeDtypeStruct(lead + (8, g, 1, p), F32))
    blocks = 4 * _nbytes((gc, th, LANES), F32) + 8 * _nbytes((gc, h, LANES), F32)
    return pl.pallas_call(
        _s5_prep_kernel,
        out_shape=outs,
        grid=lead + (g // gc,),
        in_specs=[spec(gc, 1, p), spec(gc, 1, p), spec(gc, 1, 1), spec(gc, h, p), spec(gc, h, p),
                  spec(gc, h, p), spec(gc, h, p)],
        out_specs=(pw_spec,) * 4 + (spec(gc, T_BLK, h, th), ap_spec, ap_spec),
        compiler_params=_compiler_params(("parallel", "parallel", "parallel"), blocks,
                                         16 * _nbytes((gc, th, LANES), F32)),
        name="s5_prep",
    )(lam_re.astype(F32)[:, :, :, None, :], lam_im.astype(F32)[:, :, :, None, :],
      log_step.astype(F32)[..., None, None], b_re_t, b_im_t, c_re.astype(F32), c_im.astype(F32))


def _block_diag_kernel(*refs, n_parts):
    c_refs = refs[:n_parts]
    e_ref, rg_ref, cg_ref, o_ref = refs[n_parts:]
    same_group = rg_ref[...] == cg_ref[...]
    width = e_ref.shape[1]
    for q, c_ref in enumerate(c_refs):
        spread = _dot(c_ref[...].astype(BF16), e_ref[...])
        o_ref[:, q * width:(q + 1) * width] = jnp.where(same_group, spread, 0.0).astype(o_ref.dtype)


def _block_diag(compacts, gl, row_inner, col_inner, name):
    n_cb, rows, kc = compacts[0].shape
    width = kc * gl
    tr = _pick_tile(rows, 512, BF16_ROWS)
    col = np.arange(width)
    spread = np.arange(kc)[:, None] == (col // (gl * col_inner)) * col_inner + col % col_inner
    row_group = ((np.arange(rows) // row_inner) % gl).astype(np.int32)[:, None]
    col_group = ((col // col_inner) % gl).astype(np.int32)[None, :]
    n_parts = len(compacts)
    blocks = (n_parts * _nbytes((tr, LANES), F32) + _nbytes((kc, width), BF16) + _nbytes((tr, LANES), jnp.int32)
              + _nbytes((8, width), jnp.int32) + _nbytes((tr, n_parts * width), BF16))
    return pl.pallas_call(
        functools.partial(_block_diag_kernel, n_parts=n_parts),
        out_shape=jax.ShapeDtypeStruct((n_cb, rows, n_parts * width), BF16),
        grid=(n_cb, rows // tr),
        in_specs=[pl.BlockSpec((None, tr, kc), lambda c, i: (c, i, 0))] * n_parts
                 + [pl.BlockSpec((kc, width), lambda c, i: (0, 0)),
                    pl.BlockSpec((tr, 1), lambda c, i: (i, 0)),
                    pl.BlockSpec((1, width), lambda c, i: (0, 0))],
        out_specs=pl.BlockSpec((None, tr, n_parts * width), lambda c, i: (c, i, 0)),
        compiler_params=_compiler_params(("parallel", "parallel"), blocks),
        name=name,
    )(*compacts, jnp.asarray(spread, BF16), jnp.asarray(row_group), jnp.asarray(col_group))


def _s5_weights(prep, n_cb, gl, h, p):
    e_re, e_im, ca_re, ca_im, klag, ap_re, ap_im = prep
    t = T_BLK

    def rows_by_step(x, reverse):
        x = x[::-1] if reverse else x
        x = jnp.transpose(x.reshape(t, n_cb, gl, h, p), (1, 0, 2, 3, 4))
        return x.reshape(n_cb, t * gl * h, p)

    w_s = _block_diag([rows_by_step(e_re[0], True), rows_by_step(e_im[0], True),
                       rows_by_step(e_re[1], False), rows_by_step(e_im[1], False)], gl, h, p, "s5_w_states")

    w_cre = _block_diag([rows_by_step(ca_re[0], False), rows_by_step(ca_re[1], True)], gl, h, p, "s5_w_carry_re")
    w_cim = _block_diag([rows_by_step(ca_im[0], False), rows_by_step(ca_im[1], True)], gl, h, p, "s5_w_carry_im")

    kst = (klag[0] + klag[1]).reshape(n_cb, gl, t, h, t * h)
    kst = jnp.transpose(kst, (0, 2, 1, 3, 4))
    w_i = _block_diag([kst.reshape(n_cb, t * gl * h, t * h)], gl, h, h, "s5_w_intra")

    def lanes(ap):
        ap = jnp.stack([ap[0], ap[1, ::-1]], axis=0)
        ap = ap.reshape(2, 8, n_cb, gl * p)
        return jnp.transpose(ap, (2, 0, 1, 3))
    apw = jnp.stack([lanes(ap_re), lanes(ap_im)], axis=2)

    return w_s, w_cre, w_cim, w_i, apw


def _fold_time_blocks(u_ref, lhs_ref, r):
    pieces = u_ref.shape[0]
    for tau in range(T_BLK):
        for h in range(pieces):
            col = (tau * pieces + h) * LANES
            lhs_ref[:, col:col + LANES] = u_ref[h, pl.ds(tau, r, stride=T_BLK), :].astype(lhs_ref.dtype)


def _s5_states_kernel(u_ref, w_ref, s_ref, lhs_ref, *, r):
    @pl.when(pl.program_id(2) == 0)
    def _():
        _fold_time_blocks(u_ref, lhs_ref, r)

    s_ref[...] = _dot(lhs_ref[...], w_ref[...])


def _s5_states(u, w_s, r, tn):
    n_cb, kw, ns = w_s.shape
    n = u.shape[1]
    nb = n // T_BLK
    ppc = u.shape[0] // n_cb
    blocks = _nbytes((ppc, T_BLK * r, LANES), F32) + _nbytes((kw, tn), BF16) + _nbytes((r, tn), F32)
    scratch = _nbytes((r, kw), BF16)
    return pl.pallas_call(
        functools.partial(_s5_states_kernel, r=r),
        out_shape=jax.ShapeDtypeStruct((n_cb, nb, ns), F32),
        grid=(n_cb, nb // r, ns // tn),
        in_specs=[pl.BlockSpec((ppc, T_BLK * r, LANES), lambda c, i, j: (c, i, 0)),
                  pl.BlockSpec((None, kw, tn), lambda c, i, j: (c, 0, j))],
        out_specs=pl.BlockSpec((None, r, tn), lambda c, i, j: (c, i, j)),
        scratch_shapes=[pltpu.VMEM((r, kw), BF16)],
        compiler_params=_compiler_params(("parallel", "parallel", "arbitrary"), blocks, scratch),
        name="s5_states",
    )(u, w_s)


def _s5_scan_kernel(sre_ref, sim_ref, apre_ref, apim_ref, xre_ref, xim_ref, *, n_pairs):
    d = pl.program_id(1)
    lw = sre_ref.shape[-1]
    ap_re, ap_im = apre_ref[...], apim_ref[...]
    row = lax.broadcasted_iota(jnp.int32, (8, lw), 0)

    def run(reverse):
        steps = []
        for shift in (1, 2, 4):
            src = (8 - shift) if reverse else (shift - 1)
            keep = (row < 8 - shift) if reverse else (row >= shift)
            steps.append((8 - shift if reverse else shift,
                          jnp.where(keep, ap_re[src:src + 1, :], 0.0),
                          jnp.where(keep, ap_im[src:src + 1, :], 0.0)))
        edge = 7 if reverse else 0
        out_row = 0 if reverse else 7
        nb_shift = 7 if reverse else 1

        def tile(r0, c_re, c_im):
            x_re = sre_ref[pl.ds(r0, 8), :]
            x_im = sim_ref[pl.ds(r0, 8), :]
            for amount, m_re, m_im in steps:
                s_re = pltpu.roll(x_re, amount, 0)
                s_im = pltpu.roll(x_im, amount, 0)
                x_re, x_im = x_re + m_re * s_re - m_im * s_im, x_im + m_re * s_im + m_im * s_re
            inc_re = x_re + ap_re * c_re - ap_im * c_im
            inc_im = x_im + ap_re * c_im + ap_im * c_re
            e_re = jnp.where(row == edge, c_re, pltpu.roll(inc_re, nb_shift, 0))
            e_im = jnp.where(row == edge, c_im, pltpu.roll(inc_im, nb_shift, 0))
            return e_re, e_im, inc_re[out_row:out_row + 1, :], inc_im[out_row:out_row + 1, :]

        def body(k, carry):
            c_re, c_im = carry
            pair = (n_pairs - 1 - k) if reverse else k
            r0 = pl.multiple_of(pair * BF16_ROWS, BF16_ROWS)
            e_re, e_im = [None, None], [None, None]
            for half in ((1, 0) if reverse else (0, 1)):
                e_re[half], e_im[half], c_re, c_im = tile(pl.multiple_of(r0 + 8 * half, 8), c_re, c_im)
            xre_ref[pl.ds(r0, BF16_ROWS), :] = jnp.concatenate(e_re, axis=0).astype(xre_ref.dtype)
            xim_ref[pl.ds(r0, BF16_ROWS), :] = jnp.concatenate(e_im, axis=0).astype(xim_ref.dtype)
            return c_re, c_im

        zero = jnp.zeros((1, lw), F32)
        lax.fori_loop(0, n_pairs, body, (zero, zero))

    @pl.when(d == 0)
    def _():
        run(False)

    @pl.when(d == 1)
    def _():
        run(True)


def _s5_scan(s, apw, lw):
    n_cb, nb, ns = s.shape
    glp = ns // 4
    nq = glp // lw
    blocks = 2 * _nbytes((nb, lw), F32) + 2 * _nbytes((nb, lw), BF16) + 2 * _nbytes((8, lw), F32)
    out = jax.ShapeDtypeStruct((n_cb, nb, 2 * glp), BF16)

    def s_spec(ri):
        return pl.BlockSpec((None, nb, lw), lambda c, d, q: (c, 0, (2 * d + ri) * nq + q))

    def ap_spec(ri):
        return pl.BlockSpec((None, None, None, 8, lw), lambda c, d, q: (c, d, ri, 0, q))

    x_spec = pl.BlockSpec((None, nb, lw), lambda c, d, q: (c, 0, d * nq + q))
    return pl.pallas_call(
        functools.partial(_s5_scan_kernel, n_pairs=nb // BF16_ROWS),
        out_shape=(out, out),
        grid=(n_cb, 2, nq),
        in_specs=[s_spec(0), s_spec(1), ap_spec(0), ap_spec(1)],
        out_specs=(x_spec, x_spec),
        compiler_params=_compiler_params(("parallel", "parallel", "parallel"), blocks),
        name="s5_scan",
    )(s, s, apw, apw)


def _s5_out_kernel(u_ref, xre_ref, xim_ref, wi_ref, wcre_ref, wcim_ref, y_ref, lhs_ref, *, r, tn):
    j = pl.program_id(2)
    pieces = u_ref.shape[0]

    @pl.when(j == 0)
    def _():
        _fold_time_blocks(u_ref, lhs_ref, r)

    y = _dot(lhs_ref[...], wi_ref[...])
    nt = (((1,), (1,)), ((), ()))
    y = y + lax.dot_general(xre_ref[...], wcre_ref[...], nt, preferred_element_type=F32)
    y = y + lax.dot_general(xim_ref[...], wcim_ref[...], nt, preferred_element_type=F32)

    per = tn // (pieces * LANES)
    for jj in range(T_BLK // per):
        @pl.when(j == jj)
        def _():
            for k in range(per):
                for h in range(pieces):
                    col = (k * pieces + h) * LANES
                    y_ref[h, pl.ds(jj * per + k, r, stride=T_BLK), :] = y[:, col:col + LANES]


def _s5_out(u, x_re, x_im, w_i, w_cre, w_cim, r, tn):
    n_cb, kw, _ = w_i.shape
    n = u.shape[1]
    nb = n // T_BLK
    ppc = u.shape[0] // n_cb
    kx = x_re.shape[-1]
    assert tn % (ppc * LANES) == 0 and kw % tn == 0
    blocks = (2 * _nbytes((ppc, T_BLK * r, LANES), F32) + 2 * _nbytes((r, kx), BF16) + _nbytes((kw, tn), BF16)
              + 2 * _nbytes((kx, tn), BF16) + _nbytes((r, tn), F32))
    scratch = _nbytes((r, kw), BF16)
    return pl.pallas_call(
        functools.partial(_s5_out_kernel, r=r, tn=tn),
        out_shape=jax.ShapeDtypeStruct(u.shape, F32),
        grid=(n_cb, nb // r, kw // tn),
        in_specs=[pl.BlockSpec((ppc, T_BLK * r, LANES), lambda c, i, j: (c, i, 0)),
                  pl.BlockSpec((None, r, kx), lambda c, i, j: (c, i, 0)),
                  pl.BlockSpec((None, r, kx), lambda c, i, j: (c, i, 0)),
                  pl.BlockSpec((None, kw, tn), lambda c, i, j: (c, 0, j)),
                  pl.BlockSpec((None, tn, kx), lambda c, i, j: (c, j, 0)),
                  pl.BlockSpec((None, tn, kx), lambda c, i, j: (c, j, 0))],
        out_specs=pl.BlockSpec((ppc, T_BLK * r, LANES), lambda c, i, j: (c, i, 0)),
        scratch_shapes=[pltpu.VMEM((r, kw), BF16)],
        compiler_params=_compiler_params(("parallel", "parallel", "arbitrary"), blocks, scratch),
        name="s5_out",
    )(u, x_re, x_im, w_i, w_cre, w_cim)


def _glu_merge_kernel(y_ref, u_ref, d_ref, w1_ref, w2_ref, pa_ref, gb_ref, m_ref, s_ref):
    @pl.when(pl.program_id(1) == 0)
    def _():
        for c in range(y_ref.shape[0]):
            s = jax.nn.gelu(y_ref[c] + d_ref[c] * u_ref[c])
            s_ref[:, c * LANES:(c + 1) * LANES] = s.astype(s_ref.dtype)

    a = s_ref[...]
    y_b = _dot(a, w1_ref[...]) * _sigmoid(_dot(a, w2_ref[...]))
    m_ref[...] = (pa_ref[...].astype(F32) + gb_ref[...].astype(F32) * y_b).astype(m_ref.dtype)


def _glu_merge(y, u, d_skip, w_glu, layer, pa, gates, d_model, tm, tn):
    n_pc, n, _ = y.shape
    k = n_pc * LANES
    nb = d_model // tn
    blocks = (2 * _nbytes((n_pc, tm, LANES), F32) + 2 * _nbytes((k, tn), BF16) + 3 * _nbytes((tm, tn), BF16))
    scratch = _nbytes((tm, k), BF16)
    return pl.pallas_call(
        _glu_merge_kernel,
        out_shape=jax.ShapeDtypeStruct((n, d_model), BF16),
        grid=(n // tm, nb),
        in_specs=[pl.BlockSpec((n_pc, tm, LANES), lambda i, j: (0, i, 0)),
                  pl.BlockSpec((n_pc, tm, LANES), lambda i, j: (0, i, 0)),
                  pl.BlockSpec((n_pc, 1, LANES), lambda i, j: (0, 0, 0)),
                  _w_spec(k, tn, layer, 0),
                  _w_spec(k, tn, layer, nb),
                  pl.BlockSpec((tm, tn), lambda i, j: (i, j)),
                  pl.BlockSpec((tm, tn), lambda i, j: (i, nb + j))],
        out_specs=pl.BlockSpec((tm, tn), lambda i, j: (i, j)),
        scratch_shapes=[pltpu.VMEM((tm, k), BF16)],
        compiler_params=_compiler_params(("parallel", "arbitrary"), blocks, scratch),
        name="glu_merge",
    )(y, u, d_skip, w_glu, w_glu, pa, gates)


def _ffn_up_kernel(h_ref, hp_ref, hn_ref, wg_ref, wv_ref, cg_ref, cv_ref, act_ref, lhs_ref, *, tm):
    i = pl.program_id(0)
    j = pl.program_id(1)
    last = pl.num_programs(0) - 1
    halo = BF16_ROWS

    @pl.when(j == 0)
    def _():
        row = lax.broadcasted_iota(jnp.int32, hp_ref.shape, 0)
        after = jnp.where(jnp.logical_and(row == 0, i < last), hn_ref[...].astype(F32), 0.0)
        edge = jnp.where(jnp.logical_and(row == halo - 1, i > 0), hp_ref[...].astype(F32), after)
        lhs_ref[0:tm, :] = h_ref[...]
        lhs_ref[tm:, :] = edge.astype(lhs_ref.dtype)

    a = lhs_ref[...]
    rows = tm + halo

    def conv(w_ref, cw_ref):
        u = _dot(a, w_ref[...])
        cw = cw_ref[...]
        up = pltpu.roll(u, 1, 0)[0:tm]
        dn = pltpu.roll(u, rows - 1, 0)[0:tm]
        return up * cw[0:1, :] + u[0:tm] * cw[1:2, :] + dn * cw[2:3, :]

    act_ref[...] = (jax.nn.gelu(conv(wg_ref, cg_ref)) * conv(wv_ref, cv_ref)).astype(act_ref.dtype)


def _ffn_up(hn, w_up, layer, conv_w, d_ff, tm, tn, cast=None):
    n, k = hn.shape
    nb = d_ff // tn
    hb = tm // BF16_ROWS
    n_hb = n // BF16_ROWS
    blocks = (_nbytes((tm + 2 * BF16_ROWS, k), BF16) + 2 * _nbytes((k, tn), BF16) + 2 * _nbytes((3, tn), F32)
              + _nbytes((tm, tn), BF16))
    scratch = _nbytes((tm + BF16_ROWS, k), BF16)
    outs = _call_with_cast(
        functools.partial(_ffn_up_kernel, tm=tm), cast,
        grid=(n // tm, nb),
        in_specs=[pl.BlockSpec((tm, k), lambda i, j: (i, 0)),
                  pl.BlockSpec((BF16_ROWS, k), lambda i, j: (jnp.maximum(i * hb - 1, 0), 0)),
                  pl.BlockSpec((BF16_ROWS, k), lambda i, j: (jnp.minimum((i + 1) * hb, n_hb - 1), 0)),
                  _w_spec(k, tn, layer, 0),
                  _w_spec(k, tn, layer, nb),
                  pl.BlockSpec((3, tn), lambda i, j: (0, j)),
                  pl.BlockSpec((3, tn), lambda i, j: (0, nb + j))],
        out_specs=[pl.BlockSpec((tm, tn), lambda i, j: (i, j))],
        out_shape=[jax.ShapeDtypeStruct((n, d_ff), BF16)],
        operands=(hn, hn, hn, w_up, w_up, conv_w, conv_w),
        scratch_shapes=[pltpu.VMEM((tm + BF16_ROWS, k), BF16)],
        block_bytes=blocks, scratch_bytes=scratch, name="ffn_up")
    return outs[0] if cast is None else outs


def kernel(x, meta_tokens, norm_mix, w_in, conv_a_w, w_a, ssm_lambda_re, ssm_lambda_im, ssm_log_step,
           ssm_b_re, ssm_b_im, ssm_c_re, ssm_c_im, ssm_d, w_glu, w_out, norm_ffn, w_up, conv_ffn_w,
           w_down, norm_final):
    bsz, seq, d_model = x.shape
    n_meta = meta_tokens.shape[0]
    depth = w_in.shape[0]
    d_conv = w_a.shape[1]
    d_ssm = ssm_d.shape[1]
    d_ff = w_down.shape[1]
    n_groups, p_state, h_grp = ssm_b_re.shape[2:]
    assert w_in.shape[2] == 3 * d_conv + d_ssm + 2 * d_model and n_groups * h_grp == d_ssm

    cbw = min(S5_CB_WIDTH, d_ssm)
    n_cb = d_ssm // cbw
    gl = cbw // h_grp
    assert n_cb * cbw == d_ssm and gl * h_grp == cbw

    n_true = n_meta + seq
    row_align = T_BLK * BF16_ROWS
    n_pad = _round_up(n_true, PAD_ROWS if n_true >= 4 * PAD_ROWS else row_align)
    assert n_pad % row_align == 0
    nb = n_pad // T_BLK

    tm = _pick_tile(n_pad, TM_TARGET, BF16_ROWS)
    tm_up = _pick_tile(n_pad, TM_FFN_UP_TARGET, BF16_ROWS)
    tm_half = _pick_tile(n_pad, TM_HALF_TARGET, BF16_ROWS)
    tm_ew = _pick_tile(n_pad, TM_EW_TARGET, BF16_ROWS)
    r_s5 = _pick_tile(nb, R_S5_TARGET, BF16_ROWS)
    tn = lambda n, target=512: _pick_tile(n, target, 128)

    prep = _s5_prep(ssm_lambda_re, ssm_lambda_im, ssm_log_step, ssm_b_re, ssm_b_im, ssm_c_re, ssm_c_im)
    d_skip = ssm_d.astype(F32).reshape(depth, d_ssm // LANES, 1, LANES)

    residual = functools.partial(_epi_residual, n_valid=n_true)
    tile_spec = lambda t, rows=tm: pl.BlockSpec((rows, t), lambda i, j: (i, j))
    u_off = 3 * d_conv
    gate_off = u_off + d_ssm

    w_a_b, w_glu_b, w_out_b = (w.astype(BF16) for w in (w_a, w_glu, w_out))
    w_in_first = w_in[0].astype(BF16)

    outs = []
    for b in range(bsz):
        w_in_l = w_in_first
        for l in range(depth):
            w_s, w_cre, w_cim, w_i, apw = _s5_weights([t[l] for t in prep], n_cb, gl, h_grp, p_state)

            if l == 0:
                h_res, hn = _assemble_norm(meta_tokens, x[b], norm_mix[0], n_pad, tm_ew)
            else:
                hn = _rmsnorm(h_res, norm_mix[l], BF16, tm_ew)
            bg, cv, u, w_down_l = _inproj_conv(hn, w_in_l, None, d_conv, d_ssm, tm_half, tn(d_conv),
                                               cast=(w_down, l))
            t_g = tn(2 * d_model)
            gates, w_up_l = _mm(hn, w_in_l, None, col0=gate_off, n_cols=2 * d_model, tm=tm, tn=t_g,
                                epilogue=_epi_sigmoid,
                                out_shape=jax.ShapeDtypeStruct((n_pad, 2 * d_model), BF16),
                                out_spec=tile_spec(t_g), name="inproj_gates", cast=(w_up, l))

            pa = _branch_a(bg, cv, conv_a_w[l].astype(F32), w_a_b, l, gates, d_model, tm, tn(d_model, 1024))
            t_d = tn(d_model)

            states = _s5_states(u, w_s, _pick_tile(nb, R_S5_STATES_TARGET, BF16_ROWS), tn(w_s.shape[-1]))
            x_re, x_im = _s5_scan(states, apw, tn(gl * p_state, SCAN_LANES_TARGET))
            y = _s5_out(u, x_re, x_im, w_i, w_cre, w_cim, r_s5, tn(T_BLK * cbw))

            merged = _glu_merge(y, u, d_skip[l], w_glu_b, l, pa, gates, d_model, tm_half, tn(d_model, 1024))
            h_res = _mm(merged, w_out_b, l, col0=0, n_cols=d_model, tm=tm, tn=t_d, epilogue=residual,
                        extras=(h_res,), extra_specs=(tile_spec(t_d),),
                        out_shape=jax.ShapeDtypeStruct((n_pad, d_model), F32), out_spec=tile_spec(t_d),
                        name="out_proj")

            hn = _rmsnorm(h_res, norm_ffn[l], BF16, tm_ew)
            if l + 1 < depth:
                act, w_in_l = _ffn_up(hn, w_up_l, None, conv_ffn_w[l].astype(F32), d_ff, tm_up, tn(d_ff),
                                      cast=(w_in, l + 1))
            else:
                act = _ffn_up(hn, w_up_l, None, conv_ffn_w[l].astype(F32), d_ff, tm_up, tn(d_ff))
            t_o = tn(d_model, 256)
            h_res = _mm(act, w_down_l, None, col0=0, n_cols=d_model, tm=tm_half, tn=t_o, epilogue=residual,
                        extras=(h_res,), extra_specs=(tile_spec(t_o, tm_half),),
                        out_shape=jax.ShapeDtypeStruct((n_pad, d_model), F32), out_spec=tile_spec(t_o, tm_half),
                        name="ffn_down")

        outs.append(_final_norm(h_res, norm_final, n_meta, seq, x.dtype, _pick_tile(seq, TM_EW_TARGET, n_meta)))
    return jnp.stack(outs, axis=0)
```

```python
import functools

import numpy as np
import jax
import jax.numpy as jnp
from jax import lax
from jax.experimental import pallas as pl
from jax.experimental.pallas import tpu as pltpu

F32 = jnp.float32
BF16 = jnp.bfloat16
EPS = 1e-6

T_BLK = 16
S5_CB_WIDTH = 128
PAD_ROWS = 1280
BF16_ROWS = 16
LANES = 128
VMEM_CAP_BYTES = 60000 * 1024
VMEM_TEMP_BYTES = 20 * 1024 * 1024

TM_TARGET = 1280
TM_FFN_UP_TARGET = 832
TM_HALF_TARGET = 640
R_S5_STATES_TARGET = 1040
TM_EW_TARGET = 320
R_S5_TARGET = 1040
SCAN_LANES_TARGET = 256


def _round_up(n, m):
    return (n + m - 1) // m * m


def _pick_tile(n, target, align):
    best = 0
    for t in range(align, min(n, target) + 1, align):
        if n % t == 0:
            best = t
    assert best > 0, (n, target, align)
    return best


def _nbytes(shape, dtype):
    return int(np.prod(shape)) * jnp.dtype(dtype).itemsize


def _compiler_params(semantics, block_bytes, scratch_bytes=0):
    est = 2 * block_bytes + scratch_bytes + VMEM_TEMP_BYTES
    return pltpu.CompilerParams(dimension_semantics=semantics,
                                vmem_limit_bytes=int(min(VMEM_CAP_BYTES, est)))


def _dot(a, b):
    return jnp.dot(a, b, preferred_element_type=F32)


def _sigmoid(x):
    return 0.5 * (1.0 + jnp.tanh(0.5 * x))


def _rmsnorm_kernel(x_ref, g_ref, o_ref):
    x = x_ref[...]
    ms = jnp.mean(x * x, axis=-1, keepdims=True)
    o_ref[...] = (x * lax.rsqrt(ms + EPS) * g_ref[...]).astype(o_ref.dtype)


def _rmsnorm(x, g, out_dtype, tm):
    n, d = x.shape
    blocks = _nbytes((tm, d), F32) + _nbytes((tm, d), out_dtype)
    return pl.pallas_call(
        _rmsnorm_kernel,
        out_shape=jax.ShapeDtypeStruct((n, d), out_dtype),
        grid=(n // tm,),
        in_specs=[pl.BlockSpec((tm, d), lambda i: (i, 0)),
                  pl.BlockSpec((1, d), lambda i: (0, 0))],
        out_specs=pl.BlockSpec((tm, d), lambda i: (i, 0)),
        compiler_params=_compiler_params(("parallel",), blocks),
        name="rmsnorm",
    )(x, g.reshape(1, d).astype(F32))


def _assemble_norm_kernel(meta_ref, prev_ref, x_ref, g_ref, h_ref, hn_ref, *, n_valid):
    i = pl.program_id(0)
    tm = h_ref.shape[0]
    n_meta = meta_ref.shape[0]
    head = jnp.where(i == 0, meta_ref[...], prev_ref[...])
    tile = jnp.concatenate([head, x_ref[0:tm - n_meta, :]], axis=0)
    rows = i * tm + lax.broadcasted_iota(jnp.int32, tile.shape, 0)
    tile = jnp.where(rows < n_valid, tile, 0.0)
    h_ref[...] = tile
    ms = jnp.mean(tile * tile, axis=-1, keepdims=True)
    hn_ref[...] = (tile * lax.rsqrt(ms + EPS) * g_ref[...]).astype(hn_ref.dtype)


def _assemble_norm(meta, x, g, n_pad, tm):
    n_meta, d = meta.shape
    seq = x.shape[0]
    assert n_meta % 8 == 0 and tm % n_meta == 0 and seq % n_meta == 0 and n_pad % tm == 0
    per = tm // n_meta
    last_x = pl.cdiv(seq, tm) - 1
    last_prev = seq // n_meta - 1
    blocks = _nbytes((tm + 2 * n_meta, d), F32) + _nbytes((tm, d), F32) + _nbytes((tm, d), BF16)
    return pl.pallas_call(
        functools.partial(_assemble_norm_kernel, n_valid=n_meta + seq),
        out_shape=(jax.ShapeDtypeStruct((n_pad, d), F32), jax.ShapeDtypeStruct((n_pad, d), BF16)),
        grid=(n_pad // tm,),
        in_specs=[pl.BlockSpec((n_meta, d), lambda i: (0, 0)),
                  pl.BlockSpec((n_meta, d), lambda i: (jnp.clip(i * per - 1, 0, last_prev), 0)),
                  pl.BlockSpec((tm, d), lambda i: (jnp.minimum(i, last_x), 0)),
                  pl.BlockSpec((1, d), lambda i: (0, 0))],
        out_specs=(pl.BlockSpec((tm, d), lambda i: (i, 0)), pl.BlockSpec((tm, d), lambda i: (i, 0))),
        compiler_params=_compiler_params(("parallel",), blocks),
        name="assemble_norm",
    )(meta.astype(F32), x.astype(F32), x.astype(F32), g.reshape(1, d).astype(F32))


def _final_norm_kernel(a_ref, b_ref, g_ref, o_ref, *, skip):
    x = jnp.concatenate([a_ref[skip:, :], b_ref[...]], axis=0)
    ms = jnp.mean(x * x, axis=-1, keepdims=True)
    o_ref[...] = (x * lax.rsqrt(ms + EPS) * g_ref[...]).astype(o_ref.dtype)


def _final_norm(x, g, skip, n_out, out_dtype, tm):
    n, d = x.shape
    assert skip % 8 == 0 and tm % skip == 0 and n_out % tm == 0 and n >= n_out + skip
    per = tm // skip
    blocks = _nbytes((tm + skip, d), F32) + _nbytes((tm, d), out_dtype)
    return pl.pallas_call(
        functools.partial(_final_norm_kernel, skip=skip),
        out_shape=jax.ShapeDtypeStruct((n_out, d), out_dtype),
        grid=(n_out // tm,),
        in_specs=[pl.BlockSpec((tm, d), lambda i: (i, 0)),
                  pl.BlockSpec((skip, d), lambda i: ((i + 1) * per, 0)),
                  pl.BlockSpec((1, d), lambda i: (0, 0))],
        out_specs=pl.BlockSpec((tm, d), lambda i: (i, 0)),
        compiler_params=_compiler_params(("parallel",), blocks),
        name="final_norm",
    )(x, x, g.reshape(1, d).astype(F32))


def _mm_kernel(*refs, n_extra, epilogue, tm):
    a_ref, w_ref = refs[0], refs[1]
    extras = refs[2:2 + n_extra]
    o_ref = refs[2 + n_extra]
    acc = _dot(a_ref[...], w_ref[...])
    row0 = pl.program_id(0) * tm
    o_ref[...] = epilogue(acc, row0, *[e[...] for e in extras]).astype(o_ref.dtype)


def _w_spec(k, tn, layer, jb0):
    if layer is None:
        return pl.BlockSpec((k, tn), lambda i, j: (0, jb0 + j))
    return pl.BlockSpec((None, k, tn), lambda i, j: (layer, 0, jb0 + j))


def _call_with_cast(kernel_fn, cast, *, grid, in_specs, out_specs, out_shape, operands, scratch_shapes=(),
                    block_bytes, scratch_bytes=0, name):
    if cast is None:
        return pl.pallas_call(
            kernel_fn, out_shape=tuple(out_shape), grid=grid, in_specs=list(in_specs), out_specs=tuple(out_specs),
            scratch_shapes=list(scratch_shapes),
            compiler_params=_compiler_params(("parallel", "arbitrary"), block_bytes, scratch_bytes), name=name,
        )(*operands)

    stack, layer = cast
    _, k, n = stack.shape
    n_steps = grid[0] * grid[1]
    options = []
    for rows in range(BF16_ROWS, k + 1, BF16_ROWS):
        if k % rows == 0 and k // rows <= n_steps:
            splits = max(c for c in range(1, n_steps // (k // rows) + 1) if n % (c * LANES) == 0)
            options.append((rows * (n // splits), rows, splits))
    _, rows, splits = min(options)
    cols = n // splits
    last_chunk = (k // rows) * splits - 1
    n_in, n_out = len(in_specs), len(out_specs)

    def chunk(i, j):
        q = jnp.minimum(i * grid[1] + j, last_chunk)
        return q // splits, q % splits

    def body(*refs):
        src_ref, dst_ref = refs[n_in], refs[n_in + 1 + n_out]
        dst_ref[...] = src_ref[...].astype(dst_ref.dtype)
        kernel_fn(*refs[:n_in], *refs[n_in + 1:n_in + 1 + n_out], *refs[n_in + 2 + n_out:])

    cast_bytes = _nbytes((rows, cols), F32) + _nbytes((rows, cols), BF16)
    return pl.pallas_call(
        body,
        out_shape=(*out_shape, jax.ShapeDtypeStruct((k, n), BF16)),
        grid=grid,
        in_specs=[*in_specs, pl.BlockSpec((None, rows, cols), lambda i, j: (layer, *chunk(i, j)))],
        out_specs=(*out_specs, pl.BlockSpec((rows, cols), lambda i, j: chunk(i, j))),
        scratch_shapes=list(scratch_shapes),
        compiler_params=_compiler_params(("arbitrary", "arbitrary"), block_bytes + cast_bytes, scratch_bytes),
        name=name,
    )(*operands, stack)


def _mm(a, w, layer, *, col0, n_cols, tm, tn, epilogue, extras=(), extra_specs=(), out_shape, out_spec, name,
        cast=None):
    n, k = a.shape
    assert n % tm == 0 and n_cols % tn == 0 and col0 % tn == 0
    blocks = (_nbytes((tm, k), a.dtype) + _nbytes((k, tn), w.dtype) + _nbytes((tm, tn), F32)
              + sum(_nbytes((tm, tn), e.dtype) for e in extras))
    outs = _call_with_cast(
        functools.partial(_mm_kernel, n_extra=len(extras), epilogue=epilogue, tm=tm), cast,
        grid=(n // tm, n_cols // tn),
        in_specs=[pl.BlockSpec((tm, k), lambda i, j: (i, 0)),
                  _w_spec(k, tn, layer, col0 // tn),
                  *extra_specs],
        out_specs=[out_spec], out_shape=[out_shape], operands=(a, w, *extras), block_bytes=blocks, name=name)
    return outs[0] if cast is None else outs


def _epi_pieces(acc, row0):
    return jnp.stack([acc[:, h * LANES:(h + 1) * LANES] for h in range(acc.shape[1] // LANES)], axis=0)


def _epi_sigmoid(acc, row0):
    return _sigmoid(acc)


def _epi_residual(acc, row0, res, *, n_valid):
    rows = row0 + lax.broadcasted_iota(jnp.int32, acc.shape, 0)
    return jnp.where(rows < n_valid, res + acc, 0.0)


def _inproj_conv_kernel(a_ref, wb_ref, wc_ref, wv_ref, wu_ref, bg_ref, cv_ref, u_ref):
    a = a_ref[...]
    bg_ref[...] = _dot(a, wb_ref[...]).astype(bg_ref.dtype)
    cv_ref[...] = (_dot(a, wc_ref[...]) * _dot(a, wv_ref[...])).astype(cv_ref.dtype)
    u_ref[...] = _epi_pieces(_dot(a, wu_ref[...]), 0)


def _inproj_conv(hn, w_in, layer, d_conv, d_ssm, tm, tn, cast=None):
    n, k = hn.shape
    nb = d_conv // tn
    tn_u = d_ssm // nb
    assert tn_u % LANES == 0 and (3 * d_conv) % tn_u == 0
    blocks = (_nbytes((tm, k), BF16) + 3 * _nbytes((k, tn), BF16) + _nbytes((k, tn_u), BF16)
              + 2 * _nbytes((tm, tn), BF16) + _nbytes((tm, tn_u), F32))
    out = jax.ShapeDtypeStruct((n, d_conv), BF16)
    return _call_with_cast(
        _inproj_conv_kernel, cast,
        grid=(n // tm, nb),
        in_specs=[pl.BlockSpec((tm, k), lambda i, j: (i, 0)),
                  _w_spec(k, tn, layer, 0),
                  _w_spec(k, tn, layer, nb),
                  _w_spec(k, tn, layer, 2 * nb),
                  _w_spec(k, tn_u, layer, 3 * d_conv // tn_u)],
        out_specs=[pl.BlockSpec((tm, tn), lambda i, j: (i, j)),
                   pl.BlockSpec((tm, tn), lambda i, j: (i, j)),
                   pl.BlockSpec((tn_u // LANES, tm, LANES), lambda i, j: (j, i, 0))],
        out_shape=[out, out, jax.ShapeDtypeStruct((d_ssm // LANES, n, LANES), F32)],
        operands=(hn, w_in, w_in, w_in, w_in), block_bytes=blocks, name="inproj_conv")


def _shifted_rows(x, prev_row, next_row):
    t = x.shape[0]
    row = lax.broadcasted_iota(jnp.int32, x.shape, 0)
    up = jnp.where(row == 0, prev_row, pltpu.roll(x, 1, 0))
    dn = jnp.where(row == t - 1, next_row, pltpu.roll(x, t - 1, 0))
    return up, dn


def _branch_a_kernel(bg_ref, cv_ref, cvp_ref, cvn_ref, cw_ref, w_ref, gate_ref, o_ref, z_ref, *, chunk):
    i = pl.program_id(0)
    last = pl.num_programs(0) - 1

    @pl.when(pl.program_id(1) == 0)
    def _():
        for c0 in range(0, z_ref.shape[1], chunk):
            cols = slice(c0, c0 + chunk)
            x = cv_ref[:, cols].astype(F32)
            prev = cvp_ref[:, cols].astype(F32)[BF16_ROWS - 1:BF16_ROWS, :]
            nxt = cvn_ref[:, cols].astype(F32)[0:1, :]
            prev = jnp.where(i > 0, prev, 0.0)
            nxt = jnp.where(i < last, nxt, 0.0)
            up, dn = _shifted_rows(x, prev, nxt)
            w = cw_ref[:, cols]
            conv = up * w[0:1, :] + x * w[1:2, :] + dn * w[2:3, :]
            z_ref[:, cols] = (bg_ref[:, cols].astype(F32) * conv).astype(z_ref.dtype)

    o_ref[...] = (_dot(z_ref[...], w_ref[...]) * gate_ref[...].astype(F32)).astype(o_ref.dtype)


def _branch_a(bg, cv, conv_w, w_a, layer, gates, d_model, tm, tn):
    n, c = bg.shape
    hb = tm // BF16_ROWS
    n_hb = n // BF16_ROWS
    blocks = (2 * _nbytes((tm, c), BF16) + 2 * _nbytes((BF16_ROWS, c), BF16) + _nbytes((8, c), F32)
              + _nbytes((c, tn), BF16) + 2 * _nbytes((tm, tn), BF16))
    scratch = _nbytes((tm, c), BF16)
    return pl.pallas_call(
        functools.partial(_branch_a_kernel, chunk=_pick_tile(c, 512, LANES)),
        out_shape=jax.ShapeDtypeStruct((n, d_model), BF16),
        grid=(n // tm, d_model // tn),
        in_specs=[pl.BlockSpec((tm, c), lambda i, j: (i, 0)),
                  pl.BlockSpec((tm, c), lambda i, j: (i, 0)),
                  pl.BlockSpec((BF16_ROWS, c), lambda i, j: (jnp.maximum(i * hb - 1, 0), 0)),
                  pl.BlockSpec((BF16_ROWS, c), lambda i, j: (jnp.minimum((i + 1) * hb, n_hb - 1), 0)),
                  pl.BlockSpec((3, c), lambda i, j: (0, 0)),
                  _w_spec(c, tn, layer, 0),
                  pl.BlockSpec((tm, tn), lambda i, j: (i, j))],
        out_specs=pl.BlockSpec((tm, tn), lambda i, j: (i, j)),
        scratch_shapes=[pltpu.VMEM((tm, c), BF16)],
        compiler_params=_compiler_params(("parallel", "arbitrary"), blocks, scratch),
        name="branch_a",
    )(bg, cv, cv, cv, conv_w, w_a, gates)


def _cmul(ar, ai, br, bi):
    return ar * br - ai * bi, ar * bi + ai * br


def _s5_prep_kernel(lre_ref, lim_ref, lst_ref, bre_ref, bim_ref, cre_ref, cim_ref,
                    ere_ref, eim_ref, care_ref, caim_ref, klag_ref, apre_ref, apim_ref):
    lam_re, lam_im = lre_ref[...], lim_ref[...]
    dt = jnp.exp(lst_ref[...])
    mag = jnp.exp(lam_re * dt)
    a_re = mag * jnp.cos(lam_im * dt)
    a_im = mag * jnp.sin(lam_im * dt)
    nr, ni = a_re - 1.0, a_im
    den = lam_re * lam_re + lam_im * lam_im
    f_re = (nr * lam_re + ni * lam_im) / den
    f_im = (ni * lam_re - nr * lam_im) / den
    b_re, b_im = bre_ref[...], bim_ref[...]
    bb_re, bb_im = _cmul(f_re, f_im, b_re, b_im)
    c_re, c_im = cre_ref[...], cim_ref[...]

    pw = [(jnp.ones_like(a_re), jnp.zeros_like(a_im))]
    for _ in range(T_BLK):
        pw.append(_cmul(pw[-1][0], pw[-1][1], a_re, a_im))

    e_re, e_im, ck_re, ck_im = [], [], [], []
    for k in range(T_BLK + 1):
        cr, ci = _cmul(pw[k][0], pw[k][1], c_re, c_im)
        ck_re.append(cr)
        ck_im.append(ci)
    for k in range(T_BLK):
        er, ei = _cmul(pw[k][0], pw[k][1], bb_re, bb_im)
        e_re.append(er)
        e_im.append(ei)
    backward = pl.program_id(1) == 1
    n_blocks, _, gl = ere_ref.shape[:3]
    for k in range(T_BLK):
        e_step = jnp.where(backward, k, T_BLK - 1 - k)
        c_step = jnp.where(backward, T_BLK - 1 - k, k)
        for c in range(n_blocks):
            grp = slice(c * gl, (c + 1) * gl)
            ere_ref[c, e_step] = e_re[k][grp]
            eim_ref[c, e_step] = e_im[k][grp]
            care_ref[c, c_step] = ck_re[k + 1][grp]
            caim_ref[c, c_step] = -ck_im[k + 1][grp]

    dn = (((2,), (2,)), ((0,), (0,)))
    k_t = (lax.dot_general(bb_re, jnp.concatenate(ck_re[:-1], axis=1), dn,
                           precision=lax.Precision.HIGHEST, preferred_element_type=F32)
           - lax.dot_general(bb_im, jnp.concatenate(ck_im[:-1], axis=1), dn,
                             precision=lax.Precision.HIGHEST, preferred_element_type=F32))
    n_grp, h_out, th = k_t.shape
    h_in = th // T_BLK
    k_t = k_t.reshape(n_grp * h_out, th)
    src = lax.broadcasted_iota(jnp.int32, (th, th), 0)
    dst = lax.broadcasted_iota(jnp.int32, (th, th), 1)
    assert h_in & (h_in - 1) == 0, "group width must be a power of two"
    shift = h_in.bit_length() - 1
    same_h = (src & (h_in - 1)) == (dst & (h_in - 1))
    lag, tau = src >> shift, dst >> shift
    for sigma in range(T_BLK):
        wanted = jnp.where(backward, sigma - lag, sigma + lag)
        place = jnp.where(jnp.logical_and(same_h, tau == wanted), 1.0, 0.0)
        block = jnp.dot(k_t, place, precision=lax.Precision.HIGHEST, preferred_element_type=F32)
        klag_ref[:, sigma, :, :] = block.reshape(n_grp, h_out, th)

    q_re, q_im = pw[T_BLK]
    r_re, r_im = q_re, q_im
    for m in range(8):
        apre_ref[m] = r_re
        apim_ref[m] = r_im
        r_re, r_im = _cmul(r_re, r_im, q_re, q_im)


def _s5_prep(lam_re, lam_im, log_step, b_re, b_im, c_re, c_im, gl):
    depth, _, g, p = lam_re.shape
    h = b_re.shape[-1]
    lead = (depth, 2)
    b_re_t = jnp.swapaxes(b_re, -1, -2).astype(F32)
    b_im_t = jnp.swapaxes(b_im, -1, -2).astype(F32)

    gc = _pick_tile(g, max(16, gl), gl)

    def spec(*tail):
        zeros = (0,) * (len(tail) - 1)
        return pl.BlockSpec((None, None) + tail, lambda l, d, q: (l, d, q) + zeros)

    ap_spec = pl.BlockSpec((None, None, 8, gc, 1, p), lambda l, d, q: (l, d, 0, q, 0, 0))
    pw_spec = pl.BlockSpec((None, None, gc // gl, T_BLK, gl, h, p), lambda l, d, q: (l, d, q, 0, 0, 0, 0))
    th = T_BLK * h
    outs = (jax.ShapeDtypeStruct(lead + (g // gl, T_BLK, gl, h, p), F32),) * 4 + (
        jax.ShapeDtypeStruct(lead + (g, T_BLK, h, th), F32),
        jax.ShapeDtypeStruct(lead + (8, g, 1, p), F32),
        jax.ShapeDtypeStruct(lead + (8, g, 1, p), F32))
    blocks = 4 * _nbytes((gc, th, LANES), F32) + 8 * _nbytes((gc, h, LANES), F32)
    return pl.pallas_call(
        _s5_prep_kernel,
        out_shape=outs,
        grid=lead + (g // gc,),
        in_specs=[spec(gc, 1, p), spec(gc, 1, p), spec(gc, 1, 1), spec(gc, h, p), spec(gc, h, p),
                  spec(gc, h, p), spec(gc, h, p)],
        out_specs=(pw_spec,) * 4 + (spec(gc, T_BLK, h, th), ap_spec, ap_spec),
        compiler_params=_compiler_params(("parallel", "parallel", "parallel"), blocks,
                                         16 * _nbytes((gc, th, LANES), F32)),
        name="s5_prep",
    )(lam_re.astype(F32)[:, :, :, None, :], lam_im.astype(F32)[:, :, :, None, :],
      log_step.astype(F32)[..., None, None], b_re_t, b_im_t, c_re.astype(F32), c_im.astype(F32))


def _block_diag_kernel(*refs, n_parts):
    c_refs = refs[:n_parts]
    e_ref, rg_ref, cg_ref, o_ref = refs[n_parts:]
    same_group = rg_ref[...] == cg_ref[...]
    width = e_ref.shape[1]
    for q, c_ref in enumerate(c_refs):
        spread = _dot(c_ref[...].astype(BF16), e_ref[...])
        o_ref[:, q * width:(q + 1) * width] = jnp.where(same_group, spread, 0.0).astype(o_ref.dtype)


def _block_diag(compacts, gl, row_inner, col_inner, name):
    n_cb, rows, kc = compacts[0].shape
    width = kc * gl
    tr = _pick_tile(rows, 512, BF16_ROWS)
    col = np.arange(width)
    spread = np.arange(kc)[:, None] == (col // (gl * col_inner)) * col_inner + col % col_inner
    row_group = ((np.arange(rows) // row_inner) % gl).astype(np.int32)[:, None]
    col_group = ((col // col_inner) % gl).astype(np.int32)[None, :]
    n_parts = len(compacts)
    blocks = (n_parts * _nbytes((tr, LANES), F32) + _nbytes((kc, width), BF16) + _nbytes((tr, LANES), jnp.int32)
              + _nbytes((8, width), jnp.int32) + _nbytes((tr, n_parts * width), BF16))
    return pl.pallas_call(
        functools.partial(_block_diag_kernel, n_parts=n_parts),
        out_shape=jax.ShapeDtypeStruct((n_cb, rows, n_parts * width), BF16),
        grid=(n_cb, rows // tr),
        in_specs=[pl.BlockSpec((None, tr, kc), lambda c, i: (c, i, 0))] * n_parts
                 + [pl.BlockSpec((kc, width), lambda c, i: (0, 0)),
                    pl.BlockSpec((tr, 1), lambda c, i: (i, 0)),
                    pl.BlockSpec((1, width), lambda c, i: (0, 0))],
        out_specs=pl.BlockSpec((None, tr, n_parts * width), lambda c, i: (c, i, 0)),
        compiler_params=_compiler_params(("parallel", "parallel"), blocks),
        name=name,
    )(*compacts, jnp.asarray(spread, BF16), jnp.asarray(row_group), jnp.asarray(col_group))


def _s5_weights(prep, n_cb, gl, h, p):
    e_re, e_im, ca_re, ca_im, klag, ap_re, ap_im = prep
    t = T_BLK
    rows = lambda x: x.reshape(n_cb, t * gl * h, p)

    w_s = _block_diag([rows(e_re[0]), rows(e_im[0]), rows(e_re[1]), rows(e_im[1])], gl, h, p, "s5_w_states")

    w_cre = _block_diag([rows(ca_re[0]), rows(ca_re[1])], gl, h, p, "s5_w_carry_re")
    w_cim = _block_diag([rows(ca_im[0]), rows(ca_im[1])], gl, h, p, "s5_w_carry_im")

    kst = (klag[0] + klag[1]).reshape(n_cb, gl, t, h, t * h)
    kst = jnp.transpose(kst, (0, 2, 1, 3, 4))
    w_i = _block_diag([kst.reshape(n_cb, t * gl * h, t * h)], gl, h, h, "s5_w_intra")

    def lanes(ap):
        ap = jnp.stack([ap[0], ap[1, ::-1]], axis=0)
        ap = ap.reshape(2, 8, n_cb, gl * p)
        return jnp.transpose(ap, (2, 0, 1, 3))
    apw = jnp.stack([lanes(ap_re), lanes(ap_im)], axis=2)

    return w_s, w_cre, w_cim, w_i, apw


def _fold_time_blocks(u_ref, lhs_ref, r):
    pieces = u_ref.shape[0]
    for tau in range(T_BLK):
        for h in range(pieces):
            col = (tau * pieces + h) * LANES
            lhs_ref[:, col:col + LANES] = u_ref[h, pl.ds(tau, r, stride=T_BLK), :].astype(lhs_ref.dtype)


def _s5_states_kernel(u_ref, w_ref, s_ref, lhs_ref, *, r):
    @pl.when(pl.program_id(2) == 0)
    def _():
        _fold_time_blocks(u_ref, lhs_ref, r)

    s_ref[...] = _dot(lhs_ref[...], w_ref[...])


def _s5_states(u, w_s, r, tn):
    n_cb, kw, ns = w_s.shape
    n = u.shape[1]
    nb = n // T_BLK
    ppc = u.shape[0] // n_cb
    blocks = _nbytes((ppc, T_BLK * r, LANES), F32) + _nbytes((kw, tn), BF16) + _nbytes((r, tn), F32)
    scratch = _nbytes((r, kw), BF16)
    return pl.pallas_call(
        functools.partial(_s5_states_kernel, r=r),
        out_shape=jax.ShapeDtypeStruct((n_cb, nb, ns), F32),
        grid=(n_cb, nb // r, ns // tn),
        in_specs=[pl.BlockSpec((ppc, T_BLK * r, LANES), lambda c, i, j: (c, i, 0)),
                  pl.BlockSpec((None, kw, tn), lambda c, i, j: (c, 0, j))],
        out_specs=pl.BlockSpec((None, r, tn), lambda c, i, j: (c, i, j)),
        scratch_shapes=[pltpu.VMEM((r, kw), BF16)],
        compiler_params=_compiler_params(("parallel", "parallel", "arbitrary"), blocks, scratch),
        name="s5_states",
    )(u, w_s)


def _s5_scan_kernel(sre_ref, sim_ref, apre_ref, apim_ref, xre_ref, xim_ref, *, n_pairs):
    d = pl.program_id(1)
    lw = sre_ref.shape[-1]
    ap_re, ap_im = apre_ref[...], apim_ref[...]
    row = lax.broadcasted_iota(jnp.int32, (8, lw), 0)

    def run(reverse):
        steps = []
        for shift in (1, 2, 4):
            src = (8 - shift) if reverse else (shift - 1)
            keep = (row < 8 - shift) if reverse else (row >= shift)
            steps.append((8 - shift if reverse else shift,
                          jnp.where(keep, ap_re[src:src + 1, :], 0.0),
                          jnp.where(keep, ap_im[src:src + 1, :], 0.0)))
        edge = 7 if reverse else 0
        out_row = 0 if reverse else 7
        nb_shift = 7 if reverse else 1

        def tile(r0, c_re, c_im):
            x_re = sre_ref[pl.ds(r0, 8), :]
            x_im = sim_ref[pl.ds(r0, 8), :]
            for amount, m_re, m_im in steps:
                s_re = pltpu.roll(x_re, amount, 0)
                s_im = pltpu.roll(x_im, amount, 0)
                x_re, x_im = x_re + m_re * s_re - m_im * s_im, x_im + m_re * s_im + m_im * s_re
            inc_re = x_re + ap_re * c_re - ap_im * c_im
            inc_im = x_im + ap_re * c_im + ap_im * c_re
            e_re = jnp.where(row == edge, c_re, pltpu.roll(inc_re, nb_shift, 0))
            e_im = jnp.where(row == edge, c_im, pltpu.roll(inc_im, nb_shift, 0))
            return e_re, e_im, inc_re[out_row:out_row + 1, :], inc_im[out_row:out_row + 1, :]

        def body(k, carry):
            c_re, c_im = carry
            pair = (n_pairs - 1 - k) if reverse else k
            r0 = pl.multiple_of(pair * BF16_ROWS, BF16_ROWS)
            e_re, e_im = [None, None], [None, None]
            for half in ((1, 0) if reverse else (0, 1)):
                e_re[half], e_im[half], c_re, c_im = tile(pl.multiple_of(r0 + 8 * half, 8), c_re, c_im)
            xre_ref[pl.ds(r0, BF16_ROWS), :] = jnp.concatenate(e_re, axis=0).astype(xre_ref.dtype)
            xim_ref[pl.ds(r0, BF16_ROWS), :] = jnp.concatenate(e_im, axis=0).astype(xim_ref.dtype)
            return c_re, c_im

        zero = jnp.zeros((1, lw), F32)
        lax.fori_loop(0, n_pairs, body, (zero, zero))

    @pl.when(d == 0)
    def _():
        run(False)

    @pl.when(d == 1)
    def _():
        run(True)


def _s5_scan(s, apw, lw):
    n_cb, nb, ns = s.shape
    glp = ns // 4
    nq = glp // lw
    blocks = 2 * _nbytes((nb, lw), F32) + 2 * _nbytes((nb, lw), BF16) + 2 * _nbytes((8, lw), F32)
    out = jax.ShapeDtypeStruct((n_cb, nb, 2 * glp), BF16)

    def s_spec(ri):
        return pl.BlockSpec((None, nb, lw), lambda c, d, q: (c, 0, (2 * d + ri) * nq + q))

    def ap_spec(ri):
        return pl.BlockSpec((None, None, None, 8, lw), lambda c, d, q: (c, d, ri, 0, q))

    x_spec = pl.BlockSpec((None, nb, lw), lambda c, d, q: (c, 0, d * nq + q))
    return pl.pallas_call(
        functools.partial(_s5_scan_kernel, n_pairs=nb // BF16_ROWS),
        out_shape=(out, out),
        grid=(n_cb, 2, nq),
        in_specs=[s_spec(0), s_spec(1), ap_spec(0), ap_spec(1)],
        out_specs=(x_spec, x_spec),
        compiler_params=_compiler_params(("parallel", "parallel", "parallel"), blocks),
        name="s5_scan",
    )(s, s, apw, apw)


def _s5_out_kernel(u_ref, xre_ref, xim_ref, wi_ref, wcre_ref, wcim_ref, y_ref, lhs_ref, *, r, tn):
    j = pl.program_id(2)
    pieces = u_ref.shape[0]

    @pl.when(j == 0)
    def _():
        _fold_time_blocks(u_ref, lhs_ref, r)

    y = _dot(lhs_ref[...], wi_ref[...])
    nt = (((1,), (1,)), ((), ()))
    y = y + lax.dot_general(xre_ref[...], wcre_ref[...], nt, preferred_element_type=F32)
    y = y + lax.dot_general(xim_ref[...], wcim_ref[...], nt, preferred_element_type=F32)

    per = tn // (pieces * LANES)
    for jj in range(T_BLK // per):
        @pl.when(j == jj)
        def _():
            for k in range(per):
                for h in range(pieces):
                    col = (k * pieces + h) * LANES
                    y_ref[h, pl.ds(jj * per + k, r, stride=T_BLK), :] = y[:, col:col + LANES]


def _s5_out(u, x_re, x_im, w_i, w_cre, w_cim, r, tn):
    n_cb, kw, _ = w_i.shape
    n = u.shape[1]
    nb = n // T_BLK
    ppc = u.shape[0] // n_cb
    kx = x_re.shape[-1]
    assert tn % (ppc * LANES) == 0 and kw % tn == 0
    blocks = (2 * _nbytes((ppc, T_BLK * r, LANES), F32) + 2 * _nbytes((r, kx), BF16) + _nbytes((kw, tn), BF16)
              + 2 * _nbytes((kx, tn), BF16) + _nbytes((r, tn), F32))
    scratch = _nbytes((r, kw), BF16)
    return pl.pallas_call(
        functools.partial(_s5_out_kernel, r=r, tn=tn),
        out_shape=jax.ShapeDtypeStruct(u.shape, F32),
        grid=(n_cb, nb // r, kw // tn),
        in_specs=[pl.BlockSpec((ppc, T_BLK * r, LANES), lambda c, i, j: (c, i, 0)),
                  pl.BlockSpec((None, r, kx), lambda c, i, j: (c, i, 0)),
                  pl.BlockSpec((None, r, kx), lambda c, i, j: (c, i, 0)),
                  pl.BlockSpec((None, kw, tn), lambda c, i, j: (c, 0, j)),
                  pl.BlockSpec((None, tn, kx), lambda c, i, j: (c, j, 0)),
                  pl.BlockSpec((None, tn, kx), lambda c, i, j: (c, j, 0))],
        out_specs=pl.BlockSpec((ppc, T_BLK * r, LANES), lambda c, i, j: (c, i, 0)),
        scratch_shapes=[pltpu.VMEM((r, kw), BF16)],
        compiler_params=_compiler_params(("parallel", "parallel", "arbitrary"), blocks, scratch),
        name="s5_out",
    )(u, x_re, x_im, w_i, w_cre, w_cim)


def _glu_merge_kernel(y_ref, u_ref, d_ref, w1_ref, w2_ref, pa_ref, gb_ref, m_ref, s_ref):
    @pl.when(pl.program_id(1) == 0)
    def _():
        for c in range(y_ref.shape[0]):
            s = jax.nn.gelu(y_ref[c] + d_ref[c] * u_ref[c])
            s_ref[:, c * LANES:(c + 1) * LANES] = s.astype(s_ref.dtype)

    a = s_ref[...]
    y_b = _dot(a, w1_ref[...]) * _sigmoid(_dot(a, w2_ref[...]))
    m_ref[...] = (pa_ref[...].astype(F32) + gb_ref[...].astype(F32) * y_b).astype(m_ref.dtype)


def _glu_merge(y, u, d_skip, w_glu, layer, pa, gates, d_model, tm, tn):
    n_pc, n, _ = y.shape
    k = n_pc * LANES
    nb = d_model // tn
    blocks = (2 * _nbytes((n_pc, tm, LANES), F32) + 2 * _nbytes((k, tn), BF16) + 3 * _nbytes((tm, tn), BF16))
    scratch = _nbytes((tm, k), BF16)
    return pl.pallas_call(
        _glu_merge_kernel,
        out_shape=jax.ShapeDtypeStruct((n, d_model), BF16),
        grid=(n // tm, nb),
        in_specs=[pl.BlockSpec((n_pc, tm, LANES), lambda i, j: (0, i, 0)),
                  pl.BlockSpec((n_pc, tm, LANES), lambda i, j: (0, i, 0)),
                  pl.BlockSpec((n_pc, 1, LANES), lambda i, j: (0, 0, 0)),
                  _w_spec(k, tn, layer, 0),
                  _w_spec(k, tn, layer, nb),
                  pl.BlockSpec((tm, tn), lambda i, j: (i, j)),
                  pl.BlockSpec((tm, tn), lambda i, j: (i, nb + j))],
        out_specs=pl.BlockSpec((tm, tn), lambda i, j: (i, j)),
        scratch_shapes=[pltpu.VMEM((tm, k), BF16)],
        compiler_params=_compiler_params(("parallel", "arbitrary"), blocks, scratch),
        name="glu_merge",
    )(y, u, d_skip, w_glu, w_glu, pa, gates)


def _ffn_up_kernel(h_ref, hp_ref, hn_ref, wg_ref, wv_ref, cg_ref, cv_ref, act_ref, lhs_ref, *, tm):
    i = pl.program_id(0)
    j = pl.program_id(1)
    last = pl.num_programs(0) - 1
    halo = BF16_ROWS

    @pl.when(j == 0)
    def _():
        row = lax.broadcasted_iota(jnp.int32, hp_ref.shape, 0)
        after = jnp.where(jnp.logical_and(row == 0, i < last), hn_ref[...].astype(F32), 0.0)
        edge = jnp.where(jnp.logical_and(row == halo - 1, i > 0), hp_ref[...].astype(F32), after)
        lhs_ref[0:tm, :] = h_ref[...]
        lhs_ref[tm:, :] = edge.astype(lhs_ref.dtype)

    a = lhs_ref[...]
    rows = tm + halo

    def conv(w_ref, cw_ref):
        u = _dot(a, w_ref[...])
        cw = cw_ref[...]
        up = pltpu.roll(u, 1, 0)[0:tm]
        dn = pltpu.roll(u, rows - 1, 0)[0:tm]
        return up * cw[0:1, :] + u[0:tm] * cw[1:2, :] + dn * cw[2:3, :]

    act_ref[...] = (jax.nn.gelu(conv(wg_ref, cg_ref)) * conv(wv_ref, cv_ref)).astype(act_ref.dtype)


def _ffn_up(hn, w_up, layer, conv_w, d_ff, tm, tn, cast=None):
    n, k = hn.shape
    nb = d_ff // tn
    hb = tm // BF16_ROWS
    n_hb = n // BF16_ROWS
    blocks = (_nbytes((tm + 2 * BF16_ROWS, k), BF16) + 2 * _nbytes((k, tn), BF16) + 2 * _nbytes((3, tn), F32)
              + _nbytes((tm, tn), BF16))
    scratch = _nbytes((tm + BF16_ROWS, k), BF16)
    outs = _call_with_cast(
        functools.partial(_ffn_up_kernel, tm=tm), cast,
        grid=(n // tm, nb),
        in_specs=[pl.BlockSpec((tm, k), lambda i, j: (i, 0)),
                  pl.BlockSpec((BF16_ROWS, k), lambda i, j: (jnp.maximum(i * hb - 1, 0), 0)),
                  pl.BlockSpec((BF16_ROWS, k), lambda i, j: (jnp.minimum((i + 1) * hb, n_hb - 1), 0)),
                  _w_spec(k, tn, layer, 0),
                  _w_spec(k, tn, layer, nb),
                  pl.BlockSpec((3, tn), lambda i, j: (0, j)),
                  pl.BlockSpec((3, tn), lambda i, j: (0, nb + j))],
        out_specs=[pl.BlockSpec((tm, tn), lambda i, j: (i, j))],
        out_shape=[jax.ShapeDtypeStruct((n, d_ff), BF16)],
        operands=(hn, hn, hn, w_up, w_up, conv_w, conv_w),
        scratch_shapes=[pltpu.VMEM((tm + BF16_ROWS, k), BF16)],
        block_bytes=blocks, scratch_bytes=scratch, name="ffn_up")
    return outs[0] if cast is None else outs


def kernel(x, meta_tokens, norm_mix, w_in, conv_a_w, w_a, ssm_lambda_re, ssm_lambda_im, ssm_log_step,
           ssm_b_re, ssm_b_im, ssm_c_re, ssm_c_im, ssm_d, w_glu, w_out, norm_ffn, w_up, conv_ffn_w,
           w_down, norm_final):
    bsz, seq, d_model = x.shape
    n_meta = meta_tokens.shape[0]
    depth = w_in.shape[0]
    d_conv = w_a.shape[1]
    d_ssm = ssm_d.shape[1]
    d_ff = w_down.shape[1]
    n_groups, p_state, h_grp = ssm_b_re.shape[2:]
    assert w_in.shape[2] == 3 * d_conv + d_ssm + 2 * d_model and n_groups * h_grp == d_ssm

    cbw = min(S5_CB_WIDTH, d_ssm)
    n_cb = d_ssm // cbw
    gl = cbw // h_grp
    assert n_cb * cbw == d_ssm and gl * h_grp == cbw

    n_true = n_meta + seq
    row_align = T_BLK * BF16_ROWS
    n_pad = _round_up(n_true, PAD_ROWS if n_true >= 4 * PAD_ROWS else row_align)
    assert n_pad % row_align == 0
    nb = n_pad // T_BLK

    tm = _pick_tile(n_pad, TM_TARGET, BF16_ROWS)
    tm_up = _pick_tile(n_pad, TM_FFN_UP_TARGET, BF16_ROWS)
    tm_half = _pick_tile(n_pad, TM_HALF_TARGET, BF16_ROWS)
    tm_ew = _pick_tile(n_pad, TM_EW_TARGET, BF16_ROWS)
    r_s5 = _pick_tile(nb, R_S5_TARGET, BF16_ROWS)
    tn = lambda n, target=512: _pick_tile(n, target, 128)

    prep = _s5_prep(ssm_lambda_re, ssm_lambda_im, ssm_log_step, ssm_b_re, ssm_b_im, ssm_c_re, ssm_c_im, gl)
    d_skip = ssm_d.astype(F32).reshape(depth, d_ssm // LANES, 1, LANES)

    residual = functools.partial(_epi_residual, n_valid=n_true)
    tile_spec = lambda t, rows=tm: pl.BlockSpec((rows, t), lambda i, j: (i, j))
    u_off = 3 * d_conv
    gate_off = u_off + d_ssm

    w_a_b, w_glu_b, w_out_b = (w.astype(BF16) for w in (w_a, w_glu, w_out))
    w_in_first = w_in[0].astype(BF16)

    outs = []
    for b in range(bsz):
        w_in_l = w_in_first
        for l in range(depth):
            w_s, w_cre, w_cim, w_i, apw = _s5_weights([t[l] for t in prep], n_cb, gl, h_grp, p_state)

            if l == 0:
                h_res, hn = _assemble_norm(meta_tokens, x[b], norm_mix[0], n_pad, tm_ew)
            else:
                hn = _rmsnorm(h_res, norm_mix[l], BF16, tm_ew)
            bg, cv, u, w_down_l = _inproj_conv(hn, w_in_l, None, d_conv, d_ssm, tm_half, tn(d_conv),
                                               cast=(w_down, l))
            t_g = tn(2 * d_model)
            gates, w_up_l = _mm(hn, w_in_l, None, col0=gate_off, n_cols=2 * d_model, tm=tm, tn=t_g,
                                epilogue=_epi_sigmoid,
                                out_shape=jax.ShapeDtypeStruct((n_pad, 2 * d_model), BF16),
                                out_spec=tile_spec(t_g), name="inproj_gates", cast=(w_up, l))

            pa = _branch_a(bg, cv, conv_a_w[l].astype(F32), w_a_b, l, gates, d_model, tm, tn(d_model, 1024))
            t_d = tn(d_model)

            states = _s5_states(u, w_s, _pick_tile(nb, R_S5_STATES_TARGET, BF16_ROWS), tn(w_s.shape[-1]))
            x_re, x_im = _s5_scan(states, apw, tn(gl * p_state, SCAN_LANES_TARGET))
            y = _s5_out(u, x_re, x_im, w_i, w_cre, w_cim, r_s5, tn(T_BLK * cbw))

            merged = _glu_merge(y, u, d_skip[l], w_glu_b, l, pa, gates, d_model, tm_half, tn(d_model, 1024))
            h_res = _mm(merged, w_out_b, l, col0=0, n_cols=d_model, tm=tm, tn=t_d, epilogue=residual,
                        extras=(h_res,), extra_specs=(tile_spec(t_d),),
                        out_shape=jax.ShapeDtypeStruct((n_pad, d_model), F32), out_spec=tile_spec(t_d),
                        name="out_proj")

            hn = _rmsnorm(h_res, norm_ffn[l], BF16, tm_ew)
            if l + 1 < depth:
                act, w_in_l = _ffn_up(hn, w_up_l, None, conv_ffn_w[l].astype(F32), d_ff, tm_up, tn(d_ff),
                                      cast=(w_in, l + 1))
            else:
                act = _ffn_up(hn, w_up_l, None, conv_ffn_w[l].astype(F32), d_ff, tm_up, tn(d_ff))
            t_o = tn(d_model, 256)
            h_res = _mm(act, w_down_l, None, col0=0, n_cols=d_model, tm=tm_half, tn=t_o, epilogue=residual,
                        extras=(h_res,), extra_specs=(tile_spec(t_o, tm_half),),
                        out_shape=jax.ShapeDtypeStruct((n_pad, d_model), F32), out_spec=tile_spec(t_o, tm_half),
                        name="ffn_down")

        outs.append(_final_norm(h_res, norm_final, n_meta, seq, x.dtype, _pick_tile(seq, TM_EW_TARGET, n_meta)))
    return jnp.stack(outs, axis=0)
```

```python
import functools

import numpy as np
import jax
import jax.numpy as jnp
from jax import lax
from jax.experimental import pallas as pl
from jax.experimental.pallas import tpu as pltpu

F32 = jnp.float32
BF16 = jnp.bfloat16
EPS = 1e-6

T_BLK = 16
S5_CB_WIDTH = 128
PAD_ROWS = 1280
BF16_ROWS = 16
LANES = 128
VMEM_CAP_BYTES = 60000 * 1024
VMEM_TEMP_BYTES = 20 * 1024 * 1024

TM_TARGET = 1280
TM_FFN_UP_TARGET = 832
TM_HALF_TARGET = 640
R_S5_STATES_TARGET = 1040
TM_EW_TARGET = 320
R_S5_TARGET = 1040
SCAN_LANES_TARGET = 256


def _round_up(n, m):
    return (n + m - 1) // m * m


def _pick_tile(n, target, align):
    best = 0
    for t in range(align, min(n, target) + 1, align):
        if n % t == 0:
            best = t
    assert best > 0, (n, target, align)
    return best


def _nbytes(shape, dtype):
    return int(np.prod(shape)) * jnp.dtype(dtype).itemsize


def _compiler_params(semantics, block_bytes, scratch_bytes=0):
    est = 2 * block_bytes + scratch_bytes + VMEM_TEMP_BYTES
    return pltpu.CompilerParams(dimension_semantics=semantics,
                                vmem_limit_bytes=int(min(VMEM_CAP_BYTES, est)))


def _dot(a, b):
    return jnp.dot(a, b, preferred_element_type=F32)


def _sigmoid(x):
    return 0.5 * (1.0 + jnp.tanh(0.5 * x))


def _rmsnorm_kernel(x_ref, g_ref, o_ref):
    x = x_ref[...]
    ms = jnp.mean(x * x, axis=-1, keepdims=True)
    o_ref[...] = (x * lax.rsqrt(ms + EPS) * g_ref[...]).astype(o_ref.dtype)


def _rmsnorm(x, g, out_dtype, tm):
    n, d = x.shape
    blocks = _nbytes((tm, d), F32) + _nbytes((tm, d), out_dtype)
    return pl.pallas_call(
        _rmsnorm_kernel,
        out_shape=jax.ShapeDtypeStruct((n, d), out_dtype),
        grid=(n // tm,),
        in_specs=[pl.BlockSpec((tm, d), lambda i: (i, 0)),
                  pl.BlockSpec((1, d), lambda i: (0, 0))],
        out_specs=pl.BlockSpec((tm, d), lambda i: (i, 0)),
        compiler_params=_compiler_params(("parallel",), blocks),
        name="rmsnorm",
    )(x, g.reshape(1, d).astype(F32))


def _assemble_norm_kernel(meta_ref, prev_ref, x_ref, g_ref, h_ref, hn_ref, *, n_valid):
    i = pl.program_id(0)
    tm = h_ref.shape[0]
    n_meta = meta_ref.shape[0]
    head = jnp.where(i == 0, meta_ref[...], prev_ref[...])
    tile = jnp.concatenate([head, x_ref[0:tm - n_meta, :]], axis=0)
    rows = i * tm + lax.broadcasted_iota(jnp.int32, tile.shape, 0)
    tile = jnp.where(rows < n_valid, tile, 0.0)
    h_ref[...] = tile
    ms = jnp.mean(tile * tile, axis=-1, keepdims=True)
    hn_ref[...] = (tile * lax.rsqrt(ms + EPS) * g_ref[...]).astype(hn_ref.dtype)


def _assemble_norm(meta, x, g, n_pad, tm):
    n_meta, d = meta.shape
    seq = x.shape[0]
    assert n_meta % 8 == 0 and tm % n_meta == 0 and seq % n_meta == 0 and n_pad % tm == 0
    per = tm // n_meta
    last_x = pl.cdiv(seq, tm) - 1
    last_prev = seq // n_meta - 1
    blocks = _nbytes((tm + 2 * n_meta, d), F32) + _nbytes((tm, d), F32) + _nbytes((tm, d), BF16)
    return pl.pallas_call(
        functools.partial(_assemble_norm_kernel, n_valid=n_meta + seq),
        out_shape=(jax.ShapeDtypeStruct((n_pad, d), F32), jax.ShapeDtypeStruct((n_pad, d), BF16)),
        grid=(n_pad // tm,),
        in_specs=[pl.BlockSpec((n_meta, d), lambda i: (0, 0)),
                  pl.BlockSpec((n_meta, d), lambda i: (jnp.clip(i * per - 1, 0, last_prev), 0)),
                  pl.BlockSpec((tm, d), lambda i: (jnp.minimum(i, last_x), 0)),
                  pl.BlockSpec((1, d), lambda i: (0, 0))],
        out_specs=(pl.BlockSpec((tm, d), lambda i: (i, 0)), pl.BlockSpec((tm, d), lambda i: (i, 0))),
        compiler_params=_compiler_params(("parallel",), blocks),
        name="assemble_norm",
    )(meta.astype(F32), x.astype(F32), x.astype(F32), g.reshape(1, d).astype(F32))


def _final_norm_kernel(a_ref, b_ref, g_ref, o_ref, *, skip):
    x = jnp.concatenate([a_ref[skip:, :], b_ref[...]], axis=0)
    ms = jnp.mean(x * x, axis=-1, keepdims=True)
    o_ref[...] = (x * lax.rsqrt(ms + EPS) * g_ref[...]).astype(o_ref.dtype)


def _final_norm(x, g, skip, n_out, out_dtype, tm):
    n, d = x.shape
    assert skip % 8 == 0 and tm % skip == 0 and n_out % tm == 0 and n >= n_out + skip
    per = tm // skip
    blocks = _nbytes((tm + skip, d), F32) + _nbytes((tm, d), out_dtype)
    return pl.pallas_call(
        functools.partial(_final_norm_kernel, skip=skip),
        out_shape=jax.ShapeDtypeStruct((n_out, d), out_dtype),
        grid=(n_out // tm,),
        in_specs=[pl.BlockSpec((tm, d), lambda i: (i, 0)),
                  pl.BlockSpec((skip, d), lambda i: ((i + 1) * per, 0)),
                  pl.BlockSpec((1, d), lambda i: (0, 0))],
        out_specs=pl.BlockSpec((tm, d), lambda i: (i, 0)),
        compiler_params=_compiler_params(("parallel",), blocks),
        name="final_norm",
    )(x, x, g.reshape(1, d).astype(F32))


def _mm_kernel(*refs, n_extra, epilogue, tm):
    a_ref, w_ref = refs[0], refs[1]
    extras = refs[2:2 + n_extra]
    o_ref = refs[2 + n_extra]
    acc = _dot(a_ref[...], w_ref[...])
    row0 = pl.program_id(0) * tm
    o_ref[...] = epilogue(acc, row0, *[e[...] for e in extras]).astype(o_ref.dtype)


def _w_spec(k, tn, layer, jb0):
    if layer is None:
        return pl.BlockSpec((k, tn), lambda i, j: (0, jb0 + j))
    return pl.BlockSpec((None, k, tn), lambda i, j: (layer, 0, jb0 + j))


def _call_with_cast(kernel_fn, cast, *, grid, in_specs, out_specs, out_shape, operands, scratch_shapes=(),
                    block_bytes, scratch_bytes=0, name):
    if cast is None:
        return pl.pallas_call(
            kernel_fn, out_shape=tuple(out_shape), grid=grid, in_specs=list(in_specs), out_specs=tuple(out_specs),
            scratch_shapes=list(scratch_shapes),
            compiler_params=_compiler_params(("parallel", "arbitrary"), block_bytes, scratch_bytes), name=name,
        )(*operands)

    stack, layer = cast
    _, k, n = stack.shape
    n_steps = grid[0] * grid[1]
    options = []
    for rows in range(BF16_ROWS, k + 1, BF16_ROWS):
        if k % rows == 0 and k // rows <= n_steps:
            splits = max(c for c in range(1, n_steps // (k // rows) + 1) if n % (c * LANES) == 0)
            options.append((rows * (n // splits), rows, splits))
    _, rows, splits = min(options)
    cols = n // splits
    last_chunk = (k // rows) * splits - 1
    n_in, n_out = len(in_specs), len(out_specs)

    def chunk(i, j):
        q = jnp.minimum(i * grid[1] + j, last_chunk)
        return q // splits, q % splits

    def body(*refs):
        src_ref, dst_ref = refs[n_in], refs[n_in + 1 + n_out]
        dst_ref[...] = src_ref[...].astype(dst_ref.dtype)
        kernel_fn(*refs[:n_in], *refs[n_in + 1:n_in + 1 + n_out], *refs[n_in + 2 + n_out:])

    cast_bytes = _nbytes((rows, cols), F32) + _nbytes((rows, cols), BF16)
    return pl.pallas_call(
        body,
        out_shape=(*out_shape, jax.ShapeDtypeStruct((k, n), BF16)),
        grid=grid,
        in_specs=[*in_specs, pl.BlockSpec((None, rows, cols), lambda i, j: (layer, *chunk(i, j)))],
        out_specs=(*out_specs, pl.BlockSpec((rows, cols), lambda i, j: chunk(i, j))),
        scratch_shapes=list(scratch_shapes),
        compiler_params=_compiler_params(("arbitrary", "arbitrary"), block_bytes + cast_bytes, scratch_bytes),
        name=name,
    )(*operands, stack)


def _mm(a, w, layer, *, col0, n_cols, tm, tn, epilogue, extras=(), extra_specs=(), out_shape, out_spec, name,
        cast=None):
    n, k = a.shape
    assert n % tm == 0 and n_cols % tn == 0 and col0 % tn == 0
    blocks = (_nbytes((tm, k), a.dtype) + _nbytes((k, tn), w.dtype) + _nbytes((tm, tn), F32)
              + sum(_nbytes((tm, tn), e.dtype) for e in extras))
    outs = _call_with_cast(
        functools.partial(_mm_kernel, n_extra=len(extras), epilogue=epilogue, tm=tm), cast,
        grid=(n // tm, n_cols // tn),
        in_specs=[pl.BlockSpec((tm, k), lambda i, j: (i, 0)),
                  _w_spec(k, tn, layer, col0 // tn),
                  *extra_specs],
        out_specs=[out_spec], out_shape=[out_shape], operands=(a, w, *extras), block_bytes=blocks, name=name)
    return outs[0] if cast is None else outs


def _epi_pieces(acc, row0):
    return jnp.stack([acc[:, h * LANES:(h + 1) * LANES] for h in range(acc.shape[1] // LANES)], axis=0)


def _epi_sigmoid(acc, row0):
    return _sigmoid(acc)


def _epi_residual(acc, row0, res, *, n_valid):
    rows = row0 + lax.broadcasted_iota(jnp.int32, acc.shape, 0)
    return jnp.where(rows < n_valid, res + acc, 0.0)


def _inproj_conv_kernel(a_ref, wb_ref, wc_ref, wv_ref, wu_ref, bg_ref, cv_ref, u_ref):
    a = a_ref[...]
    bg_ref[...] = _dot(a, wb_ref[...]).astype(bg_ref.dtype)
    cv_ref[...] = (_dot(a, wc_ref[...]) * _dot(a, wv_ref[...])).astype(cv_ref.dtype)
    u_ref[...] = _epi_pieces(_dot(a, wu_ref[...]), 0)


def _inproj_conv(hn, w_in, layer, d_conv, d_ssm, tm, tn, cast=None):
    n, k = hn.shape
    nb = d_conv // tn
    tn_u = d_ssm // nb
    assert tn_u % LANES == 0 and (3 * d_conv) % tn_u == 0
    blocks = (_nbytes((tm, k), BF16) + 3 * _nbytes((k, tn), BF16) + _nbytes((k, tn_u), BF16)
              + 2 * _nbytes((tm, tn), BF16) + _nbytes((tm, tn_u), F32))
    out = jax.ShapeDtypeStruct((n, d_conv), BF16)
    return _call_with_cast(
        _inproj_conv_kernel, cast,
        grid=(n // tm, nb),
        in_specs=[pl.BlockSpec((tm, k), lambda i, j: (i, 0)),
                  _w_spec(k, tn, layer, 0),
                  _w_spec(k, tn, layer, nb),
                  _w_spec(k, tn, layer, 2 * nb),
                  _w_spec(k, tn_u, layer, 3 * d_conv // tn_u)],
        out_specs=[pl.BlockSpec((tm, tn), lambda i, j: (i, j)),
                   pl.BlockSpec((tm, tn), lambda i, j: (i, j)),
                   pl.BlockSpec((tn_u // LANES, tm, LANES), lambda i, j: (j, i, 0))],
        out_shape=[out, out, jax.ShapeDtypeStruct((d_ssm // LANES, n, LANES), F32)],
        operands=(hn, w_in, w_in, w_in, w_in), block_bytes=blocks, name="inproj_conv")


def _shifted_rows(x, prev_row, next_row):
    t = x.shape[0]
    row = lax.broadcasted_iota(jnp.int32, x.shape, 0)
    up = jnp.where(row == 0, prev_row, pltpu.roll(x, 1, 0))
    dn = jnp.where(row == t - 1, next_row, pltpu.roll(x, t - 1, 0))
    return up, dn


def _branch_a_kernel(bg_ref, cv_ref, cvp_ref, cvn_ref, cw_ref, w_ref, gate_ref, o_ref, z_ref, *, chunk):
    i = pl.program_id(0)
    last = pl.num_programs(0) - 1

    @pl.when(pl.program_id(1) == 0)
    def _():
        for c0 in range(0, z_ref.shape[1], chunk):
            cols = slice(c0, c0 + chunk)
            x = cv_ref[:, cols].astype(F32)
            prev = cvp_ref[:, cols].astype(F32)[BF16_ROWS - 1:BF16_ROWS, :]
            nxt = cvn_ref[:, cols].astype(F32)[0:1, :]
            prev = jnp.where(i > 0, prev, 0.0)
            nxt = jnp.where(i < last, nxt, 0.0)
            up, dn = _shifted_rows(x, prev, nxt)
            w = cw_ref[:, cols]
            conv = up * w[0:1, :] + x * w[1:2, :] + dn * w[2:3, :]
            z_ref[:, cols] = (bg_ref[:, cols].astype(F32) * conv).astype(z_ref.dtype)

    o_ref[...] = (_dot(z_ref[...], w_ref[...]) * gate_ref[...].astype(F32)).astype(o_ref.dtype)


def _branch_a(bg, cv, conv_w, w_a, layer, gates, d_model, tm, tn):
    n, c = bg.shape
    hb = tm // BF16_ROWS
    n_hb = n // BF16_ROWS
    blocks = (2 * _nbytes((tm, c), BF16) + 2 * _nbytes((BF16_ROWS, c), BF16) + _nbytes((8, c), F32)
              + _nbytes((c, tn), BF16) + 2 * _nbytes((tm, tn), BF16))
    scratch = _nbytes((tm, c), BF16)
    return pl.pallas_call(
        functools.partial(_branch_a_kernel, chunk=_pick_tile(c, 512, LANES)),
        out_shape=jax.ShapeDtypeStruct((n, d_model), BF16),
        grid=(n // tm, d_model // tn),
        in_specs=[pl.BlockSpec((tm, c), lambda i, j: (i, 0)),
                  pl.BlockSpec((tm, c), lambda i, j: (i, 0)),
                  pl.BlockSpec((BF16_ROWS, c), lambda i, j: (jnp.maximum(i * hb - 1, 0), 0)),
                  pl.BlockSpec((BF16_ROWS, c), lambda i, j: (jnp.minimum((i + 1) * hb, n_hb - 1), 0)),
                  pl.BlockSpec((3, c), lambda i, j: (0, 0)),
                  _w_spec(c, tn, layer, 0),
                  pl.BlockSpec((tm, tn), lambda i, j: (i, j))],
        out_specs=pl.BlockSpec((tm, tn), lambda i, j: (i, j)),
        scratch_shapes=[pltpu.VMEM((tm, c), BF16)],
        compiler_params=_compiler_params(("parallel", "arbitrary"), blocks, scratch),
        name="branch_a",
    )(bg, cv, cv, cv, conv_w, w_a, gates)


def _cmul(ar, ai, br, bi):
    return ar * br - ai * bi, ar * bi + ai * br


def _s5_prep_kernel(lre_ref, lim_ref, lst_ref, bre_ref, bim_ref, cre_ref, cim_ref,
                    ere_ref, eim_ref, care_ref, caim_ref, klag_ref, apre_ref, apim_ref):
    lam_re, lam_im = lre_ref[...], lim_ref[...]
    dt = jnp.exp(lst_ref[...])
    mag = jnp.exp(lam_re * dt)
    a_re = mag * jnp.cos(lam_im * dt)
    a_im = mag * jnp.sin(lam_im * dt)
    nr, ni = a_re - 1.0, a_im
    den = lam_re * lam_re + lam_im * lam_im
    f_re = (nr * lam_re + ni * lam_im) / den
    f_im = (ni * lam_re - nr * lam_im) / den
    b_re, b_im = bre_ref[...], bim_ref[...]
    bb_re, bb_im = _cmul(f_re, f_im, b_re, b_im)
    c_re, c_im = cre_ref[...], cim_ref[...]

    pw = [(jnp.ones_like(a_re), jnp.zeros_like(a_im))]
    for _ in range(T_BLK):
        pw.append(_cmul(pw[-1][0], pw[-1][1], a_re, a_im))

    e_re, e_im, ck_re, ck_im = [], [], [], []
    for k in range(T_BLK + 1):
        cr, ci = _cmul(pw[k][0], pw[k][1], c_re, c_im)
        ck_re.append(cr)
        ck_im.append(ci)
    for k in range(T_BLK):
        er, ei = _cmul(pw[k][0], pw[k][1], bb_re, bb_im)
        e_re.append(er)
        e_im.append(ei)
    backward = pl.program_id(1) == 1
    n_blocks, _, gl = ere_ref.shape[:3]
    for k in range(T_BLK):
        e_step = jnp.where(backward, k, T_BLK - 1 - k)
        c_step = jnp.where(backward, T_BLK - 1 - k, k)
        for c in range(n_blocks):
            grp = slice(c * gl, (c + 1) * gl)
            ere_ref[c, e_step] = e_re[k][grp]
            eim_ref[c, e_step] = e_im[k][grp]
            care_ref[c, c_step] = ck_re[k + 1][grp]
            caim_ref[c, c_step] = -ck_im[k + 1][grp]

    dn = (((2,), (2,)), ((0,), (0,)))
    k_t = (lax.dot_general(bb_re, jnp.concatenate(ck_re[:-1], axis=1), dn,
                           precision=lax.Precision.HIGHEST, preferred_element_type=F32)
           - lax.dot_general(bb_im, jnp.concatenate(ck_im[:-1], axis=1), dn,
                             precision=lax.Precision.HIGHEST, preferred_element_type=F32))
    n_grp, h_out, th = k_t.shape
    h_in = th // T_BLK
    k_t = k_t.reshape(n_grp * h_out, th)
    src = lax.broadcasted_iota(jnp.int32, (th, th), 0)
    dst = lax.broadcasted_iota(jnp.int32, (th, th), 1)
    assert h_in & (h_in - 1) == 0, "group width must be a power of two"
    shift = h_in.bit_length() - 1
    same_h = (src & (h_in - 1)) == (dst & (h_in - 1))
    lag, tau = src >> shift, dst >> shift
    for sigma in range(T_BLK):
        wanted = jnp.where(backward, sigma - lag, sigma + lag)
        place = jnp.where(jnp.logical_and(same_h, tau == wanted), 1.0, 0.0)
        block = jnp.dot(k_t, place, precision=lax.Precision.HIGHEST, preferred_element_type=F32)
        klag_ref[:, sigma, :, :] = block.reshape(n_grp, h_out, th)

    q_re, q_im = pw[T_BLK]
    r_re, r_im = q_re, q_im
    for m in range(8):
        apre_ref[m] = r_re
        apim_ref[m] = r_im
        r_re, r_im = _cmul(r_re, r_im, q_re, q_im)


def _s5_prep(lam_re, lam_im, log_step, b_re, b_im, c_re, c_im, gl):
    depth, _, g, p = lam_re.shape
    h = b_re.shape[-1]
    lead = (depth, 2)
    b_re_t = jnp.swapaxes(b_re, -1, -2).astype(F32)
    b_im_t = jnp.swapaxes(b_im, -1, -2).astype(F32)

    gc = _pick_tile(g, max(16, gl), gl)

    def spec(*tail):
        zeros = (0,) * (len(tail) - 1)
        return pl.BlockSpec((None, None) + tail, lambda l, d, q: (l, d, q) + zeros)

    ap_spec = pl.BlockSpec((None, None, 8, gc, 1, p), lambda l, d, q: (l, d, 0, q, 0, 0))
    pw_spec = pl.BlockSpec((None, None, gc // gl, T_BLK, gl, h, p), lambda l, d, q: (l, d, q, 0, 0, 0, 0))
    th = T_BLK * h
    outs = (jax.ShapeDtypeStruct(lead + (g // gl, T_BLK, gl, h, p), F32),) * 4 + (
        jax.ShapeDtypeStruct(lead + (g, T_BLK, h, th), F32),
        jax.ShapeDtypeStruct(lead + (8, g, 1, p), F32),
        jax.ShapeDtypeStruct(lead + (8, g, 1, p), F32))
    blocks = 4 * _nbytes((gc, th, LANES), F32) + 8 * _nbytes((gc, h, LANES), F32)
    return pl.pallas_call(
        _s5_prep_kernel,
        out_shape=outs,
        grid=lead + (g // gc,),
        in_specs=[spec(gc, 1, p), spec(gc, 1, p), spec(gc, 1, 1), spec(gc, h, p), spec(gc, h, p),
                  spec(gc, h, p), spec(gc, h, p)],
        out_specs=(pw_spec,) * 4 + (spec(gc, T_BLK, h, th), ap_spec, ap_spec),
        compiler_params=_compiler_params(("parallel", "parallel", "parallel"), blocks,
                                         16 * _nbytes((gc, th, LANES), F32)),
        name="s5_prep",
    )(lam_re.astype(F32)[:, :, :, None, :], lam_im.astype(F32)[:, :, :, None, :],
      log_step.astype(F32)[..., None, None], b_re_t, b_im_t, c_re.astype(F32), c_im.astype(F32))


def _block_diag_kernel(*refs, n_parts):
    c_refs = refs[:n_parts]
    e_ref, rg_ref, cg_ref, o_ref = refs[n_parts:]
    same_group = rg_ref[...] == cg_ref[...]
    width = e_ref.shape[1]
    for q, c_ref in enumerate(c_refs):
        spread = _dot(c_ref[...].astype(BF16), e_ref[...])
        o_ref[:, q * width:(q + 1) * width] = jnp.where(same_group, spread, 0.0).astype(o_ref.dtype)


def _block_diag(compacts, gl, row_inner, col_inner, name):
    compacts = [c if isinstance(c, tuple) else (c, ()) for c in compacts]
    n_cb, rows, kc = compacts[0][0].shape[-3:]
    width = kc * gl
    tr = _pick_tile(rows, 512, BF16_ROWS)
    col = np.arange(width)
    spread = np.arange(kc)[:, None] == (col // (gl * col_inner)) * col_inner + col % col_inner
    row_group = ((np.arange(rows) // row_inner) % gl).astype(np.int32)[:, None]
    col_group = ((col // col_inner) % gl).astype(np.int32)[None, :]
    n_parts = len(compacts)
    blocks = (n_parts * _nbytes((tr, LANES), F32) + _nbytes((kc, width), BF16) + _nbytes((tr, LANES), jnp.int32)
              + _nbytes((8, width), jnp.int32) + _nbytes((tr, n_parts * width), BF16))
    return pl.pallas_call(
        functools.partial(_block_diag_kernel, n_parts=n_parts),
        out_shape=jax.ShapeDtypeStruct((n_cb, rows, n_parts * width), BF16),
        grid=(n_cb, rows // tr),
        in_specs=[pl.BlockSpec((None,) * (len(lead) + 1) + (tr, kc), lambda c, i, lead=lead: (*lead, c, i, 0))
                  for _, lead in compacts]
                 + [pl.BlockSpec((kc, width), lambda c, i: (0, 0)),
                    pl.BlockSpec((tr, 1), lambda c, i: (i, 0)),
                    pl.BlockSpec((1, width), lambda c, i: (0, 0))],
        out_specs=pl.BlockSpec((None, tr, n_parts * width), lambda c, i: (c, i, 0)),
        compiler_params=_compiler_params(("parallel", "parallel"), blocks),
        name=name,
    )(*[arr for arr, _ in compacts], jnp.asarray(spread, BF16), jnp.asarray(row_group), jnp.asarray(col_group))


def _s5_weights(prep, layer, n_cb, gl, h, p):
    e_re, e_im, ca_re, ca_im, klag, ap_re, ap_im = prep
    t = T_BLK
    depth = e_re.shape[0]
    rows = lambda x: x.reshape(depth, 2, n_cb, t * gl * h, p)
    part = lambda x, d: (rows(x), (layer, d))

    w_s = _block_diag([part(e_re, 0), part(e_im, 0), part(e_re, 1), part(e_im, 1)], gl, h, p, "s5_w_states")

    w_cre = _block_diag([part(ca_re, 0), part(ca_re, 1)], gl, h, p, "s5_w_carry_re")
    w_cim = _block_diag([part(ca_im, 0), part(ca_im, 1)], gl, h, p, "s5_w_carry_im")

    kst = (klag[layer, 0] + klag[layer, 1]).reshape(n_cb, gl, t, h, t * h)
    kst = jnp.transpose(kst, (0, 2, 1, 3, 4))
    w_i = _block_diag([kst.reshape(n_cb, t * gl * h, t * h)], gl, h, h, "s5_w_intra")

    ap_re, ap_im = ap_re[layer], ap_im[layer]

    def lanes(ap):
        ap = jnp.stack([ap[0], ap[1, ::-1]], axis=0)
        ap = ap.reshape(2, 8, n_cb, gl * p)
        return jnp.transpose(ap, (2, 0, 1, 3))
    apw = jnp.stack([lanes(ap_re), lanes(ap_im)], axis=2)

    return w_s, w_cre, w_cim, w_i, apw


def _fold_time_blocks(u_ref, lhs_ref, r):
    pieces = u_ref.shape[0]
    for tau in range(T_BLK):
        for h in range(pieces):
            col = (tau * pieces + h) * LANES
            lhs_ref[:, col:col + LANES] = u_ref[h, pl.ds(tau, r, stride=T_BLK), :].astype(lhs_ref.dtype)


def _s5_states_kernel(u_ref, w_ref, s_ref, lhs_ref, *, r):
    @pl.when(pl.program_id(2) == 0)
    def _():
        _fold_time_blocks(u_ref, lhs_ref, r)

    s_ref[...] = _dot(lhs_ref[...], w_ref[...])


def _s5_states(u, w_s, r, tn):
    n_cb, kw, ns = w_s.shape
    n = u.shape[1]
    nb = n // T_BLK
    ppc = u.shape[0] // n_cb
    blocks = _nbytes((ppc, T_BLK * r, LANES), F32) + _nbytes((kw, tn), BF16) + _nbytes((r, tn), F32)
    scratch = _nbytes((r, kw), BF16)
    return pl.pallas_call(
        functools.partial(_s5_states_kernel, r=r),
        out_shape=jax.ShapeDtypeStruct((n_cb, nb, ns), F32),
        grid=(n_cb, nb // r, ns // tn),
        in_specs=[pl.BlockSpec((ppc, T_BLK * r, LANES), lambda c, i, j: (c, i, 0)),
                  pl.BlockSpec((None, kw, tn), lambda c, i, j: (c, 0, j))],
        out_specs=pl.BlockSpec((None, r, tn), lambda c, i, j: (c, i, j)),
        scratch_shapes=[pltpu.VMEM((r, kw), BF16)],
        compiler_params=_compiler_params(("parallel", "parallel", "arbitrary"), blocks, scratch),
        name="s5_states",
    )(u, w_s)


def _s5_scan_kernel(sre_ref, sim_ref, apre_ref, apim_ref, xre_ref, xim_ref, *, n_pairs):
    d = pl.program_id(1)
    lw = sre_ref.shape[-1]
    ap_re, ap_im = apre_ref[...], apim_ref[...]
    row = lax.broadcasted_iota(jnp.int32, (8, lw), 0)

    def run(reverse):
        steps = []
        for shift in (1, 2, 4):
            src = (8 - shift) if reverse else (shift - 1)
            keep = (row < 8 - shift) if reverse else (row >= shift)
            steps.append((8 - shift if reverse else shift,
                          jnp.where(keep, ap_re[src:src + 1, :], 0.0),
                          jnp.where(keep, ap_im[src:src + 1, :], 0.0)))
        edge = 7 if reverse else 0
        out_row = 0 if reverse else 7
        nb_shift = 7 if reverse else 1

        def tile(r0, c_re, c_im):
            x_re = sre_ref[pl.ds(r0, 8), :]
            x_im = sim_ref[pl.ds(r0, 8), :]
            for amount, m_re, m_im in steps:
                s_re = pltpu.roll(x_re, amount, 0)
                s_im = pltpu.roll(x_im, amount, 0)
                x_re, x_im = x_re + m_re * s_re - m_im * s_im, x_im + m_re * s_im + m_im * s_re
            inc_re = x_re + ap_re * c_re - ap_im * c_im
            inc_im = x_im + ap_re * c_im + ap_im * c_re
            e_re = jnp.where(row == edge, c_re, pltpu.roll(inc_re, nb_shift, 0))
            e_im = jnp.where(row == edge, c_im, pltpu.roll(inc_im, nb_shift, 0))
            return e_re, e_im, inc_re[out_row:out_row + 1, :], inc_im[out_row:out_row + 1, :]

        def body(k, carry):
            c_re, c_im = carry
            pair = (n_pairs - 1 - k) if reverse else k
            r0 = pl.multiple_of(pair * BF16_ROWS, BF16_ROWS)
            e_re, e_im = [None, None], [None, None]
            for half in ((1, 0) if reverse else (0, 1)):
                e_re[half], e_im[half], c_re, c_im = tile(pl.multiple_of(r0 + 8 * half, 8), c_re, c_im)
            xre_ref[pl.ds(r0, BF16_ROWS), :] = jnp.concatenate(e_re, axis=0).astype(xre_ref.dtype)
            xim_ref[pl.ds(r0, BF16_ROWS), :] = jnp.concatenate(e_im, axis=0).astype(xim_ref.dtype)
            return c_re, c_im

        zero = jnp.zeros((1, lw), F32)
        lax.fori_loop(0, n_pairs, body, (zero, zero))

    @pl.when(d == 0)
    def _():
        run(False)

    @pl.when(d == 1)
    def _():
        run(True)


def _s5_scan(s, apw, lw):
    n_cb, nb, ns = s.shape
    glp = ns // 4
    nq = glp // lw
    blocks = 2 * _nbytes((nb, lw), F32) + 2 * _nbytes((nb, lw), BF16) + 2 * _nbytes((8, lw), F32)
    out = jax.ShapeDtypeStruct((n_cb, nb, 2 * glp), BF16)

    def s_spec(ri):
        return pl.BlockSpec((None, nb, lw), lambda c, d, q: (c, 0, (2 * d + ri) * nq + q))

    def ap_spec(ri):
        return pl.BlockSpec((None, None, None, 8, lw), lambda c, d, q: (c, d, ri, 0, q))

    x_spec = pl.BlockSpec((None, nb, lw), lambda c, d, q: (c, 0, d * nq + q))
    return pl.pallas_call(
        functools.partial(_s5_scan_kernel, n_pairs=nb // BF16_ROWS),
        out_shape=(out, out),
        grid=(n_cb, 2, nq),
        in_specs=[s_spec(0), s_spec(1), ap_spec(0), ap_spec(1)],
        out_specs=(x_spec, x_spec),
        compiler_params=_compiler_params(("parallel", "parallel", "parallel"), blocks),
        name="s5_scan",
    )(s, s, apw, apw)


def _s5_out_kernel(u_ref, xre_ref, xim_ref, wi_ref, wcre_ref, wcim_ref, y_ref, lhs_ref, *, r, tn):
    j = pl.program_id(2)
    pieces = u_ref.shape[0]

    @pl.when(j == 0)
    def _():
        _fold_time_blocks(u_ref, lhs_ref, r)

    y = _dot(lhs_ref[...], wi_ref[...])
    nt = (((1,), (1,)), ((), ()))
    y = y + lax.dot_general(xre_ref[...], wcre_ref[...], nt, preferred_element_type=F32)
    y = y + lax.dot_general(xim_ref[...], wcim_ref[...], nt, preferred_element_type=F32)

    per = tn // (pieces * LANES)
    for jj in range(T_BLK // per):
        @pl.when(j == jj)
        def _():
            for k in range(per):
                for h in range(pieces):
                    col = (k * pieces + h) * LANES
                    y_ref[h, pl.ds(jj * per + k, r, stride=T_BLK), :] = y[:, col:col + LANES]


def _s5_out(u, x_re, x_im, w_i, w_cre, w_cim, r, tn):
    n_cb, kw, _ = w_i.shape
    n = u.shape[1]
    nb = n // T_BLK
    ppc = u.shape[0] // n_cb
    kx = x_re.shape[-1]
    assert tn % (ppc * LANES) == 0 and kw % tn == 0
    blocks = (2 * _nbytes((ppc, T_BLK * r, LANES), F32) + 2 * _nbytes((r, kx), BF16) + _nbytes((kw, tn), BF16)
              + 2 * _nbytes((kx, tn), BF16) + _nbytes((r, tn), F32))
    scratch = _nbytes((r, kw), BF16)
    return pl.pallas_call(
        functools.partial(_s5_out_kernel, r=r, tn=tn),
        out_shape=jax.ShapeDtypeStruct(u.shape, F32),
        grid=(n_cb, nb // r, kw // tn),
        in_specs=[pl.BlockSpec((ppc, T_BLK * r, LANES), lambda c, i, j: (c, i, 0)),
                  pl.BlockSpec((None, r, kx), lambda c, i, j: (c, i, 0)),
                  pl.BlockSpec((None, r, kx), lambda c, i, j: (c, i, 0)),
                  pl.BlockSpec((None, kw, tn), lambda c, i, j: (c, 0, j)),
                  pl.BlockSpec((None, tn, kx), lambda c, i, j: (c, j, 0)),
                  pl.BlockSpec((None, tn, kx), lambda c, i, j: (c, j, 0))],
        out_specs=pl.BlockSpec((ppc, T_BLK * r, LANES), lambda c, i, j: (c, i, 0)),
        scratch_shapes=[pltpu.VMEM((r, kw), BF16)],
        compiler_params=_compiler_params(("parallel", "parallel", "arbitrary"), blocks, scratch),
        name="s5_out",
    )(u, x_re, x_im, w_i, w_cre, w_cim)


def _glu_merge_kernel(y_ref, u_ref, d_ref, w1_ref, w2_ref, pa_ref, gb_ref, m_ref, s_ref):
    @pl.when(pl.program_id(1) == 0)
    def _():
        for c in range(y_ref.shape[0]):
            s = jax.nn.gelu(y_ref[c] + d_ref[c] * u_ref[c])
            s_ref[:, c * LANES:(c + 1) * LANES] = s.astype(s_ref.dtype)

    a = s_ref[...]
    y_b = _dot(a, w1_ref[...]) * _sigmoid(_dot(a, w2_ref[...]))
    m_ref[...] = (pa_ref[...].astype(F32) + gb_ref[...].astype(F32) * y_b).astype(m_ref.dtype)


def _glu_merge(y, u, d_skip, w_glu, layer, pa, gates, d_model, tm, tn):
    n_pc, n, _ = y.shape
    k = n_pc * LANES
    nb = d_model // tn
    blocks = (2 * _nbytes((n_pc, tm, LANES), F32) + 2 * _nbytes((k, tn), BF16) + 3 * _nbytes((tm, tn), BF16))
    scratch = _nbytes((tm, k), BF16)
    return pl.pallas_call(
        _glu_merge_kernel,
        out_shape=jax.ShapeDtypeStruct((n, d_model), BF16),
        grid=(n // tm, nb),
        in_specs=[pl.BlockSpec((n_pc, tm, LANES), lambda i, j: (0, i, 0)),
                  pl.BlockSpec((n_pc, tm, LANES), lambda i, j: (0, i, 0)),
                  pl.BlockSpec((n_pc, 1, LANES), lambda i, j: (0, 0, 0)),
                  _w_spec(k, tn, layer, 0),
                  _w_spec(k, tn, layer, nb),
                  pl.BlockSpec((tm, tn), lambda i, j: (i, j)),
                  pl.BlockSpec((tm, tn), lambda i, j: (i, nb + j))],
        out_specs=pl.BlockSpec((tm, tn), lambda i, j: (i, j)),
        scratch_shapes=[pltpu.VMEM((tm, k), BF16)],
        compiler_params=_compiler_params(("parallel", "arbitrary"), blocks, scratch),
        name="glu_merge",
    )(y, u, d_skip, w_glu, w_glu, pa, gates)


def _ffn_up_kernel(h_ref, hp_ref, hn_ref, wg_ref, wv_ref, cg_ref, cv_ref, act_ref, lhs_ref, *, tm):
    i = pl.program_id(0)
    j = pl.program_id(1)
    last = pl.num_programs(0) - 1
    halo = BF16_ROWS

    @pl.when(j == 0)
    def _():
        row = lax.broadcasted_iota(jnp.int32, hp_ref.shape, 0)
        after = jnp.where(jnp.logical_and(row == 0, i < last), hn_ref[...].astype(F32), 0.0)
        edge = jnp.where(jnp.logical_and(row == halo - 1, i > 0), hp_ref[...].astype(F32), after)
        lhs_ref[0:tm, :] = h_ref[...]
        lhs_ref[tm:, :] = edge.astype(lhs_ref.dtype)

    a = lhs_ref[...]
    rows = tm + halo

    def conv(w_ref, cw_ref):
        u = _dot(a, w_ref[...])
        cw = cw_ref[...]
        up = pltpu.roll(u, 1, 0)[0:tm]
        dn = pltpu.roll(u, rows - 1, 0)[0:tm]
        return up * cw[0:1, :] + u[0:tm] * cw[1:2, :] + dn * cw[2:3, :]

    act_ref[...] = (jax.nn.gelu(conv(wg_ref, cg_ref)) * conv(wv_ref, cv_ref)).astype(act_ref.dtype)


def _ffn_up(hn, w_up, layer, conv_w, d_ff, tm, tn, cast=None):
    n, k = hn.shape
    nb = d_ff // tn
    hb = tm // BF16_ROWS
    n_hb = n // BF16_ROWS
    blocks = (_nbytes((tm + 2 * BF16_ROWS, k), BF16) + 2 * _nbytes((k, tn), BF16) + 2 * _nbytes((3, tn), F32)
              + _nbytes((tm, tn), BF16))
    scratch = _nbytes((tm + BF16_ROWS, k), BF16)
    outs = _call_with_cast(
        functools.partial(_ffn_up_kernel, tm=tm), cast,
        grid=(n // tm, nb),
        in_specs=[pl.BlockSpec((tm, k), lambda i, j: (i, 0)),
                  pl.BlockSpec((BF16_ROWS, k), lambda i, j: (jnp.maximum(i * hb - 1, 0), 0)),
                  pl.BlockSpec((BF16_ROWS, k), lambda i, j: (jnp.minimum((i + 1) * hb, n_hb - 1), 0)),
                  _w_spec(k, tn, layer, 0),
                  _w_spec(k, tn, layer, nb),
                  pl.BlockSpec((3, tn), lambda i, j: (0, j)),
                  pl.BlockSpec((3, tn), lambda i, j: (0, nb + j))],
        out_specs=[pl.BlockSpec((tm, tn), lambda i, j: (i, j))],
        out_shape=[jax.ShapeDtypeStruct((n, d_ff), BF16)],
        operands=(hn, hn, hn, w_up, w_up, conv_w, conv_w),
        scratch_shapes=[pltpu.VMEM((tm + BF16_ROWS, k), BF16)],
        block_bytes=blocks, scratch_bytes=scratch, name="ffn_up")
    return outs[0] if cast is None else outs


def kernel(x, meta_tokens, norm_mix, w_in, conv_a_w, w_a, ssm_lambda_re, ssm_lambda_im, ssm_log_step,
           ssm_b_re, ssm_b_im, ssm_c_re, ssm_c_im, ssm_d, w_glu, w_out, norm_ffn, w_up, conv_ffn_w,
           w_down, norm_final):
    bsz, seq, d_model = x.shape
    n_meta = meta_tokens.shape[0]
    depth = w_in.shape[0]
    d_conv = w_a.shape[1]
    d_ssm = ssm_d.shape[1]
    d_ff = w_down.shape[1]
    n_groups, p_state, h_grp = ssm_b_re.shape[2:]
    assert w_in.shape[2] == 3 * d_conv + d_ssm + 2 * d_model and n_groups * h_grp == d_ssm

    cbw = min(S5_CB_WIDTH, d_ssm)
    n_cb = d_ssm // cbw
    gl = cbw // h_grp
    assert n_cb * cbw == d_ssm and gl * h_grp == cbw

    n_true = n_meta + seq
    row_align = T_BLK * BF16_ROWS
    n_pad = _round_up(n_true, PAD_ROWS if n_true >= 4 * PAD_ROWS else row_align)
    assert n_pad % row_align == 0
    nb = n_pad // T_BLK

    tm = _pick_tile(n_pad, TM_TARGET, BF16_ROWS)
    tm_up = _pick_tile(n_pad, TM_FFN_UP_TARGET, BF16_ROWS)
    tm_half = _pick_tile(n_pad, TM_HALF_TARGET, BF16_ROWS)
    tm_ew = _pick_tile(n_pad, TM_EW_TARGET, BF16_ROWS)
    r_s5 = _pick_tile(nb, R_S5_TARGET, BF16_ROWS)
    tn = lambda n, target=512: _pick_tile(n, target, 128)

    prep = _s5_prep(ssm_lambda_re, ssm_lambda_im, ssm_log_step, ssm_b_re, ssm_b_im, ssm_c_re, ssm_c_im, gl)
    d_skip = ssm_d.astype(F32).reshape(depth, d_ssm // LANES, 1, LANES)

    residual = functools.partial(_epi_residual, n_valid=n_true)
    tile_spec = lambda t, rows=tm: pl.BlockSpec((rows, t), lambda i, j: (i, j))
    u_off = 3 * d_conv
    gate_off = u_off + d_ssm

    w_a_b, w_glu_b, w_out_b = (w.astype(BF16) for w in (w_a, w_glu, w_out))
    w_in_first = w_in[0].astype(BF16)

    outs = []
    for b in range(bsz):
        w_in_l = w_in_first
        for l in range(depth):
            w_s, w_cre, w_cim, w_i, apw = _s5_weights(prep, l, n_cb, gl, h_grp, p_state)

            if l == 0:
                h_res, hn = _assemble_norm(meta_tokens, x[b], norm_mix[0], n_pad, tm_ew)
            else:
                hn = _rmsnorm(h_res, norm_mix[l], BF16, tm_ew)
            bg, cv, u, w_down_l = _inproj_conv(hn, w_in_l, None, d_conv, d_ssm, tm_half, tn(d_conv),
                                               cast=(w_down, l))
            t_g = tn(2 * d_model)
            gates, w_up_l = _mm(hn, w_in_l, None, col0=gate_off, n_cols=2 * d_model, tm=tm, tn=t_g,
                                epilogue=_epi_sigmoid,
                                out_shape=jax.ShapeDtypeStruct((n_pad, 2 * d_model), BF16),
                                out_spec=tile_spec(t_g), name="inproj_gates", cast=(w_up, l))

            pa = _branch_a(bg, cv, conv_a_w[l].astype(F32), w_a_b, l, gates, d_model, tm, tn(d_model, 1024))
            t_d = tn(d_model)

            states = _s5_states(u, w_s, _pick_tile(nb, R_S5_STATES_TARGET, BF16_ROWS), tn(w_s.shape[-1]))
            x_re, x_im = _s5_scan(states, apw, tn(gl * p_state, SCAN_LANES_TARGET))
            y = _s5_out(u, x_re, x_im, w_i, w_cre, w_cim, r_s5, tn(T_BLK * cbw))

            merged = _glu_merge(y, u, d_skip[l], w_glu_b, l, pa, gates, d_model, tm_half, tn(d_model, 1024))
            h_res = _mm(merged, w_out_b, l, col0=0, n_cols=d_model, tm=tm, tn=t_d, epilogue=residual,
                        extras=(h_res,), extra_specs=(tile_spec(t_d),),
                        out_shape=jax.ShapeDtypeStruct((n_pad, d_model), F32), out_spec=tile_spec(t_d),
                        name="out_proj")

            hn = _rmsnorm(h_res, norm_ffn[l], BF16, tm_ew)
            if l + 1 < depth:
                act, w_in_l = _ffn_up(hn, w_up_l, None, conv_ffn_w[l].astype(F32), d_ff, tm_up, tn(d_ff),
                                      cast=(w_in, l + 1))
            else:
                act = _ffn_up(hn, w_up_l, None, conv_ffn_w[l].astype(F32), d_ff, tm_up, tn(d_ff))
            t_o = tn(d_model, 256)
            h_res = _mm(act, w_down_l, None, col0=0, n_cols=d_model, tm=tm_half, tn=t_o, epilogue=residual,
                        extras=(h_res,), extra_specs=(tile_spec(t_o, tm_half),),
                        out_shape=jax.ShapeDtypeStruct((n_pad, d_model), F32), out_spec=tile_spec(t_o, tm_half),
                        name="ffn_down")

        outs.append(_final_norm(h_res, norm_final, n_meta, seq, x.dtype, _pick_tile(seq, TM_EW_TARGET, n_meta)))
    return jnp.stack(outs, axis=0)
```

```python
import functools

import numpy as np
import jax
import jax.numpy as jnp
from jax import lax
from jax.experimental import pallas as pl
from jax.experimental.pallas import tpu as pltpu

F32 = jnp.float32
BF16 = jnp.bfloat16
EPS = 1e-6

T_BLK = 16
S5_CB_WIDTH = 128
PAD_ROWS = 1280
BF16_ROWS = 16
LANES = 128
VMEM_CAP_BYTES = 60000 * 1024
VMEM_TEMP_BYTES = 20 * 1024 * 1024

TM_TARGET = 1280
TM_FFN_UP_TARGET = 832
TM_HALF_TARGET = 640
R_S5_STATES_TARGET = 1040
TM_EW_TARGET = 320
R_S5_TARGET = 1040
SCAN_LANES_TARGET = 256


def _round_up(n, m):
    return (n + m - 1) // m * m


def _pick_tile(n, target, align):
    best = 0
    for t in range(align, min(n, target) + 1, align):
        if n % t == 0:
            best = t
    assert best > 0, (n, target, align)
    return best


def _nbytes(shape, dtype):
    return int(np.prod(shape)) * jnp.dtype(dtype).itemsize


def _compiler_params(semantics, block_bytes, scratch_bytes=0):
    est = 2 * block_bytes + scratch_bytes + VMEM_TEMP_BYTES
    return pltpu.CompilerParams(dimension_semantics=semantics,
                                vmem_limit_bytes=int(min(VMEM_CAP_BYTES, est)))


def _dot(a, b):
    return jnp.dot(a, b, preferred_element_type=F32)


def _sigmoid(x):
    return 0.5 * (1.0 + jnp.tanh(0.5 * x))


def _rmsnorm_kernel(x_ref, g_ref, o_ref):
    x = x_ref[...]
    ms = jnp.mean(x * x, axis=-1, keepdims=True)
    o_ref[...] = (x * lax.rsqrt(ms + EPS) * g_ref[...]).astype(o_ref.dtype)


def _rmsnorm(x, g, out_dtype, tm):
    n, d = x.shape
    blocks = _nbytes((tm, d), F32) + _nbytes((tm, d), out_dtype)
    return pl.pallas_call(
        _rmsnorm_kernel,
        out_shape=jax.ShapeDtypeStruct((n, d), out_dtype),
        grid=(n // tm,),
        in_specs=[pl.BlockSpec((tm, d), lambda i: (i, 0)),
                  pl.BlockSpec((1, d), lambda i: (0, 0))],
        out_specs=pl.BlockSpec((tm, d), lambda i: (i, 0)),
        compiler_params=_compiler_params(("parallel",), blocks),
        name="rmsnorm",
    )(x, g.reshape(1, d).astype(F32))


def _assemble_norm_kernel(meta_ref, prev_ref, x_ref, g_ref, h_ref, hn_ref, *, n_valid):
    i = pl.program_id(0)
    tm = h_ref.shape[0]
    n_meta = meta_ref.shape[0]
    head = jnp.where(i == 0, meta_ref[...], prev_ref[...])
    tile = jnp.concatenate([head, x_ref[0:tm - n_meta, :]], axis=0)
    rows = i * tm + lax.broadcasted_iota(jnp.int32, tile.shape, 0)
    tile = jnp.where(rows < n_valid, tile, 0.0)
    h_ref[...] = tile
    ms = jnp.mean(tile * tile, axis=-1, keepdims=True)
    hn_ref[...] = (tile * lax.rsqrt(ms + EPS) * g_ref[...]).astype(hn_ref.dtype)


def _assemble_norm(meta, x, g, n_pad, tm):
    n_meta, d = meta.shape
    seq = x.shape[0]
    assert n_meta % 8 == 0 and tm % n_meta == 0 and seq % n_meta == 0 and n_pad % tm == 0
    per = tm // n_meta
    last_x = pl.cdiv(seq, tm) - 1
    last_prev = seq // n_meta - 1
    blocks = _nbytes((tm + 2 * n_meta, d), F32) + _nbytes((tm, d), F32) + _nbytes((tm, d), BF16)
    return pl.pallas_call(
        functools.partial(_assemble_norm_kernel, n_valid=n_meta + seq),
        out_shape=(jax.ShapeDtypeStruct((n_pad, d), F32), jax.ShapeDtypeStruct((n_pad, d), BF16)),
        grid=(n_pad // tm,),
        in_specs=[pl.BlockSpec((n_meta, d), lambda i: (0, 0)),
                  pl.BlockSpec((n_meta, d), lambda i: (jnp.clip(i * per - 1, 0, last_prev), 0)),
                  pl.BlockSpec((tm, d), lambda i: (jnp.minimum(i, last_x), 0)),
                  pl.BlockSpec((1, d), lambda i: (0, 0))],
        out_specs=(pl.BlockSpec((tm, d), lambda i: (i, 0)), pl.BlockSpec((tm, d), lambda i: (i, 0))),
        compiler_params=_compiler_params(("parallel",), blocks),
        name="assemble_norm",
    )(meta.astype(F32), x.astype(F32), x.astype(F32), g.reshape(1, d).astype(F32))


def _final_norm_kernel(a_ref, b_ref, g_ref, o_ref, *, skip):
    x = jnp.concatenate([a_ref[skip:, :], b_ref[...]], axis=0)
    ms = jnp.mean(x * x, axis=-1, keepdims=True)
    o_ref[...] = (x * lax.rsqrt(ms + EPS) * g_ref[...]).astype(o_ref.dtype)


def _final_norm(x, g, skip, n_out, out_dtype, tm):
    n, d = x.shape
    assert skip % 8 == 0 and tm % skip == 0 and n_out % tm == 0 and n >= n_out + skip
    per = tm // skip
    blocks = _nbytes((tm + skip, d), F32) + _nbytes((tm, d), out_dtype)
    return pl.pallas_call(
        functools.partial(_final_norm_kernel, skip=skip),
        out_shape=jax.ShapeDtypeStruct((n_out, d), out_dtype),
        grid=(n_out // tm,),
        in_specs=[pl.BlockSpec((tm, d), lambda i: (i, 0)),
                  pl.BlockSpec((skip, d), lambda i: ((i + 1) * per, 0)),
                  pl.BlockSpec((1, d), lambda i: (0, 0))],
        out_specs=pl.BlockSpec((tm, d), lambda i: (i, 0)),
        compiler_params=_compiler_params(("parallel",), blocks),
        name="final_norm",
    )(x, x, g.reshape(1, d).astype(F32))


def _mm_kernel(*refs, n_extra, epilogue, tm):
    a_ref, w_ref = refs[0], refs[1]
    extras = refs[2:2 + n_extra]
    o_ref = refs[2 + n_extra]
    acc = _dot(a_ref[...], w_ref[...])
    row0 = pl.program_id(0) * tm
    o_ref[...] = epilogue(acc, row0, *[e[...] for e in extras]).astype(o_ref.dtype)


def _w_spec(k, tn, layer, jb0):
    if layer is None:
        return pl.BlockSpec((k, tn), lambda i, j: (0, jb0 + j))
    return pl.BlockSpec((None, k, tn), lambda i, j: (layer, 0, jb0 + j))


def _call_with_cast(kernel_fn, cast, *, grid, in_specs, out_specs, out_shape, operands, scratch_shapes=(),
                    block_bytes, scratch_bytes=0, name):
    if cast is None:
        return pl.pallas_call(
            kernel_fn, out_shape=tuple(out_shape), grid=grid, in_specs=list(in_specs), out_specs=tuple(out_specs),
            scratch_shapes=list(scratch_shapes),
            compiler_params=_compiler_params(("parallel", "arbitrary"), block_bytes, scratch_bytes), name=name,
        )(*operands)

    stack, layer = cast
    _, k, n = stack.shape
    n_steps = grid[0] * grid[1]
    options = []
    for rows in range(BF16_ROWS, k + 1, BF16_ROWS):
        if k % rows == 0 and k // rows <= n_steps:
            splits = max(c for c in range(1, n_steps // (k // rows) + 1) if n % (c * LANES) == 0)
            options.append((rows * (n // splits), rows, splits))
    _, rows, splits = min(options)
    cols = n // splits
    last_chunk = (k // rows) * splits - 1
    n_in, n_out = len(in_specs), len(out_specs)

    def chunk(i, j):
        q = jnp.minimum(i * grid[1] + j, last_chunk)
        return q // splits, q % splits

    def body(*refs):
        src_ref, dst_ref = refs[n_in], refs[n_in + 1 + n_out]
        dst_ref[...] = src_ref[...].astype(dst_ref.dtype)
        kernel_fn(*refs[:n_in], *refs[n_in + 1:n_in + 1 + n_out], *refs[n_in + 2 + n_out:])

    cast_bytes = _nbytes((rows, cols), F32) + _nbytes((rows, cols), BF16)
    return pl.pallas_call(
        body,
        out_shape=(*out_shape, jax.ShapeDtypeStruct((k, n), BF16)),
        grid=grid,
        in_specs=[*in_specs, pl.BlockSpec((None, rows, cols), lambda i, j: (layer, *chunk(i, j)))],
        out_specs=(*out_specs, pl.BlockSpec((rows, cols), lambda i, j: chunk(i, j))),
        scratch_shapes=list(scratch_shapes),
        compiler_params=_compiler_params(("arbitrary", "arbitrary"), block_bytes + cast_bytes, scratch_bytes),
        name=name,
    )(*operands, stack)


def _mm(a, w, layer, *, col0, n_cols, tm, tn, epilogue, extras=(), extra_specs=(), out_shape, out_spec, name,
        cast=None):
    n, k = a.shape
    assert n % tm == 0 and n_cols % tn == 0 and col0 % tn == 0
    blocks = (_nbytes((tm, k), a.dtype) + _nbytes((k, tn), w.dtype) + _nbytes((tm, tn), F32)
              + sum(_nbytes((tm, tn), e.dtype) for e in extras))
    outs = _call_with_cast(
        functools.partial(_mm_kernel, n_extra=len(extras), epilogue=epilogue, tm=tm), cast,
        grid=(n // tm, n_cols // tn),
        in_specs=[pl.BlockSpec((tm, k), lambda i, j: (i, 0)),
                  _w_spec(k, tn, layer, col0 // tn),
                  *extra_specs],
        out_specs=[out_spec], out_shape=[out_shape], operands=(a, w, *extras), block_bytes=blocks, name=name)
    return outs[0] if cast is None else outs


def _epi_pieces(acc, row0):
    return jnp.stack([acc[:, h * LANES:(h + 1) * LANES] for h in range(acc.shape[1] // LANES)], axis=0)


def _epi_sigmoid(acc, row0):
    return _sigmoid(acc)


def _epi_residual(acc, row0, res, *, n_valid):
    rows = row0 + lax.broadcasted_iota(jnp.int32, acc.shape, 0)
    return jnp.where(rows < n_valid, res + acc, 0.0)


def _inproj_conv_kernel(a_ref, wb_ref, wc_ref, wv_ref, wu_ref, bg_ref, cv_ref, u_ref):
    a = a_ref[...]
    bg_ref[...] = _dot(a, wb_ref[...]).astype(bg_ref.dtype)
    cv_ref[...] = (_dot(a, wc_ref[...]) * _dot(a, wv_ref[...])).astype(cv_ref.dtype)
    u_ref[...] = _epi_pieces(_dot(a, wu_ref[...]), 0)


def _inproj_conv(hn, w_in, layer, d_conv, d_ssm, tm, tn, cast=None):
    n, k = hn.shape
    nb = d_conv // tn
    tn_u = d_ssm // nb
    assert tn_u % LANES == 0 and (3 * d_conv) % tn_u == 0
    blocks = (_nbytes((tm, k), BF16) + 3 * _nbytes((k, tn), BF16) + _nbytes((k, tn_u), BF16)
              + 2 * _nbytes((tm, tn), BF16) + _nbytes((tm, tn_u), F32))
    out = jax.ShapeDtypeStruct((n, d_conv), BF16)
    return _call_with_cast(
        _inproj_conv_kernel, cast,
        grid=(n // tm, nb),
        in_specs=[pl.BlockSpec((tm, k), lambda i, j: (i, 0)),
                  _w_spec(k, tn, layer, 0),
                  _w_spec(k, tn, layer, nb),
                  _w_spec(k, tn, layer, 2 * nb),
                  _w_spec(k, tn_u, layer, 3 * d_conv // tn_u)],
        out_specs=[pl.BlockSpec((tm, tn), lambda i, j: (i, j)),
                   pl.BlockSpec((tm, tn), lambda i, j: (i, j)),
                   pl.BlockSpec((tn_u // LANES, tm, LANES), lambda i, j: (j, i, 0))],
        out_shape=[out, out, jax.ShapeDtypeStruct((d_ssm // LANES, n, LANES), F32)],
        operands=(hn, w_in, w_in, w_in, w_in), block_bytes=blocks, name="inproj_conv")


def _shifted_rows(x, prev_row, next_row):
    t = x.shape[0]
    row = lax.broadcasted_iota(jnp.int32, x.shape, 0)
    up = jnp.where(row == 0, prev_row, pltpu.roll(x, 1, 0))
    dn = jnp.where(row == t - 1, next_row, pltpu.roll(x, t - 1, 0))
    return up, dn


def _branch_a_kernel(bg_ref, cv_ref, cvp_ref, cvn_ref, cw_ref, w_ref, gate_ref, o_ref, z_ref, *, chunk):
    i = pl.program_id(0)
    last = pl.num_programs(0) - 1

    @pl.when(pl.program_id(1) == 0)
    def _():
        for c0 in range(0, z_ref.shape[1], chunk):
            cols = slice(c0, c0 + chunk)
            x = cv_ref[:, cols].astype(F32)
            prev = cvp_ref[:, cols].astype(F32)[BF16_ROWS - 1:BF16_ROWS, :]
            nxt = cvn_ref[:, cols].astype(F32)[0:1, :]
            prev = jnp.where(i > 0, prev, 0.0)
            nxt = jnp.where(i < last, nxt, 0.0)
            up, dn = _shifted_rows(x, prev, nxt)
            w = cw_ref[:, cols]
            conv = up * w[0:1, :] + x * w[1:2, :] + dn * w[2:3, :]
            z_ref[:, cols] = (bg_ref[:, cols].astype(F32) * conv).astype(z_ref.dtype)

    o_ref[...] = (_dot(z_ref[...], w_ref[...]) * gate_ref[...].astype(F32)).astype(o_ref.dtype)


def _branch_a(bg, cv, conv_w, w_a, layer, gates, d_model, tm, tn):
    n, c = bg.shape
    hb = tm // BF16_ROWS
    n_hb = n // BF16_ROWS
    blocks = (2 * _nbytes((tm, c), BF16) + 2 * _nbytes((BF16_ROWS, c), BF16) + _nbytes((8, c), F32)
              + _nbytes((c, tn), BF16) + 2 * _nbytes((tm, tn), BF16))
    scratch = _nbytes((tm, c), BF16)
    return pl.pallas_call(
        functools.partial(_branch_a_kernel, chunk=_pick_tile(c, 512, LANES)),
        out_shape=jax.ShapeDtypeStruct((n, d_model), BF16),
        grid=(n // tm, d_model // tn),
        in_specs=[pl.BlockSpec((tm, c), lambda i, j: (i, 0)),
                  pl.BlockSpec((tm, c), lambda i, j: (i, 0)),
                  pl.BlockSpec((BF16_ROWS, c), lambda i, j: (jnp.maximum(i * hb - 1, 0), 0)),
                  pl.BlockSpec((BF16_ROWS, c), lambda i, j: (jnp.minimum((i + 1) * hb, n_hb - 1), 0)),
                  pl.BlockSpec((3, c), lambda i, j: (0, 0)),
                  _w_spec(c, tn, layer, 0),
                  pl.BlockSpec((tm, tn), lambda i, j: (i, j))],
        out_specs=pl.BlockSpec((tm, tn), lambda i, j: (i, j)),
        scratch_shapes=[pltpu.VMEM((tm, c), BF16)],
        compiler_params=_compiler_params(("parallel", "arbitrary"), blocks, scratch),
        name="branch_a",
    )(bg, cv, cv, cv, conv_w, w_a, gates)


def _cmul(ar, ai, br, bi):
    return ar * br - ai * bi, ar * bi + ai * br


def _s5_prep_kernel(lre_ref, lim_ref, lst_ref, bre_ref, bim_ref, cre_ref, cim_ref,
                    ere_ref, eim_ref, care_ref, caim_ref, klag_ref, apre_ref, apim_ref):
    lam_re, lam_im = lre_ref[...], lim_ref[...]
    dt = jnp.exp(lst_ref[...])
    mag = jnp.exp(lam_re * dt)
    a_re = mag * jnp.cos(lam_im * dt)
    a_im = mag * jnp.sin(lam_im * dt)
    nr, ni = a_re - 1.0, a_im
    den = lam_re * lam_re + lam_im * lam_im
    f_re = (nr * lam_re + ni * lam_im) / den
    f_im = (ni * lam_re - nr * lam_im) / den
    b_re, b_im = bre_ref[...], bim_ref[...]
    bb_re, bb_im = _cmul(f_re, f_im, b_re, b_im)
    c_re, c_im = cre_ref[...], cim_ref[...]

    pw = [(jnp.ones_like(a_re), jnp.zeros_like(a_im))]
    for _ in range(T_BLK):
        pw.append(_cmul(pw[-1][0], pw[-1][1], a_re, a_im))

    e_re, e_im, ck_re, ck_im = [], [], [], []
    for k in range(T_BLK + 1):
        cr, ci = _cmul(pw[k][0], pw[k][1], c_re, c_im)
        ck_re.append(cr)
        ck_im.append(ci)
    for k in range(T_BLK):
        er, ei = _cmul(pw[k][0], pw[k][1], bb_re, bb_im)
        e_re.append(er)
        e_im.append(ei)
    backward = pl.program_id(1) == 1
    n_blocks, _, gl = ere_ref.shape[:3]
    for k in range(T_BLK):
        e_step = jnp.where(backward, k, T_BLK - 1 - k)
        c_step = jnp.where(backward, T_BLK - 1 - k, k)
        for c in range(n_blocks):
            grp = slice(c * gl, (c + 1) * gl)
            ere_ref[c, e_step] = e_re[k][grp]
            eim_ref[c, e_step] = e_im[k][grp]
            care_ref[c, c_step] = ck_re[k + 1][grp]
            caim_ref[c, c_step] = -ck_im[k + 1][grp]

    dn = (((2,), (2,)), ((0,), (0,)))
    k_t = (lax.dot_general(bb_re, jnp.concatenate(ck_re[:-1], axis=1), dn,
                           precision=lax.Precision.HIGHEST, preferred_element_type=F32)
           - lax.dot_general(bb_im, jnp.concatenate(ck_im[:-1], axis=1), dn,
                             precision=lax.Precision.HIGHEST, preferred_element_type=F32))
    n_grp, h_out, th = k_t.shape
    h_in = th // T_BLK
    k_t = k_t.reshape(n_grp * h_out, th)
    src = lax.broadcasted_iota(jnp.int32, (th, th), 0)
    dst = lax.broadcasted_iota(jnp.int32, (th, th), 1)
    assert h_in & (h_in - 1) == 0, "group width must be a power of two"
    shift = h_in.bit_length() - 1
    same_h = (src & (h_in - 1)) == (dst & (h_in - 1))
    lag, tau = src >> shift, dst >> shift
    for sigma in range(T_BLK):
        wanted = jnp.where(backward, sigma - lag, sigma + lag)
        place = jnp.where(jnp.logical_and(same_h, tau == wanted), 1.0, 0.0)
        block = jnp.dot(k_t, place, precision=lax.Precision.HIGHEST, preferred_element_type=F32)
        klag_ref[:, sigma, :, :] = block.reshape(n_grp, h_out, th)

    q_re, q_im = pw[T_BLK]
    r_re, r_im = q_re, q_im
    for m in range(8):
        apre_ref[m] = r_re
        apim_ref[m] = r_im
        r_re, r_im = _cmul(r_re, r_im, q_re, q_im)


def _s5_prep(lam_re, lam_im, log_step, b_re, b_im, c_re, c_im, gl):
    depth, _, g, p = lam_re.shape
    h = b_re.shape[-1]
    lead = (depth, 2)
    b_re_t = jnp.swapaxes(b_re, -1, -2).astype(F32)
    b_im_t = jnp.swapaxes(b_im, -1, -2).astype(F32)

    gc = _pick_tile(g, max(16, gl), gl)

    def spec(*tail):
        zeros = (0,) * (len(tail) - 1)
        return pl.BlockSpec((None, None) + tail, lambda l, d, q: (l, d, q) + zeros)

    ap_spec = pl.BlockSpec((None, None, 8, gc, 1, p), lambda l, d, q: (l, d, 0, q, 0, 0))
    pw_spec = pl.BlockSpec((None, None, gc // gl, T_BLK, gl, h, p), lambda l, d, q: (l, d, q, 0, 0, 0, 0))
    th = T_BLK * h
    outs = (jax.ShapeDtypeStruct(lead + (g // gl, T_BLK, gl, h, p), F32),) * 4 + (
        jax.ShapeDtypeStruct(lead + (g, T_BLK, h, th), F32),
        jax.ShapeDtypeStruct(lead + (8, g, 1, p), F32),
        jax.ShapeDtypeStruct(lead + (8, g, 1, p), F32))
    blocks = 4 * _nbytes((gc, th, LANES), F32) + 8 * _nbytes((gc, h, LANES), F32)
    return pl.pallas_call(
        _s5_prep_kernel,
        out_shape=outs,
        grid=lead + (g // gc,),
        in_specs=[spec(gc, 1, p), spec(gc, 1, p), spec(gc, 1, 1), spec(gc, h, p), spec(gc, h, p),
                  spec(gc, h, p), spec(gc, h, p)],
        out_specs=(pw_spec,) * 4 + (spec(gc, T_BLK, h, th), ap_spec, ap_spec),
        compiler_params=_compiler_params(("parallel", "parallel", "parallel"), blocks,
                                         16 * _nbytes((gc, th, LANES), F32)),
        name="s5_prep",
    )(lam_re.astype(F32)[:, :, :, None, :], lam_im.astype(F32)[:, :, :, None, :],
      log_step.astype(F32)[..., None, None], b_re_t, b_im_t, c_re.astype(F32), c_im.astype(F32))


def _block_diag_kernel(*refs, n_parts):
    c_refs = refs[:n_parts]
    e_ref, rg_ref, cg_ref, o_ref = refs[n_parts:]
    same_group = rg_ref[...] == cg_ref[...]
    width = e_ref.shape[1]
    for q, c_ref in enumerate(c_refs):
        spread = _dot(c_ref[...].astype(BF16), e_ref[...])
        o_ref[:, q * width:(q + 1) * width] = jnp.where(same_group, spread, 0.0).astype(o_ref.dtype)


def _block_diag(compacts, gl, row_inner, col_inner, name):
    compacts = [c if isinstance(c, tuple) else (c, ()) for c in compacts]
    n_cb, rows, kc = compacts[0][0].shape[-3:]
    width = kc * gl
    tr = _pick_tile(rows, 2048, BF16_ROWS)
    col = np.arange(width)
    spread = np.arange(kc)[:, None] == (col // (gl * col_inner)) * col_inner + col % col_inner
    row_group = ((np.arange(rows) // row_inner) % gl).astype(np.int32)[:, None]
    col_group = ((col // col_inner) % gl).astype(np.int32)[None, :]
    n_parts = len(compacts)
    blocks = (n_parts * _nbytes((tr, LANES), F32) + _nbytes((kc, width), BF16) + _nbytes((tr, LANES), jnp.int32)
              + _nbytes((8, width), jnp.int32) + _nbytes((tr, n_parts * width), BF16))
    return pl.pallas_call(
        functools.partial(_block_diag_kernel, n_parts=n_parts),
        out_shape=jax.ShapeDtypeStruct((n_cb, rows, n_parts * width), BF16),
        grid=(n_cb, rows // tr),
        in_specs=[pl.BlockSpec((None,) * (len(lead) + 1) + (tr, kc), lambda c, i, lead=lead: (*lead, c, i, 0))
                  for _, lead in compacts]
                 + [pl.BlockSpec((kc, width), lambda c, i: (0, 0)),
                    pl.BlockSpec((tr, 1), lambda c, i: (i, 0)),
                    pl.BlockSpec((1, width), lambda c, i: (0, 0))],
        out_specs=pl.BlockSpec((None, tr, n_parts * width), lambda c, i: (c, i, 0)),
        compiler_params=_compiler_params(("parallel", "parallel"), blocks),
        name=name,
    )(*[arr for arr, _ in compacts], jnp.asarray(spread, BF16), jnp.asarray(row_group), jnp.asarray(col_group))


def _s5_weights(prep, layer, n_cb, gl, h, p):
    e_re, e_im, ca_re, ca_im, klag, ap_re, ap_im = prep
    t = T_BLK
    depth = e_re.shape[0]
    rows = lambda x: x.reshape(depth, 2, n_cb, t * gl * h, p)
    part = lambda x, d: (rows(x), (layer, d))

    w_s = _block_diag([part(e_re, 0), part(e_im, 0), part(e_re, 1), part(e_im, 1)], gl, h, p, "s5_w_states")

    w_cre = _block_diag([part(ca_re, 0), part(ca_re, 1)], gl, h, p, "s5_w_carry_re")
    w_cim = _block_diag([part(ca_im, 0), part(ca_im, 1)], gl, h, p, "s5_w_carry_im")

    kst = (klag[layer, 0] + klag[layer, 1]).reshape(n_cb, gl, t, h, t * h)
    kst = jnp.transpose(kst, (0, 2, 1, 3, 4))
    w_i = _block_diag([kst.reshape(n_cb, t * gl * h, t * h)], gl, h, h, "s5_w_intra")

    ap_re, ap_im = ap_re[layer], ap_im[layer]

    def lanes(ap):
        ap = jnp.stack([ap[0], ap[1, ::-1]], axis=0)
        ap = ap.reshape(2, 8, n_cb, gl * p)
        return jnp.transpose(ap, (2, 0, 1, 3))
    apw = jnp.stack([lanes(ap_re), lanes(ap_im)], axis=2)

    return w_s, w_cre, w_cim, w_i, apw


def _fold_time_blocks(u_ref, lhs_ref, r):
    pieces = u_ref.shape[0]
    for tau in range(T_BLK):
        for h in range(pieces):
            col = (tau * pieces + h) * LANES
            lhs_ref[:, col:col + LANES] = u_ref[h, pl.ds(tau, r, stride=T_BLK), :].astype(lhs_ref.dtype)


def _s5_states_kernel(u_ref, w_ref, s_ref, lhs_ref, *, r):
    @pl.when(pl.program_id(2) == 0)
    def _():
        _fold_time_blocks(u_ref, lhs_ref, r)

    s_ref[...] = _dot(lhs_ref[...], w_ref[...])


def _s5_states(u, w_s, r, tn):
    n_cb, kw, ns = w_s.shape
    n = u.shape[1]
    nb = n // T_BLK
    ppc = u.shape[0] // n_cb
    blocks = _nbytes((ppc, T_BLK * r, LANES), F32) + _nbytes((kw, tn), BF16) + _nbytes((r, tn), F32)
    scratch = _nbytes((r, kw), BF16)
    return pl.pallas_call(
        functools.partial(_s5_states_kernel, r=r),
        out_shape=jax.ShapeDtypeStruct((n_cb, nb, ns), F32),
        grid=(n_cb, nb // r, ns // tn),
        in_specs=[pl.BlockSpec((ppc, T_BLK * r, LANES), lambda c, i, j: (c, i, 0)),
                  pl.BlockSpec((None, kw, tn), lambda c, i, j: (c, 0, j))],
        out_specs=pl.BlockSpec((None, r, tn), lambda c, i, j: (c, i, j)),
        scratch_shapes=[pltpu.VMEM((r, kw), BF16)],
        compiler_params=_compiler_params(("parallel", "parallel", "arbitrary"), blocks, scratch),
        name="s5_states",
    )(u, w_s)


def _s5_scan_kernel(sre_ref, sim_ref, apre_ref, apim_ref, xre_ref, xim_ref, *, n_pairs):
    d = pl.program_id(1)
    lw = sre_ref.shape[-1]
    ap_re, ap_im = apre_ref[...], apim_ref[...]
    row = lax.broadcasted_iota(jnp.int32, (8, lw), 0)

    def run(reverse):
        steps = []
        for shift in (1, 2, 4):
            src = (8 - shift) if reverse else (shift - 1)
            keep = (row < 8 - shift) if reverse else (row >= shift)
            steps.append((8 - shift if reverse else shift,
                          jnp.where(keep, ap_re[src:src + 1, :], 0.0),
                          jnp.where(keep, ap_im[src:src + 1, :], 0.0)))
        edge = 7 if reverse else 0
        out_row = 0 if reverse else 7
        nb_shift = 7 if reverse else 1

        def tile(r0, c_re, c_im):
            x_re = sre_ref[pl.ds(r0, 8), :]
            x_im = sim_ref[pl.ds(r0, 8), :]
            for amount, m_re, m_im in steps:
                s_re = pltpu.roll(x_re, amount, 0)
                s_im = pltpu.roll(x_im, amount, 0)
                x_re, x_im = x_re + m_re * s_re - m_im * s_im, x_im + m_re * s_im + m_im * s_re
            inc_re = x_re + ap_re * c_re - ap_im * c_im
            inc_im = x_im + ap_re * c_im + ap_im * c_re
            e_re = jnp.where(row == edge, c_re, pltpu.roll(inc_re, nb_shift, 0))
            e_im = jnp.where(row == edge, c_im, pltpu.roll(inc_im, nb_shift, 0))
            return e_re, e_im, inc_re[out_row:out_row + 1, :], inc_im[out_row:out_row + 1, :]

        def body(k, carry):
            c_re, c_im = carry
            pair = (n_pairs - 1 - k) if reverse else k
            r0 = pl.multiple_of(pair * BF16_ROWS, BF16_ROWS)
            e_re, e_im = [None, None], [None, None]
            for half in ((1, 0) if reverse else (0, 1)):
                e_re[half], e_im[half], c_re, c_im = tile(pl.multiple_of(r0 + 8 * half, 8), c_re, c_im)
            xre_ref[pl.ds(r0, BF16_ROWS), :] = jnp.concatenate(e_re, axis=0).astype(xre_ref.dtype)
            xim_ref[pl.ds(r0, BF16_ROWS), :] = jnp.concatenate(e_im, axis=0).astype(xim_ref.dtype)
            return c_re, c_im

        zero = jnp.zeros((1, lw), F32)
        lax.fori_loop(0, n_pairs, body, (zero, zero))

    @pl.when(d == 0)
    def _():
        run(False)

    @pl.when(d == 1)
    def _():
        run(True)


def _s5_scan(s, apw, lw):
    n_cb, nb, ns = s.shape
    glp = ns // 4
    nq = glp // lw
    blocks = 2 * _nbytes((nb, lw), F32) + 2 * _nbytes((nb, lw), BF16) + 2 * _nbytes((8, lw), F32)
    out = jax.ShapeDtypeStruct((n_cb, nb, 2 * glp), BF16)

    def s_spec(ri):
        return pl.BlockSpec((None, nb, lw), lambda c, d, q: (c, 0, (2 * d + ri) * nq + q))

    def ap_spec(ri):
        return pl.BlockSpec((None, None, None, 8, lw), lambda c, d, q: (c, d, ri, 0, q))

    x_spec = pl.BlockSpec((None, nb, lw), lambda c, d, q: (c, 0, d * nq + q))
    return pl.pallas_call(
        functools.partial(_s5_scan_kernel, n_pairs=nb // BF16_ROWS),
        out_shape=(out, out),
        grid=(n_cb, 2, nq),
        in_specs=[s_spec(0), s_spec(1), ap_spec(0), ap_spec(1)],
        out_specs=(x_spec, x_spec),
        compiler_params=_compiler_params(("parallel", "parallel", "parallel"), blocks),
        name="s5_scan",
    )(s, s, apw, apw)


def _s5_out_kernel(u_ref, xre_ref, xim_ref, wi_ref, wcre_ref, wcim_ref, y_ref, lhs_ref, *, r, tn):
    j = pl.program_id(2)
    pieces = u_ref.shape[0]

    @pl.when(j == 0)
    def _():
        _fold_time_blocks(u_ref, lhs_ref, r)

    y = _dot(lhs_ref[...], wi_ref[...])
    nt = (((1,), (1,)), ((), ()))
    y = y + lax.dot_general(xre_ref[...], wcre_ref[...], nt, preferred_element_type=F32)
    y = y + lax.dot_general(xim_ref[...], wcim_ref[...], nt, preferred_element_type=F32)

    per = tn // (pieces * LANES)
    for jj in range(T_BLK // per):
        @pl.when(j == jj)
        def _():
            for k in range(per):
                for h in range(pieces):
                    col = (k * pieces + h) * LANES
                    y_ref[h, pl.ds(jj * per + k, r, stride=T_BLK), :] = y[:, col:col + LANES]


def _s5_out(u, x_re, x_im, w_i, w_cre, w_cim, r, tn):
    n_cb, kw, _ = w_i.shape
    n = u.shape[1]
    nb = n // T_BLK
    ppc = u.shape[0] // n_cb
    kx = x_re.shape[-1]
    assert tn % (ppc * LANES) == 0 and kw % tn == 0
    blocks = (2 * _nbytes((ppc, T_BLK * r, LANES), F32) + 2 * _nbytes((r, kx), BF16) + _nbytes((kw, tn), BF16)
              + 2 * _nbytes((kx, tn), BF16) + _nbytes((r, tn), F32))
    scratch = _nbytes((r, kw), BF16)
    return pl.pallas_call(
        functools.partial(_s5_out_kernel, r=r, tn=tn),
        out_shape=jax.ShapeDtypeStruct(u.shape, F32),
        grid=(n_cb, nb // r, kw // tn),
        in_specs=[pl.BlockSpec((ppc, T_BLK * r, LANES), lambda c, i, j: (c, i, 0)),
                  pl.BlockSpec((None, r, kx), lambda c, i, j: (c, i, 0)),
                  pl.BlockSpec((None, r, kx), lambda c, i, j: (c, i, 0)),
                  pl.BlockSpec((None, kw, tn), lambda c, i, j: (c, 0, j)),
                  pl.BlockSpec((None, tn, kx), lambda c, i, j: (c, j, 0)),
                  pl.BlockSpec((None, tn, kx), lambda c, i, j: (c, j, 0))],
        out_specs=pl.BlockSpec((ppc, T_BLK * r, LANES), lambda c, i, j: (c, i, 0)),
        scratch_shapes=[pltpu.VMEM((r, kw), BF16)],
        compiler_params=_compiler_params(("parallel", "parallel", "arbitrary"), blocks, scratch),
        name="s5_out",
    )(u, x_re, x_im, w_i, w_cre, w_cim)


def _glu_merge_kernel(y_ref, u_ref, d_ref, w1_ref, w2_ref, pa_ref, gb_ref, m_ref, s_ref):
    @pl.when(pl.program_id(1) == 0)
    def _():
        for c in range(y_ref.shape[0]):
            s = jax.nn.gelu(y_ref[c] + d_ref[c] * u_ref[c])
            s_ref[:, c * LANES:(c + 1) * LANES] = s.astype(s_ref.dtype)

    a = s_ref[...]
    y_b = _dot(a, w1_ref[...]) * _sigmoid(_dot(a, w2_ref[...]))
    m_ref[...] = (pa_ref[...].astype(F32) + gb_ref[...].astype(F32) * y_b).astype(m_ref.dtype)


def _glu_merge(y, u, d_skip, w_glu, layer, pa, gates, d_model, tm, tn):
    n_pc, n, _ = y.shape
    k = n_pc * LANES
    nb = d_model // tn
    blocks = (2 * _nbytes((n_pc, tm, LANES), F32) + 2 * _nbytes((k, tn), BF16) + 3 * _nbytes((tm, tn), BF16))
    scratch = _nbytes((tm, k), BF16)
    return pl.pallas_call(
        _glu_merge_kernel,
        out_shape=jax.ShapeDtypeStruct((n, d_model), BF16),
        grid=(n // tm, nb),
        in_specs=[pl.BlockSpec((n_pc, tm, LANES), lambda i, j: (0, i, 0)),
                  pl.BlockSpec((n_pc, tm, LANES), lambda i, j: (0, i, 0)),
                  pl.BlockSpec((n_pc, 1, LANES), lambda i, j: (0, 0, 0)),
                  _w_spec(k, tn, layer, 0),
                  _w_spec(k, tn, layer, nb),
                  pl.BlockSpec((tm, tn), lambda i, j: (i, j)),
                  pl.BlockSpec((tm, tn), lambda i, j: (i, nb + j))],
        out_specs=pl.BlockSpec((tm, tn), lambda i, j: (i, j)),
        scratch_shapes=[pltpu.VMEM((tm, k), BF16)],
        compiler_params=_compiler_params(("parallel", "arbitrary"), blocks, scratch),
        name="glu_merge",
    )(y, u, d_skip, w_glu, w_glu, pa, gates)


def _ffn_up_kernel(h_ref, hp_ref, hn_ref, wg_ref, wv_ref, cg_ref, cv_ref, act_ref, lhs_ref, *, tm):
    i = pl.program_id(0)
    j = pl.program_id(1)
    last = pl.num_programs(0) - 1
    halo = BF16_ROWS

    @pl.when(j == 0)
    def _():
        row = lax.broadcasted_iota(jnp.int32, hp_ref.shape, 0)
        after = jnp.where(jnp.logical_and(row == 0, i < last), hn_ref[...].astype(F32), 0.0)
        edge = jnp.where(jnp.logical_and(row == halo - 1, i > 0), hp_ref[...].astype(F32), after)
        lhs_ref[0:tm, :] = h_ref[...]
        lhs_ref[tm:, :] = edge.astype(lhs_ref.dtype)

    a = lhs_ref[...]
    rows = tm + halo

    def conv(w_ref, cw_ref):
        u = _dot(a, w_ref[...])
        cw = cw_ref[...]
        up = pltpu.roll(u, 1, 0)[0:tm]
        dn = pltpu.roll(u, rows - 1, 0)[0:tm]
        return up * cw[0:1, :] + u[0:tm] * cw[1:2, :] + dn * cw[2:3, :]

    act_ref[...] = (jax.nn.gelu(conv(wg_ref, cg_ref)) * conv(wv_ref, cv_ref)).astype(act_ref.dtype)


def _ffn_up(hn, w_up, layer, conv_w, d_ff, tm, tn, cast=None):
    n, k = hn.shape
    nb = d_ff // tn
    hb = tm // BF16_ROWS
    n_hb = n // BF16_ROWS
    blocks = (_nbytes((tm + 2 * BF16_ROWS, k), BF16) + 2 * _nbytes((k, tn), BF16) + 2 * _nbytes((3, tn), F32)
              + _nbytes((tm, tn), BF16))
    scratch = _nbytes((tm + BF16_ROWS, k), BF16)
    outs = _call_with_cast(
        functools.partial(_ffn_up_kernel, tm=tm), cast,
        grid=(n // tm, nb),
        in_specs=[pl.BlockSpec((tm, k), lambda i, j: (i, 0)),
                  pl.BlockSpec((BF16_ROWS, k), lambda i, j: (jnp.maximum(i * hb - 1, 0), 0)),
                  pl.BlockSpec((BF16_ROWS, k), lambda i, j: (jnp.minimum((i + 1) * hb, n_hb - 1), 0)),
                  _w_spec(k, tn, layer, 0),
                  _w_spec(k, tn, layer, nb),
                  pl.BlockSpec((3, tn), lambda i, j: (0, j)),
                  pl.BlockSpec((3, tn), lambda i, j: (0, nb + j))],
        out_specs=[pl.BlockSpec((tm, tn), lambda i, j: (i, j))],
        out_shape=[jax.ShapeDtypeStruct((n, d_ff), BF16)],
        operands=(hn, hn, hn, w_up, w_up, conv_w, conv_w),
        scratch_shapes=[pltpu.VMEM((tm + BF16_ROWS, k), BF16)],
        block_bytes=blocks, scratch_bytes=scratch, name="ffn_up")
    return outs[0] if cast is None else outs


def kernel(x, meta_tokens, norm_mix, w_in, conv_a_w, w_a, ssm_lambda_re, ssm_lambda_im, ssm_log_step,
           ssm_b_re, ssm_b_im, ssm_c_re, ssm_c_im, ssm_d, w_glu, w_out, norm_ffn, w_up, conv_ffn_w,
           w_down, norm_final):
    bsz, seq, d_model = x.shape
    n_meta = meta_tokens.shape[0]
    depth = w_in.shape[0]
    d_conv = w_a.shape[1]
    d_ssm = ssm_d.shape[1]
    d_ff = w_down.shape[1]
    n_groups, p_state, h_grp = ssm_b_re.shape[2:]
    assert w_in.shape[2] == 3 * d_conv + d_ssm + 2 * d_model and n_groups * h_grp == d_ssm

    cbw = min(S5_CB_WIDTH, d_ssm)
    n_cb = d_ssm // cbw
    gl = cbw // h_grp
    assert n_cb * cbw == d_ssm and gl * h_grp == cbw

    n_true = n_meta + seq
    row_align = T_BLK * BF16_ROWS
    n_pad = _round_up(n_true, PAD_ROWS if n_true >= 4 * PAD_ROWS else row_align)
    assert n_pad % row_align == 0
    nb = n_pad // T_BLK

    tm = _pick_tile(n_pad, TM_TARGET, BF16_ROWS)
    tm_up = _pick_tile(n_pad, TM_FFN_UP_TARGET, BF16_ROWS)
    tm_half = _pick_tile(n_pad, TM_HALF_TARGET, BF16_ROWS)
    tm_ew = _pick_tile(n_pad, TM_EW_TARGET, BF16_ROWS)
    r_s5 = _pick_tile(nb, R_S5_TARGET, BF16_ROWS)
    tn = lambda n, target=512: _pick_tile(n, target, 128)

    prep = _s5_prep(ssm_lambda_re, ssm_lambda_im, ssm_log_step, ssm_b_re, ssm_b_im, ssm_c_re, ssm_c_im, gl)
    d_skip = ssm_d.astype(F32).reshape(depth, d_ssm // LANES, 1, LANES)

    residual = functools.partial(_epi_residual, n_valid=n_true)
    tile_spec = lambda t, rows=tm: pl.BlockSpec((rows, t), lambda i, j: (i, j))
    u_off = 3 * d_conv
    gate_off = u_off + d_ssm

    w_a_b, w_glu_b, w_out_b = (w.astype(BF16) for w in (w_a, w_glu, w_out))
    w_in_first = w_in[0].astype(BF16)

    outs = []
    for b in range(bsz):
        w_in_l = w_in_first
        for l in range(depth):
            w_s, w_cre, w_cim, w_i, apw = _s5_weights(prep, l, n_cb, gl, h_grp, p_state)

            if l == 0:
                h_res, hn = _assemble_norm(meta_tokens, x[b], norm_mix[0], n_pad, tm_ew)
            else:
                hn = _rmsnorm(h_res, norm_mix[l], BF16, tm_ew)
            bg, cv, u, w_down_l = _inproj_conv(hn, w_in_l, None, d_conv, d_ssm, tm_half, tn(d_conv),
                                               cast=(w_down, l))
            t_g = tn(2 * d_model)
            gates, w_up_l = _mm(hn, w_in_l, None, col0=gate_off, n_cols=2 * d_model, tm=tm, tn=t_g,
                                epilogue=_epi_sigmoid,
                                out_shape=jax.ShapeDtypeStruct((n_pad, 2 * d_model), BF16),
                                out_spec=tile_spec(t_g), name="inproj_gates", cast=(w_up, l))

            pa = _branch_a(bg, cv, conv_a_w[l].astype(F32), w_a_b, l, gates, d_model, tm, tn(d_model, 1024))
            t_d = tn(d_model)

            states = _s5_states(u, w_s, _pick_tile(nb, R_S5_STATES_TARGET, BF16_ROWS), tn(w_s.shape[-1]))
            x_re, x_im = _s5_scan(states, apw, tn(gl * p_state, SCAN_LANES_TARGET))
            y = _s5_out(u, x_re, x_im, w_i, w_cre, w_cim, r_s5, tn(T_BLK * cbw))

            merged = _glu_merge(y, u, d_skip[l], w_glu_b, l, pa, gates, d_model, tm_half, tn(d_model, 1024))
            h_res = _mm(merged, w_out_b, l, col0=0, n_cols=d_model, tm=tm, tn=t_d, epilogue=residual,
                        extras=(h_res,), extra_specs=(tile_spec(t_d),),
                        out_shape=jax.ShapeDtypeStruct((n_pad, d_model), F32), out_spec=tile_spec(t_d),
                        name="out_proj")

            hn = _rmsnorm(h_res, norm_ffn[l], BF16, tm_ew)
            if l + 1 < depth:
                act, w_in_l = _ffn_up(hn, w_up_l, None, conv_ffn_w[l].astype(F32), d_ff, tm_up, tn(d_ff),
                                      cast=(w_in, l + 1))
            else:
                act = _ffn_up(hn, w_up_l, None, conv_ffn_w[l].astype(F32), d_ff, tm_up, tn(d_ff))
            t_o = tn(d_model, 256)
            h_res = _mm(act, w_down_l, None, col0=0, n_cols=d_model, tm=tm_half, tn=t_o, epilogue=residual,
                        extras=(h_res,), extra_specs=(tile_spec(t_o, tm_half),),
                        out_shape=jax.ShapeDtypeStruct((n_pad, d_model), F32), out_spec=tile_spec(t_o, tm_half),
                        name="ffn_down")

        outs.append(_final_norm(h_res, norm_final, n_meta, seq, x.dtype, _pick_tile(seq, TM_EW_TARGET, n_meta)))
    return jnp.stack(outs, axis=0)
```

```python
import functools

import numpy as np
import jax
import jax.numpy as jnp
from jax import lax
from jax.experimental import pallas as pl
from jax.experimental.pallas import tpu as pltpu

F32 = jnp.float32
BF16 = jnp.bfloat16
EPS = 1e-6

T_BLK = 16
S5_CB_WIDTH = 128
PAD_ROWS = 1280
BF16_ROWS = 16
LANES = 128
VMEM_CAP_BYTES = 60000 * 1024
VMEM_TEMP_BYTES = 20 * 1024 * 1024

TM_TARGET = 1280
TM_FFN_UP_TARGET = 832
TM_HALF_TARGET = 640
R_S5_STATES_TARGET = 1040
TM_EW_TARGET = 320
R_S5_TARGET = 1040
SCAN_LANES_TARGET = 512


def _round_up(n, m):
    return (n + m - 1) // m * m


def _pick_tile(n, target, align):
    best = 0
    for t in range(align, min(n, target) + 1, align):
        if n % t == 0:
            best = t
    assert best > 0, (n, target, align)
    return best


def _nbytes(shape, dtype):
    return int(np.prod(shape)) * jnp.dtype(dtype).itemsize


def _compiler_params(semantics, block_bytes, scratch_bytes=0):
    est = 2 * block_bytes + scratch_bytes + VMEM_TEMP_BYTES
    return pltpu.CompilerParams(dimension_semantics=semantics,
                                vmem_limit_bytes=int(min(VMEM_CAP_BYTES, est)))


def _dot(a, b):
    return jnp.dot(a, b, preferred_element_type=F32)


def _sigmoid(x):
    return 0.5 * (1.0 + jnp.tanh(0.5 * x))


def _rmsnorm_kernel(x_ref, g_ref, o_ref):
    x = x_ref[...]
    ms = jnp.mean(x * x, axis=-1, keepdims=True)
    o_ref[...] = (x * lax.rsqrt(ms + EPS) * g_ref[...]).astype(o_ref.dtype)


def _rmsnorm(x, g, out_dtype, tm):
    n, d = x.shape
    blocks = _nbytes((tm, d), F32) + _nbytes((tm, d), out_dtype)
    return pl.pallas_call(
        _rmsnorm_kernel,
        out_shape=jax.ShapeDtypeStruct((n, d), out_dtype),
        grid=(n // tm,),
        in_specs=[pl.BlockSpec((tm, d), lambda i: (i, 0)),
                  pl.BlockSpec((1, d), lambda i: (0, 0))],
        out_specs=pl.BlockSpec((tm, d), lambda i: (i, 0)),
        compiler_params=_compiler_params(("parallel",), blocks),
        name="rmsnorm",
    )(x, g.reshape(1, d).astype(F32))


def _assemble_norm_kernel(meta_ref, prev_ref, x_ref, g_ref, h_ref, hn_ref, *, n_valid):
    i = pl.program_id(0)
    tm = h_ref.shape[0]
    n_meta = meta_ref.shape[0]
    head = jnp.where(i == 0, meta_ref[...], prev_ref[...])
    tile = jnp.concatenate([head, x_ref[0:tm - n_meta, :]], axis=0)
    rows = i * tm + lax.broadcasted_iota(jnp.int32, tile.shape, 0)
    tile = jnp.where(rows < n_valid, tile, 0.0)
    h_ref[...] = tile
    ms = jnp.mean(tile * tile, axis=-1, keepdims=True)
    hn_ref[...] = (tile * lax.rsqrt(ms + EPS) * g_ref[...]).astype(hn_ref.dtype)


def _assemble_norm(meta, x, g, n_pad, tm):
    n_meta, d = meta.shape
    seq = x.shape[0]
    assert n_meta % 8 == 0 and tm % n_meta == 0 and seq % n_meta == 0 and n_pad % tm == 0
    per = tm // n_meta
    last_x = pl.cdiv(seq, tm) - 1
    last_prev = seq // n_meta - 1
    blocks = _nbytes((tm + 2 * n_meta, d), F32) + _nbytes((tm, d), F32) + _nbytes((tm, d), BF16)
    return pl.pallas_call(
        functools.partial(_assemble_norm_kernel, n_valid=n_meta + seq),
        out_shape=(jax.ShapeDtypeStruct((n_pad, d), F32), jax.ShapeDtypeStruct((n_pad, d), BF16)),
        grid=(n_pad // tm,),
        in_specs=[pl.BlockSpec((n_meta, d), lambda i: (0, 0)),
                  pl.BlockSpec((n_meta, d), lambda i: (jnp.clip(i * per - 1, 0, last_prev), 0)),
                  pl.BlockSpec((tm, d), lambda i: (jnp.minimum(i, last_x), 0)),
                  pl.BlockSpec((1, d), lambda i: (0, 0))],
        out_specs=(pl.BlockSpec((tm, d), lambda i: (i, 0)), pl.BlockSpec((tm, d), lambda i: (i, 0))),
        compiler_params=_compiler_params(("parallel",), blocks),
        name="assemble_norm",
    )(meta.astype(F32), x.astype(F32), x.astype(F32), g.reshape(1, d).astype(F32))


def _final_norm_kernel(a_ref, b_ref, g_ref, o_ref, *, skip):
    x = jnp.concatenate([a_ref[skip:, :], b_ref[...]], axis=0)
    ms = jnp.mean(x * x, axis=-1, keepdims=True)
    o_ref[...] = (x * lax.rsqrt(ms + EPS) * g_ref[...]).astype(o_ref.dtype)


def _final_norm(x, g, skip, n_out, out_dtype, tm):
    n, d = x.shape
    assert skip % 8 == 0 and tm % skip == 0 and n_out % tm == 0 and n >= n_out + skip
    per = tm // skip
    blocks = _nbytes((tm + skip, d), F32) + _nbytes((tm, d), out_dtype)
    return pl.pallas_call(
        functools.partial(_final_norm_kernel, skip=skip),
        out_shape=jax.ShapeDtypeStruct((n_out, d), out_dtype),
        grid=(n_out // tm,),
        in_specs=[pl.BlockSpec((tm, d), lambda i: (i, 0)),
                  pl.BlockSpec((skip, d), lambda i: ((i + 1) * per, 0)),
                  pl.BlockSpec((1, d), lambda i: (0, 0))],
        out_specs=pl.BlockSpec((tm, d), lambda i: (i, 0)),
        compiler_params=_compiler_params(("parallel",), blocks),
        name="final_norm",
    )(x, x, g.reshape(1, d).astype(F32))


def _mm_kernel(*refs, n_extra, epilogue, tm):
    a_ref, w_ref = refs[0], refs[1]
    extras = refs[2:2 + n_extra]
    o_ref = refs[2 + n_extra]
    acc = _dot(a_ref[...], w_ref[...])
    row0 = pl.program_id(0) * tm
    o_ref[...] = epilogue(acc, row0, *[e[...] for e in extras]).astype(o_ref.dtype)


def _w_spec(k, tn, layer, jb0):
    if layer is None:
        return pl.BlockSpec((k, tn), lambda i, j: (0, jb0 + j))
    return pl.BlockSpec((None, k, tn), lambda i, j: (layer, 0, jb0 + j))


def _call_with_cast(kernel_fn, cast, *, grid, in_specs, out_specs, out_shape, operands, scratch_shapes=(),
                    block_bytes, scratch_bytes=0, name):
    if cast is None:
        return pl.pallas_call(
            kernel_fn, out_shape=tuple(out_shape), grid=grid, in_specs=list(in_specs), out_specs=tuple(out_specs),
            scratch_shapes=list(scratch_shapes),
            compiler_params=_compiler_params(("parallel", "arbitrary"), block_bytes, scratch_bytes), name=name,
        )(*operands)

    stack, layer = cast
    _, k, n = stack.shape
    n_steps = grid[0] * grid[1]
    options = []
    for rows in range(BF16_ROWS, k + 1, BF16_ROWS):
        if k % rows == 0 and k // rows <= n_steps:
            splits = max(c for c in range(1, n_steps // (k // rows) + 1) if n % (c * LANES) == 0)
            options.append((rows * (n // splits), rows, splits))
    _, rows, splits = min(options)
    cols = n // splits
    last_chunk = (k // rows) * splits - 1
    n_in, n_out = len(in_specs), len(out_specs)

    def chunk(i, j):
        q = jnp.minimum(i * grid[1] + j, last_chunk)
        return q // splits, q % splits

    def body(*refs):
        src_ref, dst_ref = refs[n_in], refs[n_in + 1 + n_out]
        dst_ref[...] = src_ref[...].astype(dst_ref.dtype)
        kernel_fn(*refs[:n_in], *refs[n_in + 1:n_in + 1 + n_out], *refs[n_in + 2 + n_out:])

    cast_bytes = _nbytes((rows, cols), F32) + _nbytes((rows, cols), BF16)
    return pl.pallas_call(
        body,
        out_shape=(*out_shape, jax.ShapeDtypeStruct((k, n), BF16)),
        grid=grid,
        in_specs=[*in_specs, pl.BlockSpec((None, rows, cols), lambda i, j: (layer, *chunk(i, j)))],
        out_specs=(*out_specs, pl.BlockSpec((rows, cols), lambda i, j: chunk(i, j))),
        scratch_shapes=list(scratch_shapes),
        compiler_params=_compiler_params(("arbitrary", "arbitrary"), block_bytes + cast_bytes, scratch_bytes),
        name=name,
    )(*operands, stack)


def _mm(a, w, layer, *, col0, n_cols, tm, tn, epilogue, extras=(), extra_specs=(), out_shape, out_spec, name,
        cast=None):
    n, k = a.shape
    assert n % tm == 0 and n_cols % tn == 0 and col0 % tn == 0
    blocks = (_nbytes((tm, k), a.dtype) + _nbytes((k, tn), w.dtype) + _nbytes((tm, tn), F32)
              + sum(_nbytes((tm, tn), e.dtype) for e in extras))
    outs = _call_with_cast(
        functools.partial(_mm_kernel, n_extra=len(extras), epilogue=epilogue, tm=tm), cast,
        grid=(n // tm, n_cols // tn),
        in_specs=[pl.BlockSpec((tm, k), lambda i, j: (i, 0)),
                  _w_spec(k, tn, layer, col0 // tn),
                  *extra_specs],
        out_specs=[out_spec], out_shape=[out_shape], operands=(a, w, *extras), block_bytes=blocks, name=name)
    return outs[0] if cast is None else outs


def _epi_pieces(acc, row0):
    return jnp.stack([acc[:, h * LANES:(h + 1) * LANES] for h in range(acc.shape[1] // LANES)], axis=0)


def _epi_sigmoid(acc, row0):
    return _sigmoid(acc)


def _epi_residual(acc, row0, res, *, n_valid):
    rows = row0 + lax.broadcasted_iota(jnp.int32, acc.shape, 0)
    return jnp.where(rows < n_valid, res + acc, 0.0)


def _inproj_conv_kernel(a_ref, wb_ref, wc_ref, wv_ref, wu_ref, bg_ref, cv_ref, u_ref):
    a = a_ref[...]
    bg_ref[...] = _dot(a, wb_ref[...]).astype(bg_ref.dtype)
    cv_ref[...] = (_dot(a, wc_ref[...]) * _dot(a, wv_ref[...])).astype(cv_ref.dtype)
    u_ref[...] = _epi_pieces(_dot(a, wu_ref[...]), 0)


def _inproj_conv(hn, w_in, layer, d_conv, d_ssm, tm, tn, cast=None):
    n, k = hn.shape
    nb = d_conv // tn
    tn_u = d_ssm // nb
    assert tn_u % LANES == 0 and (3 * d_conv) % tn_u == 0
    blocks = (_nbytes((tm, k), BF16) + 3 * _nbytes((k, tn), BF16) + _nbytes((k, tn_u), BF16)
              + 2 * _nbytes((tm, tn), BF16) + _nbytes((tm, tn_u), F32))
    out = jax.ShapeDtypeStruct((n, d_conv), BF16)
    return _call_with_cast(
        _inproj_conv_kernel, cast,
        grid=(n // tm, nb),
        in_specs=[pl.BlockSpec((tm, k), lambda i, j: (i, 0)),
                  _w_spec(k, tn, layer, 0),
                  _w_spec(k, tn, layer, nb),
                  _w_spec(k, tn, layer, 2 * nb),
                  _w_spec(k, tn_u, layer, 3 * d_conv // tn_u)],
        out_specs=[pl.BlockSpec((tm, tn), lambda i, j: (i, j)),
                   pl.BlockSpec((tm, tn), lambda i, j: (i, j)),
                   pl.BlockSpec((tn_u // LANES, tm, LANES), lambda i, j: (j, i, 0))],
        out_shape=[out, out, jax.ShapeDtypeStruct((d_ssm // LANES, n, LANES), F32)],
        operands=(hn, w_in, w_in, w_in, w_in), block_bytes=blocks, name="inproj_conv")


def _shifted_rows(x, prev_row, next_row):
    t = x.shape[0]
    row = lax.broadcasted_iota(jnp.int32, x.shape, 0)
    up = jnp.where(row == 0, prev_row, pltpu.roll(x, 1, 0))
    dn = jnp.where(row == t - 1, next_row, pltpu.roll(x, t - 1, 0))
    return up, dn


def _branch_a_kernel(bg_ref, cv_ref, cvp_ref, cvn_ref, cw_ref, w_ref, gate_ref, o_ref, z_ref, *, chunk):
    i = pl.program_id(0)
    last = pl.num_programs(0) - 1

    @pl.when(pl.program_id(1) == 0)
    def _():
        for c0 in range(0, z_ref.shape[1], chunk):
            cols = slice(c0, c0 + chunk)
            x = cv_ref[:, cols].astype(F32)
            prev = cvp_ref[:, cols].astype(F32)[BF16_ROWS - 1:BF16_ROWS, :]
            nxt = cvn_ref[:, cols].astype(F32)[0:1, :]
            prev = jnp.where(i > 0, prev, 0.0)
            nxt = jnp.where(i < last, nxt, 0.0)
            up, dn = _shifted_rows(x, prev, nxt)
            w = cw_ref[:, cols]
            conv = up * w[0:1, :] + x * w[1:2, :] + dn * w[2:3, :]
            z_ref[:, cols] = (bg_ref[:, cols].astype(F32) * conv).astype(z_ref.dtype)

    o_ref[...] = (_dot(z_ref[...], w_ref[...]) * gate_ref[...].astype(F32)).astype(o_ref.dtype)


def _branch_a(bg, cv, conv_w, w_a, layer, gates, d_model, tm, tn):
    n, c = bg.shape
    hb = tm // BF16_ROWS
    n_hb = n // BF16_ROWS
    blocks = (2 * _nbytes((tm, c), BF16) + 2 * _nbytes((BF16_ROWS, c), BF16) + _nbytes((8, c), F32)
              + _nbytes((c, tn), BF16) + 2 * _nbytes((tm, tn), BF16))
    scratch = _nbytes((tm, c), BF16)
    return pl.pallas_call(
        functools.partial(_branch_a_kernel, chunk=_pick_tile(c, 512, LANES)),
        out_shape=jax.ShapeDtypeStruct((n, d_model), BF16),
        grid=(n // tm, d_model // tn),
        in_specs=[pl.BlockSpec((tm, c), lambda i, j: (i, 0)),
                  pl.BlockSpec((tm, c), lambda i, j: (i, 0)),
                  pl.BlockSpec((BF16_ROWS, c), lambda i, j: (jnp.maximum(i * hb - 1, 0), 0)),
                  pl.BlockSpec((BF16_ROWS, c), lambda i, j: (jnp.minimum((i + 1) * hb, n_hb - 1), 0)),
                  pl.BlockSpec((3, c), lambda i, j: (0, 0)),
                  _w_spec(c, tn, layer, 0),
                  pl.BlockSpec((tm, tn), lambda i, j: (i, j))],
        out_specs=pl.BlockSpec((tm, tn), lambda i, j: (i, j)),
        scratch_shapes=[pltpu.VMEM((tm, c), BF16)],
        compiler_params=_compiler_params(("parallel", "arbitrary"), blocks, scratch),
        name="branch_a",
    )(bg, cv, cv, cv, conv_w, w_a, gates)


def _cmul(ar, ai, br, bi):
    return ar * br - ai * bi, ar * bi + ai * br


def _s5_prep_kernel(lre_ref, lim_ref, lst_ref, bre_ref, bim_ref, cre_ref, cim_ref,
                    ere_ref, eim_ref, care_ref, caim_ref, klag_ref, apre_ref, apim_ref):
    lam_re, lam_im = lre_ref[...], lim_ref[...]
    dt = jnp.exp(lst_ref[...])
    mag = jnp.exp(lam_re * dt)
    a_re = mag * jnp.cos(lam_im * dt)
    a_im = mag * jnp.sin(lam_im * dt)
    nr, ni = a_re - 1.0, a_im
    den = lam_re * lam_re + lam_im * lam_im
    f_re = (nr * lam_re + ni * lam_im) / den
    f_im = (ni * lam_re - nr * lam_im) / den
    b_re, b_im = bre_ref[...], bim_ref[...]
    bb_re, bb_im = _cmul(f_re, f_im, b_re, b_im)
    c_re, c_im = cre_ref[...], cim_ref[...]

    pw = [(jnp.ones_like(a_re), jnp.zeros_like(a_im))]
    for _ in range(T_BLK):
        pw.append(_cmul(pw[-1][0], pw[-1][1], a_re, a_im))

    e_re, e_im, ck_re, ck_im = [], [], [], []
    for k in range(T_BLK + 1):
        cr, ci = _cmul(pw[k][0], pw[k][1], c_re, c_im)
        ck_re.append(cr)
        ck_im.append(ci)
    for k in range(T_BLK):
        er, ei = _cmul(pw[k][0], pw[k][1], bb_re, bb_im)
        e_re.append(er)
        e_im.append(ei)
    backward = pl.program_id(1) == 1
    n_blocks, _, gl = ere_ref.shape[:3]
    for k in range(T_BLK):
        e_step = jnp.where(backward, k, T_BLK - 1 - k)
        c_step = jnp.where(backward, T_BLK - 1 - k, k)
        for c in range(n_blocks):
            grp = slice(c * gl, (c + 1) * gl)
            ere_ref[c, e_step] = e_re[k][grp]
            eim_ref[c, e_step] = e_im[k][grp]
            care_ref[c, c_step] = ck_re[k + 1][grp]
            caim_ref[c, c_step] = -ck_im[k + 1][grp]

    dn = (((2,), (2,)), ((0,), (0,)))
    k_t = (lax.dot_general(bb_re, jnp.concatenate(ck_re[:-1], axis=1), dn,
                           precision=lax.Precision.HIGHEST, preferred_element_type=F32)
           - lax.dot_general(bb_im, jnp.concatenate(ck_im[:-1], axis=1), dn,
                             precision=lax.Precision.HIGHEST, preferred_element_type=F32))
    n_grp, h_out, th = k_t.shape
    h_in = th // T_BLK
    k_t = k_t.reshape(n_grp * h_out, th)
    src = lax.broadcasted_iota(jnp.int32, (th, th), 0)
    dst = lax.broadcasted_iota(jnp.int32, (th, th), 1)
    assert h_in & (h_in - 1) == 0, "group width must be a power of two"
    shift = h_in.bit_length() - 1
    same_h = (src & (h_in - 1)) == (dst & (h_in - 1))
    lag, tau = src >> shift, dst >> shift
    for sigma in range(T_BLK):
        wanted = jnp.where(backward, sigma - lag, sigma + lag)
        place = jnp.where(jnp.logical_and(same_h, tau == wanted), 1.0, 0.0)
        block = jnp.dot(k_t, place, precision=lax.Precision.HIGHEST, preferred_element_type=F32)
        klag_ref[:, sigma, :, :] = block.reshape(n_grp, h_out, th)

    q_re, q_im = pw[T_BLK]
    r_re, r_im = q_re, q_im
    for m in range(8):
        apre_ref[m] = r_re
        apim_ref[m] = r_im
        r_re, r_im = _cmul(r_re, r_im, q_re, q_im)


def _s5_prep(lam_re, lam_im, log_step, b_re, b_im, c_re, c_im, gl):
    depth, _, g, p = lam_re.shape
    h = b_re.shape[-1]
    lead = (depth, 2)
    b_re_t = jnp.swapaxes(b_re, -1, -2).astype(F32)
    b_im_t = jnp.swapaxes(b_im, -1, -2).astype(F32)

    gc = _pick_tile(g, max(16, gl), gl)

    def spec(*tail):
        zeros = (0,) * (len(tail) - 1)
        return pl.BlockSpec((None, None) + tail, lambda l, d, q: (l, d, q) + zeros)

    ap_spec = pl.BlockSpec((None, None, 8, gc, 1, p), lambda l, d, q: (l, d, 0, q, 0, 0))
    pw_spec = pl.BlockSpec((None, None, gc // gl, T_BLK, gl, h, p), lambda l, d, q: (l, d, q, 0, 0, 0, 0))
    th = T_BLK * h
    outs = (jax.ShapeDtypeStruct(lead + (g // gl, T_BLK, gl, h, p), F32),) * 4 + (
        jax.ShapeDtypeStruct(lead + (g, T_BLK, h, th), F32),
        jax.ShapeDtypeStruct(lead + (8, g, 1, p), F32),
        jax.ShapeDtypeStruct(lead + (8, g, 1, p), F32))
    blocks = 4 * _nbytes((gc, th, LANES), F32) + 8 * _nbytes((gc, h, LANES), F32)
    return pl.pallas_call(
        _s5_prep_kernel,
        out_shape=outs,
        grid=lead + (g // gc,),
        in_specs=[spec(gc, 1, p), spec(gc, 1, p), spec(gc, 1, 1), spec(gc, h, p), spec(gc, h, p),
                  spec(gc, h, p), spec(gc, h, p)],
        out_specs=(pw_spec,) * 4 + (spec(gc, T_BLK, h, th), ap_spec, ap_spec),
        compiler_params=_compiler_params(("parallel", "parallel", "parallel"), blocks,
                                         16 * _nbytes((gc, th, LANES), F32)),
        name="s5_prep",
    )(lam_re.astype(F32)[:, :, :, None, :], lam_im.astype(F32)[:, :, :, None, :],
      log_step.astype(F32)[..., None, None], b_re_t, b_im_t, c_re.astype(F32), c_im.astype(F32))


def _block_diag_kernel(*refs, n_parts):
    c_refs = refs[:n_parts]
    e_ref, rg_ref, cg_ref, o_ref = refs[n_parts:]
    same_group = rg_ref[...] == cg_ref[...]
    width = e_ref.shape[1]
    for q, c_ref in enumerate(c_refs):
        spread = _dot(c_ref[...].astype(BF16), e_ref[...])
        o_ref[:, q * width:(q + 1) * width] = jnp.where(same_group, spread, 0.0).astype(o_ref.dtype)


def _block_diag(compacts, gl, row_inner, col_inner, name):
    compacts = [c if isinstance(c, tuple) else (c, ()) for c in compacts]
    n_cb, rows, kc = compacts[0][0].shape[-3:]
    width = kc * gl
    tr = _pick_tile(rows, 2048, BF16_ROWS)
    col = np.arange(width)
    spread = np.arange(kc)[:, None] == (col // (gl * col_inner)) * col_inner + col % col_inner
    row_group = ((np.arange(rows) // row_inner) % gl).astype(np.int32)[:, None]
    col_group = ((col // col_inner) % gl).astype(np.int32)[None, :]
    n_parts = len(compacts)
    blocks = (n_parts * _nbytes((tr, LANES), F32) + _nbytes((kc, width), BF16) + _nbytes((tr, LANES), jnp.int32)
              + _nbytes((8, width), jnp.int32) + _nbytes((tr, n_parts * width), BF16))
    return pl.pallas_call(
        functools.partial(_block_diag_kernel, n_parts=n_parts),
        out_shape=jax.ShapeDtypeStruct((n_cb, rows, n_parts * width), BF16),
        grid=(n_cb, rows // tr),
        in_specs=[pl.BlockSpec((None,) * (len(lead) + 1) + (tr, kc), lambda c, i, lead=lead: (*lead, c, i, 0))
                  for _, lead in compacts]
                 + [pl.BlockSpec((kc, width), lambda c, i: (0, 0)),
                    pl.BlockSpec((tr, 1), lambda c, i: (i, 0)),
                    pl.BlockSpec((1, width), lambda c, i: (0, 0))],
        out_specs=pl.BlockSpec((None, tr, n_parts * width), lambda c, i: (c, i, 0)),
        compiler_params=_compiler_params(("parallel", "parallel"), blocks),
        name=name,
    )(*[arr for arr, _ in compacts], jnp.asarray(spread, BF16), jnp.asarray(row_group), jnp.asarray(col_group))


def _s5_weights(prep, layer, n_cb, gl, h, p):
    e_re, e_im, ca_re, ca_im, klag, ap_re, ap_im = prep
    t = T_BLK
    depth = e_re.shape[0]
    rows = lambda x: x.reshape(depth, 2, n_cb, t * gl * h, p)
    part = lambda x, d: (rows(x), (layer, d))

    w_s = _block_diag([part(e_re, 0), part(e_im, 0), part(e_re, 1), part(e_im, 1)], gl, h, p, "s5_w_states")

    w_cre = _block_diag([part(ca_re, 0), part(ca_re, 1)], gl, h, p, "s5_w_carry_re")
    w_cim = _block_diag([part(ca_im, 0), part(ca_im, 1)], gl, h, p, "s5_w_carry_im")

    kst = (klag[layer, 0] + klag[layer, 1]).reshape(n_cb, gl, t, h, t * h)
    kst = jnp.transpose(kst, (0, 2, 1, 3, 4))
    w_i = _block_diag([kst.reshape(n_cb, t * gl * h, t * h)], gl, h, h, "s5_w_intra")

    ap_re, ap_im = ap_re[layer], ap_im[layer]

    def lanes(ap):
        ap = jnp.stack([ap[0], ap[1, ::-1]], axis=0)
        ap = ap.reshape(2, 8, n_cb, gl * p)
        return jnp.transpose(ap, (2, 0, 1, 3))
    apw = jnp.stack([lanes(ap_re), lanes(ap_im)], axis=2)

    return w_s, w_cre, w_cim, w_i, apw


def _fold_time_blocks(u_ref, lhs_ref, r):
    pieces = u_ref.shape[0]
    for tau in range(T_BLK):
        for h in range(pieces):
            col = (tau * pieces + h) * LANES
            lhs_ref[:, col:col + LANES] = u_ref[h, pl.ds(tau, r, stride=T_BLK), :].astype(lhs_ref.dtype)


def _s5_states_kernel(u_ref, w_ref, s_ref, lhs_ref, *, r):
    @pl.when(pl.program_id(2) == 0)
    def _():
        _fold_time_blocks(u_ref, lhs_ref, r)

    s_ref[...] = _dot(lhs_ref[...], w_ref[...])


def _s5_states(u, w_s, r, tn):
    n_cb, kw, ns = w_s.shape
    n = u.shape[1]
    nb = n // T_BLK
    ppc = u.shape[0] // n_cb
    blocks = _nbytes((ppc, T_BLK * r, LANES), F32) + _nbytes((kw, tn), BF16) + _nbytes((r, tn), F32)
    scratch = _nbytes((r, kw), BF16)
    return pl.pallas_call(
        functools.partial(_s5_states_kernel, r=r),
        out_shape=jax.ShapeDtypeStruct((n_cb, nb, ns), F32),
        grid=(n_cb, nb // r, ns // tn),
        in_specs=[pl.BlockSpec((ppc, T_BLK * r, LANES), lambda c, i, j: (c, i, 0)),
                  pl.BlockSpec((None, kw, tn), lambda c, i, j: (c, 0, j))],
        out_specs=pl.BlockSpec((None, r, tn), lambda c, i, j: (c, i, j)),
        scratch_shapes=[pltpu.VMEM((r, kw), BF16)],
        compiler_params=_compiler_params(("parallel", "parallel", "arbitrary"), blocks, scratch),
        name="s5_states",
    )(u, w_s)


def _s5_scan_kernel(sre_ref, sim_ref, apre_ref, apim_ref, xre_ref, xim_ref, *, n_pairs):
    d = pl.program_id(1)
    lw = sre_ref.shape[-1]
    ap_re, ap_im = apre_ref[...], apim_ref[...]
    row = lax.broadcasted_iota(jnp.int32, (8, lw), 0)

    def run(reverse):
        steps = []
        for shift in (1, 2, 4):
            src = (8 - shift) if reverse else (shift - 1)
            keep = (row < 8 - shift) if reverse else (row >= shift)
            steps.append((8 - shift if reverse else shift,
                          jnp.where(keep, ap_re[src:src + 1, :], 0.0),
                          jnp.where(keep, ap_im[src:src + 1, :], 0.0)))
        edge = 7 if reverse else 0
        out_row = 0 if reverse else 7
        nb_shift = 7 if reverse else 1

        def tile(r0, c_re, c_im):
            x_re = sre_ref[pl.ds(r0, 8), :]
            x_im = sim_ref[pl.ds(r0, 8), :]
            for amount, m_re, m_im in steps:
                s_re = pltpu.roll(x_re, amount, 0)
                s_im = pltpu.roll(x_im, amount, 0)
                x_re, x_im = x_re + m_re * s_re - m_im * s_im, x_im + m_re * s_im + m_im * s_re
            inc_re = x_re + ap_re * c_re - ap_im * c_im
            inc_im = x_im + ap_re * c_im + ap_im * c_re
            e_re = jnp.where(row == edge, c_re, pltpu.roll(inc_re, nb_shift, 0))
            e_im = jnp.where(row == edge, c_im, pltpu.roll(inc_im, nb_shift, 0))
            return e_re, e_im, inc_re[out_row:out_row + 1, :], inc_im[out_row:out_row + 1, :]

        def body(k, carry):
            c_re, c_im = carry
            pair = (n_pairs - 1 - k) if reverse else k
            r0 = pl.multiple_of(pair * BF16_ROWS, BF16_ROWS)
            e_re, e_im = [None, None], [None, None]
            for half in ((1, 0) if reverse else (0, 1)):
                e_re[half], e_im[half], c_re, c_im = tile(pl.multiple_of(r0 + 8 * half, 8), c_re, c_im)
            xre_ref[pl.ds(r0, BF16_ROWS), :] = jnp.concatenate(e_re, axis=0).astype(xre_ref.dtype)
            xim_ref[pl.ds(r0, BF16_ROWS), :] = jnp.concatenate(e_im, axis=0).astype(xim_ref.dtype)
            return c_re, c_im

        zero = jnp.zeros((1, lw), F32)
        lax.fori_loop(0, n_pairs, body, (zero, zero))

    @pl.when(d == 0)
    def _():
        run(False)

    @pl.when(d == 1)
    def _():
        run(True)


def _s5_scan(s, apw, lw):
    n_cb, nb, ns = s.shape
    glp = ns // 4
    nq = glp // lw
    blocks = 2 * _nbytes((nb, lw), F32) + 2 * _nbytes((nb, lw), BF16) + 2 * _nbytes((8, lw), F32)
    out = jax.ShapeDtypeStruct((n_cb, nb, 2 * glp), BF16)

    def s_spec(ri):
        return pl.BlockSpec((None, nb, lw), lambda c, d, q: (c, 0, (2 * d + ri) * nq + q))

    def ap_spec(ri):
        return pl.BlockSpec((None, None, None, 8, lw), lambda c, d, q: (c, d, ri, 0, q))

    x_spec = pl.BlockSpec((None, nb, lw), lambda c, d, q: (c, 0, d * nq + q))
    return pl.pallas_call(
        functools.partial(_s5_scan_kernel, n_pairs=nb // BF16_ROWS),
        out_shape=(out, out),
        grid=(n_cb, 2, nq),
        in_specs=[s_spec(0), s_spec(1), ap_spec(0), ap_spec(1)],
        out_specs=(x_spec, x_spec),
        compiler_params=_compiler_params(("parallel", "parallel", "parallel"), blocks),
        name="s5_scan",
    )(s, s, apw, apw)


def _s5_out_kernel(u_ref, xre_ref, xim_ref, wi_ref, wcre_ref, wcim_ref, y_ref, lhs_ref, *, r, tn):
    j = pl.program_id(2)
    pieces = u_ref.shape[0]

    @pl.when(j == 0)
    def _():
        _fold_time_blocks(u_ref, lhs_ref, r)

    y = _dot(lhs_ref[...], wi_ref[...])
    nt = (((1,), (1,)), ((), ()))
    y = y + lax.dot_general(xre_ref[...], wcre_ref[...], nt, preferred_element_type=F32)
    y = y + lax.dot_general(xim_ref[...], wcim_ref[...], nt, preferred_element_type=F32)

    per = tn // (pieces * LANES)
    for jj in range(T_BLK // per):
        @pl.when(j == jj)
        def _():
            for k in range(per):
                for h in range(pieces):
                    col = (k * pieces + h) * LANES
                    y_ref[h, pl.ds(jj * per + k, r, stride=T_BLK), :] = y[:, col:col + LANES]


def _s5_out(u, x_re, x_im, w_i, w_cre, w_cim, r, tn):
    n_cb, kw, _ = w_i.shape
    n = u.shape[1]
    nb = n // T_BLK
    ppc = u.shape[0] // n_cb
    kx = x_re.shape[-1]
    assert tn % (ppc * LANES) == 0 and kw % tn == 0
    blocks = (2 * _nbytes((ppc, T_BLK * r, LANES), F32) + 2 * _nbytes((r, kx), BF16) + _nbytes((kw, tn), BF16)
              + 2 * _nbytes((kx, tn), BF16) + _nbytes((r, tn), F32))
    scratch = _nbytes((r, kw), BF16)
    return pl.pallas_call(
        functools.partial(_s5_out_kernel, r=r, tn=tn),
        out_shape=jax.ShapeDtypeStruct(u.shape, F32),
        grid=(n_cb, nb // r, kw // tn),
        in_specs=[pl.BlockSpec((ppc, T_BLK * r, LANES), lambda c, i, j: (c, i, 0)),
                  pl.BlockSpec((None, r, kx), lambda c, i, j: (c, i, 0)),
                  pl.BlockSpec((None, r, kx), lambda c, i, j: (c, i, 0)),
                  pl.BlockSpec((None, kw, tn), lambda c, i, j: (c, 0, j)),
                  pl.BlockSpec((None, tn, kx), lambda c, i, j: (c, j, 0)),
                  pl.BlockSpec((None, tn, kx), lambda c, i, j: (c, j, 0))],
        out_specs=pl.BlockSpec((ppc, T_BLK * r, LANES), lambda c, i, j: (c, i, 0)),
        scratch_shapes=[pltpu.VMEM((r, kw), BF16)],
        compiler_params=_compiler_params(("parallel", "parallel", "arbitrary"), blocks, scratch),
        name="s5_out",
    )(u, x_re, x_im, w_i, w_cre, w_cim)


def _glu_merge_kernel(y_ref, u_ref, d_ref, w1_ref, w2_ref, pa_ref, gb_ref, m_ref, s_ref):
    @pl.when(pl.program_id(1) == 0)
    def _():
        for c in range(y_ref.shape[0]):
            s = jax.nn.gelu(y_ref[c] + d_ref[c] * u_ref[c])
            s_ref[:, c * LANES:(c + 1) * LANES] = s.astype(s_ref.dtype)

    a = s_ref[...]
    y_b = _dot(a, w1_ref[...]) * _sigmoid(_dot(a, w2_ref[...]))
    m_ref[...] = (pa_ref[...].astype(F32) + gb_ref[...].astype(F32) * y_b).astype(m_ref.dtype)


def _glu_merge(y, u, d_skip, w_glu, layer, pa, gates, d_model, tm, tn):
    n_pc, n, _ = y.shape
    k = n_pc * LANES
    nb = d_model // tn
    blocks = (2 * _nbytes((n_pc, tm, LANES), F32) + 2 * _nbytes((k, tn), BF16) + 3 * _nbytes((tm, tn), BF16))
    scratch = _nbytes((tm, k), BF16)
    return pl.pallas_call(
        _glu_merge_kernel,
        out_shape=jax.ShapeDtypeStruct((n, d_model), BF16),
        grid=(n // tm, nb),
        in_specs=[pl.BlockSpec((n_pc, tm, LANES), lambda i, j: (0, i, 0)),
                  pl.BlockSpec((n_pc, tm, LANES), lambda i, j: (0, i, 0)),
                  pl.BlockSpec((n_pc, 1, LANES), lambda i, j: (0, 0, 0)),
                  _w_spec(k, tn, layer, 0),
                  _w_spec(k, tn, layer, nb),
                  pl.BlockSpec((tm, tn), lambda i, j: (i, j)),
                  pl.BlockSpec((tm, tn), lambda i, j: (i, nb + j))],
        out_specs=pl.BlockSpec((tm, tn), lambda i, j: (i, j)),
        scratch_shapes=[pltpu.VMEM((tm, k), BF16)],
        compiler_params=_compiler_params(("parallel", "arbitrary"), blocks, scratch),
        name="glu_merge",
    )(y, u, d_skip, w_glu, w_glu, pa, gates)


def _ffn_up_kernel(h_ref, hp_ref, hn_ref, wg_ref, wv_ref, cg_ref, cv_ref, act_ref, lhs_ref, *, tm):
    i = pl.program_id(0)
    j = pl.program_id(1)
    last = pl.num_programs(0) - 1
    halo = BF16_ROWS

    @pl.when(j == 0)
    def _():
        row = lax.broadcasted_iota(jnp.int32, hp_ref.shape, 0)
        after = jnp.where(jnp.logical_and(row == 0, i < last), hn_ref[...].astype(F32), 0.0)
        edge = jnp.where(jnp.logical_and(row == halo - 1, i > 0), hp_ref[...].astype(F32), after)
        lhs_ref[0:tm, :] = h_ref[...]
        lhs_ref[tm:, :] = edge.astype(lhs_ref.dtype)

    a = lhs_ref[...]
    rows = tm + halo

    def conv(w_ref, cw_ref):
        u = _dot(a, w_ref[...])
        cw = cw_ref[...]
        up = pltpu.roll(u, 1, 0)[0:tm]
        dn = pltpu.roll(u, rows - 1, 0)[0:tm]
        return up * cw[0:1, :] + u[0:tm] * cw[1:2, :] + dn * cw[2:3, :]

    act_ref[...] = (jax.nn.gelu(conv(wg_ref, cg_ref)) * conv(wv_ref, cv_ref)).astype(act_ref.dtype)


def _ffn_up(hn, w_up, layer, conv_w, d_ff, tm, tn, cast=None):
    n, k = hn.shape
    nb = d_ff // tn
    hb = tm // BF16_ROWS
    n_hb = n // BF16_ROWS
    blocks = (_nbytes((tm + 2 * BF16_ROWS, k), BF16) + 2 * _nbytes((k, tn), BF16) + 2 * _nbytes((3, tn), F32)
              + _nbytes((tm, tn), BF16))
    scratch = _nbytes((tm + BF16_ROWS, k), BF16)
    outs = _call_with_cast(
        functools.partial(_ffn_up_kernel, tm=tm), cast,
        grid=(n // tm, nb),
        in_specs=[pl.BlockSpec((tm, k), lambda i, j: (i, 0)),
                  pl.BlockSpec((BF16_ROWS, k), lambda i, j: (jnp.maximum(i * hb - 1, 0), 0)),
                  pl.BlockSpec((BF16_ROWS, k), lambda i, j: (jnp.minimum((i + 1) * hb, n_hb - 1), 0)),
                  _w_spec(k, tn, layer, 0),
                  _w_spec(k, tn, layer, nb),
                  pl.BlockSpec((3, tn), lambda i, j: (0, j)),
                  pl.BlockSpec((3, tn), lambda i, j: (0, nb + j))],
        out_specs=[pl.BlockSpec((tm, tn), lambda i, j: (i, j))],
        out_shape=[jax.ShapeDtypeStruct((n, d_ff), BF16)],
        operands=(hn, hn, hn, w_up, w_up, conv_w, conv_w),
        scratch_shapes=[pltpu.VMEM((tm + BF16_ROWS, k), BF16)],
        block_bytes=blocks, scratch_bytes=scratch, name="ffn_up")
    return outs[0] if cast is None else outs


def kernel(x, meta_tokens, norm_mix, w_in, conv_a_w, w_a, ssm_lambda_re, ssm_lambda_im, ssm_log_step,
           ssm_b_re, ssm_b_im, ssm_c_re, ssm_c_im, ssm_d, w_glu, w_out, norm_ffn, w_up, conv_ffn_w,
           w_down, norm_final):
    bsz, seq, d_model = x.shape
    n_meta = meta_tokens.shape[0]
    depth = w_in.shape[0]
    d_conv = w_a.shape[1]
    d_ssm = ssm_d.shape[1]
    d_ff = w_down.shape[1]
    n_groups, p_state, h_grp = ssm_b_re.shape[2:]
    assert w_in.shape[2] == 3 * d_conv + d_ssm + 2 * d_model and n_groups * h_grp == d_ssm

    cbw = min(S5_CB_WIDTH, d_ssm)
    n_cb = d_ssm // cbw
    gl = cbw // h_grp
    assert n_cb * cbw == d_ssm and gl * h_grp == cbw

    n_true = n_meta + seq
    row_align = T_BLK * BF16_ROWS
    n_pad = _round_up(n_true, PAD_ROWS if n_true >= 4 * PAD_ROWS else row_align)
    assert n_pad % row_align == 0
    nb = n_pad // T_BLK

    tm = _pick_tile(n_pad, TM_TARGET, BF16_ROWS)
    tm_up = _pick_tile(n_pad, TM_FFN_UP_TARGET, BF16_ROWS)
    tm_half = _pick_tile(n_pad, TM_HALF_TARGET, BF16_ROWS)
    tm_ew = _pick_tile(n_pad, TM_EW_TARGET, BF16_ROWS)
    r_s5 = _pick_tile(nb, R_S5_TARGET, BF16_ROWS)
    tn = lambda n, target=512: _pick_tile(n, target, 128)

    prep = _s5_prep(ssm_lambda_re, ssm_lambda_im, ssm_log_step, ssm_b_re, ssm_b_im, ssm_c_re, ssm_c_im, gl)
    d_skip = ssm_d.astype(F32).reshape(depth, d_ssm // LANES, 1, LANES)

    residual = functools.partial(_epi_residual, n_valid=n_true)
    tile_spec = lambda t, rows=tm: pl.BlockSpec((rows, t), lambda i, j: (i, j))
    u_off = 3 * d_conv
    gate_off = u_off + d_ssm

    w_a_b, w_glu_b, w_out_b = (w.astype(BF16) for w in (w_a, w_glu, w_out))
    w_in_first = w_in[0].astype(BF16)

    outs = []
    for b in range(bsz):
        w_in_l = w_in_first
        for l in range(depth):
            w_s, w_cre, w_cim, w_i, apw = _s5_weights(prep, l, n_cb, gl, h_grp, p_state)

            if l == 0:
                h_res, hn = _assemble_norm(meta_tokens, x[b], norm_mix[0], n_pad, tm_ew)
            else:
                hn = _rmsnorm(h_res, norm_mix[l], BF16, tm_ew)
            bg, cv, u, w_down_l = _inproj_conv(hn, w_in_l, None, d_conv, d_ssm, tm_half, tn(d_conv),
                                               cast=(w_down, l))
            t_g = tn(2 * d_model)
            gates, w_up_l = _mm(hn, w_in_l, None, col0=gate_off, n_cols=2 * d_model, tm=tm, tn=t_g,
                                epilogue=_epi_sigmoid,
                                out_shape=jax.ShapeDtypeStruct((n_pad, 2 * d_model), BF16),
                                out_spec=tile_spec(t_g), name="inproj_gates", cast=(w_up, l))

            pa = _branch_a(bg, cv, conv_a_w[l].astype(F32), w_a_b, l, gates, d_model, tm, tn(d_model, 1024))
            t_d = tn(d_model)

            states = _s5_states(u, w_s, _pick_tile(nb, R_S5_STATES_TARGET, BF16_ROWS), tn(w_s.shape[-1]))
            x_re, x_im = _s5_scan(states, apw, tn(gl * p_state, SCAN_LANES_TARGET))
            y = _s5_out(u, x_re, x_im, w_i, w_cre, w_cim, r_s5, tn(T_BLK * cbw))

            merged = _glu_merge(y, u, d_skip[l], w_glu_b, l, pa, gates, d_model, tm_half, tn(d_model, 1024))
            h_res = _mm(merged, w_out_b, l, col0=0, n_cols=d_model, tm=tm, tn=t_d, epilogue=residual,
                        extras=(h_res,), extra_specs=(tile_spec(t_d),),
                        out_shape=jax.ShapeDtypeStruct((n_pad, d_model), F32), out_spec=tile_spec(t_d),
                        name="out_proj")

            hn = _rmsnorm(h_res, norm_ffn[l], BF16, tm_ew)
            if l + 1 < depth:
                act, w_in_l = _ffn_up(hn, w_up_l, None, conv_ffn_w[l].astype(F32), d_ff, tm_up, tn(d_ff),
                                      cast=(w_in, l + 1))
            else:
                act = _ffn_up(hn, w_up_l, None, conv_ffn_w[l].astype(F32), d_ff, tm_up, tn(d_ff))
            t_o = tn(d_model, 256)
            h_res = _mm(act, w_down_l, None, col0=0, n_cols=d_model, tm=tm_half, tn=t_o, epilogue=residual,
                        extras=(h_res,), extra_specs=(tile_spec(t_o, tm_half),),
                        out_shape=jax.ShapeDtypeStruct((n_pad, d_model), F32), out_spec=tile_spec(t_o, tm_half),
                        name="ffn_down")

        outs.append(_final_norm(h_res, norm_final, n_meta, seq, x.dtype, _pick_tile(seq, TM_EW_TARGET, n_meta)))
    return jnp.stack(outs, axis=0)
```
